```python
import math
import jax, jax.numpy as jnp
from jax import lax
import numpy as np

D_MODEL = 2048
BATCH = 8
SEQ = 4096
DEPTH = 4

HEAD_DIM = 128
A_PAIRS = ((128, 1), (512, 4), (2048, 16))
A_GROUPS = len(A_PAIRS)
A_HEADS = D_MODEL // 256
A_OUT = A_HEADS * HEAD_DIM
A_COLS = A_GROUPS * 3 * A_HEADS * HEAD_DIM
B_CHUNK = 128
B_GROUPS = D_MODEL // 256
B_WIDTH = B_GROUPS * HEAD_DIM
B_COLS = 2 * B_WIDTH
MEM_TOKENS = 256
MEM_HEADS = 4
MEM_WIDTH = MEM_HEADS * HEAD_DIM
MIX_OUT = A_OUT + MEM_WIDTH
D_FF = 4 * D_MODEL
N_LAYERS_A = (DEPTH + 1) // 2
N_LAYERS_B = DEPTH // 2
ALPHA = (2.0 * DEPTH) ** 0.25
BETA = (8.0 * DEPTH) ** -0.25
LN_EPS = 1e-5

kernel_name = "hybrid_dilated_gmlp_memory_deepnorm"


def _layer_norm(x, g, b):
    xf = x.astype(jnp.float32)
    mu = jnp.mean(xf, axis=-1, keepdims=True)
    var = jnp.mean(jnp.square(xf - mu), axis=-1, keepdims=True)
    y = (xf - mu) * lax.rsqrt(var + LN_EPS) * g.astype(jnp.float32) + b.astype(jnp.float32)
    return y.astype(x.dtype)


def _dilated_branch(q, k, v, window, dilation):
    b, s, h, dh = q.shape
    w_sub = window // dilation
    blk = w_sub
    L = s // dilation
    nb = -(-L // blk)
    Lp = nb * blk

    def to_sub(t):
        t = t.reshape(b, L, dilation, h, dh).transpose(0, 2, 1, 3, 4)
        t = jnp.pad(t, ((0, 0), (0, 0), (0, Lp - L), (0, 0), (0, 0)))
        return t.reshape(b, dilation, nb, blk, h, dh)

    qb, kb, vb = to_sub(q), to_sub(k), to_sub(v)

    def with_prev(t):
        prev = jnp.pad(t[:, :, :-1], ((0, 0), (0, 0), (1, 0), (0, 0), (0, 0), (0, 0)))
        return jnp.concatenate([prev, t], axis=3)

    kk, vv = with_prev(kb), with_prev(vb)
    scores = jnp.einsum('brnqhd,brnkhd->brnhqk', qb, kk).astype(jnp.float32) * (dh ** -0.5)
    qi = jnp.arange(blk)[:, None]
    kj = jnp.arange(2 * blk)[None, :]
    dist = qi + blk - kj
    band = (dist >= 0) & (dist <= w_sub)
    exists = (jnp.arange(nb) > 0)[:, None, None] | (kj >= blk)[None]
    mask = band[None] & exists
    scores = jnp.where(mask[None, None, :, None], scores, -jnp.inf)
    m = jnp.max(scores, axis=-1, keepdims=True)
    p = jnp.exp(scores - m)
    l = jnp.sum(p, axis=-1, keepdims=True)
    o = jnp.einsum('brnhqk,brnkhd->brnqhd', p, vv.astype(jnp.float32))
    o = o / jnp.moveaxis(l, 3, 4)
    lse = jnp.moveaxis((m + jnp.log(l))[..., 0], 3, 4)

    def from_sub(t):
        t = t.reshape((b, dilation, Lp) + t.shape[4:])[:, :, :L]
        t = jnp.moveaxis(t, 1, 2)
        return t.reshape((b, s) + t.shape[3:])

    return from_sub(o), from_sub(lse)


def _mixer_dilated(x, w_in):
    b, s, _ = x.shape
    proj = x @ w_in
    qkv = proj[..., :A_COLS].reshape(b, s, A_GROUPS, 3, A_HEADS, HEAD_DIM)
    mem_q = proj[..., A_COLS:]
    outs, lses = [], []
    for g, (window, dilation) in enumerate(A_PAIRS):
        o, lse = _dilated_branch(qkv[:, :, g, 0], qkv[:, :, g, 1], qkv[:, :, g, 2], window, dilation)
        outs.append(o)
        lses.append(lse)
    wts = jax.nn.softmax(jnp.stack(lses, axis=0), axis=0)
    o = jnp.sum(wts[..., None] * jnp.stack(outs, axis=0), axis=0)
    return o.reshape(b, s, A_OUT).astype(x.dtype), mem_q


def _mixer_gmlp(x, w_in, w_s, b_s, vnorm_g, vnorm_b):
    b, s, _ = x.shape
    proj = x @ w_in
    z = jax.nn.gelu(proj[..., :B_COLS], approximate=False)
    u, v = z[..., :B_WIDTH], z[..., B_WIDTH:]
    mem_q = proj[..., B_COLS:]
    v = _layer_norm(v, vnorm_g, vnorm_b)
    v = v.reshape(b, s // B_CHUNK, B_CHUNK, B_GROUPS, HEAD_DIM)
    causal = jnp.tril(jnp.ones((B_CHUNK, B_CHUNK), dtype=w_s.dtype))
    ws = w_s * causal[None]
    sg = jnp.einsum('gts,bcsgd->bctgd', ws, v) + b_s.T[:, :, None]
    return u * sg.reshape(b, s, B_WIDTH), mem_q


def _memory_attention(mem_q, mem, w_mem_kv):
    b, s, _ = mem_q.shape
    q = mem_q.reshape(b, s, MEM_HEADS, HEAD_DIM)
    kv = (mem @ w_mem_kv).reshape(b, MEM_TOKENS, 2, MEM_HEADS, HEAD_DIM)
    k, v = kv[:, :, 0], kv[:, :, 1]
    scores = jnp.einsum('bshd,bmhd->bhsm', q, k).astype(jnp.float32) * (HEAD_DIM ** -0.5)
    p = jax.nn.softmax(scores, axis=-1)
    o = jnp.einsum('bhsm,bmhd->bshd', p, v.astype(jnp.float32))
    return o.reshape(b, s, MEM_WIDTH).astype(mem_q.dtype)


def _fwd_setup_inputs(seed: int = 0) -> dict:
    key = jax.random.key(seed)
    ks = jax.random.split(key, 20)

    def nrm(k, shape, std):
        return jax.random.normal(k, shape, jnp.float32) * std

    x = nrm(ks[0], (BATCH, SEQ, D_MODEL), 1.0)
    mem = nrm(ks[1], (BATCH, MEM_TOKENS, D_MODEL), 1.0)
    a_scale = np.ones((A_GROUPS, 3, A_HEADS * HEAD_DIM), np.float32)
    a_scale[:, 2, :] = BETA
    a_scale = jnp.asarray(np.concatenate([a_scale.reshape(-1), np.ones(MEM_WIDTH, np.float32)]))
    w_in_a = nrm(ks[2], (N_LAYERS_A, D_MODEL, A_COLS + MEM_WIDTH), D_MODEL ** -0.5) * a_scale
    b_scale = jnp.asarray(np.concatenate([np.full(B_COLS, BETA, np.float32), np.ones(MEM_WIDTH, np.float32)]))
    w_in_b = nrm(ks[3], (N_LAYERS_B, D_MODEL, B_COLS + MEM_WIDTH), D_MODEL ** -0.5) * b_scale
    w_s = nrm(ks[4], (N_LAYERS_B, B_GROUPS, B_CHUNK, B_CHUNK), B_CHUNK ** -0.5)
    b_s = 1.0 + nrm(ks[5], (N_LAYERS_B, B_GROUPS, B_CHUNK), 0.02)
    vnorm_g = 1.0 + nrm(ks[6], (N_LAYERS_B, B_WIDTH), 0.02)
    vnorm_b = nrm(ks[7], (N_LAYERS_B, B_WIDTH), 0.02)
    kv_scale = jnp.concatenate([jnp.ones((MEM_WIDTH,), jnp.float32), jnp.full((MEM_WIDTH,), BETA, jnp.float32)])
    w_mem_kv = nrm(ks[8], (DEPTH, D_MODEL, 2 * MEM_WIDTH), D_MODEL ** -0.5) * kv_scale
    w_out = nrm(ks[9], (DEPTH, MIX_OUT, D_MODEL), BETA * MIX_OUT ** -0.5)
    ln1_g = 1.0 + nrm(ks[10], (DEPTH, D_MODEL), 0.02)
    ln1_b = nrm(ks[11], (DEPTH, D_MODEL), 0.02)
    w_ff1 = nrm(ks[12], (DEPTH, D_MODEL, D_FF), BETA * D_MODEL ** -0.5)
    w_ff2 = nrm(ks[13], (DEPTH, D_FF, D_MODEL), BETA * D_FF ** -0.5)
    ln2_g = 1.0 + nrm(ks[14], (DEPTH, D_MODEL), 0.02)
    ln2_b = nrm(ks[15], (DEPTH, D_MODEL), 0.02)
    return {"x": x, "mem": mem, "w_in_a": w_in_a, "w_in_b": w_in_b, "w_s": w_s, "b_s": b_s,
            "vnorm_g": vnorm_g, "vnorm_b": vnorm_b, "w_mem_kv": w_mem_kv, "w_out": w_out,
            "ln1_g": ln1_g, "ln1_b": ln1_b, "w_ff1": w_ff1, "w_ff2": w_ff2,
            "ln2_g": ln2_g, "ln2_b": ln2_b}


def _fwd_reference(x, mem, w_in_a, w_in_b, w_s, b_s, vnorm_g, vnorm_b, w_mem_kv, w_out,
              ln1_g, ln1_b, w_ff1, w_ff2, ln2_g, ln2_b):
    for i in range(DEPTH):
        j = i // 2
        if i % 2 == 0:
            mix, mem_q = _mixer_dilated(x, w_in_a[j])
        else:
            mix, mem_q = _mixer_gmlp(x, w_in_b[j], w_s[j], b_s[j], vnorm_g[j], vnorm_b[j])
        mem_o = _memory_attention(mem_q, mem, w_mem_kv[i])
        y = jnp.concatenate([mix, mem_o], axis=-1) @ w_out[i]
        x = _layer_norm(ALPHA * x + y, ln1_g[i], ln1_b[i])
        hidden = jnp.square(jax.nn.relu(x @ w_ff1[i]))
        x = _layer_norm(ALPHA * x + hidden @ w_ff2[i], ln2_g[i], ln2_b[i])
    return x


import jax as _jax
import jax.numpy as _jnp

TWIN_FORMAT = 'train_step'
FWD_PARAMS = ['x', 'mem', 'w_in_a', 'w_in_b', 'w_s', 'b_s', 'vnorm_g', 'vnorm_b', 'w_mem_kv', 'w_out', 'ln1_g', 'ln1_b', 'w_ff1', 'w_ff2', 'ln2_g', 'ln2_b']
TWIN_WEIGHTS = ['w_in_a', 'w_in_b', 'w_s', 'b_s', 'vnorm_g', 'vnorm_b', 'w_mem_kv', 'w_out', 'ln1_g', 'ln1_b', 'w_ff1', 'w_ff2', 'ln2_g', 'ln2_b']
TWIN_DIFF_INPUT = 'x'
TWIN_INPUTS = ['x', 'mem', 'w_in_a', 'w_in_b', 'w_s', 'b_s', 'vnorm_g', 'vnorm_b', 'w_mem_kv', 'w_out', 'ln1_g', 'ln1_b', 'w_ff1', 'w_ff2', 'ln2_g', 'ln2_b', 'loss_target', 'm_w_in_a', 'm_w_in_b', 'm_w_s', 'm_b_s', 'm_vnorm_g', 'm_vnorm_b', 'm_w_mem_kv', 'm_w_out', 'm_ln1_g', 'm_ln1_b', 'm_w_ff1', 'm_w_ff2', 'm_ln2_g', 'm_ln2_b', 'v_w_in_a', 'v_w_in_b', 'v_w_s', 'v_b_s', 'v_vnorm_g', 'v_vnorm_b', 'v_w_mem_kv', 'v_w_out', 'v_ln1_g', 'v_ln1_b', 'v_w_ff1', 'v_w_ff2', 'v_ln2_g', 'v_ln2_b']
TWIN_OUTPUTS = ['loss', 'grad_x', 'grad_w_in_a', 'grad_w_in_b', 'grad_w_s', 'grad_b_s', 'grad_vnorm_g', 'grad_vnorm_b', 'grad_w_mem_kv', 'grad_w_out', 'grad_ln1_g', 'grad_ln1_b', 'grad_w_ff1', 'grad_w_ff2', 'grad_ln2_g', 'grad_ln2_b', 'delta_w_in_a', 'delta_w_in_b', 'delta_w_s', 'delta_b_s', 'delta_vnorm_g', 'delta_vnorm_b', 'delta_w_mem_kv', 'delta_w_out', 'delta_ln1_g', 'delta_ln1_b', 'delta_w_ff1', 'delta_w_ff2', 'delta_ln2_g', 'delta_ln2_b', 'new_m_w_in_a', 'new_m_w_in_b', 'new_m_w_s', 'new_m_b_s', 'new_m_vnorm_g', 'new_m_vnorm_b', 'new_m_w_mem_kv', 'new_m_w_out', 'new_m_ln1_g', 'new_m_ln1_b', 'new_m_w_ff1', 'new_m_w_ff2', 'new_m_ln2_g', 'new_m_ln2_b', 'new_v_w_in_a', 'new_v_w_in_b', 'new_v_w_s', 'new_v_b_s', 'new_v_vnorm_g', 'new_v_vnorm_b', 'new_v_w_mem_kv', 'new_v_w_out', 'new_v_ln1_g', 'new_v_ln1_b', 'new_v_w_ff1', 'new_v_w_ff2', 'new_v_ln2_g', 'new_v_ln2_b']
TWIN_LEAF_KINDS = {'loss': 'loss', 'grad_x': 'grad_x', 'grad_w_in_a': 'grad_w', 'grad_w_in_b': 'grad_w', 'grad_w_s': 'grad_w', 'grad_b_s': 'grad_w', 'grad_vnorm_g': 'grad_w', 'grad_vnorm_b': 'grad_w', 'grad_w_mem_kv': 'grad_w', 'grad_w_out': 'grad_w', 'grad_ln1_g': 'grad_w', 'grad_ln1_b': 'grad_w', 'grad_w_ff1': 'grad_w', 'grad_w_ff2': 'grad_w', 'grad_ln2_g': 'grad_w', 'grad_ln2_b': 'grad_w', 'delta_w_in_a': 'delta_w', 'delta_w_in_b': 'delta_w', 'delta_w_s': 'delta_w', 'delta_b_s': 'delta_w', 'delta_vnorm_g': 'delta_w', 'delta_vnorm_b': 'delta_w', 'delta_w_mem_kv': 'delta_w', 'delta_w_out': 'delta_w', 'delta_ln1_g': 'delta_w', 'delta_ln1_b': 'delta_w', 'delta_w_ff1': 'delta_w', 'delta_w_ff2': 'delta_w', 'delta_ln2_g': 'delta_w', 'delta_ln2_b': 'delta_w', 'new_m_w_in_a': 'new_m', 'new_m_w_in_b': 'new_m', 'new_m_w_s': 'new_m', 'new_m_b_s': 'new_m', 'new_m_vnorm_g': 'new_m', 'new_m_vnorm_b': 'new_m', 'new_m_w_mem_kv': 'new_m', 'new_m_w_out': 'new_m', 'new_m_ln1_g': 'new_m', 'new_m_ln1_b': 'new_m', 'new_m_w_ff1': 'new_m', 'new_m_w_ff2': 'new_m', 'new_m_ln2_g': 'new_m', 'new_m_ln2_b': 'new_m', 'new_v_w_in_a': 'new_v', 'new_v_w_in_b': 'new_v', 'new_v_w_s': 'new_v', 'new_v_b_s': 'new_v', 'new_v_vnorm_g': 'new_v', 'new_v_vnorm_b': 'new_v', 'new_v_w_mem_kv': 'new_v', 'new_v_w_out': 'new_v', 'new_v_ln1_g': 'new_v', 'new_v_ln1_b': 'new_v', 'new_v_w_ff1': 'new_v', 'new_v_w_ff2': 'new_v', 'new_v_ln2_g': 'new_v', 'new_v_ln2_b': 'new_v'}


def _forward(args):
    return _fwd_reference(*[args[k] for k in FWD_PARAMS])


def _output_shape():
    def fwd():
        inp = _fwd_setup_inputs(0)
        return _fwd_reference(*[inp[k] for k in FWD_PARAMS])
    out = _jax.eval_shape(fwd)
    return out.shape, out.dtype

N_MICROBATCH = 1
ADAM_LR = 0.001
ADAM_B1 = 0.9
ADAM_B2 = 0.999
ADAM_EPS = 1e-08
ADAM_WD = 0.01
ADAM_STEP = 10
PER_EXAMPLE_BATCH_AXIS = {'x': 0, 'mem': 0, 'loss_target': 0}
SHARED_INPUTS = []
_WEIGHT_DTYPES = {'w_in_a': _jnp.float32, 'w_in_b': _jnp.float32, 'w_s': _jnp.float32, 'b_s': _jnp.float32, 'vnorm_g': _jnp.float32, 'vnorm_b': _jnp.float32, 'w_mem_kv': _jnp.float32, 'w_out': _jnp.float32, 'ln1_g': _jnp.float32, 'ln1_b': _jnp.float32, 'w_ff1': _jnp.float32, 'w_ff2': _jnp.float32, 'ln2_g': _jnp.float32, 'ln2_b': _jnp.float32}
MOMENT_SCALE = {'w_in_a': 1.818123e-03, 'w_in_b': 1.340444e-02, 'w_s': 4.263943e-03, 'b_s': 6.115870e-03, 'vnorm_g': 4.211895e-03, 'vnorm_b': 4.400076e-03, 'w_mem_kv': 2.254386e-03, 'w_out': 1.026456e-02, 'ln1_g': 6.005251e-01, 'ln1_b': 2.934849e-01, 'w_ff1': 6.614391e-03, 'w_ff2': 1.690717e-02, 'ln2_g': 8.054872e+00, 'ln2_b': 6.121262e-01}


def _to_microbatches(a, axis):
    t = _jnp.moveaxis(a, axis, 0)
    t = t.reshape((N_MICROBATCH, t.shape[0] // N_MICROBATCH) + t.shape[1:])
    return _jnp.moveaxis(t, 1, axis + 1)


def setup_inputs(seed: int = 0) -> dict:
    inp = _fwd_setup_inputs(seed)
    key = _jax.random.fold_in(_jax.random.key(seed), 7919)
    shape, _ = _output_shape()
    out = dict(inp)
    out["loss_target"] = _jax.random.normal(_jax.random.fold_in(key, 0), shape, _jnp.float32)
    for i, name in enumerate(TWIN_WEIGHTS):
        w = inp[name].astype(_jnp.float32)
        if MOMENT_SCALE is None:
            s = _jnp.sqrt(_jnp.mean(_jnp.square(w)) + 1e-30)
        else:
            s = MOMENT_SCALE[name]
        km, kv = _jax.random.split(_jax.random.fold_in(key, i + 1))
        out[name] = w
        out["m_" + name] = s * _jax.random.normal(km, w.shape, _jnp.float32)
        out["v_" + name] = (s * s) * _jax.random.uniform(kv, w.shape, _jnp.float32, 0.5, 1.5)
    if N_MICROBATCH > 1:
        for name, axis in PER_EXAMPLE_BATCH_AXIS.items():
            out[name] = _to_microbatches(out[name], axis)
    return {'x': out['x'], 'mem': out['mem'], 'w_in_a': out['w_in_a'], 'w_in_b': out['w_in_b'], 'w_s': out['w_s'], 'b_s': out['b_s'], 'vnorm_g': out['vnorm_g'], 'vnorm_b': out['vnorm_b'], 'w_mem_kv': out['w_mem_kv'], 'w_out': out['w_out'], 'ln1_g': out['ln1_g'], 'ln1_b': out['ln1_b'], 'w_ff1': out['w_ff1'], 'w_ff2': out['w_ff2'], 'ln2_g': out['ln2_g'], 'ln2_b': out['ln2_b'], 'loss_target': out['loss_target'], 'm_w_in_a': out['m_w_in_a'], 'm_w_in_b': out['m_w_in_b'], 'm_w_s': out['m_w_s'], 'm_b_s': out['m_b_s'], 'm_vnorm_g': out['m_vnorm_g'], 'm_vnorm_b': out['m_vnorm_b'], 'm_w_mem_kv': out['m_w_mem_kv'], 'm_w_out': out['m_w_out'], 'm_ln1_g': out['m_ln1_g'], 'm_ln1_b': out['m_ln1_b'], 'm_w_ff1': out['m_w_ff1'], 'm_w_ff2': out['m_w_ff2'], 'm_ln2_g': out['m_ln2_g'], 'm_ln2_b': out['m_ln2_b'], 'v_w_in_a': out['v_w_in_a'], 'v_w_in_b': out['v_w_in_b'], 'v_w_s': out['v_w_s'], 'v_b_s': out['v_b_s'], 'v_vnorm_g': out['v_vnorm_g'], 'v_vnorm_b': out['v_vnorm_b'], 'v_w_mem_kv': out['v_w_mem_kv'], 'v_w_out': out['v_w_out'], 'v_ln1_g': out['v_ln1_g'], 'v_ln1_b': out['v_ln1_b'], 'v_w_ff1': out['v_w_ff1'], 'v_w_ff2': out['v_w_ff2'], 'v_ln2_g': out['v_ln2_g'], 'v_ln2_b': out['v_ln2_b']}


def _loss(weights, diff, rest, loss_target):
    with _jax.named_scope("forward"):
        args = {**rest, TWIN_DIFF_INPUT: diff, **{k: w.astype(_WEIGHT_DTYPES[k]) for k, w in weights.items()}}
        y = _forward(args)
    with _jax.named_scope("loss_head"):
        err = _jnp.square(y.astype(_jnp.float32) - loss_target)
        return 0.5 * _jnp.sum(_jnp.mean(err, axis=-1)) if err.ndim else 0.5 * err


def _adamw(w, g, m, v):
    m = ADAM_B1 * m + (1.0 - ADAM_B1) * g
    v = ADAM_B2 * v + (1.0 - ADAM_B2) * _jnp.square(g)
    m_hat = m / (1.0 - ADAM_B1 ** ADAM_STEP)
    v_hat = v / (1.0 - ADAM_B2 ** ADAM_STEP)
    delta = -ADAM_LR * (m_hat / (_jnp.sqrt(v_hat) + ADAM_EPS) + ADAM_WD * w)
    return delta, m, v


def reference(x, mem, w_in_a, w_in_b, w_s, b_s, vnorm_g, vnorm_b, w_mem_kv, w_out, ln1_g, ln1_b, w_ff1, w_ff2, ln2_g, ln2_b, loss_target, m_w_in_a, m_w_in_b, m_w_s, m_b_s, m_vnorm_g, m_vnorm_b, m_w_mem_kv, m_w_out, m_ln1_g, m_ln1_b, m_w_ff1, m_w_ff2, m_ln2_g, m_ln2_b, v_w_in_a, v_w_in_b, v_w_s, v_b_s, v_vnorm_g, v_vnorm_b, v_w_mem_kv, v_w_out, v_ln1_g, v_ln1_b, v_w_ff1, v_w_ff2, v_ln2_g, v_ln2_b):
    given = dict(x=x, mem=mem, w_in_a=w_in_a, w_in_b=w_in_b, w_s=w_s, b_s=b_s, vnorm_g=vnorm_g, vnorm_b=vnorm_b, w_mem_kv=w_mem_kv, w_out=w_out, ln1_g=ln1_g, ln1_b=ln1_b, w_ff1=w_ff1, w_ff2=w_ff2, ln2_g=ln2_g, ln2_b=ln2_b, loss_target=loss_target, m_w_in_a=m_w_in_a, m_w_in_b=m_w_in_b, m_w_s=m_w_s, m_b_s=m_b_s, m_vnorm_g=m_vnorm_g, m_vnorm_b=m_vnorm_b, m_w_mem_kv=m_w_mem_kv, m_w_out=m_w_out, m_ln1_g=m_ln1_g, m_ln1_b=m_ln1_b, m_w_ff1=m_w_ff1, m_w_ff2=m_w_ff2, m_ln2_g=m_ln2_g, m_ln2_b=m_ln2_b, v_w_in_a=v_w_in_a, v_w_in_b=v_w_in_b, v_w_s=v_w_s, v_b_s=v_b_s, v_vnorm_g=v_vnorm_g, v_vnorm_b=v_vnorm_b, v_w_mem_kv=v_w_mem_kv, v_w_out=v_w_out, v_ln1_g=v_ln1_g, v_ln1_b=v_ln1_b, v_w_ff1=v_w_ff1, v_w_ff2=v_w_ff2, v_ln2_g=v_ln2_g, v_ln2_b=v_ln2_b)
    weights = {n: given[n] for n in TWIN_WEIGHTS}
    shared = {n: given[n] for n in SHARED_INPUTS}
    per_example = {n: given[n] for n in ['x', 'mem']}
    grad_fn = _jax.value_and_grad(_loss, argnums=(0, 1))

    def one_microbatch(ex, loss_target):
        ex = dict(ex)
        diff = ex.pop(TWIN_DIFF_INPUT)
        return grad_fn(weights, diff, {**shared, **ex}, loss_target)

    if N_MICROBATCH == 1:
        loss, (grad_w, grad_x) = one_microbatch(per_example, given["loss_target"])
    else:
        def body(carry, xs):
            loss_sum, grad_sum = carry
            l_k, (gw_k, gx_k) = one_microbatch(xs[0], xs[1])
            with _jax.named_scope("update"):
                return (loss_sum + l_k, _jax.tree.map(_jnp.add, grad_sum, gw_k)), gx_k

        init = (_jnp.zeros((), _jnp.float32), _jax.tree.map(_jnp.zeros_like, weights))
        (loss, grad_w), grad_x = _jax.lax.scan(body, init, (per_example, given["loss_target"]))
    with _jax.named_scope("update"):
        delta_w, new_m, new_v = {}, {}, {}
        for n in TWIN_WEIGHTS:
            delta_w[n], new_m[n], new_v[n] = _adamw(weights[n], grad_w[n], given["m_" + n], given["v_" + n])
    return (loss, grad_x, *[grad_w[n] for n in TWIN_WEIGHTS], *[delta_w[n] for n in TWIN_WEIGHTS],
            *[new_m[n] for n in TWIN_WEIGHTS], *[new_v[n] for n in TWIN_WEIGHTS])
```

```python
import functools
import math

import jax
import jax.numpy as jnp
from jax import lax
from jax.experimental import pallas as pl
from jax.experimental.pallas import tpu as pltpu

F32 = jnp.float32
BF16 = jnp.bfloat16

HEAD_DIM = 128
A_PAIRS = ((128, 1), (512, 4), (2048, 16))
BLK = 128
HEAD_BLOCK = 512
MEM_TOKENS = 256
MEM_HEADS = 4
MEM_WIDTH = MEM_HEADS * HEAD_DIM
LN_EPS = 1e-5
ADAM_LR, ADAM_B1, ADAM_B2, ADAM_EPS, ADAM_WD, ADAM_STEP = 0.001, 0.9, 0.999, 1e-08, 0.01, 10
NEG = -1e30
V7X_VMEM_LIMIT = 48 * 1024 * 1024
MESH = pl.DeviceIdType.MESH
HBM = pl.BlockSpec(memory_space=pltpu.HBM)


def _call(body, **kw):
    return pl.pallas_call(body, **kw)


def _params(sem):
    return pltpu.CompilerParams(dimension_semantics=sem, vmem_limit_bytes=V7X_VMEM_LIMIT)


def _tile(n, cap):
    best = 0
    for t in range(128, min(n, cap) + 1, 128):
        if n % t == 0:
            best = t
    if best == 0 or (best < 512 and n <= 2560):
        return n
    return best


def _rows(n, cap):
    best = 8
    for t in range(8, min(n, cap) + 1, 8):
        if n % t == 0:
            best = t
    return best


def _epilogue(epi, acc, extra_ref, out_refs):
    if epi == "plain":
        out_refs[0][...] = acc.astype(out_refs[0].dtype)
    elif epi == "relu2":
        out_refs[0][...] = acc
        r = jnp.maximum(acc, 0.0)
        out_refs[1][...] = (r * r).astype(out_refs[1].dtype)
    elif epi == "drelu2":
        out_refs[0][...] = (acc * (2.0 * jnp.maximum(extra_ref[...], 0.0))).astype(out_refs[0].dtype)
    elif epi == "resid":
        out_refs[0][...] = acc + extra_ref[...]
    else:
        raise ValueError(epi)


def _mm_nn(a, w, *, epi="plain", out_dtype=BF16, extra=None, name):
    M, K = a.shape
    J, K2, Nj = w.shape
    assert K == K2
    tn = _tile(Nj, 1024)
    tm = _tile(M, 1024 if tn <= 1024 else 512)
    tk = _tile(K, 512)
    nn, nk = Nj // tn, K // tk
    n_out = 2 if epi == "relu2" else 1
    has_extra = extra is not None

    def body(*refs):
        a_ref, w_ref = refs[0], refs[1]
        extra_ref = refs[2] if has_extra else None
        outs = refs[2 + has_extra: 2 + has_extra + n_out]
        acc_ref = refs[-1]
        k = pl.program_id(3)

        @pl.when(k == 0)
        def _():
            acc_ref[...] = jnp.zeros_like(acc_ref)

        acc_ref[...] += jnp.dot(a_ref[...], w_ref[...], preferred_element_type=F32)

        @pl.when(k == nk - 1)
        def _():
            _epilogue(epi, acc_ref[...], extra_ref, outs)

    omap = lambda i, j, n, k: (i, j * nn + n)
    in_specs = [pl.BlockSpec((tm, tk), lambda i, j, n, k: (i, k)),
                pl.BlockSpec((None, tk, tn), lambda i, j, n, k: (j, k, n))]
    args = [a, w]
    if has_extra:
        in_specs.append(pl.BlockSpec((tm, tn), omap))
        args.append(extra)
    if epi == "relu2":
        out_shape = (jax.ShapeDtypeStruct((M, J * Nj), F32), jax.ShapeDtypeStruct((M, J * Nj), BF16))
        out_specs = (pl.BlockSpec((tm, tn), omap), pl.BlockSpec((tm, tn), omap))
    else:
        out_shape = jax.ShapeDtypeStruct((M, J * Nj), F32 if epi == "resid" else out_dtype)
        out_specs = pl.BlockSpec((tm, tn), omap)
    return _call(body, name=name, grid=(M // tm, J, nn, nk), in_specs=in_specs, out_specs=out_specs,
                 out_shape=out_shape, scratch_shapes=[pltpu.VMEM((tm, tn), F32)],
                 compiler_params=_params(("parallel", "parallel", "parallel", "arbitrary")))(*args)


def _mm_nt(a, w, *, epi="plain", out_dtype=BF16, extra=None, name):
    M, N = a.shape
    J, K, Nj = w.shape
    assert N == J * Nj
    tc = _tile(Nj, 1024)
    tko = _tile(K, 1024)
    tm = _tile(M, 1024 if tc <= 1024 else 512)
    nc = Nj // tc
    has_extra = extra is not None

    def body(*refs):
        a_ref, w_ref = refs[0], refs[1]
        extra_ref = refs[2] if has_extra else None
        outs = refs[2 + has_extra: 3 + has_extra]
        acc_ref = refs[-1]
        j, c = pl.program_id(2), pl.program_id(3)

        @pl.when(jnp.logical_and(j == 0, c == 0))
        def _():
            acc_ref[...] = jnp.zeros_like(acc_ref)

        acc_ref[...] += lax.dot_general(a_ref[...], w_ref[...], (((1,), (1,)), ((), ())),
                                        preferred_element_type=F32)

        @pl.when(jnp.logical_and(j == J - 1, c == nc - 1))
        def _():
            _epilogue(epi, acc_ref[...], extra_ref, outs)

    omap = lambda i, ko, j, c: (i, ko)
    in_specs = [pl.BlockSpec((tm, tc), lambda i, ko, j, c: (i, j * nc + c)),
                pl.BlockSpec((None, tko, tc), lambda i, ko, j, c: (j, ko, c))]
    args = [a, w]
    if has_extra:
        in_specs.append(pl.BlockSpec((tm, tko), omap))
        args.append(extra)
    out_shape = jax.ShapeDtypeStruct((M, K), F32 if epi == "resid" else out_dtype)
    return _call(body, name=name, grid=(M // tm, K // tko, J, nc), in_specs=in_specs,
                 out_specs=pl.BlockSpec((tm, tko), omap), out_shape=out_shape,
                 scratch_shapes=[pltpu.VMEM((tm, tko), F32)],
                 compiler_params=_params(("parallel", "parallel", "arbitrary", "arbitrary")))(*args)


def _mm_tn(a, b, J, *, name):
    T, K = a.shape
    T2, N = b.shape
    assert T == T2 and N % J == 0
    Nj = N // J
    tn = _tile(Nj, 1024)
    tkr = _tile(K, 1024 if tn <= 1024 else 512)
    tt = _tile(T, 512)
    nn, nt = Nj // tn, T // tt

    def body(a_ref, b_ref, o_ref, acc_ref):
        t = pl.program_id(3)

        @pl.when(t == 0)
        def _():
            acc_ref[...] = jnp.zeros_like(acc_ref)

        acc_ref[...] += lax.dot_general(a_ref[...], b_ref[...], (((0,), (0,)), ((), ())),
                                        preferred_element_type=F32)

        @pl.when(t == nt - 1)
        def _():
            o_ref[...] = acc_ref[...].astype(o_ref.dtype)

    return _call(body, name=name, grid=(J, K // tkr, nn, nt),
                 in_specs=[pl.BlockSpec((tt, tkr), lambda j, kr, n, t: (t, kr)),
                           pl.BlockSpec((tt, tn), lambda j, kr, n, t: (t, j * nn + n))],
                 out_specs=pl.BlockSpec((None, tkr, tn), lambda j, kr, n, t: (j, kr, n)),
                 out_shape=jax.ShapeDtypeStruct((J, K, Nj), BF16),
                 scratch_shapes=[pltpu.VMEM((tkr, tn), F32)],
                 compiler_params=_params(("parallel", "parallel", "parallel", "arbitrary")))(a, b)


def _ln_fwd(x, y, g, b, alpha, *, name):
    T, D = x.shape
    tr = _rows(T, 256)

    def body(x_ref, y_ref, g_ref, b_ref, o_ref, ob_ref):
        r = alpha * x_ref[...] + y_ref[...]
        mu = jnp.mean(r, axis=-1, keepdims=True)
        xc = r - mu
        var = jnp.mean(xc * xc, axis=-1, keepdims=True)
        o = xc * lax.rsqrt(var + LN_EPS) * g_ref[...] + b_ref[...]
        o_ref[...] = o
        ob_ref[...] = o.astype(BF16)

    row = pl.BlockSpec((tr, D), lambda i: (i, 0))
    vec = pl.BlockSpec((1, D), lambda i: (0, 0))
    return _call(body, name=name, grid=(T // tr,), in_specs=[row, row, vec, vec], out_specs=(row, row),
                 out_shape=(jax.ShapeDtypeStruct((T, D), F32), jax.ShapeDtypeStruct((T, D), BF16)),
                 compiler_params=_params(("parallel",)))(x, y, g.reshape(1, D), b.reshape(1, D))


def _ln_bwd(dout, x, y, g, alpha, *, name):
    T, D = x.shape
    tr = _rows(T, 256)

    def body(do_ref, x_ref, y_ref, g_ref, drb_ref, adr_ref, dg_ref, db_ref):
        i = pl.program_id(0)
        r = alpha * x_ref[...] + y_ref[...]
        mu = jnp.mean(r, axis=-1, keepdims=True)
        xc = r - mu
        var = jnp.mean(xc * xc, axis=-1, keepdims=True)
        rstd = lax.rsqrt(var + LN_EPS)
        xhat = xc * rstd
        do = do_ref[...]
        dxh = do * g_ref[...]
        dr = rstd * (dxh - jnp.mean(dxh, axis=-1, keepdims=True)
                     - xhat * jnp.mean(dxh * xhat, axis=-1, keepdims=True))
        drb_ref[...] = dr.astype(BF16)
        adr_ref[...] = alpha * dr

        @pl.when(i == 0)
        def _():
            dg_ref[...] = jnp.zeros_like(dg_ref)
            db_ref[...] = jnp.zeros_like(db_ref)

        dg_ref[...] += jnp.sum(do * xhat, axis=0, keepdims=True)
        db_ref[...] += jnp.sum(do, axis=0, keepdims=True)

    row = pl.BlockSpec((tr, D), lambda i: (i, 0))
    vec = pl.BlockSpec((1, D), lambda i: (0, 0))
    return _call(body, name=name, grid=(T // tr,), in_specs=[row, row, row, vec], out_specs=(row, row, vec, vec),
                 out_shape=(jax.ShapeDtypeStruct((T, D), BF16), jax.ShapeDtypeStruct((T, D), F32),
                            jax.ShapeDtypeStruct((1, D), F32), jax.ShapeDtypeStruct((1, D), F32)),
                 compiler_params=_params(("arbitrary",)))(dout, x, y, g.reshape(1, D))


def _loss_head(xf, target, *, name):
    T, D = xf.shape
    tr = _rows(T, 256)

    def body(x_ref, t_ref, dy_ref, s_ref):
        i = pl.program_id(0)
        err = x_ref[...] - t_ref[...]
        dy_ref[...] = err * (1.0 / D)

        @pl.when(i == 0)
        def _():
            s_ref[...] = jnp.zeros_like(s_ref)

        s_ref[...] += jnp.sum(jnp.sum(err * err, axis=1, keepdims=True), axis=0, keepdims=True)

    row = pl.BlockSpec((tr, D), lambda i: (i, 0))
    return _call(body, name=name, grid=(T // tr,), in_specs=[row, row],
                 out_specs=(row, pl.BlockSpec((8, 128), lambda i: (0, 0))),
                 out_shape=(jax.ShapeDtypeStruct((T, D), F32), jax.ShapeDtypeStruct((8, 128), F32)),
                 compiler_params=_params(("arbitrary",)))(xf, target)


def _adamw(w, g, m, v, *, name):
    R, C = w.shape
    tr = _rows(R, max(8, (1 << 18) // C // 8 * 8))

    def body(w_ref, g_ref, m_ref, v_ref, d_ref, nm_ref, nv_ref):
        g_ = g_ref[...]
        m_ = ADAM_B1 * m_ref[...] + (1.0 - ADAM_B1) * g_
        v_ = ADAM_B2 * v_ref[...] + (1.0 - ADAM_B2) * (g_ * g_)
        m_hat = m_ / (1.0 - ADAM_B1 ** ADAM_STEP)
        v_hat = v_ / (1.0 - ADAM_B2 ** ADAM_STEP)
        d_ref[...] = -ADAM_LR * (m_hat / (jnp.sqrt(v_hat) + ADAM_EPS) + ADAM_WD * w_ref[...])
        nm_ref[...] = m_
        nv_ref[...] = v_

    blk = pl.BlockSpec((tr, C), lambda i: (i, 0))
    sds = jax.ShapeDtypeStruct((R, C), F32)
    return _call(body, name=name, grid=(R // tr,), in_specs=[blk] * 4, out_specs=(blk,) * 3,
                 out_shape=(sds,) * 3, compiler_params=_params(("parallel",)))(w, g, m, v)


def _dot_nt(a, b):
    return lax.dot_general(a, b, (((1,), (1,)), ((), ())), preferred_element_type=F32)


def _dot_tn(a, b):
    return lax.dot_general(a, b, (((0,), (0,)), ((), ())), preferred_element_type=F32)


def _band_masks():
    qi = lax.broadcasted_iota(jnp.int32, (BLK, BLK), 0)
    kj = lax.broadcasted_iota(jnp.int32, (BLK, BLK), 1)
    return qi >= kj, kj >= qi


def _attn_fwd(P, grp, d, a_out, *, name):
    T, C = P.shape
    L = T // d
    nb = L // BLK
    nhh = a_out // HEAD_BLOCK
    cb = C // HEAD_BLOCK
    q0 = grp * 3 * nhh
    scale = HEAD_DIM ** -0.5

    def body(q_ref, kc_ref, kp_ref, vc_ref, vp_ref, o_ref, l_ref):
        b = pl.program_id(1)
        mask_c, mask_p = _band_masks()
        mask_p = jnp.logical_and(mask_p, b > 0)
        for h in range(HEAD_BLOCK // HEAD_DIM):
            hs = slice(h * HEAD_DIM, (h + 1) * HEAD_DIM)
            q = q_ref[:, hs]
            s_c = jnp.where(mask_c, _dot_nt(q, kc_ref[:, hs]) * scale, NEG)
            s_p = jnp.where(mask_p, _dot_nt(q, kp_ref[:, hs]) * scale, NEG)
            m = jnp.maximum(jnp.max(s_c, axis=1, keepdims=True), jnp.max(s_p, axis=1, keepdims=True))
            p_c = jnp.exp(s_c - m)
            p_p = jnp.exp(s_p - m)
            l = jnp.sum(p_c, axis=1, keepdims=True) + jnp.sum(p_p, axis=1, keepdims=True)
            o = (jnp.dot(p_c.astype(BF16), vc_ref[:, hs], preferred_element_type=F32)
                 + jnp.dot(p_p.astype(BF16), vp_ref[:, hs], preferred_element_type=F32))
            o_ref[:, hs] = o / l
            l_ref[:, hs] = jnp.broadcast_to(m + jnp.log(l), (BLK, HEAD_DIM))

    def cur(part):
        return pl.BlockSpec((BLK, HEAD_BLOCK), lambda r, b, hh: (b, r * cb + q0 + part * nhh + hh))

    def prev(part):
        return pl.BlockSpec((BLK, HEAD_BLOCK), lambda r, b, hh: (jnp.maximum(b - 1, 0), r * cb + q0 + part * nhh + hh))

    out = pl.BlockSpec((BLK, HEAD_BLOCK), lambda r, b, hh: (b, r * nhh + hh))
    Pv = P.reshape(L, d * C)
    o, lse = _call(body, name=name, grid=(d, nb, nhh), in_specs=[cur(0), cur(1), prev(1), cur(2), prev(2)],
                   out_specs=(out, out),
                   out_shape=(jax.ShapeDtypeStruct((L, d * a_out), F32),) * 2,
                   compiler_params=_params(("parallel", "parallel", "parallel")))(Pv, Pv, Pv, Pv, Pv)
    return o.reshape(T, a_out), lse.reshape(T, a_out)


def _attn_combine(os_, lses, *, name):
    T, W = os_[0].shape
    tr = _rows(T, 256)
    n = len(os_)

    def body(*refs):
        o_refs, l_refs = refs[:n], refs[n:2 * n]
        mix_ref, lse_ref = refs[2 * n], refs[2 * n + 1]
        ls = [r[...] for r in l_refs]
        m = functools.reduce(jnp.maximum, ls)
        es = [jnp.exp(l - m) for l in ls]
        tot = functools.reduce(lambda a, b: a + b, es)
        mix = functools.reduce(lambda a, b: a + b, [(e / tot) * o[...] for e, o in zip(es, o_refs)])
        mix_ref[...] = mix
        lse_ref[...] = m + jnp.log(tot)

    blk = pl.BlockSpec((tr, W), lambda i: (i, 0))
    sds = jax.ShapeDtypeStruct((T, W), F32)
    return _call(body, name=name, grid=(T // tr,), in_specs=[blk] * (2 * n), out_specs=(blk, blk),
                 out_shape=(sds, sds), compiler_params=_params(("parallel",)))(*os_, *lses)


def _attn_bwd(P, dO, O, LSE, grp, d, a_out, *, name):
    T, C = P.shape
    L = T // d
    nb = L // BLK
    nhh = a_out // HEAD_BLOCK
    cb = C // HEAD_BLOCK
    q0 = grp * 3 * nhh
    scale = HEAD_DIM ** -0.5

    def body(q_ref, qn_ref, kc_ref, kp_ref, vc_ref, vp_ref, do_ref, don_ref, o_ref, on_ref, l_ref, ln_ref,
             dq_ref, dk_ref, dv_ref):
        b = pl.program_id(1)
        mask_c, mask_prev = _band_masks()
        mask_p = jnp.logical_and(mask_prev, b > 0)
        mask_n = jnp.logical_and(mask_prev, b < nb - 1)
        for h in range(HEAD_BLOCK // HEAD_DIM):
            hs = slice(h * HEAD_DIM, (h + 1) * HEAD_DIM)
            q, qn, kc, kp, vc, vp = (r[:, hs] for r in (q_ref, qn_ref, kc_ref, kp_ref, vc_ref, vp_ref))
            do, don = do_ref[:, hs], don_ref[:, hs]
            lse, lse_n = l_ref[:, hs], ln_ref[:, hs]
            delta = jnp.sum(do * o_ref[:, hs], axis=1, keepdims=True)
            delta_n = jnp.sum(don * on_ref[:, hs], axis=1, keepdims=True)
            dob, donb = do.astype(BF16), don.astype(BF16)
            p_c = jnp.exp(jnp.where(mask_c, _dot_nt(q, kc) * scale, NEG) - lse)
            p_p = jnp.exp(jnp.where(mask_p, _dot_nt(q, kp) * scale, NEG) - lse)
            p_n = jnp.exp(jnp.where(mask_n, _dot_nt(qn, kc) * scale, NEG) - lse_n)
            ds_c = (p_c * (_dot_nt(dob, vc) - delta) * scale).astype(BF16)
            ds_p = (p_p * (_dot_nt(dob, vp) - delta) * scale).astype(BF16)
            ds_n = (p_n * (_dot_nt(donb, vc) - delta_n) * scale).astype(BF16)
            dq = jnp.dot(ds_c, kc, preferred_element_type=F32) + jnp.dot(ds_p, kp, preferred_element_type=F32)
            dk = _dot_tn(ds_c, q) + _dot_tn(ds_n, qn)
            dv = _dot_tn(p_c.astype(BF16), dob) + _dot_tn(p_n.astype(BF16), donb)
            dq_ref[:, hs] = dq.astype(BF16)
            dk_ref[:, hs] = dk.astype(BF16)
            dv_ref[:, hs] = dv.astype(BF16)

    def pspec(part, shift):
        def imap(r, b, hh):
            return (jnp.clip(b + shift, 0, nb - 1), r * cb + q0 + part * nhh + hh)
        return pl.BlockSpec((BLK, HEAD_BLOCK), imap)

    def aspec(shift):
        return pl.BlockSpec((BLK, HEAD_BLOCK), lambda r, b, hh: (jnp.clip(b + shift, 0, nb - 1), r * nhh + hh))

    Pv = P.reshape(L, d * C)
    dOv, Ov, Lv = (t.reshape(L, d * a_out) for t in (dO, O, LSE))
    sds = jax.ShapeDtypeStruct((L, d * a_out), BF16)
    dq, dk, dv = _call(
        body, name=name, grid=(d, nb, nhh),
        in_specs=[pspec(0, 0), pspec(0, 1), pspec(1, 0), pspec(1, -1), pspec(2, 0), pspec(2, -1),
                  aspec(0), aspec(1), aspec(0), aspec(1), aspec(0), aspec(1)],
        out_specs=(aspec(0),) * 3, out_shape=(sds,) * 3,
        compiler_params=_params(("parallel", "parallel", "parallel")))(Pv, Pv, Pv, Pv, Pv, Pv, dOv, dOv, Ov, Ov, Lv, Lv)
    return [t.reshape(T, a_out) for t in (dq, dk, dv)]


def _mem_softmax(q, k, scale):
    s = _dot_nt(q, k) * scale
    e = jnp.exp(s - jnp.max(s, axis=1, keepdims=True))
    return e / jnp.sum(e, axis=1, keepdims=True)


def _mem_fwd(P, qcol, kv, *, name):
    T = P.shape[0]
    tq = _rows(T, 512)
    scale = HEAD_DIM ** -0.5

    def body(q_ref, kv_ref, o_ref):
        for h in range(MEM_HEADS):
            hs = slice(h * HEAD_DIM, (h + 1) * HEAD_DIM)
            vs = slice(MEM_WIDTH + h * HEAD_DIM, MEM_WIDTH + (h + 1) * HEAD_DIM)
            p = _mem_softmax(q_ref[:, hs].astype(BF16), kv_ref[:, hs], scale)
            o_ref[:, hs] = jnp.dot(p.astype(BF16), kv_ref[:, vs], preferred_element_type=F32)

    return _call(body, name=name, grid=(T // tq,),
                 in_specs=[pl.BlockSpec((tq, MEM_WIDTH), lambda i: (i, qcol)),
                           pl.BlockSpec((MEM_TOKENS, 2 * MEM_WIDTH), lambda i: (0, 0))],
                 out_specs=pl.BlockSpec((tq, MEM_WIDTH), lambda i: (i, 0)),
                 out_shape=jax.ShapeDtypeStruct((T, MEM_WIDTH), F32),
                 compiler_params=_params(("parallel",)))(P, kv)


def _mem_bwd(P, qcol, kv, dcat, ocol, *, name):
    T = P.shape[0]
    tq = _rows(T, 512)
    scale = HEAD_DIM ** -0.5

    def body(q_ref, kv_ref, do_ref, dq_ref, dkv_ref):
        i = pl.program_id(0)

        @pl.when(i == 0)
        def _():
            dkv_ref[...] = jnp.zeros_like(dkv_ref)

        for h in range(MEM_HEADS):
            hs = slice(h * HEAD_DIM, (h + 1) * HEAD_DIM)
            vs = slice(MEM_WIDTH + h * HEAD_DIM, MEM_WIDTH + (h + 1) * HEAD_DIM)
            q = q_ref[:, hs].astype(BF16)
            k, v = kv_ref[:, hs], kv_ref[:, vs]
            do = do_ref[:, hs].astype(BF16)
            p = _mem_softmax(q, k, scale)
            dp = _dot_nt(do, v)
            ds = (p * (dp - jnp.sum(p * dp, axis=1, keepdims=True)) * scale).astype(BF16)
            dq_ref[:, hs] = jnp.dot(ds, k, preferred_element_type=F32).astype(BF16)
            dkv_ref[:, hs] += _dot_tn(ds, q)
            dkv_ref[:, vs] += _dot_tn(p.astype(BF16), do)

    return _call(body, name=name, grid=(T // tq,),
                 in_specs=[pl.BlockSpec((tq, MEM_WIDTH), lambda i: (i, qcol)),
                           pl.BlockSpec((MEM_TOKENS, 2 * MEM_WIDTH), lambda i: (0, 0)),
                           pl.BlockSpec((tq, MEM_WIDTH), lambda i: (i, ocol))],
                 out_specs=(pl.BlockSpec((tq, MEM_WIDTH), lambda i: (i, 0)),
                            pl.BlockSpec((MEM_TOKENS, 2 * MEM_WIDTH), lambda i: (0, 0))),
                 out_shape=(jax.ShapeDtypeStruct((T, MEM_WIDTH), BF16),
                            jax.ShapeDtypeStruct((MEM_TOKENS, 2 * MEM_WIDTH), F32)),
                 compiler_params=_params(("arbitrary",)))(P, kv, dcat)


_SQRT_HALF = 0.7071067811865476
_INV_SQRT_2PI = 0.3989422804014327


def _gelu(x):
    return 0.5 * x * (1.0 + lax.erf(x * _SQRT_HALF))


def _gelu_grad(x):
    return 0.5 * (1.0 + lax.erf(x * _SQRT_HALF)) + x * (_INV_SQRT_2PI * jnp.exp(-0.5 * x * x))


def _tril():
    t = lax.broadcasted_iota(jnp.int32, (BLK, BLK), 0)
    s = lax.broadcasted_iota(jnp.int32, (BLK, BLK), 1)
    return t >= s


def _gmlp_fwd(P, w_s, bs_t, vg, vb, width, *, name):
    T = P.shape[0]
    G = width // HEAD_DIM
    tb = _rows(T, 512)

    def body(pu_ref, pv_ref, ws_ref, bs_ref, vg_ref, vb_ref, o_ref):
        u = _gelu(pu_ref[...])
        v = _gelu(pv_ref[...])
        mu = jnp.mean(v, axis=-1, keepdims=True)
        vc = v - mu
        var = jnp.mean(vc * vc, axis=-1, keepdims=True)
        vn = (vc * lax.rsqrt(var + LN_EPS) * vg_ref[...] + vb_ref[...]).astype(BF16)
        tril = _tril()
        for g in range(G):
            gs = slice(g * HEAD_DIM, (g + 1) * HEAD_DIM)
            ws = jnp.where(tril, ws_ref[g], 0.0).astype(BF16)
            bias = bs_ref[:, g:g + 1]
            for c in range(tb // BLK):
                cs = slice(c * BLK, (c + 1) * BLK)
                sg = jnp.dot(ws, vn[cs, gs], preferred_element_type=F32) + bias
                o_ref[cs, gs] = u[cs, gs] * sg

    blk = lambda col: pl.BlockSpec((tb, width), lambda i: (i, col))
    full = lambda shape: pl.BlockSpec(shape, lambda i: (0,) * len(shape))
    return _call(body, name=name, grid=(T // tb,),
                 in_specs=[blk(0), blk(1), full((G, BLK, BLK)), full((BLK, G)), full((1, width)), full((1, width))],
                 out_specs=blk(0), out_shape=jax.ShapeDtypeStruct((T, width), F32),
                 compiler_params=_params(("parallel",)))(P, P, w_s, bs_t, vg.reshape(1, width), vb.reshape(1, width))


def _gmlp_bwd(P, w_s, bs_t, vg, vb, dcat, width, *, name):
    T = P.shape[0]
    G = width // HEAD_DIM
    tb = _rows(T, 512)

    def body(pu_ref, pv_ref, ws_ref, bs_ref, vg_ref, vb_ref, dm_ref, dpu_ref, dpv_ref, dws_ref, dbs_ref, dvg_ref,
             dvb_ref, dvn_ref):
        i = pl.program_id(0)

        @pl.when(i == 0)
        def _():
            dws_ref[...] = jnp.zeros_like(dws_ref)
            dbs_ref[...] = jnp.zeros_like(dbs_ref)
            dvg_ref[...] = jnp.zeros_like(dvg_ref)
            dvb_ref[...] = jnp.zeros_like(dvb_ref)

        pu, pv = pu_ref[...], pv_ref[...]
        u = _gelu(pu)
        v = _gelu(pv)
        mu = jnp.mean(v, axis=-1, keepdims=True)
        vc = v - mu
        var = jnp.mean(vc * vc, axis=-1, keepdims=True)
        rstd = lax.rsqrt(var + LN_EPS)
        xhat = vc * rstd
        vn = (xhat * vg_ref[...] + vb_ref[...]).astype(BF16)
        dm = dm_ref[...]
        tril = _tril()
        lane = lax.broadcasted_iota(jnp.int32, (BLK, BLK), 1)
        dbs = jnp.zeros((BLK, BLK), F32)
        for g in range(G):
            gs = slice(g * HEAD_DIM, (g + 1) * HEAD_DIM)
            ws = jnp.where(tril, ws_ref[g], 0.0).astype(BF16)
            bias = bs_ref[:, g:g + 1]
            dws = jnp.zeros((BLK, BLK), F32)
            rs = jnp.zeros((BLK, 1), F32)
            for c in range(tb // BLK):
                cs = slice(c * BLK, (c + 1) * BLK)
                vn_cg = vn[cs, gs]
                sg = jnp.dot(ws, vn_cg, preferred_element_type=F32) + bias
                dm_cg = dm[cs, gs]
                dpu_ref[cs, gs] = (dm_cg * sg * _gelu_grad(pu[cs, gs])).astype(BF16)
                dsg = dm_cg * u[cs, gs]
                dsgb = dsg.astype(BF16)
                dvn_ref[cs, gs] = _dot_tn(ws, dsgb)
                dws = dws + _dot_nt(dsgb, vn_cg)
                rs = rs + jnp.sum(dsg, axis=1, keepdims=True)
            dws_ref[g] += jnp.where(tril, dws, 0.0)
            dbs = dbs + jnp.where(lane == g, rs, 0.0)
        dbs_ref[...] += dbs
        dvn = dvn_ref[...]
        dxh = dvn * vg_ref[...]
        dv = rstd * (dxh - jnp.mean(dxh, axis=-1, keepdims=True)
                     - xhat * jnp.mean(dxh * xhat, axis=-1, keepdims=True))
        dpv_ref[...] = (dv * _gelu_grad(pv)).astype(BF16)
        dvg_ref[...] += jnp.sum(dvn * xhat, axis=0, keepdims=True)
        dvb_ref[...] += jnp.sum(dvn, axis=0, keepdims=True)

    blk = lambda col: pl.BlockSpec((tb, width), lambda i: (i, col))
    full = lambda shape: pl.BlockSpec(shape, lambda i: (0,) * len(shape))
    return _call(body, name=name, grid=(T // tb,),
                 in_specs=[blk(0), blk(1), full((G, BLK, BLK)), full((BLK, G)), full((1, width)), full((1, width)),
                           blk(0)],
                 out_specs=(blk(0), blk(0), full((G, BLK, BLK)), full((BLK, BLK)), full((1, width)), full((1, width))),
                 out_shape=(jax.ShapeDtypeStruct((T, width), BF16), jax.ShapeDtypeStruct((T, width), BF16),
                            jax.ShapeDtypeStruct((G, BLK, BLK), F32), jax.ShapeDtypeStruct((BLK, BLK), F32),
                            jax.ShapeDtypeStruct((1, width), F32), jax.ShapeDtypeStruct((1, width), F32)),
                 scratch_shapes=[pltpu.VMEM((tb, width), F32)],
                 compiler_params=_params(("arbitrary",)))(P, P, w_s, bs_t, vg.reshape(1, width), vb.reshape(1, width), dcat)


def _place():
    x, y, c = lax.axis_index("x"), lax.axis_index("y"), lax.axis_index("c")
    chips = [(1 - x, y), (x, 1 - y), (1 - x, 1 - y)]
    return x, y, c, 2 * x + y, chips, [2 * px + py for px, py in chips]


def _half(ref, c, rows):
    return ref.at[pl.ds(c * (rows // 2), rows // 2)]


def _all_gather(shards, *, name):
    n = len(shards)
    split = [s.shape[0] % 16 == 0 for s in shards]
    ns = sum(split)

    def body(*refs):
        src, dst = refs[:n], refs[n:2 * n]
        send1, recv1, send2, recv2, lsem = refs[2 * n:]
        x, y, c, j, chips, pj = _place()
        sib = (x, y, 1 - c)
        local = [pltpu.make_async_copy(src[t], dst[t].at[j], lsem.at[t]) for t in range(n)]
        for cp in local:
            cp.start()

        def window(t, chip_idx, core):
            w = dst[t].at[chip_idx]
            return _half(w, core, shards[t].shape[0]) if split[t] else w

        def ici(t, k, to):
            s = _half(src[t], c, shards[t].shape[0]) if split[t] else src[t]
            return pltpu.make_async_remote_copy(src_ref=s, dst_ref=window(t, j, c), send_sem=send1.at[3 * t + k],
                                                recv_sem=recv1.at[3 * t + k], device_id=to, device_id_type=MESH)

        firsts = [ici(t, k, (*chips[k], c)) for t in range(n) for k in range(3)]
        for cp in firsts:
            cp.start()

        def landed(t, k):
            return pltpu.make_async_remote_copy(src_ref=window(t, pj[k], c), dst_ref=window(t, pj[k], c),
                                                send_sem=send1.at[3 * t + k], recv_sem=recv1.at[3 * t + k],
                                                device_id=sib, device_id_type=MESH)

        def d2d(u, t, k, core):
            return pltpu.make_async_remote_copy(src_ref=window(t, pj[k], core), dst_ref=window(t, pj[k], core),
                                                send_sem=send2.at[3 * u + k], recv_sem=recv2.at[3 * u + k],
                                                device_id=sib, device_id_type=MESH)

        seconds = []
        for k in range(3):
            u = 0
            for t in range(n):
                landed(t, k).wait_recv()
                if split[t]:
                    cp = d2d(u, t, k, c)
                    cp.start()
                    seconds.append(cp)
                    u += 1
        for k in range(3):
            u = 0
            for t in range(n):
                if split[t]:
                    d2d(u, t, k, 1 - c).wait_recv()
                    u += 1
        for cp in firsts + seconds:
            cp.wait_send()
        for cp in local:
            cp.wait()

    out_shape = [jax.ShapeDtypeStruct((4,) + s.shape, s.dtype) for s in shards]
    return _call(body, name=name, in_specs=[HBM] * n, out_specs=[HBM] * n, out_shape=out_shape,
                 scratch_shapes=[pltpu.SemaphoreType.DMA((3 * n,)), pltpu.SemaphoreType.DMA((3 * n,)),
                                 pltpu.SemaphoreType.DMA((3 * max(ns, 1),)), pltpu.SemaphoreType.DMA((3 * max(ns, 1),)),
                                 pltpu.SemaphoreType.DMA((n,))])(*shards)


def _pair_exchange(grads, *, name):
    n = len(grads)

    def body(*refs):
        g = refs[:n]
        own, theirs = refs[n:2 * n], refs[2 * n:3 * n]
        send, recv, lsem = refs[3 * n:]
        x, y, c, _, _, _ = _place()
        sib = (x, y, 1 - c)
        copies = []
        for t in range(n):
            h = grads[t].shape[1] // 2
            lc = pltpu.make_async_copy(g[t].at[:, pl.ds(c * h, h)], own[t], lsem.at[t])
            rc = pltpu.make_async_remote_copy(src_ref=g[t].at[:, pl.ds((1 - c) * h, h)], dst_ref=theirs[t],
                                              send_sem=send.at[t], recv_sem=recv.at[t], device_id=sib,
                                              device_id_type=MESH)
            lc.start()
            rc.start()
            copies.append((lc, rc))
        for lc, rc in copies:
            rc.wait()
            lc.wait()

    halves = [jax.ShapeDtypeStruct((4, g.shape[1] // 2, g.shape[2]), g.dtype) for g in grads]
    outs = _call(body, name=name, in_specs=[HBM] * n, out_specs=[HBM] * (2 * n), out_shape=halves + halves,
                 scratch_shapes=[pltpu.SemaphoreType.DMA((n,))] * 3)(*grads)
    return outs[:n], outs[n:]


def _chip_exchange(pairs, *, name):
    n = len(pairs)

    def body(*refs):
        s, r = refs[:n], refs[n:2 * n]
        send, recv, lsem = refs[2 * n:]
        x, y, c, j, chips, pj = _place()
        local = [pltpu.make_async_copy(s[t].at[j], r[t].at[0], lsem.at[t]) for t in range(n)]
        for cp in local:
            cp.start()
        remote = [pltpu.make_async_remote_copy(src_ref=s[t].at[pj[k]], dst_ref=r[t].at[1 + k],
                                               send_sem=send.at[3 * t + k], recv_sem=recv.at[3 * t + k],
                                               device_id=(*chips[k], c), device_id_type=MESH)
                  for t in range(n) for k in range(3)]
        for cp in remote:
            cp.start()
        for cp in remote:
            cp.wait()
        for cp in local:
            cp.wait()

    return _call(body, name=name, in_specs=[HBM] * n, out_specs=[HBM] * n,
                 out_shape=[jax.ShapeDtypeStruct(p.shape, p.dtype) for p in pairs],
                 scratch_shapes=[pltpu.SemaphoreType.DMA((3 * n,)), pltpu.SemaphoreType.DMA((3 * n,)),
                                 pltpu.SemaphoreType.DMA((n,))])(*pairs)


def _half_swap(halves, *, name):
    n = len(halves)

    def body(*refs):
        s, full = refs[:n], refs[n:2 * n]
        send, recv, lsem = refs[2 * n:]
        x, y, c, _, _, _ = _place()
        sib = (x, y, 1 - c)
        copies = []
        for t in range(n):
            h = halves[t].shape[0]
            mine = full[t].at[pl.ds(c * h, h)]
            lc = pltpu.make_async_copy(s[t], mine, lsem.at[t])
            rc = pltpu.make_async_remote_copy(src_ref=s[t], dst_ref=mine, send_sem=send.at[t], recv_sem=recv.at[t],
                                              device_id=sib, device_id_type=MESH)
            lc.start()
            rc.start()
            copies.append((lc, rc, full[t].at[pl.ds((1 - c) * h, h)]))
        for t, (lc, rc, other) in enumerate(copies):
            rc.wait_send()
            pltpu.make_async_remote_copy(src_ref=s[t], dst_ref=other, send_sem=send.at[t], recv_sem=recv.at[t],
                                         device_id=sib, device_id_type=MESH).wait_recv()
            lc.wait()

    return _call(body, name=name, in_specs=[HBM] * n, out_specs=[HBM] * n,
                 out_shape=[jax.ShapeDtypeStruct((2 * h.shape[0], h.shape[1]), h.dtype) for h in halves],
                 scratch_shapes=[pltpu.SemaphoreType.DMA((n,))] * 3)(*halves)


def _all_reduce_small(packed, *, name):
    R, C = packed.shape

    def body(p_ref, o_ref, slots, send, recv, lsem):
        x, y, c = lax.axis_index("x"), lax.axis_index("y"), lax.axis_index("c")
        me = 4 * x + 2 * y + c
        lc = pltpu.make_async_copy(p_ref, slots.at[me], lsem.at[0])
        lc.start()
        copies = []
        for rel in range(1, 8):
            fx, fy, fc = (rel >> 2) & 1, (rel >> 1) & 1, rel & 1
            to = (1 - x if fx else x, 1 - y if fy else y, 1 - c if fc else c)
            cp = pltpu.make_async_remote_copy(src_ref=p_ref, dst_ref=slots.at[me], send_sem=send.at[rel - 1],
                                              recv_sem=recv.at[rel - 1], device_id=to, device_id_type=MESH)
            cp.start()
            copies.append((cp, 4 * to[0] + 2 * to[1] + to[2]))
        for rel, (cp, frm) in enumerate(copies):
            cp.wait_send()
            pltpu.make_async_remote_copy(src_ref=p_ref, dst_ref=slots.at[frm], send_sem=send.at[rel],
                                         recv_sem=recv.at[rel], device_id=(x, y, c), device_id_type=MESH).wait_recv()
        lc.wait()
        acc = slots[0]
        for dev in range(1, 8):
            acc = acc + slots[dev]
        o_ref[...] = acc

    return _call(body, name=name, in_specs=[pl.BlockSpec(memory_space=pltpu.VMEM)],
                 out_specs=pl.BlockSpec(memory_space=pltpu.VMEM), out_shape=jax.ShapeDtypeStruct((R, C), F32),
                 scratch_shapes=[pltpu.VMEM((8, R, C), F32), pltpu.SemaphoreType.DMA((7,)),
                                 pltpu.SemaphoreType.DMA((7,)), pltpu.SemaphoreType.DMA((1,))],
                 compiler_params=pltpu.CompilerParams(vmem_limit_bytes=V7X_VMEM_LIMIT))(packed)


def _pair_sum(own, theirs, *, name):
    J, H, C = own.shape
    tr = _rows(H, max(8, (1 << 19) // C // 8 * 8))

    def body(a_ref, b_ref, o_ref):
        o_ref[...] = (a_ref[...].astype(F32) + b_ref[...].astype(F32)).astype(BF16)

    blk = pl.BlockSpec((None, tr, C), lambda j, i: (j, i, 0))
    return _call(body, name=name, grid=(J, H // tr), in_specs=[blk, blk], out_specs=blk,
                 out_shape=jax.ShapeDtypeStruct((J, H, C), BF16),
                 compiler_params=_params(("parallel", "parallel")))(own, theirs)


def _chip_sum(slots, *, name):
    J, H, C = slots.shape
    tr = _rows(H, max(8, (1 << 19) // C // 8 * 8))

    def body(s0, s1, s2, s3, o_ref):
        o_ref[...] = ((s0[...].astype(F32) + s1[...].astype(F32)) + s2[...].astype(F32)) + s3[...].astype(F32)

    def slot(k):
        return pl.BlockSpec((None, tr, C), lambda i: (k, i, 0))

    return _call(body, name=name, grid=(H // tr,), in_specs=[slot(0), slot(1), slot(2), slot(3)],
                 out_specs=pl.BlockSpec((tr, C), lambda i: (i, 0)), out_shape=jax.ShapeDtypeStruct((H, C), F32),
                 compiler_params=_params(("parallel",)))(slots, slots, slots, slots)


def _reduce_scatter(grads, tag):
    own, theirs = _pair_exchange(grads, name=f"rs_pair_{tag}")
    pairs = [_pair_sum(o, t, name=f"rs_pair_sum_{tag}_{i}") for i, (o, t) in enumerate(zip(own, theirs))]
    slots = _chip_exchange(pairs, name=f"rs_chip_{tag}")
    halves = [_chip_sum(s, name=f"rs_chip_sum_{tag}_{i}") for i, s in enumerate(slots)]
    return _half_swap(halves, name=f"rs_swap_{tag}")


def kernel(x, mem, w_in_a, w_in_b, w_s, b_s, vnorm_g, vnorm_b, w_mem_kv, w_out, ln1_g, ln1_b, w_ff1, w_ff2, ln2_g, ln2_b, loss_target, m_w_in_a, m_w_in_b, m_w_s, m_b_s, m_vnorm_g, m_vnorm_b, m_w_mem_kv, m_w_out, m_ln1_g, m_ln1_b, m_w_ff1, m_w_ff2, m_ln2_g, m_ln2_b, v_w_in_a, v_w_in_b, v_w_s, v_b_s, v_vnorm_g, v_vnorm_b, v_w_mem_kv, v_w_out, v_ln1_g, v_ln1_b, v_w_ff1, v_w_ff2, v_ln2_g, v_ln2_b):
    T, D = x.shape[1], x.shape[2]
    depth = w_ff1.shape[0]
    alpha = (2.0 * depth) ** 0.25
    a_out = (D // 256) * HEAD_DIM
    a_cols = len(A_PAIRS) * 3 * a_out
    b_width = (D // 256) * HEAD_DIM
    G = b_width // HEAD_DIM
    assert a_out % HEAD_BLOCK == 0 and T % (BLK * A_PAIRS[-1][1]) == 0

    xf = x.reshape(T, D)
    mem_b = mem.reshape(MEM_TOKENS, D).astype(BF16)
    target = loss_target.reshape(T, D)
    c_idx = lax.axis_index("x") * 2 + lax.axis_index("y")

    saved = []
    for i in range(depth):
        jl = i // 2
        is_a = i % 2 == 0
        w_in_l = (w_in_a if is_a else w_in_b)[jl]
        shards = [w_in_l.astype(BF16), w_mem_kv[i].astype(BF16), w_out[i].astype(BF16), w_ff1[i].astype(BF16),
                  w_ff2[i].astype(BF16)]
        if not is_a:
            shards += [vnorm_g[jl].reshape(1, -1), vnorm_b[jl].reshape(1, -1)]
        gathered = _all_gather(shards, name=f"all_gather_{'a' if is_a else 'b'}")
        win, wkv, wout, wff1, wff2 = gathered[:5]
        wkv = wkv.reshape(1, D, 2 * MEM_WIDTH)
        wff2 = wff2.reshape(1, 4 * D, D)
        xb = xf.astype(BF16)
        P = _mm_nn(xb, win, out_dtype=BF16 if is_a else F32, name=f"proj_in_{'a' if is_a else 'b'}")
        kv = _mm_nn(mem_b, wkv, name="proj_kv")
        if is_a:
            outs = [_attn_fwd(P, g, d, a_out, name=f"attn_fwd_d{d}") for g, (_, d) in enumerate(A_PAIRS)]
            mix, lse = _attn_combine([o for o, _ in outs], [l for _, l in outs], name="attn_combine")
            qcol = a_cols // MEM_WIDTH
            extra = (lse,)
        else:
            vg_full = gathered[5].reshape(-1)
            vb_full = gathered[6].reshape(-1)
            bs_t = b_s[jl].T
            mix = _gmlp_fwd(P, w_s[jl], bs_t, vg_full, vb_full, b_width, name="gmlp_fwd")
            qcol = 2 * b_width // MEM_WIDTH
            extra = (vg_full, vb_full, bs_t)
        mem_o = _mem_fwd(P, qcol, kv, name=f"mem_fwd_{'a' if is_a else 'b'}")
        cat = jnp.concatenate([mix.astype(BF16), mem_o.astype(BF16)], axis=1)
        y = _mm_nn(cat, wout, out_dtype=F32, name="proj_out")
        x1, x1b = _ln_fwd(xf, y, ln1_g[i], ln1_b[i], alpha, name="ln_fwd")
        a_pre, hid = _mm_nn(x1b, wff1, epi="relu2", name="ff1")
        f = _mm_nn(hid, wff2, out_dtype=F32, name="ff2")
        x2, _ = _ln_fwd(x1, f, ln2_g[i], ln2_b[i], alpha, name="ln_fwd")
        saved.append(dict(xf=xf, xb=xb, P=P, kv=kv, mix=mix, cat=cat, y=y, x1=x1, x1b=x1b, a_pre=a_pre, hid=hid, f=f,
                          extra=extra, w=(win, wkv, wout, wff1, wff2), qcol=qcol))
        xf = x2

    dx, sq = _loss_head(xf, target, name="loss_head")
    loss = lax.psum(sq[0, 0] * (0.5 / D), ("x", "y", "c"))

    g_big = dict(w_in_a=[None] * ((depth + 1) // 2), w_in_b=[None] * (depth // 2), w_mem_kv=[None] * depth,
                 w_out=[None] * depth, w_ff1=[None] * depth, w_ff2=[None] * depth)
    small = {k: [None] * depth for k in ("ln1_g", "ln1_b", "ln2_g", "ln2_b")}
    small_b = {k: [None] * (depth // 2) for k in ("w_s", "b_s", "vnorm_g", "vnorm_b")}
    for i in reversed(range(depth)):
        jl = i // 2
        is_a = i % 2 == 0
        s = saved[i]
        win, wkv, wout, wff1, wff2 = s["w"]
        tag = "a" if is_a else "b"
        d_f, adr2, dg2, db2 = _ln_bwd(dx, s["x1"], s["f"], ln2_g[i], alpha, name="ln_bwd")
        small["ln2_g"][i], small["ln2_b"][i] = dg2, db2
        gw_ff2 = _mm_tn(s["hid"], d_f, 1, name="grad_ff2").reshape(4, D, D)
        da = _mm_nt(d_f, wff2, epi="drelu2", extra=s["a_pre"], name="ff2_bwd")
        gw_ff1 = _mm_tn(s["x1b"], da, 4, name="grad_ff1")
        dx1 = _mm_nt(da, wff1, epi="resid", extra=adr2, name="ff1_bwd")
        d_y, adr1, dg1, db1 = _ln_bwd(dx1, s["xf"], s["y"], ln1_g[i], alpha, name="ln_bwd")
        small["ln1_g"][i], small["ln1_b"][i] = dg1, db1
        gw_out = _mm_tn(s["cat"], d_y, 4, name="grad_out")
        dcat = _mm_nt(d_y, wout, out_dtype=F32, name="proj_out_bwd")
        ocol = dcat.shape[1] // MEM_WIDTH - 1
        dmq, dkv = _mem_bwd(s["P"], s["qcol"], s["kv"], dcat, ocol, name=f"mem_bwd_{tag}")
        gw_kv = _mm_tn(mem_b, dkv.astype(BF16), 1, name="grad_kv").reshape(4, D // 4, 2 * MEM_WIDTH)
        if is_a:
            (lse,) = s["extra"]
            dmix = dcat[:, :a_out]
            parts = []
            for g, (_, d) in enumerate(A_PAIRS):
                parts += _attn_bwd(s["P"], dmix, s["mix"], lse, g, d, a_out, name=f"attn_bwd_d{d}")
            dP = jnp.concatenate(parts + [dmq], axis=1)
        else:
            vg_full, vb_full, bs_t = s["extra"]
            dpu, dpv, dws, dbs, dvg, dvb = _gmlp_bwd(s["P"], w_s[jl], bs_t, vg_full, vb_full, dcat, b_width,
                                                     name="gmlp_bwd")
            small_b["w_s"][jl], small_b["b_s"][jl] = dws, dbs[:, :G].T
            small_b["vnorm_g"][jl], small_b["vnorm_b"][jl] = dvg, dvb
            dP = jnp.concatenate([dpu, dpv, dmq], axis=1)
        gw_in = _mm_tn(s["xb"], dP, 4, name=f"grad_in_{tag}")
        dx = _mm_nt(dP, win, epi="resid", extra=adr1, name=f"proj_in_bwd_{tag}")
        red = _reduce_scatter([gw_in, gw_kv, gw_out, gw_ff1, gw_ff2], tag)
        g_big["w_in_a" if is_a else "w_in_b"][jl] = red[0]
        for k, r in zip(("w_mem_kv", "w_out", "w_ff1", "w_ff2"), red[1:]):
            g_big[k][i] = r
    grad_x = dx.reshape(x.shape)

    nb_layers = depth // 2
    pieces = ([jnp.stack(small_b["w_s"]).reshape(-1, 128), jnp.stack(small_b["b_s"]).reshape(-1, 128),
               jnp.stack(small_b["vnorm_g"]).reshape(-1, 128), jnp.stack(small_b["vnorm_b"]).reshape(-1, 128)]
              + [jnp.stack(small[k]).reshape(-1, 128) for k in ("ln1_g", "ln1_b", "ln2_g", "ln2_b")])
    sizes = [p.shape[0] for p in pieces]
    pad = (-sum(sizes)) % 8
    packed = jnp.concatenate(pieces + ([jnp.zeros((pad, 128), F32)] if pad else []), axis=0)
    summed = _all_reduce_small(packed, name="all_reduce_small")
    offs = [0]
    for n_ in sizes:
        offs.append(offs[-1] + n_)
    sp = [summed[offs[k]:offs[k + 1]] for k in range(len(sizes))]
    vshard = vnorm_g.shape[1]
    g_small = dict(
        w_s=sp[0].reshape(w_s.shape), b_s=sp[1].reshape(b_s.shape),
        vnorm_g=lax.dynamic_slice_in_dim(sp[2].reshape(nb_layers, -1), c_idx * vshard, vshard, axis=1),
        vnorm_b=lax.dynamic_slice_in_dim(sp[3].reshape(nb_layers, -1), c_idx * vshard, vshard, axis=1),
        ln1_g=sp[4].reshape(ln1_g.shape), ln1_b=sp[5].reshape(ln1_b.shape),
        ln2_g=sp[6].reshape(ln2_g.shape), ln2_b=sp[7].reshape(ln2_b.shape))

    names = ["w_in_a", "w_in_b", "w_s", "b_s", "vnorm_g", "vnorm_b", "w_mem_kv", "w_out", "ln1_g", "ln1_b", "w_ff1",
             "w_ff2", "ln2_g", "ln2_b"]
    ws = dict(w_in_a=w_in_a, w_in_b=w_in_b, w_s=w_s, b_s=b_s, vnorm_g=vnorm_g, vnorm_b=vnorm_b, w_mem_kv=w_mem_kv,
              w_out=w_out, ln1_g=ln1_g, ln1_b=ln1_b, w_ff1=w_ff1, w_ff2=w_ff2, ln2_g=ln2_g, ln2_b=ln2_b)
    ms = dict(w_in_a=m_w_in_a, w_in_b=m_w_in_b, w_s=m_w_s, b_s=m_b_s, vnorm_g=m_vnorm_g, vnorm_b=m_vnorm_b,
              w_mem_kv=m_w_mem_kv, w_out=m_w_out, ln1_g=m_ln1_g, ln1_b=m_ln1_b, w_ff1=m_w_ff1, w_ff2=m_w_ff2,
              ln2_g=m_ln2_g, ln2_b=m_ln2_b)
    vs = dict(w_in_a=v_w_in_a, w_in_b=v_w_in_b, w_s=v_w_s, b_s=v_b_s, vnorm_g=v_vnorm_g, vnorm_b=v_vnorm_b,
              w_mem_kv=v_w_mem_kv, w_out=v_w_out, ln1_g=v_ln1_g, ln1_b=v_ln1_b, w_ff1=v_w_ff1, w_ff2=v_w_ff2,
              ln2_g=v_ln2_g, ln2_b=v_ln2_b)
    grads, deltas, new_m, new_v = {}, {}, {}, {}
    for k in g_big:
        g = jnp.stack(g_big[k]).reshape(ws[k].shape)
        cols = ws[k].shape[-1]
        d_, m_, v_ = _adamw(ws[k].reshape(-1, cols), g.reshape(-1, cols), ms[k].reshape(-1, cols),
                            vs[k].reshape(-1, cols), name=f"adamw_{k}")
        grads[k] = g
        deltas[k], new_m[k], new_v[k] = (t.reshape(ws[k].shape) for t in (d_, m_, v_))
    small_names = [k for k in names if k not in g_big]

    def pack(tree):
        flat = jnp.concatenate([tree[k].reshape(-1) for k in small_names])
        padn = (-flat.shape[0]) % 1024
        return jnp.pad(flat, (0, padn)).reshape(-1, 128)

    d_, m_, v_ = _adamw(pack(ws), pack(g_small), pack(ms), pack(vs), name="adamw_small")
    off = 0
    for k in small_names:
        n_ = ws[k].size
        grads[k] = g_small[k]
        deltas[k], new_m[k], new_v[k] = (t.reshape(-1)[off:off + n_].reshape(ws[k].shape) for t in (d_, m_, v_))
        off += n_

    return (loss, grad_x, *[grads[k] for k in names], *[deltas[k] for k in names], *[new_m[k] for k in names],
            *[new_v[k] for k in names])
```

```python
import functools
import math

import jax
import jax.numpy as jnp
from jax import lax
from jax.experimental import pallas as pl
from jax.experimental.pallas import tpu as pltpu

F32 = jnp.float32
BF16 = jnp.bfloat16

HEAD_DIM = 128
A_PAIRS = ((128, 1), (512, 4), (2048, 16))
BLK = 128
HEAD_BLOCK = 512
MEM_TOKENS = 256
MEM_HEADS = 4
MEM_WIDTH = MEM_HEADS * HEAD_DIM
LN_EPS = 1e-5
ADAM_LR, ADAM_B1, ADAM_B2, ADAM_EPS, ADAM_WD, ADAM_STEP = 0.001, 0.9, 0.999, 1e-08, 0.01, 10
NEG = -1e30
V7X_VMEM_LIMIT = 48 * 1024 * 1024
MESH = pl.DeviceIdType.MESH
HBM = pl.BlockSpec(memory_space=pltpu.HBM)


def _call(body, **kw):
    return pl.pallas_call(body, **kw)


def _params(sem):
    return pltpu.CompilerParams(dimension_semantics=sem, vmem_limit_bytes=V7X_VMEM_LIMIT)


def _tile(n, cap):
    best = 0
    for t in range(128, min(n, cap) + 1, 128):
        if n % t == 0:
            best = t
    if best == 0 or (best < 512 and n <= 2560):
        return n
    return best


def _rows(n, cap):
    best = 8
    for t in range(8, min(n, cap) + 1, 8):
        if n % t == 0:
            best = t
    return best


def _epilogue(epi, acc, extra_ref, out_refs):
    if epi == "plain":
        out_refs[0][...] = acc.astype(out_refs[0].dtype)
    elif epi == "relu2":
        out_refs[0][...] = acc
        r = jnp.maximum(acc, 0.0)
        out_refs[1][...] = (r * r).astype(out_refs[1].dtype)
    elif epi == "drelu2":
        out_refs[0][...] = (acc * (2.0 * jnp.maximum(extra_ref[...], 0.0))).astype(out_refs[0].dtype)
    elif epi == "resid":
        out_refs[0][...] = acc + extra_ref[...]
    else:
        raise ValueError(epi)


def _mm_nn(a, w, *, epi="plain", out_dtype=BF16, extra=None, name):
    M, K = a.shape
    J, K2, Nj = w.shape
    assert K == K2
    tn = _tile(Nj, 1024)
    tm = _tile(M, 1024 if tn <= 1024 else 512)
    tk = _tile(K, 512)
    nn, nk = Nj // tn, K // tk
    n_out = 2 if epi == "relu2" else 1
    has_extra = extra is not None

    def body(*refs):
        a_ref, w_ref = refs[0], refs[1]
        extra_ref = refs[2] if has_extra else None
        outs = refs[2 + has_extra: 2 + has_extra + n_out]
        acc_ref = refs[-1]
        k = pl.program_id(3)

        @pl.when(k == 0)
        def _():
            acc_ref[...] = jnp.zeros_like(acc_ref)

        acc_ref[...] += jnp.dot(a_ref[...], w_ref[...], preferred_element_type=F32)

        @pl.when(k == nk - 1)
        def _():
            _epilogue(epi, acc_ref[...], extra_ref, outs)

    omap = lambda i, j, n, k: (i, j * nn + n)
    in_specs = [pl.BlockSpec((tm, tk), lambda i, j, n, k: (i, k)),
                pl.BlockSpec((None, tk, tn), lambda i, j, n, k: (j, k, n))]
    args = [a, w]
    if has_extra:
        in_specs.append(pl.BlockSpec((tm, tn), omap))
        args.append(extra)
    if epi == "relu2":
        out_shape = (jax.ShapeDtypeStruct((M, J * Nj), F32), jax.ShapeDtypeStruct((M, J * Nj), BF16))
        out_specs = (pl.BlockSpec((tm, tn), omap), pl.BlockSpec((tm, tn), omap))
    else:
        out_shape = jax.ShapeDtypeStruct((M, J * Nj), F32 if epi == "resid" else out_dtype)
        out_specs = pl.BlockSpec((tm, tn), omap)
    return _call(body, name=name, grid=(M // tm, J, nn, nk), in_specs=in_specs, out_specs=out_specs,
                 out_shape=out_shape, scratch_shapes=[pltpu.VMEM((tm, tn), F32)],
                 compiler_params=_params(("parallel", "parallel", "parallel", "arbitrary")))(*args)


def _mm_nt(a, w, *, epi="plain", out_dtype=BF16, extra=None, name):
    M, N = a.shape
    J, K, Nj = w.shape
    assert N == J * Nj
    tc = _tile(Nj, 1024)
    tko = _tile(K, 1024)
    tm = _tile(M, 1024 if tc <= 1024 else 512)
    nc = Nj // tc
    has_extra = extra is not None

    def body(*refs):
        a_ref, w_ref = refs[0], refs[1]
        extra_ref = refs[2] if has_extra else None
        outs = refs[2 + has_extra: 3 + has_extra]
        acc_ref = refs[-1]
        j, c = pl.program_id(2), pl.program_id(3)

        @pl.when(jnp.logical_and(j == 0, c == 0))
        def _():
            acc_ref[...] = jnp.zeros_like(acc_ref)

        acc_ref[...] += lax.dot_general(a_ref[...], w_ref[...], (((1,), (1,)), ((), ())),
                                        preferred_element_type=F32)

        @pl.when(jnp.logical_and(j == J - 1, c == nc - 1))
        def _():
            _epilogue(epi, acc_ref[...], extra_ref, outs)

    omap = lambda i, ko, j, c: (i, ko)
    in_specs = [pl.BlockSpec((tm, tc), lambda i, ko, j, c: (i, j * nc + c)),
                pl.BlockSpec((None, tko, tc), lambda i, ko, j, c: (j, ko, c))]
    args = [a, w]
    if has_extra:
        in_specs.append(pl.BlockSpec((tm, tko), omap))
        args.append(extra)
    out_shape = jax.ShapeDtypeStruct((M, K), F32 if epi == "resid" else out_dtype)
    return _call(body, name=name, grid=(M // tm, K // tko, J, nc), in_specs=in_specs,
                 out_specs=pl.BlockSpec((tm, tko), omap), out_shape=out_shape,
                 scratch_shapes=[pltpu.VMEM((tm, tko), F32)],
                 compiler_params=_params(("parallel", "parallel", "arbitrary", "arbitrary")))(*args)


def _mm_tn(a, b, J, *, name):
    T, K = a.shape
    T2, N = b.shape
    assert T == T2 and N % J == 0
    Nj = N // J
    tn = _tile(Nj, 1024)
    tkr = _tile(K, 1024 if tn <= 1024 else 512)
    tt = _tile(T, 512)
    nn, nt = Nj // tn, T // tt

    def body(a_ref, b_ref, o_ref, acc_ref):
        t = pl.program_id(3)

        @pl.when(t == 0)
        def _():
            acc_ref[...] = jnp.zeros_like(acc_ref)

        acc_ref[...] += lax.dot_general(a_ref[...], b_ref[...], (((0,), (0,)), ((), ())),
                                        preferred_element_type=F32)

        @pl.when(t == nt - 1)
        def _():
            o_ref[...] = acc_ref[...].astype(o_ref.dtype)

    return _call(body, name=name, grid=(J, K // tkr, nn, nt),
                 in_specs=[pl.BlockSpec((tt, tkr), lambda j, kr, n, t: (t, kr)),
                           pl.BlockSpec((tt, tn), lambda j, kr, n, t: (t, j * nn + n))],
                 out_specs=pl.BlockSpec((None, tkr, tn), lambda j, kr, n, t: (j, kr, n)),
                 out_shape=jax.ShapeDtypeStruct((J, K, Nj), BF16),
                 scratch_shapes=[pltpu.VMEM((tkr, tn), F32)],
                 compiler_params=_params(("parallel", "parallel", "parallel", "arbitrary")))(a, b)


def _ln_fwd(x, y, g, b, alpha, *, name):
    T, D = x.shape
    tr = _rows(T, 256)

    def body(x_ref, y_ref, g_ref, b_ref, o_ref, ob_ref):
        r = alpha * x_ref[...] + y_ref[...]
        mu = jnp.mean(r, axis=-1, keepdims=True)
        xc = r - mu
        var = jnp.mean(xc * xc, axis=-1, keepdims=True)
        o = xc * lax.rsqrt(var + LN_EPS) * g_ref[...] + b_ref[...]
        o_ref[...] = o
        ob_ref[...] = o.astype(BF16)

    row = pl.BlockSpec((tr, D), lambda i: (i, 0))
    vec = pl.BlockSpec((1, D), lambda i: (0, 0))
    return _call(body, name=name, grid=(T // tr,), in_specs=[row, row, vec, vec], out_specs=(row, row),
                 out_shape=(jax.ShapeDtypeStruct((T, D), F32), jax.ShapeDtypeStruct((T, D), BF16)),
                 compiler_params=_params(("parallel",)))(x, y, g.reshape(1, D), b.reshape(1, D))


def _ln_bwd(dout, x, y, g, alpha, *, name):
    T, D = x.shape
    tr = _rows(T, 256)

    def body(do_ref, x_ref, y_ref, g_ref, drb_ref, adr_ref, dg_ref, db_ref):
        i = pl.program_id(0)
        r = alpha * x_ref[...] + y_ref[...]
        mu = jnp.mean(r, axis=-1, keepdims=True)
        xc = r - mu
        var = jnp.mean(xc * xc, axis=-1, keepdims=True)
        rstd = lax.rsqrt(var + LN_EPS)
        xhat = xc * rstd
        do = do_ref[...]
        dxh = do * g_ref[...]
        dr = rstd * (dxh - jnp.mean(dxh, axis=-1, keepdims=True)
                     - xhat * jnp.mean(dxh * xhat, axis=-1, keepdims=True))
        drb_ref[...] = dr.astype(BF16)
        adr_ref[...] = alpha * dr

        @pl.when(i == 0)
        def _():
            dg_ref[...] = jnp.zeros_like(dg_ref)
            db_ref[...] = jnp.zeros_like(db_ref)

        dg_ref[...] += jnp.sum(do * xhat, axis=0, keepdims=True)
        db_ref[...] += jnp.sum(do, axis=0, keepdims=True)

    row = pl.BlockSpec((tr, D), lambda i: (i, 0))
    vec = pl.BlockSpec((1, D), lambda i: (0, 0))
    return _call(body, name=name, grid=(T // tr,), in_specs=[row, row, row, vec], out_specs=(row, row, vec, vec),
                 out_shape=(jax.ShapeDtypeStruct((T, D), BF16), jax.ShapeDtypeStruct((T, D), F32),
                            jax.ShapeDtypeStruct((1, D), F32), jax.ShapeDtypeStruct((1, D), F32)),
                 compiler_params=_params(("arbitrary",)))(dout, x, y, g.reshape(1, D))


def _loss_head(xf, target, *, name):
    T, D = xf.shape
    tr = _rows(T, 256)

    def body(x_ref, t_ref, dy_ref, s_ref):
        i = pl.program_id(0)
        err = x_ref[...] - t_ref[...]
        dy_ref[...] = err * (1.0 / D)

        @pl.when(i == 0)
        def _():
            s_ref[...] = jnp.zeros_like(s_ref)

        s_ref[...] += jnp.sum(jnp.sum(err * err, axis=1, keepdims=True), axis=0, keepdims=True)

    row = pl.BlockSpec((tr, D), lambda i: (i, 0))
    return _call(body, name=name, grid=(T // tr,), in_specs=[row, row],
                 out_specs=(row, pl.BlockSpec((8, 128), lambda i: (0, 0))),
                 out_shape=(jax.ShapeDtypeStruct((T, D), F32), jax.ShapeDtypeStruct((8, 128), F32)),
                 compiler_params=_params(("arbitrary",)))(xf, target)


def _adamw(w, g, m, v, *, name):
    R, C = w.shape
    tr = _rows(R, max(8, (1 << 18) // C // 8 * 8))

    def body(w_ref, g_ref, m_ref, v_ref, d_ref, nm_ref, nv_ref):
        g_ = g_ref[...]
        m_ = ADAM_B1 * m_ref[...] + (1.0 - ADAM_B1) * g_
        v_ = ADAM_B2 * v_ref[...] + (1.0 - ADAM_B2) * (g_ * g_)
        m_hat = m_ / (1.0 - ADAM_B1 ** ADAM_STEP)
        v_hat = v_ / (1.0 - ADAM_B2 ** ADAM_STEP)
        d_ref[...] = -ADAM_LR * (m_hat / (jnp.sqrt(v_hat) + ADAM_EPS) + ADAM_WD * w_ref[...])
        nm_ref[...] = m_
        nv_ref[...] = v_

    blk = pl.BlockSpec((tr, C), lambda i: (i, 0))
    sds = jax.ShapeDtypeStruct((R, C), F32)
    return _call(body, name=name, grid=(R // tr,), in_specs=[blk] * 4, out_specs=(blk,) * 3,
                 out_shape=(sds,) * 3, compiler_params=_params(("parallel",)))(w, g, m, v)


def _dot_nt(a, b):
    return lax.dot_general(a, b, (((1,), (1,)), ((), ())), preferred_element_type=F32)


def _dot_tn(a, b):
    return lax.dot_general(a, b, (((0,), (0,)), ((), ())), preferred_element_type=F32)


def _band_masks():
    qi = lax.broadcasted_iota(jnp.int32, (BLK, BLK), 0)
    kj = lax.broadcasted_iota(jnp.int32, (BLK, BLK), 1)
    return qi >= kj, kj >= qi


def _attn_fwd(P, grp, d, a_out, *, name):
    T, C = P.shape
    L = T // d
    nb = L // BLK
    nhh = a_out // HEAD_BLOCK
    cb = C // HEAD_BLOCK
    q0 = grp * 3 * nhh
    scale = HEAD_DIM ** -0.5

    def body(q_ref, kc_ref, kp_ref, vc_ref, vp_ref, o_ref, l_ref):
        b = pl.program_id(1)
        mask_c, mask_p = _band_masks()
        mask_p = jnp.logical_and(mask_p, b > 0)
        for h in range(HEAD_BLOCK // HEAD_DIM):
            hs = slice(h * HEAD_DIM, (h + 1) * HEAD_DIM)
            q = q_ref[:, hs]
            s_c = jnp.where(mask_c, _dot_nt(q, kc_ref[:, hs]) * scale, NEG)
            s_p = jnp.where(mask_p, _dot_nt(q, kp_ref[:, hs]) * scale, NEG)
            m = jnp.maximum(jnp.max(s_c, axis=1, keepdims=True), jnp.max(s_p, axis=1, keepdims=True))
            p_c = jnp.exp(s_c - m)
            p_p = jnp.exp(s_p - m)
            l = jnp.sum(p_c, axis=1, keepdims=True) + jnp.sum(p_p, axis=1, keepdims=True)
            o = (jnp.dot(p_c.astype(BF16), vc_ref[:, hs], preferred_element_type=F32)
                 + jnp.dot(p_p.astype(BF16), vp_ref[:, hs], preferred_element_type=F32))
            o_ref[:, hs] = o / l
            l_ref[:, hs] = jnp.broadcast_to(m + jnp.log(l), (BLK, HEAD_DIM))

    def cur(part):
        return pl.BlockSpec((BLK, HEAD_BLOCK), lambda r, b, hh: (b, r * cb + q0 + part * nhh + hh))

    def prev(part):
        return pl.BlockSpec((BLK, HEAD_BLOCK), lambda r, b, hh: (jnp.maximum(b - 1, 0), r * cb + q0 + part * nhh + hh))

    out = pl.BlockSpec((BLK, HEAD_BLOCK), lambda r, b, hh: (b, r * nhh + hh))
    Pv = P.reshape(L, d * C)
    o, lse = _call(body, name=name, grid=(d, nb, nhh), in_specs=[cur(0), cur(1), prev(1), cur(2), prev(2)],
                   out_specs=(out, out),
                   out_shape=(jax.ShapeDtypeStruct((L, d * a_out), F32),) * 2,
                   compiler_params=_params(("parallel", "parallel", "parallel")))(Pv, Pv, Pv, Pv, Pv)
    return o.reshape(T, a_out), lse.reshape(T, a_out)


def _attn_combine(os_, lses, *, name):
    T, W = os_[0].shape
    tr = _rows(T, 256)
    n = len(os_)

    def body(*refs):
        o_refs, l_refs = refs[:n], refs[n:2 * n]
        mix_ref, lse_ref = refs[2 * n], refs[2 * n + 1]
        ls = [r[...] for r in l_refs]
        m = functools.reduce(jnp.maximum, ls)
        es = [jnp.exp(l - m) for l in ls]
        tot = functools.reduce(lambda a, b: a + b, es)
        mix = functools.reduce(lambda a, b: a + b, [(e / tot) * o[...] for e, o in zip(es, o_refs)])
        mix_ref[...] = mix
        lse_ref[...] = m + jnp.log(tot)

    blk = pl.BlockSpec((tr, W), lambda i: (i, 0))
    sds = jax.ShapeDtypeStruct((T, W), F32)
    return _call(body, name=name, grid=(T // tr,), in_specs=[blk] * (2 * n), out_specs=(blk, blk),
                 out_shape=(sds, sds), compiler_params=_params(("parallel",)))(*os_, *lses)


def _attn_bwd(P, dO, O, LSE, grp, d, a_out, *, name):
    T, C = P.shape
    L = T // d
    nb = L // BLK
    nhh = a_out // HEAD_BLOCK
    cb = C // HEAD_BLOCK
    q0 = grp * 3 * nhh
    scale = HEAD_DIM ** -0.5

    def body(q_ref, qn_ref, kc_ref, kp_ref, vc_ref, vp_ref, do_ref, don_ref, o_ref, on_ref, l_ref, ln_ref,
             dq_ref, dk_ref, dv_ref):
        b = pl.program_id(1)
        mask_c, mask_prev = _band_masks()
        mask_p = jnp.logical_and(mask_prev, b > 0)
        mask_n = jnp.logical_and(mask_prev, b < nb - 1)
        for h in range(HEAD_BLOCK // HEAD_DIM):
            hs = slice(h * HEAD_DIM, (h + 1) * HEAD_DIM)
            q, qn, kc, kp, vc, vp = (r[:, hs] for r in (q_ref, qn_ref, kc_ref, kp_ref, vc_ref, vp_ref))
            do, don = do_ref[:, hs], don_ref[:, hs]
            lse, lse_n = l_ref[:, hs], ln_ref[:, hs]
            delta = jnp.sum(do * o_ref[:, hs], axis=1, keepdims=True)
            delta_n = jnp.sum(don * on_ref[:, hs], axis=1, keepdims=True)
            dob, donb = do.astype(BF16), don.astype(BF16)
            p_c = jnp.exp(jnp.where(mask_c, _dot_nt(q, kc) * scale, NEG) - lse)
            p_p = jnp.exp(jnp.where(mask_p, _dot_nt(q, kp) * scale, NEG) - lse)
            p_n = jnp.exp(jnp.where(mask_n, _dot_nt(qn, kc) * scale, NEG) - lse_n)
            ds_c = (p_c * (_dot_nt(dob, vc) - delta) * scale).astype(BF16)
            ds_p = (p_p * (_dot_nt(dob, vp) - delta) * scale).astype(BF16)
            ds_n = (p_n * (_dot_nt(donb, vc) - delta_n) * scale).astype(BF16)
            dq = jnp.dot(ds_c, kc, preferred_element_type=F32) + jnp.dot(ds_p, kp, preferred_element_type=F32)
            dk = _dot_tn(ds_c, q) + _dot_tn(ds_n, qn)
            dv = _dot_tn(p_c.astype(BF16), dob) + _dot_tn(p_n.astype(BF16), donb)
            dq_ref[:, hs] = dq.astype(BF16)
            dk_ref[:, hs] = dk.astype(BF16)
            dv_ref[:, hs] = dv.astype(BF16)

    def pspec(part, shift):
        def imap(r, b, hh):
            return (jnp.clip(b + shift, 0, nb - 1), r * cb + q0 + part * nhh + hh)
        return pl.BlockSpec((BLK, HEAD_BLOCK), imap)

    def aspec(shift):
        return pl.BlockSpec((BLK, HEAD_BLOCK), lambda r, b, hh: (jnp.clip(b + shift, 0, nb - 1), r * nhh + hh))

    Pv = P.reshape(L, d * C)
    dOv, Ov, Lv = (t.reshape(L, d * a_out) for t in (dO, O, LSE))
    sds = jax.ShapeDtypeStruct((L, d * a_out), BF16)
    dq, dk, dv = _call(
        body, name=name, grid=(d, nb, nhh),
        in_specs=[pspec(0, 0), pspec(0, 1), pspec(1, 0), pspec(1, -1), pspec(2, 0), pspec(2, -1),
                  aspec(0), aspec(1), aspec(0), aspec(1), aspec(0), aspec(1)],
        out_specs=(aspec(0),) * 3, out_shape=(sds,) * 3,
        compiler_params=_params(("parallel", "parallel", "parallel")))(Pv, Pv, Pv, Pv, Pv, Pv, dOv, dOv, Ov, Ov, Lv, Lv)
    return [t.reshape(T, a_out) for t in (dq, dk, dv)]


def _mem_softmax(q, k, scale):
    s = _dot_nt(q, k) * scale
    e = jnp.exp(s - jnp.max(s, axis=1, keepdims=True))
    return e / jnp.sum(e, axis=1, keepdims=True)


def _mem_fwd(P, qcol, kv, *, name):
    T = P.shape[0]
    tq = _rows(T, 512)
    scale = HEAD_DIM ** -0.5

    def body(q_ref, kv_ref, o_ref):
        for h in range(MEM_HEADS):
            hs = slice(h * HEAD_DIM, (h + 1) * HEAD_DIM)
            vs = slice(MEM_WIDTH + h * HEAD_DIM, MEM_WIDTH + (h + 1) * HEAD_DIM)
            p = _mem_softmax(q_ref[:, hs].astype(BF16), kv_ref[:, hs], scale)
            o_ref[:, hs] = jnp.dot(p.astype(BF16), kv_ref[:, vs], preferred_element_type=F32)

    return _call(body, name=name, grid=(T // tq,),
                 in_specs=[pl.BlockSpec((tq, MEM_WIDTH), lambda i: (i, qcol)),
                           pl.BlockSpec((MEM_TOKENS, 2 * MEM_WIDTH), lambda i: (0, 0))],
                 out_specs=pl.BlockSpec((tq, MEM_WIDTH), lambda i: (i, 0)),
                 out_shape=jax.ShapeDtypeStruct((T, MEM_WIDTH), F32),
                 compiler_params=_params(("parallel",)))(P, kv)


def _mem_bwd(P, qcol, kv, dcat, ocol, *, name):
    T = P.shape[0]
    tq = _rows(T, 512)
    scale = HEAD_DIM ** -0.5

    def body(q_ref, kv_ref, do_ref, dq_ref, dkv_ref):
        i = pl.program_id(0)

        @pl.when(i == 0)
        def _():
            dkv_ref[...] = jnp.zeros_like(dkv_ref)

        for h in range(MEM_HEADS):
            hs = slice(h * HEAD_DIM, (h + 1) * HEAD_DIM)
            vs = slice(MEM_WIDTH + h * HEAD_DIM, MEM_WIDTH + (h + 1) * HEAD_DIM)
            q = q_ref[:, hs].astype(BF16)
            k, v = kv_ref[:, hs], kv_ref[:, vs]
            do = do_ref[:, hs].astype(BF16)
            p = _mem_softmax(q, k, scale)
            dp = _dot_nt(do, v)
            ds = (p * (dp - jnp.sum(p * dp, axis=1, keepdims=True)) * scale).astype(BF16)
            dq_ref[:, hs] = jnp.dot(ds, k, preferred_element_type=F32).astype(BF16)
            dkv_ref[:, hs] += _dot_tn(ds, q)
            dkv_ref[:, vs] += _dot_tn(p.astype(BF16), do)

    return _call(body, name=name, grid=(T // tq,),
                 in_specs=[pl.BlockSpec((tq, MEM_WIDTH), lambda i: (i, qcol)),
                           pl.BlockSpec((MEM_TOKENS, 2 * MEM_WIDTH), lambda i: (0, 0)),
                           pl.BlockSpec((tq, MEM_WIDTH), lambda i: (i, ocol))],
                 out_specs=(pl.BlockSpec((tq, MEM_WIDTH), lambda i: (i, 0)),
                            pl.BlockSpec((MEM_TOKENS, 2 * MEM_WIDTH), lambda i: (0, 0))),
                 out_shape=(jax.ShapeDtypeStruct((T, MEM_WIDTH), BF16),
                            jax.ShapeDtypeStruct((MEM_TOKENS, 2 * MEM_WIDTH), F32)),
                 compiler_params=_params(("arbitrary",)))(P, kv, dcat)


_SQRT_HALF = 0.7071067811865476
_INV_SQRT_2PI = 0.3989422804014327


def _gelu(x):
    return 0.5 * x * (1.0 + lax.erf(x * _SQRT_HALF))


def _gelu_grad(x):
    return 0.5 * (1.0 + lax.erf(x * _SQRT_HALF)) + x * (_INV_SQRT_2PI * jnp.exp(-0.5 * x * x))


def _tril():
    t = lax.broadcasted_iota(jnp.int32, (BLK, BLK), 0)
    s = lax.broadcasted_iota(jnp.int32, (BLK, BLK), 1)
    return t >= s


def _gmlp_fwd(P, w_s, bs_t, vg, vb, width, *, name):
    T = P.shape[0]
    G = width // HEAD_DIM
    tb = _rows(T, 512)

    def body(pu_ref, pv_ref, ws_ref, bs_ref, vg_ref, vb_ref, o_ref):
        u = _gelu(pu_ref[...])
        v = _gelu(pv_ref[...])
        mu = jnp.mean(v, axis=-1, keepdims=True)
        vc = v - mu
        var = jnp.mean(vc * vc, axis=-1, keepdims=True)
        vn = (vc * lax.rsqrt(var + LN_EPS) * vg_ref[...] + vb_ref[...]).astype(BF16)
        tril = _tril()
        for g in range(G):
            gs = slice(g * HEAD_DIM, (g + 1) * HEAD_DIM)
            ws = jnp.where(tril, ws_ref[g], 0.0).astype(BF16)
            bias = bs_ref[:, g:g + 1]
            for c in range(tb // BLK):
                cs = slice(c * BLK, (c + 1) * BLK)
                sg = jnp.dot(ws, vn[cs, gs], preferred_element_type=F32) + bias
                o_ref[cs, gs] = u[cs, gs] * sg

    blk = lambda col: pl.BlockSpec((tb, width), lambda i: (i, col))
    full = lambda shape: pl.BlockSpec(shape, lambda i: (0,) * len(shape))
    return _call(body, name=name, grid=(T // tb,),
                 in_specs=[blk(0), blk(1), full((G, BLK, BLK)), full((BLK, G)), full((1, width)), full((1, width))],
                 out_specs=blk(0), out_shape=jax.ShapeDtypeStruct((T, width), F32),
                 compiler_params=_params(("parallel",)))(P, P, w_s, bs_t, vg.reshape(1, width), vb.reshape(1, width))


def _gmlp_bwd(P, w_s, bs_t, vg, vb, dcat, width, *, name):
    T = P.shape[0]
    G = width // HEAD_DIM
    tb = _rows(T, 512)

    def body(pu_ref, pv_ref, ws_ref, bs_ref, vg_ref, vb_ref, dm_ref, dpu_ref, dpv_ref, dws_ref, dbs_ref, dvg_ref,
             dvb_ref, dvn_ref):
        i = pl.program_id(0)

        @pl.when(i == 0)
        def _():
            dws_ref[...] = jnp.zeros_like(dws_ref)
            dbs_ref[...] = jnp.zeros_like(dbs_ref)
            dvg_ref[...] = jnp.zeros_like(dvg_ref)
            dvb_ref[...] = jnp.zeros_like(dvb_ref)

        pu, pv = pu_ref[...], pv_ref[...]
        u = _gelu(pu)
        v = _gelu(pv)
        mu = jnp.mean(v, axis=-1, keepdims=True)
        vc = v - mu
        var = jnp.mean(vc * vc, axis=-1, keepdims=True)
        rstd = lax.rsqrt(var + LN_EPS)
        xhat = vc * rstd
        vn = (xhat * vg_ref[...] + vb_ref[...]).astype(BF16)
        dm = dm_ref[...]
        tril = _tril()
        lane = lax.broadcasted_iota(jnp.int32, (BLK, BLK), 1)
        dbs = jnp.zeros((BLK, BLK), F32)
        for g in range(G):
            gs = slice(g * HEAD_DIM, (g + 1) * HEAD_DIM)
            ws = jnp.where(tril, ws_ref[g], 0.0).astype(BF16)
            bias = bs_ref[:, g:g + 1]
            dws = jnp.zeros((BLK, BLK), F32)
            rs = jnp.zeros((BLK, 1), F32)
            for c in range(tb // BLK):
                cs = slice(c * BLK, (c + 1) * BLK)
                vn_cg = vn[cs, gs]
                sg = jnp.dot(ws, vn_cg, preferred_element_type=F32) + bias
                dm_cg = dm[cs, gs]
                dpu_ref[cs, gs] = (dm_cg * sg * _gelu_grad(pu[cs, gs])).astype(BF16)
                dsg = dm_cg * u[cs, gs]
                dsgb = dsg.astype(BF16)
                dvn_ref[cs, gs] = _dot_tn(ws, dsgb)
                dws = dws + _dot_nt(dsgb, vn_cg)
                rs = rs + jnp.sum(dsg, axis=1, keepdims=True)
            dws_ref[g] += jnp.where(tril, dws, 0.0)
            dbs = dbs + jnp.where(lane == g, rs, 0.0)
        dbs_ref[...] += dbs
        dvn = dvn_ref[...]
        dxh = dvn * vg_ref[...]
        dv = rstd * (dxh - jnp.mean(dxh, axis=-1, keepdims=True)
                     - xhat * jnp.mean(dxh * xhat, axis=-1, keepdims=True))
        dpv_ref[...] = (dv * _gelu_grad(pv)).astype(BF16)
        dvg_ref[...] += jnp.sum(dvn * xhat, axis=0, keepdims=True)
        dvb_ref[...] += jnp.sum(dvn, axis=0, keepdims=True)

    blk = lambda col: pl.BlockSpec((tb, width), lambda i: (i, col))
    full = lambda shape: pl.BlockSpec(shape, lambda i: (0,) * len(shape))
    return _call(body, name=name, grid=(T // tb,),
                 in_specs=[blk(0), blk(1), full((G, BLK, BLK)), full((BLK, G)), full((1, width)), full((1, width)),
                           blk(0)],
                 out_specs=(blk(0), blk(0), full((G, BLK, BLK)), full((BLK, BLK)), full((1, width)), full((1, width))),
                 out_shape=(jax.ShapeDtypeStruct((T, width), BF16), jax.ShapeDtypeStruct((T, width), BF16),
                            jax.ShapeDtypeStruct((G, BLK, BLK), F32), jax.ShapeDtypeStruct((BLK, BLK), F32),
                            jax.ShapeDtypeStruct((1, width), F32), jax.ShapeDtypeStruct((1, width), F32)),
                 scratch_shapes=[pltpu.VMEM((tb, width), F32)],
                 compiler_params=_params(("arbitrary",)))(P, P, w_s, bs_t, vg.reshape(1, width), vb.reshape(1, width), dcat)


def _place():
    x, y, c = lax.axis_index("x"), lax.axis_index("y"), lax.axis_index("c")
    chips = [(1 - x, y), (x, 1 - y), (1 - x, 1 - y)]
    return x, y, c, 2 * x + y, chips, [2 * px + py for px, py in chips]


def _half(ref, c, rows):
    return ref.at[pl.ds(c * (rows // 2), rows // 2)]


def _place_shard(w, chip, dtype, *, name):
    R, C = w.shape
    tr = _rows(R, max(8, (1 << 19) // C // 8 * 8)) if R % 8 == 0 else R

    def body(chip_ref, w_ref, o_ref):
        o_ref[...] = w_ref[...].astype(dtype)

    grid_spec = pltpu.PrefetchScalarGridSpec(
        num_scalar_prefetch=1, grid=(R // tr,), in_specs=[pl.BlockSpec((tr, C), lambda i, s: (i, 0))],
        out_specs=pl.BlockSpec((None, tr, C), lambda i, s: (s[0], i, 0)))
    return _call(body, name=name, grid_spec=grid_spec, out_shape=jax.ShapeDtypeStruct((4, R, C), dtype),
                 compiler_params=_params(("parallel",)))(chip, w)


def _all_gather(bufs, *, name):
    n = len(bufs)
    shards = [jax.ShapeDtypeStruct(b.shape[1:], b.dtype) for b in bufs]
    split = [s.shape[0] % 16 == 0 for s in shards]
    ns = sum(split)

    def body(*refs):
        dst = refs[n:2 * n]
        send1, recv1, send2, recv2 = refs[2 * n:]
        x, y, c, j, chips, pj = _place()
        sib = (x, y, 1 - c)

        def window(t, chip_idx, core):
            w = dst[t].at[chip_idx]
            return _half(w, core, shards[t].shape[0]) if split[t] else w

        def ici(t, k, to):
            return pltpu.make_async_remote_copy(src_ref=window(t, j, c), dst_ref=window(t, j, c),
                                                send_sem=send1.at[3 * t + k], recv_sem=recv1.at[3 * t + k],
                                                device_id=to, device_id_type=MESH)

        firsts = [ici(t, k, (*chips[k], c)) for t in range(n) for k in range(3)]
        for cp in firsts:
            cp.start()

        def landed(t, k):
            return pltpu.make_async_remote_copy(src_ref=window(t, pj[k], c), dst_ref=window(t, pj[k], c),
                                                send_sem=send1.at[3 * t + k], recv_sem=recv1.at[3 * t + k],
                                                device_id=sib, device_id_type=MESH)

        def d2d(u, t, k, core):
            return pltpu.make_async_remote_copy(src_ref=window(t, pj[k], core), dst_ref=window(t, pj[k], core),
                                                send_sem=send2.at[3 * u + k], recv_sem=recv2.at[3 * u + k],
                                                device_id=sib, device_id_type=MESH)

        seconds = []
        for k in range(3):
            u = 0
            for t in range(n):
                landed(t, k).wait_recv()
                if split[t]:
                    cp = d2d(u, t, k, c)
                    cp.start()
                    seconds.append(cp)
                    u += 1
        for k in range(3):
            u = 0
            for t in range(n):
                if split[t]:
                    d2d(u, t, k, 1 - c).wait_recv()
                    u += 1
        for cp in firsts + seconds:
            cp.wait_send()

    return _call(body, name=name, in_specs=[HBM] * n, out_specs=[HBM] * n,
                 out_shape=[jax.ShapeDtypeStruct(b.shape, b.dtype) for b in bufs],
                 input_output_aliases={t: t for t in range(n)},
                 scratch_shapes=[pltpu.SemaphoreType.DMA((3 * n,)), pltpu.SemaphoreType.DMA((3 * n,)),
                                 pltpu.SemaphoreType.DMA((3 * max(ns, 1),)), pltpu.SemaphoreType.DMA((3 * max(ns, 1),))])(*bufs)


def _pair_exchange(grads, *, name):
    n = len(grads)

    def body(*refs):
        g, theirs = refs[:n], refs[n:2 * n]
        send, recv = refs[2 * n:]
        x, y, c, _, _, _ = _place()
        copies = []
        for t in range(n):
            h = grads[t].shape[1] // 2
            rc = pltpu.make_async_remote_copy(src_ref=g[t].at[:, pl.ds((1 - c) * h, h)], dst_ref=theirs[t],
                                              send_sem=send.at[t], recv_sem=recv.at[t], device_id=(x, y, 1 - c),
                                              device_id_type=MESH)
            rc.start()
            copies.append(rc)
        for rc in copies:
            rc.wait()

    halves = [jax.ShapeDtypeStruct((4, g.shape[1] // 2, g.shape[2]), g.dtype) for g in grads]
    return _call(body, name=name, in_specs=[HBM] * n, out_specs=[HBM] * n, out_shape=halves,
                 scratch_shapes=[pltpu.SemaphoreType.DMA((n,))] * 2)(*grads)


def _chip_exchange(pairs, *, name):
    n = len(pairs)

    def body(*refs):
        s, r = refs[:n], refs[n:2 * n]
        send, recv = refs[2 * n:]
        x, y, c, j, chips, pj = _place()
        remote = [pltpu.make_async_remote_copy(src_ref=s[t].at[pj[k]], dst_ref=r[t].at[k],
                                               send_sem=send.at[3 * t + k], recv_sem=recv.at[3 * t + k],
                                               device_id=(*chips[k], c), device_id_type=MESH)
                  for t in range(n) for k in range(3)]
        for cp in remote:
            cp.start()
        for cp in remote:
            cp.wait()

    return _call(body, name=name, in_specs=[HBM] * n, out_specs=[HBM] * n,
                 out_shape=[jax.ShapeDtypeStruct((3,) + p.shape[1:], p.dtype) for p in pairs],
                 scratch_shapes=[pltpu.SemaphoreType.DMA((3 * n,)), pltpu.SemaphoreType.DMA((3 * n,))])(*pairs)


def _half_swap(fulls, *, name):
    n = len(fulls)

    def body(*refs):
        full = refs[n:2 * n]
        send, recv = refs[2 * n:]
        x, y, c, _, _, _ = _place()
        copies = []
        for t in range(n):
            h = fulls[t].shape[0] // 2
            mine = full[t].at[pl.ds(c * h, h)]
            rc = pltpu.make_async_remote_copy(src_ref=mine, dst_ref=mine, send_sem=send.at[t], recv_sem=recv.at[t],
                                              device_id=(x, y, 1 - c), device_id_type=MESH)
            rc.start()
            copies.append(rc)
        for rc in copies:
            rc.wait()

    return _call(body, name=name, in_specs=[HBM] * n, out_specs=[HBM] * n,
                 out_shape=[jax.ShapeDtypeStruct(f.shape, f.dtype) for f in fulls],
                 input_output_aliases={t: t for t in range(n)},
                 scratch_shapes=[pltpu.SemaphoreType.DMA((n,))] * 2)(*fulls)


def _all_reduce_small(packed, *, name):
    R, C = packed.shape

    def body(p_ref, o_ref, slots, send, recv, lsem):
        x, y, c = lax.axis_index("x"), lax.axis_index("y"), lax.axis_index("c")
        me = 4 * x + 2 * y + c
        lc = pltpu.make_async_copy(p_ref, slots.at[me], lsem.at[0])
        lc.start()
        copies = []
        for rel in range(1, 8):
            fx, fy, fc = (rel >> 2) & 1, (rel >> 1) & 1, rel & 1
            to = (1 - x if fx else x, 1 - y if fy else y, 1 - c if fc else c)
            cp = pltpu.make_async_remote_copy(src_ref=p_ref, dst_ref=slots.at[me], send_sem=send.at[rel - 1],
                                              recv_sem=recv.at[rel - 1], device_id=to, device_id_type=MESH)
            cp.start()
            copies.append((cp, 4 * to[0] + 2 * to[1] + to[2]))
        for rel, (cp, frm) in enumerate(copies):
            cp.wait_send()
            pltpu.make_async_remote_copy(src_ref=p_ref, dst_ref=slots.at[frm], send_sem=send.at[rel],
                                         recv_sem=recv.at[rel], device_id=(x, y, c), device_id_type=MESH).wait_recv()
        lc.wait()
        acc = slots[0]
        for dev in range(1, 8):
            acc = acc + slots[dev]
        o_ref[...] = acc

    return _call(body, name=name, in_specs=[pl.BlockSpec(memory_space=pltpu.VMEM)],
                 out_specs=pl.BlockSpec(memory_space=pltpu.VMEM), out_shape=jax.ShapeDtypeStruct((R, C), F32),
                 scratch_shapes=[pltpu.VMEM((8, R, C), F32), pltpu.SemaphoreType.DMA((7,)),
                                 pltpu.SemaphoreType.DMA((7,)), pltpu.SemaphoreType.DMA((1,))],
                 compiler_params=pltpu.CompilerParams(vmem_limit_bytes=V7X_VMEM_LIMIT))(packed)


def _pair_sum(grad, theirs, core, *, name):
    J, H, C = theirs.shape
    tr = _rows(H, max(8, (1 << 19) // C // 8 * 8))

    def body(core_ref, a_ref, b_ref, o_ref):
        o_ref[...] = (a_ref[...].astype(F32) + b_ref[...].astype(F32)).astype(BF16)

    blk = pl.BlockSpec((None, tr, C), lambda j, i, s: (j, i, 0))
    mine = pl.BlockSpec((None, None, tr, C), lambda j, i, s: (j, s[0], i, 0))
    grid_spec = pltpu.PrefetchScalarGridSpec(num_scalar_prefetch=1, grid=(J, H // tr), in_specs=[mine, blk],
                                             out_specs=blk)
    return _call(body, name=name, grid_spec=grid_spec, out_shape=jax.ShapeDtypeStruct((J, H, C), BF16),
                 compiler_params=_params(("parallel", "parallel")))(core, grad.reshape(J, 2, H, C), theirs)


def _chip_sum(pairs, slots, place, *, name):
    _, H, C = slots.shape
    tr = _rows(H, max(8, (1 << 19) // C // 8 * 8))
    nr = H // tr

    def body(place_ref, s0, s1, s2, s3, o_ref):
        o_ref[...] = ((s0[...].astype(F32) + s1[...].astype(F32)) + s2[...].astype(F32)) + s3[...].astype(F32)

    def slot(k):
        return pl.BlockSpec((None, tr, C), lambda i, s: (k, i, 0))

    own = pl.BlockSpec((None, tr, C), lambda i, s: (s[0], i, 0))
    grid_spec = pltpu.PrefetchScalarGridSpec(
        num_scalar_prefetch=1, grid=(nr,), in_specs=[own, slot(0), slot(1), slot(2)],
        out_specs=pl.BlockSpec((tr, C), lambda i, s: (s[1] * nr + i, 0)))
    return _call(body, name=name, grid_spec=grid_spec, out_shape=jax.ShapeDtypeStruct((2 * H, C), F32),
                 compiler_params=_params(("parallel",)))(place, pairs, slots, slots, slots)


def _reduce_scatter(grads, place, tag):
    theirs = _pair_exchange(grads, name=f"rs_pair_{tag}")
    pairs = [_pair_sum(g, t, place[1:], name=f"rs_pair_sum_{tag}_{i}") for i, (g, t) in enumerate(zip(grads, theirs))]
    slots = _chip_exchange(pairs, name=f"rs_chip_{tag}")
    fulls = [_chip_sum(p, s, place, name=f"rs_chip_sum_{tag}_{i}") for i, (p, s) in enumerate(zip(pairs, slots))]
    return _half_swap(fulls, name=f"rs_swap_{tag}")


def kernel(x, mem, w_in_a, w_in_b, w_s, b_s, vnorm_g, vnorm_b, w_mem_kv, w_out, ln1_g, ln1_b, w_ff1, w_ff2, ln2_g, ln2_b, loss_target, m_w_in_a, m_w_in_b, m_w_s, m_b_s, m_vnorm_g, m_vnorm_b, m_w_mem_kv, m_w_out, m_ln1_g, m_ln1_b, m_w_ff1, m_w_ff2, m_ln2_g, m_ln2_b, v_w_in_a, v_w_in_b, v_w_s, v_b_s, v_vnorm_g, v_vnorm_b, v_w_mem_kv, v_w_out, v_ln1_g, v_ln1_b, v_w_ff1, v_w_ff2, v_ln2_g, v_ln2_b):
    T, D = x.shape[1], x.shape[2]
    depth = w_ff1.shape[0]
    alpha = (2.0 * depth) ** 0.25
    a_out = (D // 256) * HEAD_DIM
    a_cols = len(A_PAIRS) * 3 * a_out
    b_width = (D // 256) * HEAD_DIM
    G = b_width // HEAD_DIM
    assert a_out % HEAD_BLOCK == 0 and T % (BLK * A_PAIRS[-1][1]) == 0

    xf = x.reshape(T, D)
    mem_b = mem.reshape(MEM_TOKENS, D).astype(BF16)
    target = loss_target.reshape(T, D)
    c_idx = lax.axis_index("x") * 2 + lax.axis_index("y")
    place = jnp.stack([c_idx, lax.axis_index("c")]).astype(jnp.int32)

    saved = []
    for i in range(depth):
        jl = i // 2
        is_a = i % 2 == 0
        tag = "a" if is_a else "b"
        w_in_l = (w_in_a if is_a else w_in_b)[jl]
        bufs = [_place_shard(w, place[:1], BF16, name=f"place_{tag}_{n_}")
                for n_, w in enumerate((w_in_l, w_mem_kv[i], w_out[i], w_ff1[i], w_ff2[i]))]
        if not is_a:
            bufs += [_place_shard(v_[jl].reshape(1, -1), place[:1], F32, name="place_vnorm") for v_ in (vnorm_g, vnorm_b)]
        gathered = _all_gather(bufs, name=f"all_gather_{tag}")
        win, wkv, wout, wff1, wff2 = gathered[:5]
        wkv = wkv.reshape(1, D, 2 * MEM_WIDTH)
        wff2 = wff2.reshape(1, 4 * D, D)
        xb = xf.astype(BF16)
        P = _mm_nn(xb, win, out_dtype=BF16 if is_a else F32, name=f"proj_in_{'a' if is_a else 'b'}")
        kv = _mm_nn(mem_b, wkv, name="proj_kv")
        if is_a:
            outs = [_attn_fwd(P, g, d, a_out, name=f"attn_fwd_d{d}") for g, (_, d) in enumerate(A_PAIRS)]
            mix, lse = _attn_combine([o for o, _ in outs], [l for _, l in outs], name="attn_combine")
            qcol = a_cols // MEM_WIDTH
            extra = (lse,)
        else:
            vg_full = gathered[5].reshape(-1)
            vb_full = gathered[6].reshape(-1)
            bs_t = b_s[jl].T
            mix = _gmlp_fwd(P, w_s[jl], bs_t, vg_full, vb_full, b_width, name="gmlp_fwd")
            qcol = 2 * b_width // MEM_WIDTH
            extra = (vg_full, vb_full, bs_t)
        mem_o = _mem_fwd(P, qcol, kv, name=f"mem_fwd_{'a' if is_a else 'b'}")
        cat = jnp.concatenate([mix.astype(BF16), mem_o.astype(BF16)], axis=1)
        y = _mm_nn(cat, wout, out_dtype=F32, name="proj_out")
        x1, x1b = _ln_fwd(xf, y, ln1_g[i], ln1_b[i], alpha, name="ln_fwd")
        a_pre, hid = _mm_nn(x1b, wff1, epi="relu2", name="ff1")
        f = _mm_nn(hid, wff2, out_dtype=F32, name="ff2")
        x2, _ = _ln_fwd(x1, f, ln2_g[i], ln2_b[i], alpha, name="ln_fwd")
        saved.append(dict(xf=xf, xb=xb, P=P, kv=kv, mix=mix, cat=cat, y=y, x1=x1, x1b=x1b, a_pre=a_pre, hid=hid, f=f,
                          extra=extra, w=(win, wkv, wout, wff1, wff2), qcol=qcol))
        xf = x2

    dx, sq = _loss_head(xf, target, name="loss_head")
    loss = lax.psum(sq[0, 0] * (0.5 / D), ("x", "y", "c"))

    g_big = dict(w_in_a=[None] * ((depth + 1) // 2), w_in_b=[None] * (depth // 2), w_mem_kv=[None] * depth,
                 w_out=[None] * depth, w_ff1=[None] * depth, w_ff2=[None] * depth)
    small = {k: [None] * depth for k in ("ln1_g", "ln1_b", "ln2_g", "ln2_b")}
    small_b = {k: [None] * (depth // 2) for k in ("w_s", "b_s", "vnorm_g", "vnorm_b")}
    for i in reversed(range(depth)):
        jl = i // 2
        is_a = i % 2 == 0
        s = saved[i]
        win, wkv, wout, wff1, wff2 = s["w"]
        tag = "a" if is_a else "b"
        d_f, adr2, dg2, db2 = _ln_bwd(dx, s["x1"], s["f"], ln2_g[i], alpha, name="ln_bwd")
        small["ln2_g"][i], small["ln2_b"][i] = dg2, db2
        gw_ff2 = _mm_tn(s["hid"], d_f, 1, name="grad_ff2").reshape(4, D, D)
        da = _mm_nt(d_f, wff2, epi="drelu2", extra=s["a_pre"], name="ff2_bwd")
        gw_ff1 = _mm_tn(s["x1b"], da, 4, name="grad_ff1")
        dx1 = _mm_nt(da, wff1, epi="resid", extra=adr2, name="ff1_bwd")
        d_y, adr1, dg1, db1 = _ln_bwd(dx1, s["xf"], s["y"], ln1_g[i], alpha, name="ln_bwd")
        small["ln1_g"][i], small["ln1_b"][i] = dg1, db1
        gw_out = _mm_tn(s["cat"], d_y, 4, name="grad_out")
        dcat = _mm_nt(d_y, wout, out_dtype=F32, name="proj_out_bwd")
        ocol = dcat.shape[1] // MEM_WIDTH - 1
        dmq, dkv = _mem_bwd(s["P"], s["qcol"], s["kv"], dcat, ocol, name=f"mem_bwd_{tag}")
        gw_kv = _mm_tn(mem_b, dkv.astype(BF16), 1, name="grad_kv").reshape(4, D // 4, 2 * MEM_WIDTH)
        if is_a:
            (lse,) = s["extra"]
            dmix = dcat[:, :a_out]
            parts = []
            for g, (_, d) in enumerate(A_PAIRS):
                parts += _attn_bwd(s["P"], dmix, s["mix"], lse, g, d, a_out, name=f"attn_bwd_d{d}")
            dP = jnp.concatenate(parts + [dmq], axis=1)
        else:
            vg_full, vb_full, bs_t = s["extra"]
            dpu, dpv, dws, dbs, dvg, dvb = _gmlp_bwd(s["P"], w_s[jl], bs_t, vg_full, vb_full, dcat, b_width,
                                                     name="gmlp_bwd")
            small_b["w_s"][jl], small_b["b_s"][jl] = dws, dbs[:, :G].T
            small_b["vnorm_g"][jl], small_b["vnorm_b"][jl] = dvg, dvb
            dP = jnp.concatenate([dpu, dpv, dmq], axis=1)
        gw_in = _mm_tn(s["xb"], dP, 4, name=f"grad_in_{tag}")
        dx = _mm_nt(dP, win, epi="resid", extra=adr1, name=f"proj_in_bwd_{tag}")
        red = _reduce_scatter([gw_in, gw_kv, gw_out, gw_ff1, gw_ff2], place, tag)
        g_big["w_in_a" if is_a else "w_in_b"][jl] = red[0]
        for k, r in zip(("w_mem_kv", "w_out", "w_ff1", "w_ff2"), red[1:]):
            g_big[k][i] = r
    grad_x = dx.reshape(x.shape)

    nb_layers = depth // 2
    pieces = ([jnp.stack(small_b["w_s"]).reshape(-1, 128), jnp.stack(small_b["b_s"]).reshape(-1, 128),
               jnp.stack(small_b["vnorm_g"]).reshape(-1, 128), jnp.stack(small_b["vnorm_b"]).reshape(-1, 128)]
              + [jnp.stack(small[k]).reshape(-1, 128) for k in ("ln1_g", "ln1_b", "ln2_g", "ln2_b")])
    sizes = [p.shape[0] for p in pieces]
    pad = (-sum(sizes)) % 8
    packed = jnp.concatenate(pieces + ([jnp.zeros((pad, 128), F32)] if pad else []), axis=0)
    summed = _all_reduce_small(packed, name="all_reduce_small")
    offs = [0]
    for n_ in sizes:
        offs.append(offs[-1] + n_)
    sp = [summed[offs[k]:offs[k + 1]] for k in range(len(sizes))]
    vshard = vnorm_g.shape[1]
    g_small = dict(
        w_s=sp[0].reshape(w_s.shape), b_s=sp[1].reshape(b_s.shape),
        vnorm_g=lax.dynamic_slice_in_dim(sp[2].reshape(nb_layers, -1), c_idx * vshard, vshard, axis=1),
        vnorm_b=lax.dynamic_slice_in_dim(sp[3].reshape(nb_layers, -1), c_idx * vshard, vshard, axis=1),
        ln1_g=sp[4].reshape(ln1_g.shape), ln1_b=sp[5].reshape(ln1_b.shape),
        ln2_g=sp[6].reshape(ln2_g.shape), ln2_b=sp[7].reshape(ln2_b.shape))

    names = ["w_in_a", "w_in_b", "w_s", "b_s", "vnorm_g", "vnorm_b", "w_mem_kv", "w_out", "ln1_g", "ln1_b", "w_ff1",
             "w_ff2", "ln2_g", "ln2_b"]
    ws = dict(w_in_a=w_in_a, w_in_b=w_in_b, w_s=w_s, b_s=b_s, vnorm_g=vnorm_g, vnorm_b=vnorm_b, w_mem_kv=w_mem_kv,
              w_out=w_out, ln1_g=ln1_g, ln1_b=ln1_b, w_ff1=w_ff1, w_ff2=w_ff2, ln2_g=ln2_g, ln2_b=ln2_b)
    ms = dict(w_in_a=m_w_in_a, w_in_b=m_w_in_b, w_s=m_w_s, b_s=m_b_s, vnorm_g=m_vnorm_g, vnorm_b=m_vnorm_b,
              w_mem_kv=m_w_mem_kv, w_out=m_w_out, ln1_g=m_ln1_g, ln1_b=m_ln1_b, w_ff1=m_w_ff1, w_ff2=m_w_ff2,
              ln2_g=m_ln2_g, ln2_b=m_ln2_b)
    vs = dict(w_in_a=v_w_in_a, w_in_b=v_w_in_b, w_s=v_w_s, b_s=v_b_s, vnorm_g=v_vnorm_g, vnorm_b=v_vnorm_b,
              w_mem_kv=v_w_mem_kv, w_out=v_w_out, ln1_g=v_ln1_g, ln1_b=v_ln1_b, w_ff1=v_w_ff1, w_ff2=v_w_ff2,
              ln2_g=v_ln2_g, ln2_b=v_ln2_b)
    grads, deltas, new_m, new_v = {}, {}, {}, {}
    for k in g_big:
        g = jnp.stack(g_big[k]).reshape(ws[k].shape)
        cols = ws[k].shape[-1]
        d_, m_, v_ = _adamw(ws[k].reshape(-1, cols), g.reshape(-1, cols), ms[k].reshape(-1, cols),
                            vs[k].reshape(-1, cols), name=f"adamw_{k}")
        grads[k] = g
        deltas[k], new_m[k], new_v[k] = (t.reshape(ws[k].shape) for t in (d_, m_, v_))
    small_names = [k for k in names if k not in g_big]

    def pack(tree):
        flat = jnp.concatenate([tree[k].reshape(-1) for k in small_names])
        padn = (-flat.shape[0]) % 1024
        return jnp.pad(flat, (0, padn)).reshape(-1, 128)

    d_, m_, v_ = _adamw(pack(ws), pack(g_small), pack(ms), pack(vs), name="adamw_small")
    off = 0
    for k in small_names:
        n_ = ws[k].size
        grads[k] = g_small[k]
        deltas[k], new_m[k], new_v[k] = (t.reshape(-1)[off:off + n_].reshape(ws[k].shape) for t in (d_, m_, v_))
        off += n_

    return (loss, grad_x, *[grads[k] for k in names], *[deltas[k] for k in names], *[new_m[k] for k in names],
            *[new_v[k] for k in names])
```

```python
import functools
import math

import jax
import jax.numpy as jnp
from jax import lax
from jax.experimental import pallas as pl
from jax.experimental.pallas import tpu as pltpu

F32 = jnp.float32
BF16 = jnp.bfloat16

HEAD_DIM = 128
A_PAIRS = ((128, 1), (512, 4), (2048, 16))
BLK = 128
HEAD_BLOCK = 512
MEM_TOKENS = 256
MEM_HEADS = 4
MEM_WIDTH = MEM_HEADS * HEAD_DIM
LN_EPS = 1e-5
ADAM_LR, ADAM_B1, ADAM_B2, ADAM_EPS, ADAM_WD, ADAM_STEP = 0.001, 0.9, 0.999, 1e-08, 0.01, 10
NEG = -1e30
V7X_VMEM_LIMIT = 48 * 1024 * 1024
MESH = pl.DeviceIdType.MESH
HBM = pl.BlockSpec(memory_space=pltpu.HBM)


def _call(body, **kw):
    return pl.pallas_call(body, **kw)


def _params(sem):
    return pltpu.CompilerParams(dimension_semantics=sem, vmem_limit_bytes=V7X_VMEM_LIMIT)


def _tile(n, cap):
    best = 0
    for t in range(128, min(n, cap) + 1, 128):
        if n % t == 0:
            best = t
    if best == 0 or (best < 512 and n <= 2560):
        return n
    return best


def _rows(n, cap):
    best = 8
    for t in range(8, min(n, cap) + 1, 8):
        if n % t == 0:
            best = t
    return best


def _epilogue(epi, acc, extra_ref, out_refs):
    if epi == "plain":
        out_refs[0][...] = acc.astype(out_refs[0].dtype)
    elif epi == "relu2":
        out_refs[0][...] = acc
        r = jnp.maximum(acc, 0.0)
        out_refs[1][...] = (r * r).astype(out_refs[1].dtype)
    elif epi == "drelu2":
        out_refs[0][...] = (acc * (2.0 * jnp.maximum(extra_ref[...], 0.0))).astype(out_refs[0].dtype)
    elif epi == "resid":
        out_refs[0][...] = acc + extra_ref[...]
    else:
        raise ValueError(epi)


def _mm_nn(a, w, *, epi="plain", out_dtype=BF16, extra=None, name):
    M, K = a.shape
    J, K2, Nj = w.shape
    assert K == K2
    tn = _tile(Nj, 1024)
    tm = _tile(M, 1024 if tn <= 1024 else 512)
    tk = _tile(K, 512)
    nn, nk = Nj // tn, K // tk
    n_out = 2 if epi == "relu2" else 1
    has_extra = extra is not None

    def body(*refs):
        a_ref, w_ref = refs[0], refs[1]
        extra_ref = refs[2] if has_extra else None
        outs = refs[2 + has_extra: 2 + has_extra + n_out]
        acc_ref = refs[-1]
        k = pl.program_id(3)

        @pl.when(k == 0)
        def _():
            acc_ref[...] = jnp.zeros_like(acc_ref)

        acc_ref[...] += jnp.dot(a_ref[...], w_ref[...], preferred_element_type=F32)

        @pl.when(k == nk - 1)
        def _():
            _epilogue(epi, acc_ref[...], extra_ref, outs)

    omap = lambda i, j, n, k: (i, j * nn + n)
    in_specs = [pl.BlockSpec((tm, tk), lambda i, j, n, k: (i, k)),
                pl.BlockSpec((None, tk, tn), lambda i, j, n, k: (j, k, n))]
    args = [a, w]
    if has_extra:
        in_specs.append(pl.BlockSpec((tm, tn), omap))
        args.append(extra)
    if epi == "relu2":
        out_shape = (jax.ShapeDtypeStruct((M, J * Nj), F32), jax.ShapeDtypeStruct((M, J * Nj), BF16))
        out_specs = (pl.BlockSpec((tm, tn), omap), pl.BlockSpec((tm, tn), omap))
    else:
        out_shape = jax.ShapeDtypeStruct((M, J * Nj), F32 if epi == "resid" else out_dtype)
        out_specs = pl.BlockSpec((tm, tn), omap)
    return _call(body, name=name, grid=(M // tm, J, nn, nk), in_specs=in_specs, out_specs=out_specs,
                 out_shape=out_shape, scratch_shapes=[pltpu.VMEM((tm, tn), F32)],
                 compiler_params=_params(("parallel", "parallel", "parallel", "arbitrary")))(*args)


def _mm_nt(a, w, *, epi="plain", out_dtype=BF16, extra=None, name):
    M, N = a.shape
    J, K, Nj = w.shape
    assert N == J * Nj
    tc = _tile(Nj, 1024)
    tko = _tile(K, 1024)
    tm = _tile(M, 1024 if tc <= 1024 else 512)
    nc = Nj // tc
    has_extra = extra is not None

    def body(*refs):
        a_ref, w_ref = refs[0], refs[1]
        extra_ref = refs[2] if has_extra else None
        outs = refs[2 + has_extra: 3 + has_extra]
        acc_ref = refs[-1]
        j, c = pl.program_id(2), pl.program_id(3)

        @pl.when(jnp.logical_and(j == 0, c == 0))
        def _():
            acc_ref[...] = jnp.zeros_like(acc_ref)

        acc_ref[...] += lax.dot_general(a_ref[...], w_ref[...], (((1,), (1,)), ((), ())),
                                        preferred_element_type=F32)

        @pl.when(jnp.logical_and(j == J - 1, c == nc - 1))
        def _():
            _epilogue(epi, acc_ref[...], extra_ref, outs)

    omap = lambda i, ko, j, c: (i, ko)
    in_specs = [pl.BlockSpec((tm, tc), lambda i, ko, j, c: (i, j * nc + c)),
                pl.BlockSpec((None, tko, tc), lambda i, ko, j, c: (j, ko, c))]
    args = [a, w]
    if has_extra:
        in_specs.append(pl.BlockSpec((tm, tko), omap))
        args.append(extra)
    out_shape = jax.ShapeDtypeStruct((M, K), F32 if epi == "resid" else out_dtype)
    return _call(body, name=name, grid=(M // tm, K // tko, J, nc), in_specs=in_specs,
                 out_specs=pl.BlockSpec((tm, tko), omap), out_shape=out_shape,
                 scratch_shapes=[pltpu.VMEM((tm, tko), F32)],
                 compiler_params=_params(("parallel", "parallel", "arbitrary", "arbitrary")))(*args)


def _mm_tn(a, b, J, *, name):
    T, K = a.shape
    T2, N = b.shape
    assert T == T2 and N % J == 0
    Nj = N // J
    tn = _tile(Nj, 1024)
    tkr = _tile(K, 1024 if tn <= 1024 else 512)
    tt = _tile(T, 512)
    nn, nt = Nj // tn, T // tt

    def body(a_ref, b_ref, o_ref, acc_ref):
        t = pl.program_id(3)

        @pl.when(t == 0)
        def _():
            acc_ref[...] = jnp.zeros_like(acc_ref)

        acc_ref[...] += lax.dot_general(a_ref[...], b_ref[...], (((0,), (0,)), ((), ())),
                                        preferred_element_type=F32)

        @pl.when(t == nt - 1)
        def _():
            o_ref[...] = acc_ref[...].astype(o_ref.dtype)

    return _call(body, name=name, grid=(J, K // tkr, nn, nt),
                 in_specs=[pl.BlockSpec((tt, tkr), lambda j, kr, n, t: (t, kr)),
                           pl.BlockSpec((tt, tn), lambda j, kr, n, t: (t, j * nn + n))],
                 out_specs=pl.BlockSpec((None, tkr, tn), lambda j, kr, n, t: (j, kr, n)),
                 out_shape=jax.ShapeDtypeStruct((J, K, Nj), BF16),
                 scratch_shapes=[pltpu.VMEM((tkr, tn), F32)],
                 compiler_params=_params(("parallel", "parallel", "parallel", "arbitrary")))(a, b)


def _ln_fwd(x, y, g, b, alpha, *, name):
    T, D = x.shape
    tr = _rows(T, 256)

    def body(x_ref, y_ref, g_ref, b_ref, o_ref, ob_ref):
        r = alpha * x_ref[...] + y_ref[...]
        mu = jnp.mean(r, axis=-1, keepdims=True)
        xc = r - mu
        var = jnp.mean(xc * xc, axis=-1, keepdims=True)
        o = xc * lax.rsqrt(var + LN_EPS) * g_ref[...] + b_ref[...]
        o_ref[...] = o
        ob_ref[...] = o.astype(BF16)

    row = pl.BlockSpec((tr, D), lambda i: (i, 0))
    vec = pl.BlockSpec((1, D), lambda i: (0, 0))
    return _call(body, name=name, grid=(T // tr,), in_specs=[row, row, vec, vec], out_specs=(row, row),
                 out_shape=(jax.ShapeDtypeStruct((T, D), F32), jax.ShapeDtypeStruct((T, D), BF16)),
                 compiler_params=_params(("parallel",)))(x, y, g.reshape(1, D), b.reshape(1, D))


def _ln_bwd(dout, x, y, g, alpha, *, name):
    T, D = x.shape
    tr = _rows(T, 256)

    def body(do_ref, x_ref, y_ref, g_ref, drb_ref, adr_ref, dg_ref, db_ref):
        i = pl.program_id(0)
        r = alpha * x_ref[...] + y_ref[...]
        mu = jnp.mean(r, axis=-1, keepdims=True)
        xc = r - mu
        var = jnp.mean(xc * xc, axis=-1, keepdims=True)
        rstd = lax.rsqrt(var + LN_EPS)
        xhat = xc * rstd
        do = do_ref[...]
        dxh = do * g_ref[...]
        dr = rstd * (dxh - jnp.mean(dxh, axis=-1, keepdims=True)
                     - xhat * jnp.mean(dxh * xhat, axis=-1, keepdims=True))
        drb_ref[...] = dr.astype(BF16)
        adr_ref[...] = alpha * dr

        @pl.when(i == 0)
        def _():
            dg_ref[...] = jnp.zeros_like(dg_ref)
            db_ref[...] = jnp.zeros_like(db_ref)

        dg_ref[...] += jnp.sum(do * xhat, axis=0, keepdims=True)
        db_ref[...] += jnp.sum(do, axis=0, keepdims=True)

    row = pl.BlockSpec((tr, D), lambda i: (i, 0))
    vec = pl.BlockSpec((1, D), lambda i: (0, 0))
    return _call(body, name=name, grid=(T // tr,), in_specs=[row, row, row, vec], out_specs=(row, row, vec, vec),
                 out_shape=(jax.ShapeDtypeStruct((T, D), BF16), jax.ShapeDtypeStruct((T, D), F32),
                            jax.ShapeDtypeStruct((1, D), F32), jax.ShapeDtypeStruct((1, D), F32)),
                 compiler_params=_params(("arbitrary",)))(dout, x, y, g.reshape(1, D))


def _loss_head(xf, target, *, name):
    T, D = xf.shape
    tr = _rows(T, 256)

    def body(x_ref, t_ref, dy_ref, s_ref):
        i = pl.program_id(0)
        err = x_ref[...] - t_ref[...]
        dy_ref[...] = err * (1.0 / D)

        @pl.when(i == 0)
        def _():
            s_ref[...] = jnp.zeros_like(s_ref)

        s_ref[...] += jnp.sum(jnp.sum(err * err, axis=1, keepdims=True), axis=0, keepdims=True)

    row = pl.BlockSpec((tr, D), lambda i: (i, 0))
    return _call(body, name=name, grid=(T // tr,), in_specs=[row, row],
                 out_specs=(row, pl.BlockSpec((8, 128), lambda i: (0, 0))),
                 out_shape=(jax.ShapeDtypeStruct((T, D), F32), jax.ShapeDtypeStruct((8, 128), F32)),
                 compiler_params=_params(("arbitrary",)))(xf, target)


def _adamw(w, g, m, v, *, name):
    R, C = w.shape
    tr = _rows(R, max(8, (1 << 18) // C // 8 * 8))

    def body(w_ref, g_ref, m_ref, v_ref, d_ref, nm_ref, nv_ref):
        g_ = g_ref[...]
        m_ = ADAM_B1 * m_ref[...] + (1.0 - ADAM_B1) * g_
        v_ = ADAM_B2 * v_ref[...] + (1.0 - ADAM_B2) * (g_ * g_)
        m_hat = m_ / (1.0 - ADAM_B1 ** ADAM_STEP)
        v_hat = v_ / (1.0 - ADAM_B2 ** ADAM_STEP)
        d_ref[...] = -ADAM_LR * (m_hat / (jnp.sqrt(v_hat) + ADAM_EPS) + ADAM_WD * w_ref[...])
        nm_ref[...] = m_
        nv_ref[...] = v_

    blk = pl.BlockSpec((tr, C), lambda i: (i, 0))
    sds = jax.ShapeDtypeStruct((R, C), F32)
    return _call(body, name=name, grid=(R // tr,), in_specs=[blk] * 4, out_specs=(blk,) * 3,
                 out_shape=(sds,) * 3, compiler_params=_params(("parallel",)))(w, g, m, v)


def _dot_nt(a, b):
    return lax.dot_general(a, b, (((1,), (1,)), ((), ())), preferred_element_type=F32)


def _dot_tn(a, b):
    return lax.dot_general(a, b, (((0,), (0,)), ((), ())), preferred_element_type=F32)


def _band_masks():
    qi = lax.broadcasted_iota(jnp.int32, (BLK, BLK), 0)
    kj = lax.broadcasted_iota(jnp.int32, (BLK, BLK), 1)
    return qi >= kj, kj >= qi


def _attn_fwd(P, grp, d, a_out, *, name):
    T, C = P.shape
    L = T // d
    nb = L // BLK
    nhh = a_out // HEAD_BLOCK
    cb = C // HEAD_BLOCK
    q0 = grp * 3 * nhh
    scale = HEAD_DIM ** -0.5

    def body(q_ref, kc_ref, kp_ref, vc_ref, vp_ref, o_ref, l_ref):
        b = pl.program_id(1)
        mask_c, mask_p = _band_masks()
        mask_p = jnp.logical_and(mask_p, b > 0)
        for h in range(HEAD_BLOCK // HEAD_DIM):
            hs = slice(h * HEAD_DIM, (h + 1) * HEAD_DIM)
            q = q_ref[:, hs]
            s_c = jnp.where(mask_c, _dot_nt(q, kc_ref[:, hs]) * scale, NEG)
            s_p = jnp.where(mask_p, _dot_nt(q, kp_ref[:, hs]) * scale, NEG)
            m = jnp.maximum(jnp.max(s_c, axis=1, keepdims=True), jnp.max(s_p, axis=1, keepdims=True))
            p_c = jnp.exp(s_c - m)
            p_p = jnp.exp(s_p - m)
            l = jnp.sum(p_c, axis=1, keepdims=True) + jnp.sum(p_p, axis=1, keepdims=True)
            o = (jnp.dot(p_c.astype(BF16), vc_ref[:, hs], preferred_element_type=F32)
                 + jnp.dot(p_p.astype(BF16), vp_ref[:, hs], preferred_element_type=F32))
            o_ref[:, hs] = o / l
            l_ref[:, hs] = jnp.broadcast_to(m + jnp.log(l), (BLK, HEAD_DIM))

    def cur(part):
        return pl.BlockSpec((BLK, HEAD_BLOCK), lambda r, b, hh: (b, r * cb + q0 + part * nhh + hh))

    def prev(part):
        return pl.BlockSpec((BLK, HEAD_BLOCK), lambda r, b, hh: (jnp.maximum(b - 1, 0), r * cb + q0 + part * nhh + hh))

    out = pl.BlockSpec((BLK, HEAD_BLOCK), lambda r, b, hh: (b, r * nhh + hh))
    Pv = P.reshape(L, d * C)
    o, lse = _call(body, name=name, grid=(d, nb, nhh), in_specs=[cur(0), cur(1), prev(1), cur(2), prev(2)],
                   out_specs=(out, out),
                   out_shape=(jax.ShapeDtypeStruct((L, d * a_out), F32),) * 2,
                   compiler_params=_params(("parallel", "parallel", "parallel")))(Pv, Pv, Pv, Pv, Pv)
    return o.reshape(T, a_out), lse.reshape(T, a_out)


def _attn_combine(os_, lses, *, name):
    T, W = os_[0].shape
    tr = _rows(T, 256)
    n = len(os_)

    def body(*refs):
        o_refs, l_refs = refs[:n], refs[n:2 * n]
        mix_ref, lse_ref = refs[2 * n], refs[2 * n + 1]
        ls = [r[...] for r in l_refs]
        m = functools.reduce(jnp.maximum, ls)
        es = [jnp.exp(l - m) for l in ls]
        tot = functools.reduce(lambda a, b: a + b, es)
        mix = functools.reduce(lambda a, b: a + b, [(e / tot) * o[...] for e, o in zip(es, o_refs)])
        mix_ref[...] = mix
        lse_ref[...] = m + jnp.log(tot)

    blk = pl.BlockSpec((tr, W), lambda i: (i, 0))
    sds = jax.ShapeDtypeStruct((T, W), F32)
    return _call(body, name=name, grid=(T // tr,), in_specs=[blk] * (2 * n), out_specs=(blk, blk),
                 out_shape=(sds, sds), compiler_params=_params(("parallel",)))(*os_, *lses)


def _attn_bwd(P, dO, O, LSE, grp, d, a_out, *, name):
    T, C = P.shape
    L = T // d
    nb = L // BLK
    nhh = a_out // HEAD_BLOCK
    cb = C // HEAD_BLOCK
    q0 = grp * 3 * nhh
    scale = HEAD_DIM ** -0.5

    def body(q_ref, qn_ref, kc_ref, kp_ref, vc_ref, vp_ref, do_ref, don_ref, o_ref, on_ref, l_ref, ln_ref,
             dq_ref, dk_ref, dv_ref):
        b = pl.program_id(1)
        mask_c, mask_prev = _band_masks()
        mask_p = jnp.logical_and(mask_prev, b > 0)
        mask_n = jnp.logical_and(mask_prev, b < nb - 1)
        for h in range(HEAD_BLOCK // HEAD_DIM):
            hs = slice(h * HEAD_DIM, (h + 1) * HEAD_DIM)
            q, qn, kc, kp, vc, vp = (r[:, hs] for r in (q_ref, qn_ref, kc_ref, kp_ref, vc_ref, vp_ref))
            do, don = do_ref[:, hs], don_ref[:, hs]
            lse, lse_n = l_ref[:, hs], ln_ref[:, hs]
            delta = jnp.sum(do * o_ref[:, hs], axis=1, keepdims=True)
            delta_n = jnp.sum(don * on_ref[:, hs], axis=1, keepdims=True)
            dob, donb = do.astype(BF16), don.astype(BF16)
            p_c = jnp.exp(jnp.where(mask_c, _dot_nt(q, kc) * scale, NEG) - lse)
            p_p = jnp.exp(jnp.where(mask_p, _dot_nt(q, kp) * scale, NEG) - lse)
            p_n = jnp.exp(jnp.where(mask_n, _dot_nt(qn, kc) * scale, NEG) - lse_n)
            ds_c = (p_c * (_dot_nt(dob, vc) - delta) * scale).astype(BF16)
            ds_p = (p_p * (_dot_nt(dob, vp) - delta) * scale).astype(BF16)
            ds_n = (p_n * (_dot_nt(donb, vc) - delta_n) * scale).astype(BF16)
            dq = jnp.dot(ds_c, kc, preferred_element_type=F32) + jnp.dot(ds_p, kp, preferred_element_type=F32)
            dk = _dot_tn(ds_c, q) + _dot_tn(ds_n, qn)
            dv = _dot_tn(p_c.astype(BF16), dob) + _dot_tn(p_n.astype(BF16), donb)
            dq_ref[:, hs] = dq.astype(BF16)
            dk_ref[:, hs] = dk.astype(BF16)
            dv_ref[:, hs] = dv.astype(BF16)

    def pspec(part, shift):
        def imap(r, b, hh):
            return (jnp.clip(b + shift, 0, nb - 1), r * cb + q0 + part * nhh + hh)
        return pl.BlockSpec((BLK, HEAD_BLOCK), imap)

    def aspec(shift):
        return pl.BlockSpec((BLK, HEAD_BLOCK), lambda r, b, hh: (jnp.clip(b + shift, 0, nb - 1), r * nhh + hh))

    Pv = P.reshape(L, d * C)
    dOv, Ov, Lv = (t.reshape(L, d * a_out) for t in (dO, O, LSE))
    sds = jax.ShapeDtypeStruct((L, d * a_out), BF16)
    dq, dk, dv = _call(
        body, name=name, grid=(d, nb, nhh),
        in_specs=[pspec(0, 0), pspec(0, 1), pspec(1, 0), pspec(1, -1), pspec(2, 0), pspec(2, -1),
                  aspec(0), aspec(1), aspec(0), aspec(1), aspec(0), aspec(1)],
        out_specs=(aspec(0),) * 3, out_shape=(sds,) * 3,
        compiler_params=_params(("parallel", "parallel", "parallel")))(Pv, Pv, Pv, Pv, Pv, Pv, dOv, dOv, Ov, Ov, Lv, Lv)
    return [t.reshape(T, a_out) for t in (dq, dk, dv)]


def _mem_softmax(q, k, scale):
    s = _dot_nt(q, k) * scale
    e = jnp.exp(s - jnp.max(s, axis=1, keepdims=True))
    return e / jnp.sum(e, axis=1, keepdims=True)


def _mem_fwd(P, qcol, kv, *, name):
    T = P.shape[0]
    tq = _rows(T, 512)
    scale = HEAD_DIM ** -0.5

    def body(q_ref, kv_ref, o_ref):
        for h in range(MEM_HEADS):
            hs = slice(h * HEAD_DIM, (h + 1) * HEAD_DIM)
            vs = slice(MEM_WIDTH + h * HEAD_DIM, MEM_WIDTH + (h + 1) * HEAD_DIM)
            p = _mem_softmax(q_ref[:, hs].astype(BF16), kv_ref[:, hs], scale)
            o_ref[:, hs] = jnp.dot(p.astype(BF16), kv_ref[:, vs], preferred_element_type=F32)

    return _call(body, name=name, grid=(T // tq,),
                 in_specs=[pl.BlockSpec((tq, MEM_WIDTH), lambda i: (i, qcol)),
                           pl.BlockSpec((MEM_TOKENS, 2 * MEM_WIDTH), lambda i: (0, 0))],
                 out_specs=pl.BlockSpec((tq, MEM_WIDTH), lambda i: (i, 0)),
                 out_shape=jax.ShapeDtypeStruct((T, MEM_WIDTH), F32),
                 compiler_params=_params(("parallel",)))(P, kv)


def _mem_bwd(P, qcol, kv, dcat, ocol, *, name):
    T = P.shape[0]
    tq = _rows(T, 512)
    scale = HEAD_DIM ** -0.5

    def body(q_ref, kv_ref, do_ref, dq_ref, dkv_ref):
        i = pl.program_id(0)

        @pl.when(i == 0)
        def _():
            dkv_ref[...] = jnp.zeros_like(dkv_ref)

        for h in range(MEM_HEADS):
            hs = slice(h * HEAD_DIM, (h + 1) * HEAD_DIM)
            vs = slice(MEM_WIDTH + h * HEAD_DIM, MEM_WIDTH + (h + 1) * HEAD_DIM)
            q = q_ref[:, hs].astype(BF16)
            k, v = kv_ref[:, hs], kv_ref[:, vs]
            do = do_ref[:, hs].astype(BF16)
            p = _mem_softmax(q, k, scale)
            dp = _dot_nt(do, v)
            ds = (p * (dp - jnp.sum(p * dp, axis=1, keepdims=True)) * scale).astype(BF16)
            dq_ref[:, hs] = jnp.dot(ds, k, preferred_element_type=F32).astype(BF16)
            dkv_ref[:, hs] += _dot_tn(ds, q)
            dkv_ref[:, vs] += _dot_tn(p.astype(BF16), do)

    return _call(body, name=name, grid=(T // tq,),
                 in_specs=[pl.BlockSpec((tq, MEM_WIDTH), lambda i: (i, qcol)),
                           pl.BlockSpec((MEM_TOKENS, 2 * MEM_WIDTH), lambda i: (0, 0)),
                           pl.BlockSpec((tq, MEM_WIDTH), lambda i: (i, ocol))],
                 out_specs=(pl.BlockSpec((tq, MEM_WIDTH), lambda i: (i, 0)),
                            pl.BlockSpec((MEM_TOKENS, 2 * MEM_WIDTH), lambda i: (0, 0))),
                 out_shape=(jax.ShapeDtypeStruct((T, MEM_WIDTH), BF16),
                            jax.ShapeDtypeStruct((MEM_TOKENS, 2 * MEM_WIDTH), F32)),
                 compiler_params=_params(("arbitrary",)))(P, kv, dcat)


_SQRT_HALF = 0.7071067811865476
_INV_SQRT_2PI = 0.3989422804014327


def _gelu(x):
    return 0.5 * x * (1.0 + lax.erf(x * _SQRT_HALF))


def _gelu_grad(x):
    return 0.5 * (1.0 + lax.erf(x * _SQRT_HALF)) + x * (_INV_SQRT_2PI * jnp.exp(-0.5 * x * x))


def _tril():
    t = lax.broadcasted_iota(jnp.int32, (BLK, BLK), 0)
    s = lax.broadcasted_iota(jnp.int32, (BLK, BLK), 1)
    return t >= s


def _gmlp_fwd(P, w_s, bs_t, vg, vb, width, *, name):
    T = P.shape[0]
    G = width // HEAD_DIM
    tb = _rows(T, 512)

    def body(pu_ref, pv_ref, ws_ref, bs_ref, vg_ref, vb_ref, o_ref):
        u = _gelu(pu_ref[...])
        v = _gelu(pv_ref[...])
        mu = jnp.mean(v, axis=-1, keepdims=True)
        vc = v - mu
        var = jnp.mean(vc * vc, axis=-1, keepdims=True)
        vn = (vc * lax.rsqrt(var + LN_EPS) * vg_ref[...] + vb_ref[...]).astype(BF16)
        tril = _tril()
        for g in range(G):
            gs = slice(g * HEAD_DIM, (g + 1) * HEAD_DIM)
            ws = jnp.where(tril, ws_ref[g], 0.0).astype(BF16)
            bias = bs_ref[:, g:g + 1]
            for c in range(tb // BLK):
                cs = slice(c * BLK, (c + 1) * BLK)
                sg = jnp.dot(ws, vn[cs, gs], preferred_element_type=F32) + bias
                o_ref[cs, gs] = u[cs, gs] * sg

    blk = lambda col: pl.BlockSpec((tb, width), lambda i: (i, col))
    full = lambda shape: pl.BlockSpec(shape, lambda i: (0,) * len(shape))
    return _call(body, name=name, grid=(T // tb,),
                 in_specs=[blk(0), blk(1), full((G, BLK, BLK)), full((BLK, G)), full((1, width)), full((1, width))],
                 out_specs=blk(0), out_shape=jax.ShapeDtypeStruct((T, width), F32),
                 compiler_params=_params(("parallel",)))(P, P, w_s, bs_t, vg.reshape(1, width), vb.reshape(1, width))


def _gmlp_bwd(P, w_s, bs_t, vg, vb, dcat, width, *, name):
    T = P.shape[0]
    G = width // HEAD_DIM
    tb = _rows(T, 512)

    def body(pu_ref, pv_ref, ws_ref, bs_ref, vg_ref, vb_ref, dm_ref, dpu_ref, dpv_ref, dws_ref, dbs_ref, dvg_ref,
             dvb_ref, dvn_ref):
        i = pl.program_id(0)

        @pl.when(i == 0)
        def _():
            dws_ref[...] = jnp.zeros_like(dws_ref)
            dbs_ref[...] = jnp.zeros_like(dbs_ref)
            dvg_ref[...] = jnp.zeros_like(dvg_ref)
            dvb_ref[...] = jnp.zeros_like(dvb_ref)

        pu, pv = pu_ref[...], pv_ref[...]
        u = _gelu(pu)
        v = _gelu(pv)
        mu = jnp.mean(v, axis=-1, keepdims=True)
        vc = v - mu
        var = jnp.mean(vc * vc, axis=-1, keepdims=True)
        rstd = lax.rsqrt(var + LN_EPS)
        xhat = vc * rstd
        vn = (xhat * vg_ref[...] + vb_ref[...]).astype(BF16)
        dm = dm_ref[...]
        tril = _tril()
        lane = lax.broadcasted_iota(jnp.int32, (BLK, BLK), 1)
        dbs = jnp.zeros((BLK, BLK), F32)
        for g in range(G):
            gs = slice(g * HEAD_DIM, (g + 1) * HEAD_DIM)
            ws = jnp.where(tril, ws_ref[g], 0.0).astype(BF16)
            bias = bs_ref[:, g:g + 1]
            dws = jnp.zeros((BLK, BLK), F32)
            rs = jnp.zeros((BLK, 1), F32)
            for c in range(tb // BLK):
                cs = slice(c * BLK, (c + 1) * BLK)
                vn_cg = vn[cs, gs]
                sg = jnp.dot(ws, vn_cg, preferred_element_type=F32) + bias
                dm_cg = dm[cs, gs]
                dpu_ref[cs, gs] = (dm_cg * sg * _gelu_grad(pu[cs, gs])).astype(BF16)
                dsg = dm_cg * u[cs, gs]
                dsgb = dsg.astype(BF16)
                dvn_ref[cs, gs] = _dot_tn(ws, dsgb)
                dws = dws + _dot_nt(dsgb, vn_cg)
                rs = rs + jnp.sum(dsg, axis=1, keepdims=True)
            dws_ref[g] += jnp.where(tril, dws, 0.0)
            dbs = dbs + jnp.where(lane == g, rs, 0.0)
        dbs_ref[...] += dbs
        dvn = dvn_ref[...]
        dxh = dvn * vg_ref[...]
        dv = rstd * (dxh - jnp.mean(dxh, axis=-1, keepdims=True)
                     - xhat * jnp.mean(dxh * xhat, axis=-1, keepdims=True))
        dpv_ref[...] = (dv * _gelu_grad(pv)).astype(BF16)
        dvg_ref[...] += jnp.sum(dvn * xhat, axis=0, keepdims=True)
        dvb_ref[...] += jnp.sum(dvn, axis=0, keepdims=True)

    blk = lambda col: pl.BlockSpec((tb, width), lambda i: (i, col))
    full = lambda shape: pl.BlockSpec(shape, lambda i: (0,) * len(shape))
    return _call(body, name=name, grid=(T // tb,),
                 in_specs=[blk(0), blk(1), full((G, BLK, BLK)), full((BLK, G)), full((1, width)), full((1, width)),
                           blk(0)],
                 out_specs=(blk(0), blk(0), full((G, BLK, BLK)), full((BLK, BLK)), full((1, width)), full((1, width))),
                 out_shape=(jax.ShapeDtypeStruct((T, width), BF16), jax.ShapeDtypeStruct((T, width), BF16),
                            jax.ShapeDtypeStruct((G, BLK, BLK), F32), jax.ShapeDtypeStruct((BLK, BLK), F32),
                            jax.ShapeDtypeStruct((1, width), F32), jax.ShapeDtypeStruct((1, width), F32)),
                 scratch_shapes=[pltpu.VMEM((tb, width), F32)],
                 compiler_params=_params(("arbitrary",)))(P, P, w_s, bs_t, vg.reshape(1, width), vb.reshape(1, width), dcat)


def _place():
    x, y, c = lax.axis_index("x"), lax.axis_index("y"), lax.axis_index("c")
    chips = [(1 - x, y), (x, 1 - y), (1 - x, 1 - y)]
    return x, y, c, 2 * x + y, chips, [2 * px + py for px, py in chips]


def _half(ref, c, rows):
    return ref.at[pl.ds(c * (rows // 2), rows // 2)]


def _place_shard(w, chip, dtype, *, name):
    R, C = w.shape
    tr = _rows(R, max(8, (1 << 19) // C // 8 * 8)) if R % 8 == 0 else R

    def body(chip_ref, w_ref, o_ref):
        o_ref[...] = w_ref[...].astype(dtype)

    grid_spec = pltpu.PrefetchScalarGridSpec(
        num_scalar_prefetch=1, grid=(R // tr,), in_specs=[pl.BlockSpec((tr, C), lambda i, s: (i, 0))],
        out_specs=pl.BlockSpec((None, tr, C), lambda i, s: (s[0], i, 0)))
    return _call(body, name=name, grid_spec=grid_spec, out_shape=jax.ShapeDtypeStruct((4, R, C), dtype),
                 compiler_params=_params(("parallel",)))(chip, w)


def _pair_exchange(grads, *, name):
    n = len(grads)

    def body(*refs):
        g, theirs = refs[:n], refs[n:2 * n]
        send, recv = refs[2 * n:]
        x, y, c, _, _, _ = _place()
        copies = []
        for t in range(n):
            h = grads[t].shape[1] // 2
            rc = pltpu.make_async_remote_copy(src_ref=g[t].at[:, pl.ds((1 - c) * h, h)], dst_ref=theirs[t],
                                              send_sem=send.at[t], recv_sem=recv.at[t], device_id=(x, y, 1 - c),
                                              device_id_type=MESH)
            rc.start()
            copies.append(rc)
        for rc in copies:
            rc.wait()

    halves = [jax.ShapeDtypeStruct((4, g.shape[1] // 2, g.shape[2]), g.dtype) for g in grads]
    return _call(body, name=name, in_specs=[HBM] * n, out_specs=[HBM] * n, out_shape=halves,
                 scratch_shapes=[pltpu.SemaphoreType.DMA((n,))] * 2)(*grads)


def _half_swap(fulls, *, name):
    n = len(fulls)

    def body(*refs):
        full = refs[n:2 * n]
        send, recv = refs[2 * n:]
        x, y, c, _, _, _ = _place()
        copies = []
        for t in range(n):
            h = fulls[t].shape[0] // 2
            mine = full[t].at[pl.ds(c * h, h)]
            rc = pltpu.make_async_remote_copy(src_ref=mine, dst_ref=mine, send_sem=send.at[t], recv_sem=recv.at[t],
                                              device_id=(x, y, 1 - c), device_id_type=MESH)
            rc.start()
            copies.append(rc)
        for rc in copies:
            rc.wait()

    return _call(body, name=name, in_specs=[HBM] * n, out_specs=[HBM] * n,
                 out_shape=[jax.ShapeDtypeStruct(f.shape, f.dtype) for f in fulls],
                 input_output_aliases={t: t for t in range(n)},
                 scratch_shapes=[pltpu.SemaphoreType.DMA((n,))] * 2)(*fulls)


SEM = pl.BlockSpec(memory_space=pltpu.SEMAPHORE)
EFFECT = pltpu.SideEffectType.DATAFLOW_SIDE_EFFECTING


def _hbm(a):
    return pltpu.with_memory_space_constraint(a, pltpu.HBM)


def _gather_windows(shapes):
    split = [s[1] % 16 == 0 for s in shapes]

    def window(ref, t, chip_idx, core):
        w = ref.at[chip_idx]
        return _half(w, core, shapes[t][1]) if split[t] else w

    return split, window


def _gather_start(bufs, *, name):
    n = len(bufs)
    split, window = _gather_windows([b.shape for b in bufs])

    def body(*refs):
        b = refs[:n]
        send, recv = refs[n], refs[n + 1]
        token = refs[2 * n + 2]
        x, y, c, j, chips, pj = _place()
        for t in range(n):
            for k in range(3):
                pltpu.make_async_remote_copy(src_ref=window(b[t], t, j, c), dst_ref=window(b[t], t, j, c),
                                             send_sem=send.at[3 * t + k], recv_sem=recv.at[3 * t + k],
                                             device_id=(*chips[k], c), device_id_type=MESH).start()
        token[...] = jnp.zeros_like(token)

    outs = _call(body, name=name, in_specs=[HBM] * n,
                 out_specs=(SEM, SEM, *([HBM] * n), pl.BlockSpec(memory_space=pltpu.VMEM)),
                 out_shape=(pltpu.SemaphoreType.DMA((3 * n,)), pltpu.SemaphoreType.DMA((3 * n,)),
                            *[pltpu.HBM(b.shape, b.dtype) for b in bufs], jax.ShapeDtypeStruct((8, 128), F32)),
                 input_output_aliases={t: 2 + t for t in range(n)},
                 compiler_params=pltpu.CompilerParams(has_side_effects=EFFECT))(*[_hbm(b) for b in bufs])
    return outs[0], outs[1], list(outs[2:2 + n]), outs[2 + n]


def _gather_wait(send, recv, bufs, after, *, name):
    n = len(bufs)
    split, window = _gather_windows([b.shape for b in bufs])

    def body(*refs):
        b = refs[:n]
        send, recv = refs[n], refs[n + 1]
        x, y, c, j, chips, pj = _place()
        for t in range(n):
            for k in range(3):
                out = pltpu.make_async_remote_copy(src_ref=window(b[t], t, j, c), dst_ref=window(b[t], t, j, c),
                                                   send_sem=send.at[3 * t + k], recv_sem=recv.at[3 * t + k],
                                                   device_id=(*chips[k], c), device_id_type=MESH)
                out.wait_send()
                back = pltpu.make_async_remote_copy(src_ref=window(b[t], t, pj[k], c), dst_ref=window(b[t], t, pj[k], c),
                                                    send_sem=send.at[3 * t + k], recv_sem=recv.at[3 * t + k],
                                                    device_id=(*chips[k], c), device_id_type=MESH)
                back.wait_recv()

    outs = _call(body, name=name, in_specs=[HBM] * n + [SEM, SEM, pl.BlockSpec(memory_space=pl.ANY)],
                 out_specs=[HBM] * n, out_shape=[pltpu.HBM(b.shape, b.dtype) for b in bufs],
                 input_output_aliases={t: t for t in range(n)},
                 compiler_params=pltpu.CompilerParams(has_side_effects=EFFECT))(*bufs, send, recv, after)
    return list(outs)


def _gather_pass(bufs, *, name):
    n = len(bufs)
    split, window = _gather_windows([b.shape for b in bufs])
    idx = [t for t in range(n) if split[t]]

    def body(*refs):
        b = refs[n:2 * n]
        send, recv = refs[2 * n:]
        x, y, c, j, chips, pj = _place()

        def d2d(u, t, k, core):
            w = window(b[t], t, pj[k], core)
            return pltpu.make_async_remote_copy(src_ref=w, dst_ref=w, send_sem=send.at[3 * u + k],
                                                recv_sem=recv.at[3 * u + k], device_id=(x, y, 1 - c),
                                                device_id_type=MESH)

        sent = [d2d(u, t, k, c) for u, t in enumerate(idx) for k in range(3)]
        for cp in sent:
            cp.start()
        for u, t in enumerate(idx):
            for k in range(3):
                d2d(u, t, k, 1 - c).wait_recv()
        for cp in sent:
            cp.wait_send()

    return _call(body, name=name, in_specs=[HBM] * n, out_specs=[HBM] * n,
                 out_shape=[jax.ShapeDtypeStruct(b.shape, b.dtype) for b in bufs],
                 input_output_aliases={t: t for t in range(n)},
                 scratch_shapes=[pltpu.SemaphoreType.DMA((3 * len(idx),))] * 2)(*bufs)


def _chip_start(pairs, *, name):
    n = len(pairs)
    lands = [lax.empty((3,) + p.shape[1:], p.dtype) for p in pairs]

    def body(*refs):
        s, r = refs[:n], refs[n:2 * n]
        send, recv = refs[2 * n], refs[2 * n + 1]
        token = refs[4 * n + 2]
        x, y, c, j, chips, pj = _place()
        for t in range(n):
            for k in range(3):
                pltpu.make_async_remote_copy(src_ref=s[t].at[pj[k]], dst_ref=r[t].at[k], send_sem=send.at[3 * t + k],
                                             recv_sem=recv.at[3 * t + k], device_id=(*chips[k], c),
                                             device_id_type=MESH).start()
        token[...] = jnp.zeros_like(token)

    outs = _call(body, name=name, in_specs=[HBM] * (2 * n),
                 out_specs=(SEM, SEM, *([HBM] * (2 * n)), pl.BlockSpec(memory_space=pltpu.VMEM)),
                 out_shape=(pltpu.SemaphoreType.DMA((3 * n,)), pltpu.SemaphoreType.DMA((3 * n,)),
                            *[pltpu.HBM(a.shape, a.dtype) for a in list(pairs) + lands],
                            jax.ShapeDtypeStruct((8, 128), F32)),
                 input_output_aliases={t: 2 + t for t in range(2 * n)},
                 compiler_params=pltpu.CompilerParams(has_side_effects=EFFECT))(*[_hbm(a) for a in list(pairs) + lands])
    return outs[0], outs[1], list(outs[2:2 + n]), list(outs[2 + n:2 + 2 * n]), outs[2 + 2 * n]


def _chip_wait(send, recv, pairs, lands, after, *, name):
    n = len(pairs)

    def body(*refs):
        s, r = refs[:n], refs[n:2 * n]
        send, recv = refs[2 * n], refs[2 * n + 1]
        x, y, c, j, chips, pj = _place()
        for t in range(n):
            for k in range(3):
                cp = pltpu.make_async_remote_copy(src_ref=s[t].at[pj[k]], dst_ref=r[t].at[k], send_sem=send.at[3 * t + k],
                                                  recv_sem=recv.at[3 * t + k], device_id=(*chips[k], c),
                                                  device_id_type=MESH)
                cp.wait_send()
                cp.wait_recv()

    outs = _call(body, name=name, in_specs=[HBM] * (2 * n) + [SEM, SEM, pl.BlockSpec(memory_space=pl.ANY)],
                 out_specs=[HBM] * (2 * n), out_shape=[pltpu.HBM(a.shape, a.dtype) for a in list(pairs) + list(lands)],
                 input_output_aliases={t: t for t in range(2 * n)},
                 compiler_params=pltpu.CompilerParams(has_side_effects=EFFECT))(*pairs, *lands, send, recv, after)
    return list(outs[:n]), list(outs[n:])


def _all_reduce_small(packed, *, name):
    R, C = packed.shape

    def body(p_ref, o_ref, slots, send, recv, lsem):
        x, y, c = lax.axis_index("x"), lax.axis_index("y"), lax.axis_index("c")
        me = 4 * x + 2 * y + c
        lc = pltpu.make_async_copy(p_ref, slots.at[me], lsem.at[0])
        lc.start()
        copies = []
        for rel in range(1, 8):
            fx, fy, fc = (rel >> 2) & 1, (rel >> 1) & 1, rel & 1
            to = (1 - x if fx else x, 1 - y if fy else y, 1 - c if fc else c)
            cp = pltpu.make_async_remote_copy(src_ref=p_ref, dst_ref=slots.at[me], send_sem=send.at[rel - 1],
                                              recv_sem=recv.at[rel - 1], device_id=to, device_id_type=MESH)
            cp.start()
            copies.append((cp, 4 * to[0] + 2 * to[1] + to[2]))
        for rel, (cp, frm) in enumerate(copies):
            cp.wait_send()
            pltpu.make_async_remote_copy(src_ref=p_ref, dst_ref=slots.at[frm], send_sem=send.at[rel],
                                         recv_sem=recv.at[rel], device_id=(x, y, c), device_id_type=MESH).wait_recv()
        lc.wait()
        acc = slots[0]
        for dev in range(1, 8):
            acc = acc + slots[dev]
        o_ref[...] = acc

    return _call(body, name=name, in_specs=[pl.BlockSpec(memory_space=pltpu.VMEM)],
                 out_specs=pl.BlockSpec(memory_space=pltpu.VMEM), out_shape=jax.ShapeDtypeStruct((R, C), F32),
                 scratch_shapes=[pltpu.VMEM((8, R, C), F32), pltpu.SemaphoreType.DMA((7,)),
                                 pltpu.SemaphoreType.DMA((7,)), pltpu.SemaphoreType.DMA((1,))],
                 compiler_params=pltpu.CompilerParams(vmem_limit_bytes=V7X_VMEM_LIMIT))(packed)


def _pair_sum(grad, theirs, core, *, name):
    J, H, C = theirs.shape
    tr = _rows(H, max(8, (1 << 19) // C // 8 * 8))

    def body(core_ref, a_ref, b_ref, o_ref):
        o_ref[...] = (a_ref[...].astype(F32) + b_ref[...].astype(F32)).astype(BF16)

    blk = pl.BlockSpec((None, tr, C), lambda j, i, s: (j, i, 0))
    mine = pl.BlockSpec((None, None, tr, C), lambda j, i, s: (j, s[0], i, 0))
    grid_spec = pltpu.PrefetchScalarGridSpec(num_scalar_prefetch=1, grid=(J, H // tr), in_specs=[mine, blk],
                                             out_specs=blk)
    return _call(body, name=name, grid_spec=grid_spec, out_shape=jax.ShapeDtypeStruct((J, H, C), BF16),
                 compiler_params=_params(("parallel", "parallel")))(core, grad.reshape(J, 2, H, C), theirs)


def _chip_sum(pairs, slots, place, *, name):
    _, H, C = slots.shape
    tr = _rows(H, max(8, (1 << 19) // C // 8 * 8))
    nr = H // tr

    def body(place_ref, s0, s1, s2, s3, o_ref):
        o_ref[...] = ((s0[...].astype(F32) + s1[...].astype(F32)) + s2[...].astype(F32)) + s3[...].astype(F32)

    def slot(k):
        return pl.BlockSpec((None, tr, C), lambda i, s: (k, i, 0))

    own = pl.BlockSpec((None, tr, C), lambda i, s: (s[0], i, 0))
    grid_spec = pltpu.PrefetchScalarGridSpec(
        num_scalar_prefetch=1, grid=(nr,), in_specs=[own, slot(0), slot(1), slot(2)],
        out_specs=pl.BlockSpec((tr, C), lambda i, s: (s[1] * nr + i, 0)))
    return _call(body, name=name, grid_spec=grid_spec, out_shape=jax.ShapeDtypeStruct((2 * H, C), F32),
                 compiler_params=_params(("parallel",)))(place, pairs, slots, slots, slots)


def _reduce_scatter_begin(grads, place, tag):
    theirs = _pair_exchange(grads, name=f"rs_pair_{tag}")
    pairs = [_pair_sum(g, t, place[1:], name=f"rs_pair_sum_{tag}_{i}") for i, (g, t) in enumerate(zip(grads, theirs))]
    return _chip_start(pairs, name=f"rs_chip_start_{tag}")


def _reduce_scatter_end(started, after, place, tag):
    send, recv, pairs, lands, _ = started
    pairs, slots = _chip_wait(send, recv, pairs, lands, after, name=f"rs_chip_wait_{tag}")
    fulls = [_chip_sum(p, s, place, name=f"rs_chip_sum_{tag}_{i}") for i, (p, s) in enumerate(zip(pairs, slots))]
    return _half_swap(fulls, name=f"rs_swap_{tag}")


def kernel(x, mem, w_in_a, w_in_b, w_s, b_s, vnorm_g, vnorm_b, w_mem_kv, w_out, ln1_g, ln1_b, w_ff1, w_ff2, ln2_g, ln2_b, loss_target, m_w_in_a, m_w_in_b, m_w_s, m_b_s, m_vnorm_g, m_vnorm_b, m_w_mem_kv, m_w_out, m_ln1_g, m_ln1_b, m_w_ff1, m_w_ff2, m_ln2_g, m_ln2_b, v_w_in_a, v_w_in_b, v_w_s, v_b_s, v_vnorm_g, v_vnorm_b, v_w_mem_kv, v_w_out, v_ln1_g, v_ln1_b, v_w_ff1, v_w_ff2, v_ln2_g, v_ln2_b):
    T, D = x.shape[1], x.shape[2]
    depth = w_ff1.shape[0]
    alpha = (2.0 * depth) ** 0.25
    a_out = (D // 256) * HEAD_DIM
    a_cols = len(A_PAIRS) * 3 * a_out
    b_width = (D // 256) * HEAD_DIM
    G = b_width // HEAD_DIM
    assert a_out % HEAD_BLOCK == 0 and T % (BLK * A_PAIRS[-1][1]) == 0

    xf = x.reshape(T, D)
    mem_b = mem.reshape(MEM_TOKENS, D).astype(BF16)
    target = loss_target.reshape(T, D)
    c_idx = lax.axis_index("x") * 2 + lax.axis_index("y")
    place = jnp.stack([c_idx, lax.axis_index("c")]).astype(jnp.int32)

    def gather_begin(i):
        jl, tag = i // 2, "a" if i % 2 == 0 else "b"
        w_in_l = (w_in_a if i % 2 == 0 else w_in_b)[jl]
        bufs = [_place_shard(w, place[:1], BF16, name=f"place_{tag}_{n_}")
                for n_, w in enumerate((w_in_l, w_mem_kv[i], w_out[i], w_ff1[i], w_ff2[i]))]
        if i % 2:
            bufs += [_place_shard(v_[jl].reshape(1, -1), place[:1], F32, name="place_vnorm") for v_ in (vnorm_g, vnorm_b)]
        return _gather_start(bufs, name=f"gather_start_{tag}")

    def group_view(P, g, d):
        return (P, g) if d == 1 else (P[:, g * 3 * a_out:(g + 1) * 3 * a_out], 0)

    saved = []
    started = gather_begin(0)
    for i in range(depth):
        jl = i // 2
        is_a = i % 2 == 0
        tag = "a" if is_a else "b"
        send, recv, bufs, _ = started
        bufs = _gather_wait(send, recv, bufs, xf, name=f"gather_wait_{tag}")
        gathered = _gather_pass(bufs, name=f"gather_pass_{tag}")
        win, wkv, wout, wff1, wff2 = gathered[:5]
        wkv = wkv.reshape(1, D, 2 * MEM_WIDTH)
        wff2 = wff2.reshape(1, 4 * D, D)
        if i + 1 < depth:
            started = gather_begin(i + 1)
            xb = (xf + started[3][0, 0]).astype(BF16)
        else:
            xb = xf.astype(BF16)
        P = _mm_nn(xb, win, out_dtype=BF16 if is_a else F32, name=f"proj_in_{'a' if is_a else 'b'}")
        kv = _mm_nn(mem_b, wkv, name="proj_kv")
        if is_a:
            outs = [_attn_fwd(*group_view(P, g, d), d, a_out, name=f"attn_fwd_d{d}") for g, (_, d) in enumerate(A_PAIRS)]
            mix, lse = _attn_combine([o for o, _ in outs], [l for _, l in outs], name="attn_combine")
            qcol = a_cols // MEM_WIDTH
            extra = (lse,)
        else:
            vg_full = gathered[5].reshape(-1)
            vb_full = gathered[6].reshape(-1)
            bs_t = b_s[jl].T
            mix = _gmlp_fwd(P, w_s[jl], bs_t, vg_full, vb_full, b_width, name="gmlp_fwd")
            qcol = 2 * b_width // MEM_WIDTH
            extra = (vg_full, vb_full, bs_t)
        mem_o = _mem_fwd(P, qcol, kv, name=f"mem_fwd_{'a' if is_a else 'b'}")
        cat = jnp.concatenate([mix.astype(BF16), mem_o.astype(BF16)], axis=1)
        y = _mm_nn(cat, wout, out_dtype=F32, name="proj_out")
        x1, x1b = _ln_fwd(xf, y, ln1_g[i], ln1_b[i], alpha, name="ln_fwd")
        a_pre, hid = _mm_nn(x1b, wff1, epi="relu2", name="ff1")
        f = _mm_nn(hid, wff2, out_dtype=F32, name="ff2")
        x2, _ = _ln_fwd(x1, f, ln2_g[i], ln2_b[i], alpha, name="ln_fwd")
        saved.append(dict(xf=xf, xb=xb, P=P, kv=kv, mix=mix, cat=cat, y=y, x1=x1, x1b=x1b, a_pre=a_pre, hid=hid, f=f,
                          extra=extra, w=(win, wkv, wout, wff1, wff2), qcol=qcol))
        xf = x2

    dx, sq = _loss_head(xf, target, name="loss_head")
    loss = lax.psum(sq[0, 0] * (0.5 / D), ("x", "y", "c"))

    g_big = dict(w_in_a=[None] * ((depth + 1) // 2), w_in_b=[None] * (depth // 2), w_mem_kv=[None] * depth,
                 w_out=[None] * depth, w_ff1=[None] * depth, w_ff2=[None] * depth)
    small = {k: [None] * depth for k in ("ln1_g", "ln1_b", "ln2_g", "ln2_b")}
    small_b = {k: [None] * (depth // 2) for k in ("w_s", "b_s", "vnorm_g", "vnorm_b")}
    def scatter_end(started, layer, after):
        red = _reduce_scatter_end(started, after, place, "a" if layer % 2 == 0 else "b")
        g_big["w_in_a" if layer % 2 == 0 else "w_in_b"][layer // 2] = red[0]
        for k, r in zip(("w_mem_kv", "w_out", "w_ff1", "w_ff2"), red[1:]):
            g_big[k][layer] = r

    started = None
    for i in reversed(range(depth)):
        jl = i // 2
        is_a = i % 2 == 0
        s = saved[i]
        win, wkv, wout, wff1, wff2 = s["w"]
        tag = "a" if is_a else "b"
        if started is not None:
            dx = dx + started[4][0, 0]
        d_f, adr2, dg2, db2 = _ln_bwd(dx, s["x1"], s["f"], ln2_g[i], alpha, name="ln_bwd")
        small["ln2_g"][i], small["ln2_b"][i] = dg2, db2
        gw_ff2 = _mm_tn(s["hid"], d_f, 1, name="grad_ff2").reshape(4, D, D)
        da = _mm_nt(d_f, wff2, epi="drelu2", extra=s["a_pre"], name="ff2_bwd")
        gw_ff1 = _mm_tn(s["x1b"], da, 4, name="grad_ff1")
        dx1 = _mm_nt(da, wff1, epi="resid", extra=adr2, name="ff1_bwd")
        d_y, adr1, dg1, db1 = _ln_bwd(dx1, s["xf"], s["y"], ln1_g[i], alpha, name="ln_bwd")
        small["ln1_g"][i], small["ln1_b"][i] = dg1, db1
        gw_out = _mm_tn(s["cat"], d_y, 4, name="grad_out")
        dcat = _mm_nt(d_y, wout, out_dtype=F32, name="proj_out_bwd")
        ocol = dcat.shape[1] // MEM_WIDTH - 1
        dmq, dkv = _mem_bwd(s["P"], s["qcol"], s["kv"], dcat, ocol, name=f"mem_bwd_{tag}")
        gw_kv = _mm_tn(mem_b, dkv.astype(BF16), 1, name="grad_kv").reshape(4, D // 4, 2 * MEM_WIDTH)
        if is_a:
            (lse,) = s["extra"]
            dmix = dcat[:, :a_out]
            parts = []
            for g, (_, d) in enumerate(A_PAIRS):
                Pg, g0 = group_view(s["P"], g, d)
                parts += _attn_bwd(Pg, dmix, s["mix"], lse, g0, d, a_out, name=f"attn_bwd_d{d}")
            dP = jnp.concatenate(parts + [dmq], axis=1)
        else:
            vg_full, vb_full, bs_t = s["extra"]
            dpu, dpv, dws, dbs, dvg, dvb = _gmlp_bwd(s["P"], w_s[jl], bs_t, vg_full, vb_full, dcat, b_width,
                                                     name="gmlp_bwd")
            small_b["w_s"][jl], small_b["b_s"][jl] = dws, dbs[:, :G].T
            small_b["vnorm_g"][jl], small_b["vnorm_b"][jl] = dvg, dvb
            dP = jnp.concatenate([dpu, dpv, dmq], axis=1)
        gw_in = _mm_tn(s["xb"], dP, 4, name=f"grad_in_{tag}")
        dx = _mm_nt(dP, win, epi="resid", extra=adr1, name=f"proj_in_bwd_{tag}")
        if started is not None:
            scatter_end(started, i + 1, dx)
        started = _reduce_scatter_begin([gw_in, gw_kv, gw_out, gw_ff1, gw_ff2], place, tag)
    scatter_end(started, 0, dx)
    grad_x = dx.reshape(x.shape)

    nb_layers = depth // 2
    pieces = ([jnp.stack(small_b["w_s"]).reshape(-1, 128), jnp.stack(small_b["b_s"]).reshape(-1, 128),
               jnp.stack(small_b["vnorm_g"]).reshape(-1, 128), jnp.stack(small_b["vnorm_b"]).reshape(-1, 128)]
              + [jnp.stack(small[k]).reshape(-1, 128) for k in ("ln1_g", "ln1_b", "ln2_g", "ln2_b")])
    sizes = [p.shape[0] for p in pieces]
    pad = (-sum(sizes)) % 8
    packed = jnp.concatenate(pieces + ([jnp.zeros((pad, 128), F32)] if pad else []), axis=0)
    summed = _all_reduce_small(packed, name="all_reduce_small")
    offs = [0]
    for n_ in sizes:
        offs.append(offs[-1] + n_)
    sp = [summed[offs[k]:offs[k + 1]] for k in range(len(sizes))]
    vshard = vnorm_g.shape[1]
    g_small = dict(
        w_s=sp[0].reshape(w_s.shape), b_s=sp[1].reshape(b_s.shape),
        vnorm_g=lax.dynamic_slice_in_dim(sp[2].reshape(nb_layers, -1), c_idx * vshard, vshard, axis=1),
        vnorm_b=lax.dynamic_slice_in_dim(sp[3].reshape(nb_layers, -1), c_idx * vshard, vshard, axis=1),
        ln1_g=sp[4].reshape(ln1_g.shape), ln1_b=sp[5].reshape(ln1_b.shape),
        ln2_g=sp[6].reshape(ln2_g.shape), ln2_b=sp[7].reshape(ln2_b.shape))

    names = ["w_in_a", "w_in_b", "w_s", "b_s", "vnorm_g", "vnorm_b", "w_mem_kv", "w_out", "ln1_g", "ln1_b", "w_ff1",
             "w_ff2", "ln2_g", "ln2_b"]
    ws = dict(w_in_a=w_in_a, w_in_b=w_in_b, w_s=w_s, b_s=b_s, vnorm_g=vnorm_g, vnorm_b=vnorm_b, w_mem_kv=w_mem_kv,
              w_out=w_out, ln1_g=ln1_g, ln1_b=ln1_b, w_ff1=w_ff1, w_ff2=w_ff2, ln2_g=ln2_g, ln2_b=ln2_b)
    ms = dict(w_in_a=m_w_in_a, w_in_b=m_w_in_b, w_s=m_w_s, b_s=m_b_s, vnorm_g=m_vnorm_g, vnorm_b=m_vnorm_b,
              w_mem_kv=m_w_mem_kv, w_out=m_w_out, ln1_g=m_ln1_g, ln1_b=m_ln1_b, w_ff1=m_w_ff1, w_ff2=m_w_ff2,
              ln2_g=m_ln2_g, ln2_b=m_ln2_b)
    vs = dict(w_in_a=v_w_in_a, w_in_b=v_w_in_b, w_s=v_w_s, b_s=v_b_s, vnorm_g=v_vnorm_g, vnorm_b=v_vnorm_b,
              w_mem_kv=v_w_mem_kv, w_out=v_w_out, ln1_g=v_ln1_g, ln1_b=v_ln1_b, w_ff1=v_w_ff1, w_ff2=v_w_ff2,
              ln2_g=v_ln2_g, ln2_b=v_ln2_b)
    grads, deltas, new_m, new_v = {}, {}, {}, {}
    for k in g_big:
        g = jnp.stack(g_big[k]).reshape(ws[k].shape)
        cols = ws[k].shape[-1]
        d_, m_, v_ = _adamw(ws[k].reshape(-1, cols), g.reshape(-1, cols), ms[k].reshape(-1, cols),
                            vs[k].reshape(-1, cols), name=f"adamw_{k}")
        grads[k] = g
        deltas[k], new_m[k], new_v[k] = (t.reshape(ws[k].shape) for t in (d_, m_, v_))
    small_names = [k for k in names if k not in g_big]

    def pack(tree):
        flat = jnp.concatenate([tree[k].reshape(-1) for k in small_names])
        padn = (-flat.shape[0]) % 1024
        return jnp.pad(flat, (0, padn)).reshape(-1, 128)

    d_, m_, v_ = _adamw(pack(ws), pack(g_small), pack(ms), pack(vs), name="adamw_small")
    off = 0
    for k in small_names:
        n_ = ws[k].size
        grads[k] = g_small[k]
        deltas[k], new_m[k], new_v[k] = (t.reshape(-1)[off:off + n_].reshape(ws[k].shape) for t in (d_, m_, v_))
        off += n_

    return (loss, grad_x, *[grads[k] for k in names], *[deltas[k] for k in names], *[new_m[k] for k in names],
            *[new_v[k] for k in names])
```

```python
import functools
import math

import jax
import jax.numpy as jnp
from jax import lax
from jax.experimental import pallas as pl
from jax.experimental.pallas import tpu as pltpu

F32 = jnp.float32
BF16 = jnp.bfloat16

HEAD_DIM = 128
A_PAIRS = ((128, 1), (512, 4), (2048, 16))
BLK = 128
HEAD_BLOCK = 512
MEM_TOKENS = 256
MEM_HEADS = 4
MEM_WIDTH = MEM_HEADS * HEAD_DIM
LN_EPS = 1e-5
ADAM_LR, ADAM_B1, ADAM_B2, ADAM_EPS, ADAM_WD, ADAM_STEP = 0.001, 0.9, 0.999, 1e-08, 0.01, 10
NEG = -1e30
V7X_VMEM_LIMIT = 56 * 1024 * 1024
V7X_MATMUL_VMEM_BUDGET = 40 * 1024 * 1024
MESH = pl.DeviceIdType.MESH
HBM = pl.BlockSpec(memory_space=pltpu.HBM)


def _call(body, **kw):
    return pl.pallas_call(body, **kw)


def _params(sem):
    return pltpu.CompilerParams(dimension_semantics=sem, vmem_limit_bytes=V7X_VMEM_LIMIT)


def _tile(n, cap):
    best = 0
    for t in range(128, min(n, cap) + 1, 128):
        if n % t == 0:
            best = t
    if best == 0 or (best < 512 and n <= 2560):
        return n
    return best


def _depth(k, per_unit_bytes, fixed_bytes):
    t = _tile(k, 2048)
    while t > 512 and 2 * t * per_unit_bytes + fixed_bytes > V7X_MATMUL_VMEM_BUDGET:
        smaller = _tile(k, t // 2)
        if smaller >= t:
            break
        t = smaller
    return t


def _rows(n, cap):
    best = 8
    for t in range(8, min(n, cap) + 1, 8):
        if n % t == 0:
            best = t
    return best


def _epilogue(epi, acc, extra_ref, out_refs):
    if epi == "plain":
        out_refs[0][...] = acc.astype(out_refs[0].dtype)
    elif epi == "relu2":
        out_refs[0][...] = acc
        r = jnp.maximum(acc, 0.0)
        out_refs[1][...] = (r * r).astype(out_refs[1].dtype)
    elif epi == "drelu2":
        out_refs[0][...] = (acc * (2.0 * jnp.maximum(extra_ref[...], 0.0))).astype(out_refs[0].dtype)
    elif epi == "resid":
        out_refs[0][...] = acc + extra_ref[...]
    else:
        raise ValueError(epi)


def _mm_nn(a, w, *, epi="plain", out_dtype=BF16, extra=None, name):
    M, K = a.shape
    J, K2, Nj = w.shape
    assert K == K2
    tn = _tile(Nj, 1024)
    tm = _tile(M, 1024 if tn <= 1024 else 512)
    n_out = 2 if epi == "relu2" else 1
    has_extra = extra is not None
    out_bytes = 12 if epi == "relu2" else (8 if epi == "resid" or out_dtype == F32 else 4)
    tk = _depth(K, 2 * (tm + tn), tm * tn * (4 + out_bytes + 8 * has_extra))
    nn, nk = Nj // tn, K // tk

    def body(*refs):
        a_ref, w_ref = refs[0], refs[1]
        extra_ref = refs[2] if has_extra else None
        outs = refs[2 + has_extra: 2 + has_extra + n_out]
        acc_ref = refs[-1]
        k = pl.program_id(3)

        @pl.when(k == 0)
        def _():
            acc_ref[...] = jnp.zeros_like(acc_ref)

        acc_ref[...] += jnp.dot(a_ref[...], w_ref[...], preferred_element_type=F32)

        @pl.when(k == nk - 1)
        def _():
            _epilogue(epi, acc_ref[...], extra_ref, outs)

    omap = lambda i, j, n, k: (i, j * nn + n)
    in_specs = [pl.BlockSpec((tm, tk), lambda i, j, n, k: (i, k)),
                pl.BlockSpec((None, tk, tn), lambda i, j, n, k: (j, k, n))]
    args = [a, w]
    if has_extra:
        in_specs.append(pl.BlockSpec((tm, tn), omap))
        args.append(extra)
    if epi == "relu2":
        out_shape = (jax.ShapeDtypeStruct((M, J * Nj), F32), jax.ShapeDtypeStruct((M, J * Nj), BF16))
        out_specs = (pl.BlockSpec((tm, tn), omap), pl.BlockSpec((tm, tn), omap))
    else:
        out_shape = jax.ShapeDtypeStruct((M, J * Nj), F32 if epi == "resid" else out_dtype)
        out_specs = pl.BlockSpec((tm, tn), omap)
    return _call(body, name=name, grid=(M // tm, J, nn, nk), in_specs=in_specs, out_specs=out_specs,
                 out_shape=out_shape, scratch_shapes=[pltpu.VMEM((tm, tn), F32)],
                 compiler_params=_params(("parallel", "parallel", "parallel", "arbitrary")))(*args)


def _mm_nt(a, w, *, epi="plain", out_dtype=BF16, extra=None, name):
    M, N = a.shape
    J, K, Nj = w.shape
    assert N == J * Nj
    tko = _tile(K, 1024)
    tm = _tile(M, 1024 if _tile(Nj, 1024) <= 1024 else 512)
    has_extra = extra is not None
    out_bytes = 8 if epi == "resid" or out_dtype == F32 else 4
    tc = _depth(Nj, 2 * (tm + tko), tm * tko * (4 + out_bytes + 8 * has_extra))
    nc = Nj // tc

    def body(*refs):
        a_ref, w_ref = refs[0], refs[1]
        extra_ref = refs[2] if has_extra else None
        outs = refs[2 + has_extra: 3 + has_extra]
        acc_ref = refs[-1]
        j, c = pl.program_id(2), pl.program_id(3)

        @pl.when(jnp.logical_and(j == 0, c == 0))
        def _():
            acc_ref[...] = jnp.zeros_like(acc_ref)

        acc_ref[...] += lax.dot_general(a_ref[...], w_ref[...], (((1,), (1,)), ((), ())),
                                        preferred_element_type=F32)

        @pl.when(jnp.logical_and(j == J - 1, c == nc - 1))
        def _():
            _epilogue(epi, acc_ref[...], extra_ref, outs)

    omap = lambda i, ko, j, c: (i, ko)
    in_specs = [pl.BlockSpec((tm, tc), lambda i, ko, j, c: (i, j * nc + c)),
                pl.BlockSpec((None, tko, tc), lambda i, ko, j, c: (j, ko, c))]
    args = [a, w]
    if has_extra:
        in_specs.append(pl.BlockSpec((tm, tko), omap))
        args.append(extra)
    out_shape = jax.ShapeDtypeStruct((M, K), F32 if epi == "resid" else out_dtype)
    return _call(body, name=name, grid=(M // tm, K // tko, J, nc), in_specs=in_specs,
                 out_specs=pl.BlockSpec((tm, tko), omap), out_shape=out_shape,
                 scratch_shapes=[pltpu.VMEM((tm, tko), F32)],
                 compiler_params=_params(("parallel", "parallel", "arbitrary", "arbitrary")))(*args)


def _mm_tn(a, b, J, *, name):
    T, K = a.shape
    T2, N = b.shape
    assert T == T2 and N % J == 0
    Nj = N // J
    tn = _tile(Nj, 1024)
    tkr = _tile(K, 1024 if tn <= 1024 else 512)
    tt = _depth(T, 2 * (tkr + tn), tkr * tn * (4 + 4))
    nn, nt = Nj // tn, T // tt

    def body(a_ref, b_ref, o_ref, acc_ref):
        t = pl.program_id(3)

        @pl.when(t == 0)
        def _():
            acc_ref[...] = jnp.zeros_like(acc_ref)

        acc_ref[...] += lax.dot_general(a_ref[...], b_ref[...], (((0,), (0,)), ((), ())),
                                        preferred_element_type=F32)

        @pl.when(t == nt - 1)
        def _():
            o_ref[...] = acc_ref[...].astype(o_ref.dtype)

    return _call(body, name=name, grid=(J, K // tkr, nn, nt),
                 in_specs=[pl.BlockSpec((tt, tkr), lambda j, kr, n, t: (t, kr)),
                           pl.BlockSpec((tt, tn), lambda j, kr, n, t: (t, j * nn + n))],
                 out_specs=pl.BlockSpec((None, tkr, tn), lambda j, kr, n, t: (j, kr, n)),
                 out_shape=jax.ShapeDtypeStruct((J, K, Nj), BF16),
                 scratch_shapes=[pltpu.VMEM((tkr, tn), F32)],
                 compiler_params=_params(("parallel", "parallel", "parallel", "arbitrary")))(a, b)


def _ln_fwd(x, y, g, b, alpha, *, name):
    T, D = x.shape
    tr = _rows(T, 256)

    def body(x_ref, y_ref, g_ref, b_ref, o_ref, ob_ref):
        r = alpha * x_ref[...] + y_ref[...]
        mu = jnp.mean(r, axis=-1, keepdims=True)
        xc = r - mu
        var = jnp.mean(xc * xc, axis=-1, keepdims=True)
        o = xc * lax.rsqrt(var + LN_EPS) * g_ref[...] + b_ref[...]
        o_ref[...] = o
        ob_ref[...] = o.astype(BF16)

    row = pl.BlockSpec((tr, D), lambda i: (i, 0))
    vec = pl.BlockSpec((1, D), lambda i: (0, 0))
    return _call(body, name=name, grid=(T // tr,), in_specs=[row, row, vec, vec], out_specs=(row, row),
                 out_shape=(jax.ShapeDtypeStruct((T, D), F32), jax.ShapeDtypeStruct((T, D), BF16)),
                 compiler_params=_params(("parallel",)))(x, y, g.reshape(1, D), b.reshape(1, D))


def _ln_bwd(dout, x, y, g, alpha, after=(), *, name):
    T, D = x.shape
    tr = _rows(T, 256)
    na = len(after)

    def body(*refs):
        do_ref, x_ref, y_ref, g_ref = refs[:4]
        drb_ref, adr_ref, dg_ref, db_ref = refs[4 + na:]
        i = pl.program_id(0)
        r = alpha * x_ref[...] + y_ref[...]
        mu = jnp.mean(r, axis=-1, keepdims=True)
        xc = r - mu
        var = jnp.mean(xc * xc, axis=-1, keepdims=True)
        rstd = lax.rsqrt(var + LN_EPS)
        xhat = xc * rstd
        do = do_ref[...]
        dxh = do * g_ref[...]
        dr = rstd * (dxh - jnp.mean(dxh, axis=-1, keepdims=True)
                     - xhat * jnp.mean(dxh * xhat, axis=-1, keepdims=True))
        drb_ref[...] = dr.astype(BF16)
        adr_ref[...] = alpha * dr

        @pl.when(i == 0)
        def _():
            dg_ref[...] = jnp.zeros_like(dg_ref)
            db_ref[...] = jnp.zeros_like(db_ref)

        dg_ref[...] += jnp.sum(do * xhat, axis=0, keepdims=True)
        db_ref[...] += jnp.sum(do, axis=0, keepdims=True)

    row = pl.BlockSpec((tr, D), lambda i: (i, 0))
    vec = pl.BlockSpec((1, D), lambda i: (0, 0))
    return _call(body, name=name, grid=(T // tr,),
                 in_specs=[row, row, row, vec] + [pl.BlockSpec(memory_space=pl.ANY)] * na,
                 out_specs=(row, row, vec, vec),
                 out_shape=(jax.ShapeDtypeStruct((T, D), BF16), jax.ShapeDtypeStruct((T, D), F32),
                            jax.ShapeDtypeStruct((1, D), F32), jax.ShapeDtypeStruct((1, D), F32)),
                 compiler_params=_params(("arbitrary",)))(dout, x, y, g.reshape(1, D), *after)


def _loss_head(xf, target, *, name):
    T, D = xf.shape
    tr = _rows(T, 256)

    def body(x_ref, t_ref, dy_ref, s_ref):
        i = pl.program_id(0)
        err = x_ref[...] - t_ref[...]
        dy_ref[...] = err * (1.0 / D)

        @pl.when(i == 0)
        def _():
            s_ref[...] = jnp.zeros_like(s_ref)

        s_ref[...] += jnp.sum(jnp.sum(err * err, axis=1, keepdims=True), axis=0, keepdims=True)

    row = pl.BlockSpec((tr, D), lambda i: (i, 0))
    return _call(body, name=name, grid=(T // tr,), in_specs=[row, row],
                 out_specs=(row, pl.BlockSpec((8, 128), lambda i: (0, 0))),
                 out_shape=(jax.ShapeDtypeStruct((T, D), F32), jax.ShapeDtypeStruct((8, 128), F32)),
                 compiler_params=_params(("arbitrary",)))(xf, target)


def _adamw(w, g, m, v, *, name):
    R, C = w.shape
    tr = _rows(R, max(8, (1 << 18) // C // 8 * 8))

    def body(w_ref, g_ref, m_ref, v_ref, d_ref, nm_ref, nv_ref):
        g_ = g_ref[...]
        m_ = ADAM_B1 * m_ref[...] + (1.0 - ADAM_B1) * g_
        v_ = ADAM_B2 * v_ref[...] + (1.0 - ADAM_B2) * (g_ * g_)
        m_hat = m_ / (1.0 - ADAM_B1 ** ADAM_STEP)
        v_hat = v_ / (1.0 - ADAM_B2 ** ADAM_STEP)
        d_ref[...] = -ADAM_LR * (m_hat / (jnp.sqrt(v_hat) + ADAM_EPS) + ADAM_WD * w_ref[...])
        nm_ref[...] = m_
        nv_ref[...] = v_

    blk = pl.BlockSpec((tr, C), lambda i: (i, 0))
    sds = jax.ShapeDtypeStruct((R, C), F32)
    return _call(body, name=name, grid=(R // tr,), in_specs=[blk] * 4, out_specs=(blk,) * 3,
                 out_shape=(sds,) * 3, compiler_params=_params(("parallel",)))(w, g, m, v)


def _dot_nt(a, b):
    return lax.dot_general(a, b, (((1,), (1,)), ((), ())), preferred_element_type=F32)


def _dot_tn(a, b):
    return lax.dot_general(a, b, (((0,), (0,)), ((), ())), preferred_element_type=F32)


def _band_masks():
    qi = lax.broadcasted_iota(jnp.int32, (BLK, BLK), 0)
    kj = lax.broadcasted_iota(jnp.int32, (BLK, BLK), 1)
    return qi >= kj, kj >= qi


def _attn_fwd(P, grp, d, a_out, *, name):
    T, C = P.shape
    L = T // d
    nb = L // BLK
    nhh = a_out // HEAD_BLOCK
    cb = C // HEAD_BLOCK
    q0 = grp * 3 * nhh
    scale = HEAD_DIM ** -0.5

    def body(q_ref, kc_ref, kp_ref, vc_ref, vp_ref, o_ref, l_ref):
        b = pl.program_id(1)
        mask_c, mask_p = _band_masks()
        mask_p = jnp.logical_and(mask_p, b > 0)
        for h in range(HEAD_BLOCK // HEAD_DIM):
            hs = slice(h * HEAD_DIM, (h + 1) * HEAD_DIM)
            q = q_ref[:, hs]
            s_c = jnp.where(mask_c, _dot_nt(q, kc_ref[:, hs]) * scale, NEG)
            s_p = jnp.where(mask_p, _dot_nt(q, kp_ref[:, hs]) * scale, NEG)
            m = jnp.maximum(jnp.max(s_c, axis=1, keepdims=True), jnp.max(s_p, axis=1, keepdims=True))
            p_c = jnp.exp(s_c - m)
            p_p = jnp.exp(s_p - m)
            l = jnp.sum(p_c, axis=1, keepdims=True) + jnp.sum(p_p, axis=1, keepdims=True)
            o = (jnp.dot(p_c.astype(BF16), vc_ref[:, hs], preferred_element_type=F32)
                 + jnp.dot(p_p.astype(BF16), vp_ref[:, hs], preferred_element_type=F32))
            o_ref[:, hs] = o / l
            l_ref[:, hs] = jnp.broadcast_to(m + jnp.log(l), (BLK, HEAD_DIM))

    def cur(part):
        return pl.BlockSpec((BLK, HEAD_BLOCK), lambda r, b, hh: (b, r * cb + q0 + part * nhh + hh))

    def prev(part):
        return pl.BlockSpec((BLK, HEAD_BLOCK), lambda r, b, hh: (jnp.maximum(b - 1, 0), r * cb + q0 + part * nhh + hh))

    out = pl.BlockSpec((BLK, HEAD_BLOCK), lambda r, b, hh: (b, r * nhh + hh))
    Pv = P.reshape(L, d * C)
    o, lse = _call(body, name=name, grid=(d, nb, nhh), in_specs=[cur(0), cur(1), prev(1), cur(2), prev(2)],
                   out_specs=(out, out),
                   out_shape=(jax.ShapeDtypeStruct((L, d * a_out), F32),) * 2,
                   compiler_params=_params(("parallel", "parallel", "parallel")))(Pv, Pv, Pv, Pv, Pv)
    return o.reshape(T, a_out), lse.reshape(T, a_out)


def _attn_combine(os_, lses, *, name):
    T, W = os_[0].shape
    tr = _rows(T, 256)
    n = len(os_)

    def body(*refs):
        o_refs, l_refs = refs[:n], refs[n:2 * n]
        mix_ref, lse_ref = refs[2 * n], refs[2 * n + 1]
        ls = [r[...] for r in l_refs]
        m = functools.reduce(jnp.maximum, ls)
        es = [jnp.exp(l - m) for l in ls]
        tot = functools.reduce(lambda a, b: a + b, es)
        mix = functools.reduce(lambda a, b: a + b, [(e / tot) * o[...] for e, o in zip(es, o_refs)])
        mix_ref[...] = mix
        lse_ref[...] = m + jnp.log(tot)

    blk = pl.BlockSpec((tr, W), lambda i: (i, 0))
    sds = jax.ShapeDtypeStruct((T, W), F32)
    return _call(body, name=name, grid=(T // tr,), in_specs=[blk] * (2 * n), out_specs=(blk, blk),
                 out_shape=(sds, sds), compiler_params=_params(("parallel",)))(*os_, *lses)


def _attn_bwd(P, dO, O, LSE, grp, d, a_out, *, name):
    T, C = P.shape
    L = T // d
    nb = L // BLK
    nhh = a_out // HEAD_BLOCK
    cb = C // HEAD_BLOCK
    q0 = grp * 3 * nhh
    scale = HEAD_DIM ** -0.5

    def body(q_ref, qn_ref, kc_ref, kp_ref, vc_ref, vp_ref, do_ref, don_ref, o_ref, on_ref, l_ref, ln_ref,
             dq_ref, dk_ref, dv_ref):
        b = pl.program_id(1)
        mask_c, mask_prev = _band_masks()
        mask_p = jnp.logical_and(mask_prev, b > 0)
        mask_n = jnp.logical_and(mask_prev, b < nb - 1)
        for h in range(HEAD_BLOCK // HEAD_DIM):
            hs = slice(h * HEAD_DIM, (h + 1) * HEAD_DIM)
            q, qn, kc, kp, vc, vp = (r[:, hs] for r in (q_ref, qn_ref, kc_ref, kp_ref, vc_ref, vp_ref))
            do, don = do_ref[:, hs], don_ref[:, hs]
            lse, lse_n = l_ref[:, hs], ln_ref[:, hs]
            delta = jnp.sum(do * o_ref[:, hs], axis=1, keepdims=True)
            delta_n = jnp.sum(don * on_ref[:, hs], axis=1, keepdims=True)
            dob, donb = do.astype(BF16), don.astype(BF16)
            p_c = jnp.exp(jnp.where(mask_c, _dot_nt(q, kc) * scale, NEG) - lse)
            p_p = jnp.exp(jnp.where(mask_p, _dot_nt(q, kp) * scale, NEG) - lse)
            p_n = jnp.exp(jnp.where(mask_n, _dot_nt(qn, kc) * scale, NEG) - lse_n)
            ds_c = (p_c * (_dot_nt(dob, vc) - delta) * scale).astype(BF16)
            ds_p = (p_p * (_dot_nt(dob, vp) - delta) * scale).astype(BF16)
            ds_n = (p_n * (_dot_nt(donb, vc) - delta_n) * scale).astype(BF16)
            dq = jnp.dot(ds_c, kc, preferred_element_type=F32) + jnp.dot(ds_p, kp, preferred_element_type=F32)
            dk = _dot_tn(ds_c, q) + _dot_tn(ds_n, qn)
            dv = _dot_tn(p_c.astype(BF16), dob) + _dot_tn(p_n.astype(BF16), donb)
            dq_ref[:, hs] = dq.astype(BF16)
            dk_ref[:, hs] = dk.astype(BF16)
            dv_ref[:, hs] = dv.astype(BF16)

    def pspec(part, shift):
        def imap(r, b, hh):
            return (jnp.clip(b + shift, 0, nb - 1), r * cb + q0 + part * nhh + hh)
        return pl.BlockSpec((BLK, HEAD_BLOCK), imap)

    def aspec(shift):
        return pl.BlockSpec((BLK, HEAD_BLOCK), lambda r, b, hh: (jnp.clip(b + shift, 0, nb - 1), r * nhh + hh))

    Pv = P.reshape(L, d * C)
    dOv, Ov, Lv = (t.reshape(L, d * a_out) for t in (dO, O, LSE))
    sds = jax.ShapeDtypeStruct((L, d * a_out), BF16)
    dq, dk, dv = _call(
        body, name=name, grid=(d, nb, nhh),
        in_specs=[pspec(0, 0), pspec(0, 1), pspec(1, 0), pspec(1, -1), pspec(2, 0), pspec(2, -1),
                  aspec(0), aspec(1), aspec(0), aspec(1), aspec(0), aspec(1)],
        out_specs=(aspec(0),) * 3, out_shape=(sds,) * 3,
        compiler_params=_params(("parallel", "parallel", "parallel")))(Pv, Pv, Pv, Pv, Pv, Pv, dOv, dOv, Ov, Ov, Lv, Lv)
    return [t.reshape(T, a_out) for t in (dq, dk, dv)]


def _mem_softmax(q, k, scale):
    s = _dot_nt(q, k) * scale
    e = jnp.exp(s - jnp.max(s, axis=1, keepdims=True))
    return e / jnp.sum(e, axis=1, keepdims=True)


def _mem_fwd(P, qcol, kv, *, name):
    T = P.shape[0]
    tq = _rows(T, 512)
    scale = HEAD_DIM ** -0.5

    def body(q_ref, kv_ref, o_ref):
        for h in range(MEM_HEADS):
            hs = slice(h * HEAD_DIM, (h + 1) * HEAD_DIM)
            vs = slice(MEM_WIDTH + h * HEAD_DIM, MEM_WIDTH + (h + 1) * HEAD_DIM)
            p = _mem_softmax(q_ref[:, hs].astype(BF16), kv_ref[:, hs], scale)
            o_ref[:, hs] = jnp.dot(p.astype(BF16), kv_ref[:, vs], preferred_element_type=F32)

    return _call(body, name=name, grid=(T // tq,),
                 in_specs=[pl.BlockSpec((tq, MEM_WIDTH), lambda i: (i, qcol)),
                           pl.BlockSpec((MEM_TOKENS, 2 * MEM_WIDTH), lambda i: (0, 0))],
                 out_specs=pl.BlockSpec((tq, MEM_WIDTH), lambda i: (i, 0)),
                 out_shape=jax.ShapeDtypeStruct((T, MEM_WIDTH), F32),
                 compiler_params=_params(("parallel",)))(P, kv)


def _mem_bwd(P, qcol, kv, dcat, ocol, *, name):
    T = P.shape[0]
    tq = _rows(T, 512)
    scale = HEAD_DIM ** -0.5

    def body(q_ref, kv_ref, do_ref, dq_ref, dkv_ref):
        i = pl.program_id(0)

        @pl.when(i == 0)
        def _():
            dkv_ref[...] = jnp.zeros_like(dkv_ref)

        for h in range(MEM_HEADS):
            hs = slice(h * HEAD_DIM, (h + 1) * HEAD_DIM)
            vs = slice(MEM_WIDTH + h * HEAD_DIM, MEM_WIDTH + (h + 1) * HEAD_DIM)
            q = q_ref[:, hs].astype(BF16)
            k, v = kv_ref[:, hs], kv_ref[:, vs]
            do = do_ref[:, hs].astype(BF16)
            p = _mem_softmax(q, k, scale)
            dp = _dot_nt(do, v)
            ds = (p * (dp - jnp.sum(p * dp, axis=1, keepdims=True)) * scale).astype(BF16)
            dq_ref[:, hs] = jnp.dot(ds, k, preferred_element_type=F32).astype(BF16)
            dkv_ref[:, hs] += _dot_tn(ds, q)
            dkv_ref[:, vs] += _dot_tn(p.astype(BF16), do)

    return _call(body, name=name, grid=(T // tq,),
                 in_specs=[pl.BlockSpec((tq, MEM_WIDTH), lambda i: (i, qcol)),
                           pl.BlockSpec((MEM_TOKENS, 2 * MEM_WIDTH), lambda i: (0, 0)),
                           pl.BlockSpec((tq, MEM_WIDTH), lambda i: (i, ocol))],
                 out_specs=(pl.BlockSpec((tq, MEM_WIDTH), lambda i: (i, 0)),
                            pl.BlockSpec((MEM_TOKENS, 2 * MEM_WIDTH), lambda i: (0, 0))),
                 out_shape=(jax.ShapeDtypeStruct((T, MEM_WIDTH), BF16),
                            jax.ShapeDtypeStruct((MEM_TOKENS, 2 * MEM_WIDTH), F32)),
                 compiler_params=_params(("arbitrary",)))(P, kv, dcat)


_SQRT_HALF = 0.7071067811865476
_INV_SQRT_2PI = 0.3989422804014327


def _gelu(x):
    return 0.5 * x * (1.0 + lax.erf(x * _SQRT_HALF))


def _gelu_grad(x):
    return 0.5 * (1.0 + lax.erf(x * _SQRT_HALF)) + x * (_INV_SQRT_2PI * jnp.exp(-0.5 * x * x))


def _tril():
    t = lax.broadcasted_iota(jnp.int32, (BLK, BLK), 0)
    s = lax.broadcasted_iota(jnp.int32, (BLK, BLK), 1)
    return t >= s


def _gmlp_fwd(P, w_s, bs_t, vg, vb, width, *, name):
    T = P.shape[0]
    G = width // HEAD_DIM
    tb = _rows(T, 512)

    def body(pu_ref, pv_ref, ws_ref, bs_ref, vg_ref, vb_ref, o_ref):
        u = _gelu(pu_ref[...])
        v = _gelu(pv_ref[...])
        mu = jnp.mean(v, axis=-1, keepdims=True)
        vc = v - mu
        var = jnp.mean(vc * vc, axis=-1, keepdims=True)
        vn = (vc * lax.rsqrt(var + LN_EPS) * vg_ref[...] + vb_ref[...]).astype(BF16)
        tril = _tril()
        for g in range(G):
            gs = slice(g * HEAD_DIM, (g + 1) * HEAD_DIM)
            ws = jnp.where(tril, ws_ref[g], 0.0).astype(BF16)
            bias = bs_ref[:, g:g + 1]
            for c in range(tb // BLK):
                cs = slice(c * BLK, (c + 1) * BLK)
                sg = jnp.dot(ws, vn[cs, gs], preferred_element_type=F32) + bias
                o_ref[cs, gs] = u[cs, gs] * sg

    blk = lambda col: pl.BlockSpec((tb, width), lambda i: (i, col))
    full = lambda shape: pl.BlockSpec(shape, lambda i: (0,) * len(shape))
    return _call(body, name=name, grid=(T // tb,),
                 in_specs=[blk(0), blk(1), full((G, BLK, BLK)), full((BLK, G)), full((1, width)), full((1, width))],
                 out_specs=blk(0), out_shape=jax.ShapeDtypeStruct((T, width), F32),
                 compiler_params=_params(("parallel",)))(P, P, w_s, bs_t, vg.reshape(1, width), vb.reshape(1, width))


def _gmlp_bwd(P, w_s, bs_t, vg, vb, dcat, width, *, name):
    T = P.shape[0]
    G = width // HEAD_DIM
    tb = _rows(T, 512)

    def body(pu_ref, pv_ref, ws_ref, bs_ref, vg_ref, vb_ref, dm_ref, dpu_ref, dpv_ref, dws_ref, dbs_ref, dvg_ref,
             dvb_ref, dvn_ref):
        i = pl.program_id(0)

        @pl.when(i == 0)
        def _():
            dws_ref[...] = jnp.zeros_like(dws_ref)
            dbs_ref[...] = jnp.zeros_like(dbs_ref)
            dvg_ref[...] = jnp.zeros_like(dvg_ref)
            dvb_ref[...] = jnp.zeros_like(dvb_ref)

        pu, pv = pu_ref[...], pv_ref[...]
        u = _gelu(pu)
        v = _gelu(pv)
        mu = jnp.mean(v, axis=-1, keepdims=True)
        vc = v - mu
        var = jnp.mean(vc * vc, axis=-1, keepdims=True)
        rstd = lax.rsqrt(var + LN_EPS)
        xhat = vc * rstd
        vn = (xhat * vg_ref[...] + vb_ref[...]).astype(BF16)
        dm = dm_ref[...]
        tril = _tril()
        lane = lax.broadcasted_iota(jnp.int32, (BLK, BLK), 1)
        dbs = jnp.zeros((BLK, BLK), F32)
        for g in range(G):
            gs = slice(g * HEAD_DIM, (g + 1) * HEAD_DIM)
            ws = jnp.where(tril, ws_ref[g], 0.0).astype(BF16)
            bias = bs_ref[:, g:g + 1]
            dws = jnp.zeros((BLK, BLK), F32)
            rs = jnp.zeros((BLK, 1), F32)
            for c in range(tb // BLK):
                cs = slice(c * BLK, (c + 1) * BLK)
                vn_cg = vn[cs, gs]
                sg = jnp.dot(ws, vn_cg, preferred_element_type=F32) + bias
                dm_cg = dm[cs, gs]
                dpu_ref[cs, gs] = (dm_cg * sg * _gelu_grad(pu[cs, gs])).astype(BF16)
                dsg = dm_cg * u[cs, gs]
                dsgb = dsg.astype(BF16)
                dvn_ref[cs, gs] = _dot_tn(ws, dsgb)
                dws = dws + _dot_nt(dsgb, vn_cg)
                rs = rs + jnp.sum(dsg, axis=1, keepdims=True)
            dws_ref[g] += jnp.where(tril, dws, 0.0)
            dbs = dbs + jnp.where(lane == g, rs, 0.0)
        dbs_ref[...] += dbs
        dvn = dvn_ref[...]
        dxh = dvn * vg_ref[...]
        dv = rstd * (dxh - jnp.mean(dxh, axis=-1, keepdims=True)
                     - xhat * jnp.mean(dxh * xhat, axis=-1, keepdims=True))
        dpv_ref[...] = (dv * _gelu_grad(pv)).astype(BF16)
        dvg_ref[...] += jnp.sum(dvn * xhat, axis=0, keepdims=True)
        dvb_ref[...] += jnp.sum(dvn, axis=0, keepdims=True)

    blk = lambda col: pl.BlockSpec((tb, width), lambda i: (i, col))
    full = lambda shape: pl.BlockSpec(shape, lambda i: (0,) * len(shape))
    return _call(body, name=name, grid=(T // tb,),
                 in_specs=[blk(0), blk(1), full((G, BLK, BLK)), full((BLK, G)), full((1, width)), full((1, width)),
                           blk(0)],
                 out_specs=(blk(0), blk(0), full((G, BLK, BLK)), full((BLK, BLK)), full((1, width)), full((1, width))),
                 out_shape=(jax.ShapeDtypeStruct((T, width), BF16), jax.ShapeDtypeStruct((T, width), BF16),
                            jax.ShapeDtypeStruct((G, BLK, BLK), F32), jax.ShapeDtypeStruct((BLK, BLK), F32),
                            jax.ShapeDtypeStruct((1, width), F32), jax.ShapeDtypeStruct((1, width), F32)),
                 scratch_shapes=[pltpu.VMEM((tb, width), F32)],
                 compiler_params=_params(("arbitrary",)))(P, P, w_s, bs_t, vg.reshape(1, width), vb.reshape(1, width), dcat)


def _place():
    x, y, c = lax.axis_index("x"), lax.axis_index("y"), lax.axis_index("c")
    chips = [(1 - x, y), (x, 1 - y), (1 - x, 1 - y)]
    return x, y, c, 2 * x + y, chips, [2 * px + py for px, py in chips]


def _half(ref, c, rows):
    return ref.at[pl.ds(c * (rows // 2), rows // 2)]


def _place_shard(w, chip, dtype, *, name):
    R, C = w.shape
    tr = _rows(R, max(8, (1 << 19) // C // 8 * 8)) if R % 8 == 0 else R

    def body(chip_ref, w_ref, o_ref):
        o_ref[...] = w_ref[...].astype(dtype)

    grid_spec = pltpu.PrefetchScalarGridSpec(
        num_scalar_prefetch=1, grid=(R // tr,), in_specs=[pl.BlockSpec((tr, C), lambda i, s: (i, 0))],
        out_specs=pl.BlockSpec((None, tr, C), lambda i, s: (s[0], i, 0)))
    return _call(body, name=name, grid_spec=grid_spec, out_shape=jax.ShapeDtypeStruct((4, R, C), dtype),
                 compiler_params=_params(("parallel",)))(chip, w)


def _pair_exchange(grads, *, name):
    n = len(grads)

    def body(*refs):
        g, theirs = refs[:n], refs[n:2 * n]
        send, recv = refs[2 * n:]
        x, y, c, _, _, _ = _place()
        copies = []
        for t in range(n):
            h = grads[t].shape[1] // 2
            rc = pltpu.make_async_remote_copy(src_ref=g[t].at[:, pl.ds((1 - c) * h, h)], dst_ref=theirs[t],
                                              send_sem=send.at[t], recv_sem=recv.at[t], device_id=(x, y, 1 - c),
                                              device_id_type=MESH)
            rc.start()
            copies.append(rc)
        for rc in copies:
            rc.wait()

    halves = [jax.ShapeDtypeStruct((4, g.shape[1] // 2, g.shape[2]), g.dtype) for g in grads]
    return _call(body, name=name, in_specs=[HBM] * n, out_specs=[HBM] * n, out_shape=halves,
                 scratch_shapes=[pltpu.SemaphoreType.DMA((n,))] * 2)(*grads)


def _half_swap(fulls, *, name):
    n = len(fulls)

    def body(*refs):
        full = refs[n:2 * n]
        send, recv = refs[2 * n:]
        x, y, c, _, _, _ = _place()
        copies = []
        for t in range(n):
            h = fulls[t].shape[0] // 2
            mine = full[t].at[pl.ds(c * h, h)]
            rc = pltpu.make_async_remote_copy(src_ref=mine, dst_ref=mine, send_sem=send.at[t], recv_sem=recv.at[t],
                                              device_id=(x, y, 1 - c), device_id_type=MESH)
            rc.start()
            copies.append(rc)
        for rc in copies:
            rc.wait()

    return _call(body, name=name, in_specs=[HBM] * n, out_specs=[HBM] * n,
                 out_shape=[jax.ShapeDtypeStruct(f.shape, f.dtype) for f in fulls],
                 input_output_aliases={t: t for t in range(n)},
                 scratch_shapes=[pltpu.SemaphoreType.DMA((n,))] * 2)(*fulls)


SEM = pl.BlockSpec(memory_space=pltpu.SEMAPHORE)
EFFECT = pltpu.SideEffectType.DATAFLOW_SIDE_EFFECTING


def _hbm(a):
    return pltpu.with_memory_space_constraint(a, pltpu.HBM)


def _gather_windows(shapes):
    split = [s[1] % 16 == 0 for s in shapes]

    def window(ref, t, chip_idx, core):
        w = ref.at[chip_idx]
        return _half(w, core, shapes[t][1]) if split[t] else w

    return split, window


def _gather_start(bufs, after, *, name):
    n = len(bufs)
    split, window = _gather_windows([b.shape for b in bufs])

    na = len(after)

    def body(*refs):
        b = refs[:n]
        send, recv = refs[n + na], refs[n + na + 1]
        token = refs[2 * n + na + 2]
        x, y, c, j, chips, pj = _place()
        for t in range(n):
            for k in range(3):
                pltpu.make_async_remote_copy(src_ref=window(b[t], t, j, c), dst_ref=window(b[t], t, j, c),
                                             send_sem=send.at[3 * t + k], recv_sem=recv.at[3 * t + k],
                                             device_id=(*chips[k], c), device_id_type=MESH).start()
        token[...] = jnp.zeros_like(token)

    outs = _call(body, name=name, in_specs=[HBM] * n + [pl.BlockSpec(memory_space=pl.ANY)] * na,
                 out_specs=(SEM, SEM, *([HBM] * n), pl.BlockSpec(memory_space=pltpu.VMEM)),
                 out_shape=(pltpu.SemaphoreType.DMA((3 * n,)), pltpu.SemaphoreType.DMA((3 * n,)),
                            *[pltpu.HBM(b.shape, b.dtype) for b in bufs], jax.ShapeDtypeStruct((8, 128), F32)),
                 input_output_aliases={t: 2 + t for t in range(n)},
                 compiler_params=pltpu.CompilerParams(has_side_effects=EFFECT))(*[_hbm(b) for b in bufs], *after)
    return outs[0], outs[1], list(outs[2:2 + n]), outs[2 + n]


def _gather_wait(send, recv, bufs, after, *, name):
    n = len(bufs)
    split, window = _gather_windows([b.shape for b in bufs])

    def body(*refs):
        b = refs[:n]
        send, recv = refs[n], refs[n + 1]
        x, y, c, j, chips, pj = _place()
        for t in range(n):
            for k in range(3):
                out = pltpu.make_async_remote_copy(src_ref=window(b[t], t, j, c), dst_ref=window(b[t], t, j, c),
                                                   send_sem=send.at[3 * t + k], recv_sem=recv.at[3 * t + k],
                                                   device_id=(*chips[k], c), device_id_type=MESH)
                out.wait_send()
                back = pltpu.make_async_remote_copy(src_ref=window(b[t], t, pj[k], c), dst_ref=window(b[t], t, pj[k], c),
                                                    send_sem=send.at[3 * t + k], recv_sem=recv.at[3 * t + k],
                                                    device_id=(*chips[k], c), device_id_type=MESH)
                back.wait_recv()

    outs = _call(body, name=name, in_specs=[HBM] * n + [SEM, SEM, pl.BlockSpec(memory_space=pl.ANY)],
                 out_specs=[HBM] * n, out_shape=[pltpu.HBM(b.shape, b.dtype) for b in bufs],
                 input_output_aliases={t: t for t in range(n)},
                 compiler_params=pltpu.CompilerParams(has_side_effects=EFFECT))(*bufs, send, recv, after)
    return list(outs)


def _gather_pass(bufs, *, name):
    n = len(bufs)
    split, window = _gather_windows([b.shape for b in bufs])
    idx = [t for t in range(n) if split[t]]

    def body(*refs):
        b = refs[n:2 * n]
        send, recv = refs[2 * n:]
        x, y, c, j, chips, pj = _place()

        def d2d(u, t, k, core):
            w = window(b[t], t, pj[k], core)
            return pltpu.make_async_remote_copy(src_ref=w, dst_ref=w, send_sem=send.at[3 * u + k],
                                                recv_sem=recv.at[3 * u + k], device_id=(x, y, 1 - c),
                                                device_id_type=MESH)

        sent = [d2d(u, t, k, c) for u, t in enumerate(idx) for k in range(3)]
        for cp in sent:
            cp.start()
        for u, t in enumerate(idx):
            for k in range(3):
                d2d(u, t, k, 1 - c).wait_recv()
        for cp in sent:
            cp.wait_send()

    return _call(body, name=name, in_specs=[HBM] * n, out_specs=[HBM] * n,
                 out_shape=[jax.ShapeDtypeStruct(b.shape, b.dtype) for b in bufs],
                 input_output_aliases={t: t for t in range(n)},
                 scratch_shapes=[pltpu.SemaphoreType.DMA((3 * len(idx),))] * 2)(*bufs)


def _chip_start(pairs, *, name):
    n = len(pairs)
    lands = [lax.empty((3,) + p.shape[1:], p.dtype) for p in pairs]

    def body(*refs):
        s, r = refs[:n], refs[n:2 * n]
        send, recv = refs[2 * n], refs[2 * n + 1]
        token = refs[4 * n + 2]
        x, y, c, j, chips, pj = _place()
        for t in range(n):
            for k in range(3):
                pltpu.make_async_remote_copy(src_ref=s[t].at[pj[k]], dst_ref=r[t].at[k], send_sem=send.at[3 * t + k],
                                             recv_sem=recv.at[3 * t + k], device_id=(*chips[k], c),
                                             device_id_type=MESH).start()
        token[...] = jnp.zeros_like(token)

    outs = _call(body, name=name, in_specs=[HBM] * (2 * n),
                 out_specs=(SEM, SEM, *([HBM] * (2 * n)), pl.BlockSpec(memory_space=pltpu.VMEM)),
                 out_shape=(pltpu.SemaphoreType.DMA((3 * n,)), pltpu.SemaphoreType.DMA((3 * n,)),
                            *[pltpu.HBM(a.shape, a.dtype) for a in list(pairs) + lands],
                            jax.ShapeDtypeStruct((8, 128), F32)),
                 input_output_aliases={t: 2 + t for t in range(2 * n)},
                 compiler_params=pltpu.CompilerParams(has_side_effects=EFFECT))(*[_hbm(a) for a in list(pairs) + lands])
    return outs[0], outs[1], list(outs[2:2 + n]), list(outs[2 + n:2 + 2 * n]), outs[2 + 2 * n]


def _chip_wait(send, recv, pairs, lands, after, *, name):
    n = len(pairs)

    def body(*refs):
        s, r = refs[:n], refs[n:2 * n]
        send, recv = refs[2 * n], refs[2 * n + 1]
        x, y, c, j, chips, pj = _place()
        for t in range(n):
            for k in range(3):
                cp = pltpu.make_async_remote_copy(src_ref=s[t].at[pj[k]], dst_ref=r[t].at[k], send_sem=send.at[3 * t + k],
                                                  recv_sem=recv.at[3 * t + k], device_id=(*chips[k], c),
                                                  device_id_type=MESH)
                cp.wait_send()
                cp.wait_recv()

    outs = _call(body, name=name, in_specs=[HBM] * (2 * n) + [SEM, SEM, pl.BlockSpec(memory_space=pl.ANY)],
                 out_specs=[HBM] * (2 * n), out_shape=[pltpu.HBM(a.shape, a.dtype) for a in list(pairs) + list(lands)],
                 input_output_aliases={t: t for t in range(2 * n)},
                 compiler_params=pltpu.CompilerParams(has_side_effects=EFFECT))(*pairs, *lands, send, recv, after)
    return list(outs[:n]), list(outs[n:])


def _all_reduce_small(packed, *, name):
    R, C = packed.shape

    def body(p_ref, o_ref, slots, send, recv, lsem):
        x, y, c = lax.axis_index("x"), lax.axis_index("y"), lax.axis_index("c")
        me = 4 * x + 2 * y + c
        lc = pltpu.make_async_copy(p_ref, slots.at[me], lsem.at[0])
        lc.start()
        copies = []
        for rel in range(1, 8):
            fx, fy, fc = (rel >> 2) & 1, (rel >> 1) & 1, rel & 1
            to = (1 - x if fx else x, 1 - y if fy else y, 1 - c if fc else c)
            cp = pltpu.make_async_remote_copy(src_ref=p_ref, dst_ref=slots.at[me], send_sem=send.at[rel - 1],
                                              recv_sem=recv.at[rel - 1], device_id=to, device_id_type=MESH)
            cp.start()
            copies.append((cp, 4 * to[0] + 2 * to[1] + to[2]))
        for rel, (cp, frm) in enumerate(copies):
            cp.wait_send()
            pltpu.make_async_remote_copy(src_ref=p_ref, dst_ref=slots.at[frm], send_sem=send.at[rel],
                                         recv_sem=recv.at[rel], device_id=(x, y, c), device_id_type=MESH).wait_recv()
        lc.wait()
        acc = slots[0]
        for dev in range(1, 8):
            acc = acc + slots[dev]
        o_ref[...] = acc

    return _call(body, name=name, in_specs=[pl.BlockSpec(memory_space=pltpu.VMEM)],
                 out_specs=pl.BlockSpec(memory_space=pltpu.VMEM), out_shape=jax.ShapeDtypeStruct((R, C), F32),
                 scratch_shapes=[pltpu.VMEM((8, R, C), F32), pltpu.SemaphoreType.DMA((7,)),
                                 pltpu.SemaphoreType.DMA((7,)), pltpu.SemaphoreType.DMA((1,))],
                 compiler_params=pltpu.CompilerParams(vmem_limit_bytes=V7X_VMEM_LIMIT))(packed)


def _pair_sum(grad, theirs, core, *, name):
    J, H, C = theirs.shape
    tr = _rows(H, max(8, (1 << 19) // C // 8 * 8))

    def body(core_ref, a_ref, b_ref, o_ref):
        o_ref[...] = (a_ref[...].astype(F32) + b_ref[...].astype(F32)).astype(BF16)

    blk = pl.BlockSpec((None, tr, C), lambda j, i, s: (j, i, 0))
    mine = pl.BlockSpec((None, None, tr, C), lambda j, i, s: (j, s[0], i, 0))
    grid_spec = pltpu.PrefetchScalarGridSpec(num_scalar_prefetch=1, grid=(J, H // tr), in_specs=[mine, blk],
                                             out_specs=blk)
    return _call(body, name=name, grid_spec=grid_spec, out_shape=jax.ShapeDtypeStruct((J, H, C), BF16),
                 compiler_params=_params(("parallel", "parallel")))(core, grad.reshape(J, 2, H, C), theirs)


def _chip_sum(pairs, slots, place, *, name):
    _, H, C = slots.shape
    tr = _rows(H, max(8, (1 << 19) // C // 8 * 8))
    nr = H // tr

    def body(place_ref, s0, s1, s2, s3, o_ref):
        o_ref[...] = ((s0[...].astype(F32) + s1[...].astype(F32)) + s2[...].astype(F32)) + s3[...].astype(F32)

    def slot(k):
        return pl.BlockSpec((None, tr, C), lambda i, s: (k, i, 0))

    own = pl.BlockSpec((None, tr, C), lambda i, s: (s[0], i, 0))
    grid_spec = pltpu.PrefetchScalarGridSpec(
        num_scalar_prefetch=1, grid=(nr,), in_specs=[own, slot(0), slot(1), slot(2)],
        out_specs=pl.BlockSpec((tr, C), lambda i, s: (s[1] * nr + i, 0)))
    return _call(body, name=name, grid_spec=grid_spec, out_shape=jax.ShapeDtypeStruct((2 * H, C), F32),
                 compiler_params=_params(("parallel",)))(place, pairs, slots, slots, slots)


def _reduce_scatter_begin(grads, place, tag):
    theirs = _pair_exchange(grads, name=f"rs_pair_{tag}")
    pairs = [_pair_sum(g, t, place[1:], name=f"rs_pair_sum_{tag}_{i}") for i, (g, t) in enumerate(zip(grads, theirs))]
    return _chip_start(pairs, name=f"rs_chip_start_{tag}")


def _reduce_scatter_end(started, after, place, tag):
    send, recv, pairs, lands, _ = started
    pairs, slots = _chip_wait(send, recv, pairs, lands, after, name=f"rs_chip_wait_{tag}")
    fulls = [_chip_sum(p, s, place, name=f"rs_chip_sum_{tag}_{i}") for i, (p, s) in enumerate(zip(pairs, slots))]
    return _half_swap(fulls, name=f"rs_swap_{tag}")


def kernel(x, mem, w_in_a, w_in_b, w_s, b_s, vnorm_g, vnorm_b, w_mem_kv, w_out, ln1_g, ln1_b, w_ff1, w_ff2, ln2_g, ln2_b, loss_target, m_w_in_a, m_w_in_b, m_w_s, m_b_s, m_vnorm_g, m_vnorm_b, m_w_mem_kv, m_w_out, m_ln1_g, m_ln1_b, m_w_ff1, m_w_ff2, m_ln2_g, m_ln2_b, v_w_in_a, v_w_in_b, v_w_s, v_b_s, v_vnorm_g, v_vnorm_b, v_w_mem_kv, v_w_out, v_ln1_g, v_ln1_b, v_w_ff1, v_w_ff2, v_ln2_g, v_ln2_b):
    T, D = x.shape[1], x.shape[2]
    depth = w_ff1.shape[0]
    alpha = (2.0 * depth) ** 0.25
    a_out = (D // 256) * HEAD_DIM
    a_cols = len(A_PAIRS) * 3 * a_out
    b_width = (D // 256) * HEAD_DIM
    G = b_width // HEAD_DIM
    assert a_out % HEAD_BLOCK == 0 and T % (BLK * A_PAIRS[-1][1]) == 0

    xf = x.reshape(T, D)
    mem_b = mem.reshape(MEM_TOKENS, D).astype(BF16)
    target = loss_target.reshape(T, D)
    c_idx = lax.axis_index("x") * 2 + lax.axis_index("y")
    place = jnp.stack([c_idx, lax.axis_index("c")]).astype(jnp.int32)

    def gather_begin(i, group, after):
        jl, tag = i // 2, "a" if i % 2 == 0 else "b"
        if group == "mix":
            mats = [(w_in_a if i % 2 == 0 else w_in_b)[jl], w_mem_kv[i], w_out[i]]
        else:
            mats = [w_ff1[i], w_ff2[i]]
        bufs = [_place_shard(w, place[:1], BF16, name=f"place_{tag}_{group}_{n_}") for n_, w in enumerate(mats)]
        if group == "mix" and i % 2:
            bufs += [_place_shard(v_[jl].reshape(1, -1), place[:1], F32, name="place_vnorm") for v_ in (vnorm_g, vnorm_b)]
        return _gather_start(bufs, after, name=f"gather_start_{tag}_{group}")

    def gather_end(started, after, name):
        send, recv, bufs, _ = started
        bufs = _gather_wait(send, recv, bufs, after, name=f"gather_wait_{name}")
        return _gather_pass(bufs, name=f"gather_pass_{name}")

    def group_view(P, g, d):
        return (P, g) if d == 1 else (P[:, g * 3 * a_out:(g + 1) * 3 * a_out], 0)

    saved = []
    mix_started = gather_begin(0, "mix", ())
    ff_started = None
    for i in range(depth):
        jl = i // 2
        is_a = i % 2 == 0
        tag = "a" if is_a else "b"
        gathered = gather_end(mix_started, xf, f"{tag}_mix")
        win, wkv, wout = gathered[:3]
        wkv = wkv.reshape(1, D, 2 * MEM_WIDTH)
        token = None
        if i == 0:
            ff_started = gather_begin(0, "ff", (win,))
            token = ff_started[3]
        if i + 1 < depth:
            mix_started = gather_begin(i + 1, "mix", (win,) if token is None else (token,))
            ff_next = gather_begin(i + 1, "ff", (mix_started[3],))
            token = ff_next[3]
        xb = (xf if token is None else xf + token[0, 0]).astype(BF16)
        P = _mm_nn(xb, win, out_dtype=BF16 if is_a else F32, name=f"proj_in_{'a' if is_a else 'b'}")
        kv = _mm_nn(mem_b, wkv, name="proj_kv")
        if is_a:
            outs = [_attn_fwd(*group_view(P, g, d), d, a_out, name=f"attn_fwd_d{d}") for g, (_, d) in enumerate(A_PAIRS)]
            mix, lse = _attn_combine([o for o, _ in outs], [l for _, l in outs], name="attn_combine")
            qcol = a_cols // MEM_WIDTH
            extra = (lse,)
        else:
            vg_full = gathered[3].reshape(-1)
            vb_full = gathered[4].reshape(-1)
            bs_t = b_s[jl].T
            mix = _gmlp_fwd(P, w_s[jl], bs_t, vg_full, vb_full, b_width, name="gmlp_fwd")
            qcol = 2 * b_width // MEM_WIDTH
            extra = (vg_full, vb_full, bs_t)
        mem_o = _mem_fwd(P, qcol, kv, name=f"mem_fwd_{'a' if is_a else 'b'}")
        cat = jnp.concatenate([mix.astype(BF16), mem_o.astype(BF16)], axis=1)
        y = _mm_nn(cat, wout, out_dtype=F32, name="proj_out")
        x1, x1b = _ln_fwd(xf, y, ln1_g[i], ln1_b[i], alpha, name="ln_fwd")
        wff1, wff2 = gather_end(ff_started, x1b, f"{tag}_ff")
        wff2 = wff2.reshape(1, 4 * D, D)
        if i + 1 < depth:
            ff_started = ff_next
        a_pre, hid = _mm_nn(x1b, wff1, epi="relu2", name="ff1")
        f = _mm_nn(hid, wff2, out_dtype=F32, name="ff2")
        x2, _ = _ln_fwd(x1, f, ln2_g[i], ln2_b[i], alpha, name="ln_fwd")
        saved.append(dict(xf=xf, xb=xb, P=P, kv=kv, mix=mix, cat=cat, y=y, x1=x1, x1b=x1b, a_pre=a_pre, hid=hid, f=f,
                          extra=extra, w=(win, wkv, wout, wff1, wff2), qcol=qcol))
        xf = x2

    dx, sq = _loss_head(xf, target, name="loss_head")
    loss = lax.psum(sq[0, 0] * (0.5 / D), ("x", "y", "c"))

    g_big = dict(w_in_a=[None] * ((depth + 1) // 2), w_in_b=[None] * (depth // 2), w_mem_kv=[None] * depth,
                 w_out=[None] * depth, w_ff1=[None] * depth, w_ff2=[None] * depth)
    small = {k: [None] * depth for k in ("ln1_g", "ln1_b", "ln2_g", "ln2_b")}
    small_b = {k: [None] * (depth // 2) for k in ("w_s", "b_s", "vnorm_g", "vnorm_b")}
    def scatter_end(started, layer, group, after):
        tag = "a" if layer % 2 == 0 else "b"
        red = _reduce_scatter_end(started, after, place, f"{tag}_{group}")
        if group == "ff":
            g_big["w_ff1"][layer], g_big["w_ff2"][layer] = red
        else:
            g_big["w_in_a" if layer % 2 == 0 else "w_in_b"][layer // 2], g_big["w_mem_kv"][layer], g_big["w_out"][layer] = red

    pending, tokens = [], ()
    for i in reversed(range(depth)):
        jl = i // 2
        is_a = i % 2 == 0
        s = saved[i]
        win, wkv, wout, wff1, wff2 = s["w"]
        tag = "a" if is_a else "b"
        d_f, adr2, dg2, db2 = _ln_bwd(dx, s["x1"], s["f"], ln2_g[i], alpha, tokens, name="ln_bwd")
        small["ln2_g"][i], small["ln2_b"][i] = dg2, db2
        gw_ff2 = _mm_tn(s["hid"], d_f, 1, name="grad_ff2").reshape(4, D, D)
        da = _mm_nt(d_f, wff2, epi="drelu2", extra=s["a_pre"], name="ff2_bwd")
        gw_ff1 = _mm_tn(s["x1b"], da, 4, name="grad_ff1")
        ff_started = _reduce_scatter_begin([gw_ff1, gw_ff2], place, f"{tag}_ff")
        dx1 = _mm_nt(da, wff1, epi="resid", extra=adr2, name="ff1_bwd")
        d_y, adr1, dg1, db1 = _ln_bwd(dx1, s["xf"], s["y"], ln1_g[i], alpha, (ff_started[4],), name="ln_bwd")
        small["ln1_g"][i], small["ln1_b"][i] = dg1, db1
        gw_out = _mm_tn(s["cat"], d_y, 4, name="grad_out")
        dcat = _mm_nt(d_y, wout, out_dtype=F32, name="proj_out_bwd")
        ocol = dcat.shape[1] // MEM_WIDTH - 1
        dmq, dkv = _mem_bwd(s["P"], s["qcol"], s["kv"], dcat, ocol, name=f"mem_bwd_{tag}")
        gw_kv = _mm_tn(mem_b, dkv.astype(BF16), 1, name="grad_kv").reshape(4, D // 4, 2 * MEM_WIDTH)
        if is_a:
            (lse,) = s["extra"]
            dmix = dcat[:, :a_out]
            parts = []
            for g, (_, d) in enumerate(A_PAIRS):
                Pg, g0 = group_view(s["P"], g, d)
                parts += _attn_bwd(Pg, dmix, s["mix"], lse, g0, d, a_out, name=f"attn_bwd_d{d}")
            dP = jnp.concatenate(parts + [dmq], axis=1)
        else:
            vg_full, vb_full, bs_t = s["extra"]
            dpu, dpv, dws, dbs, dvg, dvb = _gmlp_bwd(s["P"], w_s[jl], bs_t, vg_full, vb_full, dcat, b_width,
                                                     name="gmlp_bwd")
            small_b["w_s"][jl], small_b["b_s"][jl] = dws, dbs[:, :G].T
            small_b["vnorm_g"][jl], small_b["vnorm_b"][jl] = dvg, dvb
            dP = jnp.concatenate([dpu, dpv, dmq], axis=1)
        gw_in = _mm_tn(s["xb"], dP, 4, name=f"grad_in_{tag}")
        dx = _mm_nt(dP, win, epi="resid", extra=adr1, name=f"proj_in_bwd_{tag}")
        for item in pending:
            scatter_end(*item, dx)
        mix_started = _reduce_scatter_begin([gw_in, gw_kv, gw_out], place, f"{tag}_mix")
        pending, tokens = [(ff_started, i, "ff"), (mix_started, i, "mix")], (mix_started[4],)
    for item in pending:
        scatter_end(*item, dx)
    grad_x = dx.reshape(x.shape)

    nb_layers = depth // 2
    pieces = ([jnp.stack(small_b["w_s"]).reshape(-1, 128), jnp.stack(small_b["b_s"]).reshape(-1, 128),
               jnp.stack(small_b["vnorm_g"]).reshape(-1, 128), jnp.stack(small_b["vnorm_b"]).reshape(-1, 128)]
              + [jnp.stack(small[k]).reshape(-1, 128) for k in ("ln1_g", "ln1_b", "ln2_g", "ln2_b")])
    sizes = [p.shape[0] for p in pieces]
    pad = (-sum(sizes)) % 8
    packed = jnp.concatenate(pieces + ([jnp.zeros((pad, 128), F32)] if pad else []), axis=0)
    summed = _all_reduce_small(packed, name="all_reduce_small")
    offs = [0]
    for n_ in sizes:
        offs.append(offs[-1] + n_)
    sp = [summed[offs[k]:offs[k + 1]] for k in range(len(sizes))]
    vshard = vnorm_g.shape[1]
    g_small = dict(
        w_s=sp[0].reshape(w_s.shape), b_s=sp[1].reshape(b_s.shape),
        vnorm_g=lax.dynamic_slice_in_dim(sp[2].reshape(nb_layers, -1), c_idx * vshard, vshard, axis=1),
        vnorm_b=lax.dynamic_slice_in_dim(sp[3].reshape(nb_layers, -1), c_idx * vshard, vshard, axis=1),
        ln1_g=sp[4].reshape(ln1_g.shape), ln1_b=sp[5].reshape(ln1_b.shape),
        ln2_g=sp[6].reshape(ln2_g.shape), ln2_b=sp[7].reshape(ln2_b.shape))

    names = ["w_in_a", "w_in_b", "w_s", "b_s", "vnorm_g", "vnorm_b", "w_mem_kv", "w_out", "ln1_g", "ln1_b", "w_ff1",
             "w_ff2", "ln2_g", "ln2_b"]
    ws = dict(w_in_a=w_in_a, w_in_b=w_in_b, w_s=w_s, b_s=b_s, vnorm_g=vnorm_g, vnorm_b=vnorm_b, w_mem_kv=w_mem_kv,
              w_out=w_out, ln1_g=ln1_g, ln1_b=ln1_b, w_ff1=w_ff1, w_ff2=w_ff2, ln2_g=ln2_g, ln2_b=ln2_b)
    ms = dict(w_in_a=m_w_in_a, w_in_b=m_w_in_b, w_s=m_w_s, b_s=m_b_s, vnorm_g=m_vnorm_g, vnorm_b=m_vnorm_b,
              w_mem_kv=m_w_mem_kv, w_out=m_w_out, ln1_g=m_ln1_g, ln1_b=m_ln1_b, w_ff1=m_w_ff1, w_ff2=m_w_ff2,
              ln2_g=m_ln2_g, ln2_b=m_ln2_b)
    vs = dict(w_in_a=v_w_in_a, w_in_b=v_w_in_b, w_s=v_w_s, b_s=v_b_s, vnorm_g=v_vnorm_g, vnorm_b=v_vnorm_b,
              w_mem_kv=v_w_mem_kv, w_out=v_w_out, ln1_g=v_ln1_g, ln1_b=v_ln1_b, w_ff1=v_w_ff1, w_ff2=v_w_ff2,
              ln2_g=v_ln2_g, ln2_b=v_ln2_b)
    grads, deltas, new_m, new_v = {}, {}, {}, {}
    for k in g_big:
        g = jnp.stack(g_big[k]).reshape(ws[k].shape)
        cols = ws[k].shape[-1]
        d_, m_, v_ = _adamw(ws[k].reshape(-1, cols), g.reshape(-1, cols), ms[k].reshape(-1, cols),
                            vs[k].reshape(-1, cols), name=f"adamw_{k}")
        grads[k] = g
        deltas[k], new_m[k], new_v[k] = (t.reshape(ws[k].shape) for t in (d_, m_, v_))
    small_names = [k for k in names if k not in g_big]

    def pack(tree):
        flat = jnp.concatenate([tree[k].reshape(-1) for k in small_names])
        padn = (-flat.shape[0]) % 1024
        return jnp.pad(flat, (0, padn)).reshape(-1, 128)

    d_, m_, v_ = _adamw(pack(ws), pack(g_small), pack(ms), pack(vs), name="adamw_small")
    off = 0
    for k in small_names:
        n_ = ws[k].size
        grads[k] = g_small[k]
        deltas[k], new_m[k], new_v[k] = (t.reshape(-1)[off:off + n_].reshape(ws[k].shape) for t in (d_, m_, v_))
        off += n_

    return (loss, grad_x, *[grads[k] for k in names], *[deltas[k] for k in names], *[new_m[k] for k in names],
            *[new_v[k] for k in names])
```

```python
import functools
import math

import jax
import jax.numpy as jnp
from jax import lax
from jax.experimental import pallas as pl
from jax.experimental.pallas import tpu as pltpu

F32 = jnp.float32
BF16 = jnp.bfloat16

HEAD_DIM = 128
A_PAIRS = ((128, 1), (512, 4), (2048, 16))
BLK = 128
HEAD_BLOCK = 512
MEM_TOKENS = 256
MEM_HEADS = 4
MEM_WIDTH = MEM_HEADS * HEAD_DIM
LN_EPS = 1e-5
ADAM_LR, ADAM_B1, ADAM_B2, ADAM_EPS, ADAM_WD, ADAM_STEP = 0.001, 0.9, 0.999, 1e-08, 0.01, 10
NEG = -1e30
V7X_VMEM_LIMIT = 56 * 1024 * 1024
V7X_MATMUL_VMEM_BUDGET = 40 * 1024 * 1024
MESH = pl.DeviceIdType.MESH
HBM = pl.BlockSpec(memory_space=pltpu.HBM)


def _call(body, **kw):
    return pl.pallas_call(body, **kw)


def _params(sem):
    return pltpu.CompilerParams(dimension_semantics=sem, vmem_limit_bytes=V7X_VMEM_LIMIT)


def _tile(n, cap):
    best = 0
    for t in range(128, min(n, cap) + 1, 128):
        if n % t == 0:
            best = t
    if best == 0 or (best < 512 and n <= 2560):
        return n
    return best


def _depth(k, per_unit_bytes, fixed_bytes):
    t = _tile(k, 2048)
    while t > 512 and 2 * t * per_unit_bytes + fixed_bytes > V7X_MATMUL_VMEM_BUDGET:
        smaller = _tile(k, t // 2)
        if smaller >= t:
            break
        t = smaller
    return t


def _rows(n, cap):
    best = 8
    for t in range(8, min(n, cap) + 1, 8):
        if n % t == 0:
            best = t
    return best


def _epilogue(epi, acc, extra_ref, out_refs):
    if epi == "plain":
        out_refs[0][...] = acc.astype(out_refs[0].dtype)
    elif epi == "relu2":
        out_refs[0][...] = acc
        r = jnp.maximum(acc, 0.0)
        out_refs[1][...] = (r * r).astype(out_refs[1].dtype)
    elif epi == "drelu2":
        out_refs[0][...] = (acc * (2.0 * jnp.maximum(extra_ref[...], 0.0))).astype(out_refs[0].dtype)
    elif epi == "resid":
        out_refs[0][...] = acc + extra_ref[...]
    else:
        raise ValueError(epi)


def _mm_nn(a, w, *, epi="plain", out_dtype=BF16, extra=None, name):
    M, K = a.shape
    J, K2, Nj = w.shape
    assert K == K2
    tn = _tile(Nj, 1024)
    tm = _tile(M, 1024 if tn <= 1024 else 512)
    n_out = 2 if epi == "relu2" else 1
    has_extra = extra is not None
    out_bytes = 12 if epi == "relu2" else (8 if epi == "resid" or out_dtype == F32 else 4)
    tk = _depth(K, 2 * (tm + tn), tm * tn * (4 + out_bytes + 8 * has_extra))
    nn, nk = Nj // tn, K // tk

    def body(*refs):
        a_ref, w_ref = refs[0], refs[1]
        extra_ref = refs[2] if has_extra else None
        outs = refs[2 + has_extra: 2 + has_extra + n_out]
        acc_ref = refs[-1]
        k = pl.program_id(3)

        @pl.when(k == 0)
        def _():
            acc_ref[...] = jnp.zeros_like(acc_ref)

        acc_ref[...] += jnp.dot(a_ref[...], w_ref[...], preferred_element_type=F32)

        @pl.when(k == nk - 1)
        def _():
            _epilogue(epi, acc_ref[...], extra_ref, outs)

    omap = lambda i, j, n, k: (i, j * nn + n)
    in_specs = [pl.BlockSpec((tm, tk), lambda i, j, n, k: (i, k)),
                pl.BlockSpec((None, tk, tn), lambda i, j, n, k: (j, k, n))]
    args = [a, w]
    if has_extra:
        in_specs.append(pl.BlockSpec((tm, tn), omap))
        args.append(extra)
    if epi == "relu2":
        out_shape = (jax.ShapeDtypeStruct((M, J * Nj), F32), jax.ShapeDtypeStruct((M, J * Nj), BF16))
        out_specs = (pl.BlockSpec((tm, tn), omap), pl.BlockSpec((tm, tn), omap))
    else:
        out_shape = jax.ShapeDtypeStruct((M, J * Nj), F32 if epi == "resid" else out_dtype)
        out_specs = pl.BlockSpec((tm, tn), omap)
    return _call(body, name=name, grid=(M // tm, J, nn, nk), in_specs=in_specs, out_specs=out_specs,
                 out_shape=out_shape, scratch_shapes=[pltpu.VMEM((tm, tn), F32)],
                 compiler_params=_params(("parallel", "parallel", "parallel", "arbitrary")))(*args)


def _mm_nt(a, w, *, epi="plain", out_dtype=BF16, extra=None, after=(), name):
    M, N = a.shape
    J, K, Nj = w.shape
    assert N == J * Nj
    tko = _tile(K, 1024)
    tm = _tile(M, 1024 if _tile(Nj, 1024) <= 1024 else 512)
    has_extra = extra is not None
    out_bytes = 8 if epi == "resid" or out_dtype == F32 else 4
    tc = _depth(Nj, 2 * (tm + tko), tm * tko * (4 + out_bytes + 8 * has_extra))
    nc = Nj // tc

    def body(*refs):
        a_ref, w_ref = refs[0], refs[1]
        extra_ref = refs[2] if has_extra else None
        outs = refs[2 + has_extra + len(after): 3 + has_extra + len(after)]
        acc_ref = refs[-1]
        j, c = pl.program_id(2), pl.program_id(3)

        @pl.when(jnp.logical_and(j == 0, c == 0))
        def _():
            acc_ref[...] = jnp.zeros_like(acc_ref)

        acc_ref[...] += lax.dot_general(a_ref[...], w_ref[...], (((1,), (1,)), ((), ())),
                                        preferred_element_type=F32)

        @pl.when(jnp.logical_and(j == J - 1, c == nc - 1))
        def _():
            _epilogue(epi, acc_ref[...], extra_ref, outs)

    omap = lambda i, ko, j, c: (i, ko)
    in_specs = [pl.BlockSpec((tm, tc), lambda i, ko, j, c: (i, j * nc + c)),
                pl.BlockSpec((None, tko, tc), lambda i, ko, j, c: (j, ko, c))]
    args = [a, w]
    if has_extra:
        in_specs.append(pl.BlockSpec((tm, tko), omap))
        args.append(extra)
    in_specs += [pl.BlockSpec(memory_space=pl.ANY)] * len(after)
    args += list(after)
    out_shape = jax.ShapeDtypeStruct((M, K), F32 if epi == "resid" else out_dtype)
    return _call(body, name=name, grid=(M // tm, K // tko, J, nc), in_specs=in_specs,
                 out_specs=pl.BlockSpec((tm, tko), omap), out_shape=out_shape,
                 scratch_shapes=[pltpu.VMEM((tm, tko), F32)],
                 compiler_params=_params(("parallel", "parallel", "arbitrary", "arbitrary")))(*args)


def _mm_tn(a, b, J, *, name):
    T, K = a.shape
    T2, N = b.shape
    assert T == T2 and N % J == 0
    Nj = N // J
    tn = _tile(Nj, 1024)
    tkr = _tile(K, 1024 if tn <= 1024 else 512)
    tt = _depth(T, 2 * (tkr + tn), tkr * tn * (4 + 4))
    nn, nt = Nj // tn, T // tt

    def body(a_ref, b_ref, o_ref, acc_ref):
        t = pl.program_id(3)

        @pl.when(t == 0)
        def _():
            acc_ref[...] = jnp.zeros_like(acc_ref)

        acc_ref[...] += lax.dot_general(a_ref[...], b_ref[...], (((0,), (0,)), ((), ())),
                                        preferred_element_type=F32)

        @pl.when(t == nt - 1)
        def _():
            o_ref[...] = acc_ref[...].astype(o_ref.dtype)

    return _call(body, name=name, grid=(J, K // tkr, nn, nt),
                 in_specs=[pl.BlockSpec((tt, tkr), lambda j, kr, n, t: (t, kr)),
                           pl.BlockSpec((tt, tn), lambda j, kr, n, t: (t, j * nn + n))],
                 out_specs=pl.BlockSpec((None, tkr, tn), lambda j, kr, n, t: (j, kr, n)),
                 out_shape=jax.ShapeDtypeStruct((J, K, Nj), BF16),
                 scratch_shapes=[pltpu.VMEM((tkr, tn), F32)],
                 compiler_params=_params(("parallel", "parallel", "parallel", "arbitrary")))(a, b)


def _ln_fwd(x, y, g, b, alpha, *, name):
    T, D = x.shape
    tr = _rows(T, 256)

    def body(x_ref, y_ref, g_ref, b_ref, o_ref, ob_ref):
        r = alpha * x_ref[...] + y_ref[...]
        mu = jnp.mean(r, axis=-1, keepdims=True)
        xc = r - mu
        var = jnp.mean(xc * xc, axis=-1, keepdims=True)
        o = xc * lax.rsqrt(var + LN_EPS) * g_ref[...] + b_ref[...]
        o_ref[...] = o
        ob_ref[...] = o.astype(BF16)

    row = pl.BlockSpec((tr, D), lambda i: (i, 0))
    vec = pl.BlockSpec((1, D), lambda i: (0, 0))
    return _call(body, name=name, grid=(T // tr,), in_specs=[row, row, vec, vec], out_specs=(row, row),
                 out_shape=(jax.ShapeDtypeStruct((T, D), F32), jax.ShapeDtypeStruct((T, D), BF16)),
                 compiler_params=_params(("parallel",)))(x, y, g.reshape(1, D), b.reshape(1, D))


def _ln_bwd(dout, x, y, g, alpha, after=(), *, name):
    T, D = x.shape
    tr = _rows(T, 256)
    na = len(after)

    def body(*refs):
        do_ref, x_ref, y_ref, g_ref = refs[:4]
        drb_ref, adr_ref, dg_ref, db_ref = refs[4 + na:]
        i = pl.program_id(0)
        r = alpha * x_ref[...] + y_ref[...]
        mu = jnp.mean(r, axis=-1, keepdims=True)
        xc = r - mu
        var = jnp.mean(xc * xc, axis=-1, keepdims=True)
        rstd = lax.rsqrt(var + LN_EPS)
        xhat = xc * rstd
        do = do_ref[...]
        dxh = do * g_ref[...]
        dr = rstd * (dxh - jnp.mean(dxh, axis=-1, keepdims=True)
                     - xhat * jnp.mean(dxh * xhat, axis=-1, keepdims=True))
        drb_ref[...] = dr.astype(BF16)
        adr_ref[...] = alpha * dr

        @pl.when(i == 0)
        def _():
            dg_ref[...] = jnp.zeros_like(dg_ref)
            db_ref[...] = jnp.zeros_like(db_ref)

        dg_ref[...] += jnp.sum(do * xhat, axis=0, keepdims=True)
        db_ref[...] += jnp.sum(do, axis=0, keepdims=True)

    row = pl.BlockSpec((tr, D), lambda i: (i, 0))
    vec = pl.BlockSpec((1, D), lambda i: (0, 0))
    return _call(body, name=name, grid=(T // tr,),
                 in_specs=[row, row, row, vec] + [pl.BlockSpec(memory_space=pl.ANY)] * na,
                 out_specs=(row, row, vec, vec),
                 out_shape=(jax.ShapeDtypeStruct((T, D), BF16), jax.ShapeDtypeStruct((T, D), F32),
                            jax.ShapeDtypeStruct((1, D), F32), jax.ShapeDtypeStruct((1, D), F32)),
                 compiler_params=_params(("arbitrary",)))(dout, x, y, g.reshape(1, D), *after)


def _loss_head(xf, target, *, name):
    T, D = xf.shape
    tr = _rows(T, 256)

    def body(x_ref, t_ref, dy_ref, s_ref):
        i = pl.program_id(0)
        err = x_ref[...] - t_ref[...]
        dy_ref[...] = err * (1.0 / D)

        @pl.when(i == 0)
        def _():
            s_ref[...] = jnp.zeros_like(s_ref)

        s_ref[...] += jnp.sum(jnp.sum(err * err, axis=1, keepdims=True), axis=0, keepdims=True)

    row = pl.BlockSpec((tr, D), lambda i: (i, 0))
    return _call(body, name=name, grid=(T // tr,), in_specs=[row, row],
                 out_specs=(row, pl.BlockSpec((8, 128), lambda i: (0, 0))),
                 out_shape=(jax.ShapeDtypeStruct((T, D), F32), jax.ShapeDtypeStruct((8, 128), F32)),
                 compiler_params=_params(("arbitrary",)))(xf, target)


def _adamw(w, g, m, v, *, name):
    R, C = w.shape
    tr = _rows(R, max(8, (1 << 18) // C // 8 * 8))

    def body(w_ref, g_ref, m_ref, v_ref, d_ref, nm_ref, nv_ref):
        g_ = g_ref[...]
        m_ = ADAM_B1 * m_ref[...] + (1.0 - ADAM_B1) * g_
        v_ = ADAM_B2 * v_ref[...] + (1.0 - ADAM_B2) * (g_ * g_)
        m_hat = m_ / (1.0 - ADAM_B1 ** ADAM_STEP)
        v_hat = v_ / (1.0 - ADAM_B2 ** ADAM_STEP)
        d_ref[...] = -ADAM_LR * (m_hat / (jnp.sqrt(v_hat) + ADAM_EPS) + ADAM_WD * w_ref[...])
        nm_ref[...] = m_
        nv_ref[...] = v_

    blk = pl.BlockSpec((tr, C), lambda i: (i, 0))
    sds = jax.ShapeDtypeStruct((R, C), F32)
    return _call(body, name=name, grid=(R // tr,), in_specs=[blk] * 4, out_specs=(blk,) * 3,
                 out_shape=(sds,) * 3, compiler_params=_params(("parallel",)))(w, g, m, v)


def _dot_nt(a, b):
    return lax.dot_general(a, b, (((1,), (1,)), ((), ())), preferred_element_type=F32)


def _dot_tn(a, b):
    return lax.dot_general(a, b, (((0,), (0,)), ((), ())), preferred_element_type=F32)


def _band_masks():
    qi = lax.broadcasted_iota(jnp.int32, (BLK, BLK), 0)
    kj = lax.broadcasted_iota(jnp.int32, (BLK, BLK), 1)
    return qi >= kj, kj >= qi


def _attn_fwd(P, grp, d, a_out, *, name):
    T, C = P.shape
    L = T // d
    nb = L // BLK
    nhh = a_out // HEAD_BLOCK
    cb = C // HEAD_BLOCK
    q0 = grp * 3 * nhh
    scale = HEAD_DIM ** -0.5

    def body(q_ref, kc_ref, kp_ref, vc_ref, vp_ref, o_ref, l_ref):
        b = pl.program_id(1)
        mask_c, mask_p = _band_masks()
        mask_p = jnp.logical_and(mask_p, b > 0)
        for h in range(HEAD_BLOCK // HEAD_DIM):
            hs = slice(h * HEAD_DIM, (h + 1) * HEAD_DIM)
            q = q_ref[:, hs]
            s_c = jnp.where(mask_c, _dot_nt(q, kc_ref[:, hs]) * scale, NEG)
            s_p = jnp.where(mask_p, _dot_nt(q, kp_ref[:, hs]) * scale, NEG)
            m = jnp.maximum(jnp.max(s_c, axis=1, keepdims=True), jnp.max(s_p, axis=1, keepdims=True))
            p_c = jnp.exp(s_c - m)
            p_p = jnp.exp(s_p - m)
            l = jnp.sum(p_c, axis=1, keepdims=True) + jnp.sum(p_p, axis=1, keepdims=True)
            o = (jnp.dot(p_c.astype(BF16), vc_ref[:, hs], preferred_element_type=F32)
                 + jnp.dot(p_p.astype(BF16), vp_ref[:, hs], preferred_element_type=F32))
            o_ref[:, hs] = o / l
            l_ref[:, hs] = jnp.broadcast_to(m + jnp.log(l), (BLK, HEAD_DIM))

    def cur(part):
        return pl.BlockSpec((BLK, HEAD_BLOCK), lambda r, b, hh: (b, r * cb + q0 + part * nhh + hh))

    def prev(part):
        return pl.BlockSpec((BLK, HEAD_BLOCK), lambda r, b, hh: (jnp.maximum(b - 1, 0), r * cb + q0 + part * nhh + hh))

    out = pl.BlockSpec((BLK, HEAD_BLOCK), lambda r, b, hh: (b, r * nhh + hh))
    Pv = P.reshape(L, d * C)
    o, lse = _call(body, name=name, grid=(d, nb, nhh), in_specs=[cur(0), cur(1), prev(1), cur(2), prev(2)],
                   out_specs=(out, out),
                   out_shape=(jax.ShapeDtypeStruct((L, d * a_out), F32),) * 2,
                   compiler_params=_params(("parallel", "parallel", "parallel")))(Pv, Pv, Pv, Pv, Pv)
    return o.reshape(T, a_out), lse.reshape(T, a_out)


def _attn_combine(os_, lses, *, name):
    T, W = os_[0].shape
    tr = _rows(T, 256)
    n = len(os_)

    def body(*refs):
        o_refs, l_refs = refs[:n], refs[n:2 * n]
        mix_ref, lse_ref = refs[2 * n], refs[2 * n + 1]
        ls = [r[...] for r in l_refs]
        m = functools.reduce(jnp.maximum, ls)
        es = [jnp.exp(l - m) for l in ls]
        tot = functools.reduce(lambda a, b: a + b, es)
        mix = functools.reduce(lambda a, b: a + b, [(e / tot) * o[...] for e, o in zip(es, o_refs)])
        mix_ref[...] = mix
        lse_ref[...] = m + jnp.log(tot)

    blk = pl.BlockSpec((tr, W), lambda i: (i, 0))
    sds = jax.ShapeDtypeStruct((T, W), F32)
    return _call(body, name=name, grid=(T // tr,), in_specs=[blk] * (2 * n), out_specs=(blk, blk),
                 out_shape=(sds, sds), compiler_params=_params(("parallel",)))(*os_, *lses)


def _attn_bwd(P, dO, O, LSE, grp, d, a_out, *, name):
    T, C = P.shape
    L = T // d
    nb = L // BLK
    nhh = a_out // HEAD_BLOCK
    cb = C // HEAD_BLOCK
    q0 = grp * 3 * nhh
    scale = HEAD_DIM ** -0.5

    def body(q_ref, qn_ref, kc_ref, kp_ref, vc_ref, vp_ref, do_ref, don_ref, o_ref, on_ref, l_ref, ln_ref,
             dq_ref, dk_ref, dv_ref):
        b = pl.program_id(1)
        mask_c, mask_prev = _band_masks()
        mask_p = jnp.logical_and(mask_prev, b > 0)
        mask_n = jnp.logical_and(mask_prev, b < nb - 1)
        for h in range(HEAD_BLOCK // HEAD_DIM):
            hs = slice(h * HEAD_DIM, (h + 1) * HEAD_DIM)
            q, qn, kc, kp, vc, vp = (r[:, hs] for r in (q_ref, qn_ref, kc_ref, kp_ref, vc_ref, vp_ref))
            do, don = do_ref[:, hs], don_ref[:, hs]
            lse, lse_n = l_ref[:, hs], ln_ref[:, hs]
            delta = jnp.sum(do * o_ref[:, hs], axis=1, keepdims=True)
            delta_n = jnp.sum(don * on_ref[:, hs], axis=1, keepdims=True)
            dob, donb = do.astype(BF16), don.astype(BF16)
            p_c = jnp.exp(jnp.where(mask_c, _dot_nt(q, kc) * scale, NEG) - lse)
            p_p = jnp.exp(jnp.where(mask_p, _dot_nt(q, kp) * scale, NEG) - lse)
            p_n = jnp.exp(jnp.where(mask_n, _dot_nt(qn, kc) * scale, NEG) - lse_n)
            ds_c = (p_c * (_dot_nt(dob, vc) - delta) * scale).astype(BF16)
            ds_p = (p_p * (_dot_nt(dob, vp) - delta) * scale).astype(BF16)
            ds_n = (p_n * (_dot_nt(donb, vc) - delta_n) * scale).astype(BF16)
            dq = jnp.dot(ds_c, kc, preferred_element_type=F32) + jnp.dot(ds_p, kp, preferred_element_type=F32)
            dk = _dot_tn(ds_c, q) + _dot_tn(ds_n, qn)
            dv = _dot_tn(p_c.astype(BF16), dob) + _dot_tn(p_n.astype(BF16), donb)
            dq_ref[:, hs] = dq.astype(BF16)
            dk_ref[:, hs] = dk.astype(BF16)
            dv_ref[:, hs] = dv.astype(BF16)

    def pspec(part, shift):
        def imap(r, b, hh):
            return (jnp.clip(b + shift, 0, nb - 1), r * cb + q0 + part * nhh + hh)
        return pl.BlockSpec((BLK, HEAD_BLOCK), imap)

    def aspec(shift):
        return pl.BlockSpec((BLK, HEAD_BLOCK), lambda r, b, hh: (jnp.clip(b + shift, 0, nb - 1), r * nhh + hh))

    Pv = P.reshape(L, d * C)
    dOv, Ov, Lv = (t.reshape(L, d * a_out) for t in (dO, O, LSE))
    sds = jax.ShapeDtypeStruct((L, d * a_out), BF16)
    dq, dk, dv = _call(
        body, name=name, grid=(d, nb, nhh),
        in_specs=[pspec(0, 0), pspec(0, 1), pspec(1, 0), pspec(1, -1), pspec(2, 0), pspec(2, -1),
                  aspec(0), aspec(1), aspec(0), aspec(1), aspec(0), aspec(1)],
        out_specs=(aspec(0),) * 3, out_shape=(sds,) * 3,
        compiler_params=_params(("parallel", "parallel", "parallel")))(Pv, Pv, Pv, Pv, Pv, Pv, dOv, dOv, Ov, Ov, Lv, Lv)
    return [t.reshape(T, a_out) for t in (dq, dk, dv)]


def _mem_softmax(q, k, scale):
    s = _dot_nt(q, k) * scale
    e = jnp.exp(s - jnp.max(s, axis=1, keepdims=True))
    return e / jnp.sum(e, axis=1, keepdims=True)


def _mem_fwd(P, qcol, kv, *, name):
    T = P.shape[0]
    tq = _rows(T, 512)
    scale = HEAD_DIM ** -0.5

    def body(q_ref, kv_ref, o_ref):
        for h in range(MEM_HEADS):
            hs = slice(h * HEAD_DIM, (h + 1) * HEAD_DIM)
            vs = slice(MEM_WIDTH + h * HEAD_DIM, MEM_WIDTH + (h + 1) * HEAD_DIM)
            p = _mem_softmax(q_ref[:, hs].astype(BF16), kv_ref[:, hs], scale)
            o_ref[:, hs] = jnp.dot(p.astype(BF16), kv_ref[:, vs], preferred_element_type=F32)

    return _call(body, name=name, grid=(T // tq,),
                 in_specs=[pl.BlockSpec((tq, MEM_WIDTH), lambda i: (i, qcol)),
                           pl.BlockSpec((MEM_TOKENS, 2 * MEM_WIDTH), lambda i: (0, 0))],
                 out_specs=pl.BlockSpec((tq, MEM_WIDTH), lambda i: (i, 0)),
                 out_shape=jax.ShapeDtypeStruct((T, MEM_WIDTH), F32),
                 compiler_params=_params(("parallel",)))(P, kv)


def _mem_bwd(P, qcol, kv, dcat, ocol, *, name):
    T = P.shape[0]
    tq = _rows(T, 512)
    scale = HEAD_DIM ** -0.5

    def body(q_ref, kv_ref, do_ref, dq_ref, dkv_ref):
        i = pl.program_id(0)

        @pl.when(i == 0)
        def _():
            dkv_ref[...] = jnp.zeros_like(dkv_ref)

        for h in range(MEM_HEADS):
            hs = slice(h * HEAD_DIM, (h + 1) * HEAD_DIM)
            vs = slice(MEM_WIDTH + h * HEAD_DIM, MEM_WIDTH + (h + 1) * HEAD_DIM)
            q = q_ref[:, hs].astype(BF16)
            k, v = kv_ref[:, hs], kv_ref[:, vs]
            do = do_ref[:, hs].astype(BF16)
            p = _mem_softmax(q, k, scale)
            dp = _dot_nt(do, v)
            ds = (p * (dp - jnp.sum(p * dp, axis=1, keepdims=True)) * scale).astype(BF16)
            dq_ref[:, hs] = jnp.dot(ds, k, preferred_element_type=F32).astype(BF16)
            dkv_ref[:, hs] += _dot_tn(ds, q)
            dkv_ref[:, vs] += _dot_tn(p.astype(BF16), do)

    return _call(body, name=name, grid=(T // tq,),
                 in_specs=[pl.BlockSpec((tq, MEM_WIDTH), lambda i: (i, qcol)),
                           pl.BlockSpec((MEM_TOKENS, 2 * MEM_WIDTH), lambda i: (0, 0)),
                           pl.BlockSpec((tq, MEM_WIDTH), lambda i: (i, ocol))],
                 out_specs=(pl.BlockSpec((tq, MEM_WIDTH), lambda i: (i, 0)),
                            pl.BlockSpec((MEM_TOKENS, 2 * MEM_WIDTH), lambda i: (0, 0))),
                 out_shape=(jax.ShapeDtypeStruct((T, MEM_WIDTH), BF16),
                            jax.ShapeDtypeStruct((MEM_TOKENS, 2 * MEM_WIDTH), F32)),
                 compiler_params=_params(("arbitrary",)))(P, kv, dcat)


_SQRT_HALF = 0.7071067811865476
_INV_SQRT_2PI = 0.3989422804014327


def _gelu(x):
    return 0.5 * x * (1.0 + lax.erf(x * _SQRT_HALF))


def _gelu_grad(x):
    return 0.5 * (1.0 + lax.erf(x * _SQRT_HALF)) + x * (_INV_SQRT_2PI * jnp.exp(-0.5 * x * x))


def _tril():
    t = lax.broadcasted_iota(jnp.int32, (BLK, BLK), 0)
    s = lax.broadcasted_iota(jnp.int32, (BLK, BLK), 1)
    return t >= s


def _gmlp_fwd(P, w_s, bs_t, vg, vb, width, *, name):
    T = P.shape[0]
    G = width // HEAD_DIM
    tb = _rows(T, 512)

    def body(pu_ref, pv_ref, ws_ref, bs_ref, vg_ref, vb_ref, o_ref):
        u = _gelu(pu_ref[...])
        v = _gelu(pv_ref[...])
        mu = jnp.mean(v, axis=-1, keepdims=True)
        vc = v - mu
        var = jnp.mean(vc * vc, axis=-1, keepdims=True)
        vn = (vc * lax.rsqrt(var + LN_EPS) * vg_ref[...] + vb_ref[...]).astype(BF16)
        tril = _tril()
        for g in range(G):
            gs = slice(g * HEAD_DIM, (g + 1) * HEAD_DIM)
            ws = jnp.where(tril, ws_ref[g], 0.0).astype(BF16)
            bias = bs_ref[:, g:g + 1]
            for c in range(tb // BLK):
                cs = slice(c * BLK, (c + 1) * BLK)
                sg = jnp.dot(ws, vn[cs, gs], preferred_element_type=F32) + bias
                o_ref[cs, gs] = u[cs, gs] * sg

    blk = lambda col: pl.BlockSpec((tb, width), lambda i: (i, col))
    full = lambda shape: pl.BlockSpec(shape, lambda i: (0,) * len(shape))
    return _call(body, name=name, grid=(T // tb,),
                 in_specs=[blk(0), blk(1), full((G, BLK, BLK)), full((BLK, G)), full((1, width)), full((1, width))],
                 out_specs=blk(0), out_shape=jax.ShapeDtypeStruct((T, width), F32),
                 compiler_params=_params(("parallel",)))(P, P, w_s, bs_t, vg.reshape(1, width), vb.reshape(1, width))


def _gmlp_bwd(P, w_s, bs_t, vg, vb, dcat, width, *, name):
    T = P.shape[0]
    G = width // HEAD_DIM
    tb = _rows(T, 512)

    def body(pu_ref, pv_ref, ws_ref, bs_ref, vg_ref, vb_ref, dm_ref, dpu_ref, dpv_ref, dws_ref, dbs_ref, dvg_ref,
             dvb_ref, dvn_ref):
        i = pl.program_id(0)

        @pl.when(i == 0)
        def _():
            dws_ref[...] = jnp.zeros_like(dws_ref)
            dbs_ref[...] = jnp.zeros_like(dbs_ref)
            dvg_ref[...] = jnp.zeros_like(dvg_ref)
            dvb_ref[...] = jnp.zeros_like(dvb_ref)

        pu, pv = pu_ref[...], pv_ref[...]
        u = _gelu(pu)
        v = _gelu(pv)
        mu = jnp.mean(v, axis=-1, keepdims=True)
        vc = v - mu
        var = jnp.mean(vc * vc, axis=-1, keepdims=True)
        rstd = lax.rsqrt(var + LN_EPS)
        xhat = vc * rstd
        vn = (xhat * vg_ref[...] + vb_ref[...]).astype(BF16)
        dm = dm_ref[...]
        tril = _tril()
        lane = lax.broadcasted_iota(jnp.int32, (BLK, BLK), 1)
        dbs = jnp.zeros((BLK, BLK), F32)
        for g in range(G):
            gs = slice(g * HEAD_DIM, (g + 1) * HEAD_DIM)
            ws = jnp.where(tril, ws_ref[g], 0.0).astype(BF16)
            bias = bs_ref[:, g:g + 1]
            dws = jnp.zeros((BLK, BLK), F32)
            rs = jnp.zeros((BLK, 1), F32)
            for c in range(tb // BLK):
                cs = slice(c * BLK, (c + 1) * BLK)
                vn_cg = vn[cs, gs]
                sg = jnp.dot(ws, vn_cg, preferred_element_type=F32) + bias
                dm_cg = dm[cs, gs]
                dpu_ref[cs, gs] = (dm_cg * sg * _gelu_grad(pu[cs, gs])).astype(BF16)
                dsg = dm_cg * u[cs, gs]
                dsgb = dsg.astype(BF16)
                dvn_ref[cs, gs] = _dot_tn(ws, dsgb)
                dws = dws + _dot_nt(dsgb, vn_cg)
                rs = rs + jnp.sum(dsg, axis=1, keepdims=True)
            dws_ref[g] += jnp.where(tril, dws, 0.0)
            dbs = dbs + jnp.where(lane == g, rs, 0.0)
        dbs_ref[...] += dbs
        dvn = dvn_ref[...]
        dxh = dvn * vg_ref[...]
        dv = rstd * (dxh - jnp.mean(dxh, axis=-1, keepdims=True)
                     - xhat * jnp.mean(dxh * xhat, axis=-1, keepdims=True))
        dpv_ref[...] = (dv * _gelu_grad(pv)).astype(BF16)
        dvg_ref[...] += jnp.sum(dvn * xhat, axis=0, keepdims=True)
        dvb_ref[...] += jnp.sum(dvn, axis=0, keepdims=True)

    blk = lambda col: pl.BlockSpec((tb, width), lambda i: (i, col))
    full = lambda shape: pl.BlockSpec(shape, lambda i: (0,) * len(shape))
    return _call(body, name=name, grid=(T // tb,),
                 in_specs=[blk(0), blk(1), full((G, BLK, BLK)), full((BLK, G)), full((1, width)), full((1, width)),
                           blk(0)],
                 out_specs=(blk(0), blk(0), full((G, BLK, BLK)), full((BLK, BLK)), full((1, width)), full((1, width))),
                 out_shape=(jax.ShapeDtypeStruct((T, width), BF16), jax.ShapeDtypeStruct((T, width), BF16),
                            jax.ShapeDtypeStruct((G, BLK, BLK), F32), jax.ShapeDtypeStruct((BLK, BLK), F32),
                            jax.ShapeDtypeStruct((1, width), F32), jax.ShapeDtypeStruct((1, width), F32)),
                 scratch_shapes=[pltpu.VMEM((tb, width), F32)],
                 compiler_params=_params(("arbitrary",)))(P, P, w_s, bs_t, vg.reshape(1, width), vb.reshape(1, width), dcat)


def _place():
    x, y, c = lax.axis_index("x"), lax.axis_index("y"), lax.axis_index("c")
    chips = [(1 - x, y), (x, 1 - y), (1 - x, 1 - y)]
    return x, y, c, 2 * x + y, chips, [2 * px + py for px, py in chips]


def _half(ref, c, rows):
    return ref.at[pl.ds(c * (rows // 2), rows // 2)]


def _place_shard(w, chip, dtype, *, name):
    R, C = w.shape
    tr = _rows(R, max(8, (1 << 19) // C // 8 * 8)) if R % 8 == 0 else R

    def body(chip_ref, w_ref, o_ref):
        o_ref[...] = w_ref[...].astype(dtype)

    grid_spec = pltpu.PrefetchScalarGridSpec(
        num_scalar_prefetch=1, grid=(R // tr,), in_specs=[pl.BlockSpec((tr, C), lambda i, s: (i, 0))],
        out_specs=pl.BlockSpec((None, tr, C), lambda i, s: (s[0], i, 0)))
    return _call(body, name=name, grid_spec=grid_spec, out_shape=jax.ShapeDtypeStruct((4, R, C), dtype),
                 compiler_params=_params(("parallel",)))(chip, w)


SEM = pl.BlockSpec(memory_space=pltpu.SEMAPHORE)
EFFECT = pltpu.SideEffectType.DATAFLOW_SIDE_EFFECTING


def _hbm(a):
    return pltpu.with_memory_space_constraint(a, pltpu.HBM)


def _sibling_start(arrays, copies, ncopies, after, *, name):
    n, na = len(arrays), len(after)

    def body(*refs):
        a = refs[:n]
        send, recv = refs[n + na], refs[n + na + 1]
        token = refs[2 * n + na + 2]
        x, y, c, _, _, _ = _place()
        for q, (src, dst) in enumerate(copies(a, c)):
            pltpu.make_async_remote_copy(src_ref=src, dst_ref=dst, send_sem=send.at[q], recv_sem=recv.at[q],
                                         device_id=(x, y, 1 - c), device_id_type=MESH).start()
        token[...] = jnp.zeros_like(token)

    outs = _call(body, name=name, in_specs=[HBM] * n + [pl.BlockSpec(memory_space=pl.ANY)] * na,
                 out_specs=(SEM, SEM, *([HBM] * n), pl.BlockSpec(memory_space=pltpu.VMEM)),
                 out_shape=(pltpu.SemaphoreType.DMA((ncopies,)), pltpu.SemaphoreType.DMA((ncopies,)),
                            *[pltpu.HBM(a.shape, a.dtype) for a in arrays], jax.ShapeDtypeStruct((8, 128), F32)),
                 input_output_aliases={t: 2 + t for t in range(n)},
                 compiler_params=pltpu.CompilerParams(has_side_effects=EFFECT))(*[_hbm(a) for a in arrays], *after)
    return outs[0], outs[1], list(outs[2:2 + n]), outs[2 + n]


def _sibling_wait(send, recv, arrays, copies, after, *, name):
    n, na = len(arrays), len(after)

    def body(*refs):
        a = refs[:n]
        send, recv = refs[n], refs[n + 1]
        x, y, c, _, _, _ = _place()
        for q, ((src, dst), (_, landed)) in enumerate(zip(copies(a, c), copies(a, 1 - c))):
            pltpu.make_async_remote_copy(src_ref=src, dst_ref=dst, send_sem=send.at[q], recv_sem=recv.at[q],
                                         device_id=(x, y, 1 - c), device_id_type=MESH).wait_send()
            pltpu.make_async_remote_copy(src_ref=src, dst_ref=landed, send_sem=send.at[q], recv_sem=recv.at[q],
                                         device_id=(x, y, 1 - c), device_id_type=MESH).wait_recv()

    outs = _call(body, name=name, in_specs=[HBM] * n + [SEM, SEM] + [pl.BlockSpec(memory_space=pl.ANY)] * na,
                 out_specs=[HBM] * n, out_shape=[pltpu.HBM(a.shape, a.dtype) for a in arrays],
                 input_output_aliases={t: t for t in range(n)},
                 compiler_params=pltpu.CompilerParams(has_side_effects=EFFECT))(*arrays, send, recv, *after)
    return list(outs)


def _pair_copies(n):
    def copies(a, core):
        out = []
        for t in range(n):
            h = a[t].shape[1] // 2
            out.append((a[t].at[:, pl.ds((1 - core) * h, h)], a[n + t]))
        return out
    return copies


def _swap_copies(a, core):
    out = []
    for ref in a:
        h = ref.shape[0] // 2
        out.append((ref.at[pl.ds(core * h, h)],) * 2)
    return out


def _gather_windows(shapes):
    split = [s[1] % 16 == 0 for s in shapes]

    def window(ref, t, chip_idx, core):
        w = ref.at[chip_idx]
        return _half(w, core, shapes[t][1]) if split[t] else w

    return split, window


def _gather_start(bufs, after, *, name):
    n = len(bufs)
    split, window = _gather_windows([b.shape for b in bufs])

    na = len(after)

    def body(*refs):
        b = refs[:n]
        send, recv = refs[n + na], refs[n + na + 1]
        token = refs[2 * n + na + 2]
        x, y, c, j, chips, pj = _place()
        for t in range(n):
            for k in range(3):
                pltpu.make_async_remote_copy(src_ref=window(b[t], t, j, c), dst_ref=window(b[t], t, j, c),
                                             send_sem=send.at[3 * t + k], recv_sem=recv.at[3 * t + k],
                                             device_id=(*chips[k], c), device_id_type=MESH).start()
        token[...] = jnp.zeros_like(token)

    outs = _call(body, name=name, in_specs=[HBM] * n + [pl.BlockSpec(memory_space=pl.ANY)] * na,
                 out_specs=(SEM, SEM, *([HBM] * n), pl.BlockSpec(memory_space=pltpu.VMEM)),
                 out_shape=(pltpu.SemaphoreType.DMA((3 * n,)), pltpu.SemaphoreType.DMA((3 * n,)),
                            *[pltpu.HBM(b.shape, b.dtype) for b in bufs], jax.ShapeDtypeStruct((8, 128), F32)),
                 input_output_aliases={t: 2 + t for t in range(n)},
                 compiler_params=pltpu.CompilerParams(has_side_effects=EFFECT))(*[_hbm(b) for b in bufs], *after)
    return outs[0], outs[1], list(outs[2:2 + n]), outs[2 + n]


def _gather_wait(send, recv, bufs, after, *, name):
    n = len(bufs)
    split, window = _gather_windows([b.shape for b in bufs])

    def body(*refs):
        b = refs[:n]
        send, recv = refs[n], refs[n + 1]
        x, y, c, j, chips, pj = _place()
        for t in range(n):
            for k in range(3):
                out = pltpu.make_async_remote_copy(src_ref=window(b[t], t, j, c), dst_ref=window(b[t], t, j, c),
                                                   send_sem=send.at[3 * t + k], recv_sem=recv.at[3 * t + k],
                                                   device_id=(*chips[k], c), device_id_type=MESH)
                out.wait_send()
                back = pltpu.make_async_remote_copy(src_ref=window(b[t], t, pj[k], c), dst_ref=window(b[t], t, pj[k], c),
                                                    send_sem=send.at[3 * t + k], recv_sem=recv.at[3 * t + k],
                                                    device_id=(*chips[k], c), device_id_type=MESH)
                back.wait_recv()

    outs = _call(body, name=name, in_specs=[HBM] * n + [SEM, SEM, pl.BlockSpec(memory_space=pl.ANY)],
                 out_specs=[HBM] * n, out_shape=[pltpu.HBM(b.shape, b.dtype) for b in bufs],
                 input_output_aliases={t: t for t in range(n)},
                 compiler_params=pltpu.CompilerParams(has_side_effects=EFFECT))(*bufs, send, recv, after)
    return list(outs)


def _gather_pass(bufs, *, name):
    n = len(bufs)
    split, window = _gather_windows([b.shape for b in bufs])
    idx = [t for t in range(n) if split[t]]

    def body(*refs):
        b = refs[n:2 * n]
        send, recv = refs[2 * n:]
        x, y, c, j, chips, pj = _place()

        def d2d(u, t, k, core):
            w = window(b[t], t, pj[k], core)
            return pltpu.make_async_remote_copy(src_ref=w, dst_ref=w, send_sem=send.at[3 * u + k],
                                                recv_sem=recv.at[3 * u + k], device_id=(x, y, 1 - c),
                                                device_id_type=MESH)

        sent = [d2d(u, t, k, c) for u, t in enumerate(idx) for k in range(3)]
        for cp in sent:
            cp.start()
        for u, t in enumerate(idx):
            for k in range(3):
                d2d(u, t, k, 1 - c).wait_recv()
        for cp in sent:
            cp.wait_send()

    return _call(body, name=name, in_specs=[HBM] * n, out_specs=[HBM] * n,
                 out_shape=[jax.ShapeDtypeStruct(b.shape, b.dtype) for b in bufs],
                 input_output_aliases={t: t for t in range(n)},
                 scratch_shapes=[pltpu.SemaphoreType.DMA((3 * len(idx),))] * 2)(*bufs)


def _chip_start(pairs, *, name):
    n = len(pairs)
    lands = [lax.empty((3,) + p.shape[1:], p.dtype) for p in pairs]

    def body(*refs):
        s, r = refs[:n], refs[n:2 * n]
        send, recv = refs[2 * n], refs[2 * n + 1]
        token = refs[4 * n + 2]
        x, y, c, j, chips, pj = _place()
        for t in range(n):
            for k in range(3):
                pltpu.make_async_remote_copy(src_ref=s[t].at[pj[k]], dst_ref=r[t].at[k], send_sem=send.at[3 * t + k],
                                             recv_sem=recv.at[3 * t + k], device_id=(*chips[k], c),
                                             device_id_type=MESH).start()
        token[...] = jnp.zeros_like(token)

    outs = _call(body, name=name, in_specs=[HBM] * (2 * n),
                 out_specs=(SEM, SEM, *([HBM] * (2 * n)), pl.BlockSpec(memory_space=pltpu.VMEM)),
                 out_shape=(pltpu.SemaphoreType.DMA((3 * n,)), pltpu.SemaphoreType.DMA((3 * n,)),
                            *[pltpu.HBM(a.shape, a.dtype) for a in list(pairs) + lands],
                            jax.ShapeDtypeStruct((8, 128), F32)),
                 input_output_aliases={t: 2 + t for t in range(2 * n)},
                 compiler_params=pltpu.CompilerParams(has_side_effects=EFFECT))(*[_hbm(a) for a in list(pairs) + lands])
    return outs[0], outs[1], list(outs[2:2 + n]), list(outs[2 + n:2 + 2 * n]), outs[2 + 2 * n]


def _chip_wait(send, recv, pairs, lands, after, *, name):
    n = len(pairs)

    def body(*refs):
        s, r = refs[:n], refs[n:2 * n]
        send, recv = refs[2 * n], refs[2 * n + 1]
        x, y, c, j, chips, pj = _place()
        for t in range(n):
            for k in range(3):
                cp = pltpu.make_async_remote_copy(src_ref=s[t].at[pj[k]], dst_ref=r[t].at[k], send_sem=send.at[3 * t + k],
                                                  recv_sem=recv.at[3 * t + k], device_id=(*chips[k], c),
                                                  device_id_type=MESH)
                cp.wait_send()
                cp.wait_recv()

    outs = _call(body, name=name, in_specs=[HBM] * (2 * n) + [SEM, SEM, pl.BlockSpec(memory_space=pl.ANY)],
                 out_specs=[HBM] * (2 * n), out_shape=[pltpu.HBM(a.shape, a.dtype) for a in list(pairs) + list(lands)],
                 input_output_aliases={t: t for t in range(2 * n)},
                 compiler_params=pltpu.CompilerParams(has_side_effects=EFFECT))(*pairs, *lands, send, recv, after)
    return list(outs[:n]), list(outs[n:])


def _all_reduce_small(packed, *, name):
    R, C = packed.shape

    def body(p_ref, o_ref, slots, send, recv, lsem):
        x, y, c = lax.axis_index("x"), lax.axis_index("y"), lax.axis_index("c")
        me = 4 * x + 2 * y + c
        lc = pltpu.make_async_copy(p_ref, slots.at[me], lsem.at[0])
        lc.start()
        copies = []
        for rel in range(1, 8):
            fx, fy, fc = (rel >> 2) & 1, (rel >> 1) & 1, rel & 1
            to = (1 - x if fx else x, 1 - y if fy else y, 1 - c if fc else c)
            cp = pltpu.make_async_remote_copy(src_ref=p_ref, dst_ref=slots.at[me], send_sem=send.at[rel - 1],
                                              recv_sem=recv.at[rel - 1], device_id=to, device_id_type=MESH)
            cp.start()
            copies.append((cp, 4 * to[0] + 2 * to[1] + to[2]))
        for rel, (cp, frm) in enumerate(copies):
            cp.wait_send()
            pltpu.make_async_remote_copy(src_ref=p_ref, dst_ref=slots.at[frm], send_sem=send.at[rel],
                                         recv_sem=recv.at[rel], device_id=(x, y, c), device_id_type=MESH).wait_recv()
        lc.wait()
        acc = slots[0]
        for dev in range(1, 8):
            acc = acc + slots[dev]
        o_ref[...] = acc

    return _call(body, name=name, in_specs=[pl.BlockSpec(memory_space=pltpu.VMEM)],
                 out_specs=pl.BlockSpec(memory_space=pltpu.VMEM), out_shape=jax.ShapeDtypeStruct((R, C), F32),
                 scratch_shapes=[pltpu.VMEM((8, R, C), F32), pltpu.SemaphoreType.DMA((7,)),
                                 pltpu.SemaphoreType.DMA((7,)), pltpu.SemaphoreType.DMA((1,))],
                 compiler_params=pltpu.CompilerParams(vmem_limit_bytes=V7X_VMEM_LIMIT))(packed)


def _pair_sum(grad, theirs, core, *, name):
    J, H, C = theirs.shape
    tr = _rows(H, max(8, (1 << 19) // C // 8 * 8))

    def body(core_ref, a_ref, b_ref, o_ref):
        o_ref[...] = (a_ref[...].astype(F32) + b_ref[...].astype(F32)).astype(BF16)

    blk = pl.BlockSpec((None, tr, C), lambda j, i, s: (j, i, 0))
    mine = pl.BlockSpec((None, None, tr, C), lambda j, i, s: (j, s[0], i, 0))
    grid_spec = pltpu.PrefetchScalarGridSpec(num_scalar_prefetch=1, grid=(J, H // tr), in_specs=[mine, blk],
                                             out_specs=blk)
    return _call(body, name=name, grid_spec=grid_spec, out_shape=jax.ShapeDtypeStruct((J, H, C), BF16),
                 compiler_params=_params(("parallel", "parallel")))(core, grad.reshape(J, 2, H, C), theirs)


def _chip_sum(pairs, slots, place, *, name):
    _, H, C = slots.shape
    tr = _rows(H, max(8, (1 << 19) // C // 8 * 8))
    nr = H // tr

    def body(place_ref, s0, s1, s2, s3, o_ref):
        o_ref[...] = ((s0[...].astype(F32) + s1[...].astype(F32)) + s2[...].astype(F32)) + s3[...].astype(F32)

    def slot(k):
        return pl.BlockSpec((None, tr, C), lambda i, s: (k, i, 0))

    own = pl.BlockSpec((None, tr, C), lambda i, s: (s[0], i, 0))
    grid_spec = pltpu.PrefetchScalarGridSpec(
        num_scalar_prefetch=1, grid=(nr,), in_specs=[own, slot(0), slot(1), slot(2)],
        out_specs=pl.BlockSpec((tr, C), lambda i, s: (s[1] * nr + i, 0)))
    return _call(body, name=name, grid_spec=grid_spec, out_shape=jax.ShapeDtypeStruct((2 * H, C), F32),
                 compiler_params=_params(("parallel",)))(place, pairs, slots, slots, slots)


def _scatter_pair_start(grads, tag):
    lands = [lax.empty((4, g.shape[1] // 2, g.shape[2]), g.dtype) for g in grads]
    return _sibling_start(list(grads) + lands, _pair_copies(len(grads)), len(grads), (), name=f"rs_pair_start_{tag}")


def _scatter_chip_start(pair_started, after, place, tag):
    send, recv, arrays, _ = pair_started
    n = len(arrays) // 2
    arrays = _sibling_wait(send, recv, arrays, _pair_copies(n), after, name=f"rs_pair_wait_{tag}")
    pairs = [_pair_sum(g, t, place[1:], name=f"rs_pair_sum_{tag}_{i}")
             for i, (g, t) in enumerate(zip(arrays[:n], arrays[n:]))]
    return _chip_start(pairs, name=f"rs_chip_start_{tag}")


def _scatter_swap_start(chip_started, after, place, tag):
    send, recv, pairs, lands, _ = chip_started
    pairs, slots = _chip_wait(send, recv, pairs, lands, after, name=f"rs_chip_wait_{tag}")
    fulls = [_chip_sum(p, s, place, name=f"rs_chip_sum_{tag}_{i}") for i, (p, s) in enumerate(zip(pairs, slots))]
    return _sibling_start(fulls, _swap_copies, len(fulls), (), name=f"rs_swap_start_{tag}")


def _scatter_finish(swap_started, tag):
    send, recv, fulls, _ = swap_started
    return _sibling_wait(send, recv, fulls, _swap_copies, (), name=f"rs_swap_wait_{tag}")


def kernel(x, mem, w_in_a, w_in_b, w_s, b_s, vnorm_g, vnorm_b, w_mem_kv, w_out, ln1_g, ln1_b, w_ff1, w_ff2, ln2_g, ln2_b, loss_target, m_w_in_a, m_w_in_b, m_w_s, m_b_s, m_vnorm_g, m_vnorm_b, m_w_mem_kv, m_w_out, m_ln1_g, m_ln1_b, m_w_ff1, m_w_ff2, m_ln2_g, m_ln2_b, v_w_in_a, v_w_in_b, v_w_s, v_b_s, v_vnorm_g, v_vnorm_b, v_w_mem_kv, v_w_out, v_ln1_g, v_ln1_b, v_w_ff1, v_w_ff2, v_ln2_g, v_ln2_b):
    T, D = x.shape[1], x.shape[2]
    depth = w_ff1.shape[0]
    alpha = (2.0 * depth) ** 0.25
    a_out = (D // 256) * HEAD_DIM
    a_cols = len(A_PAIRS) * 3 * a_out
    b_width = (D // 256) * HEAD_DIM
    G = b_width // HEAD_DIM
    assert a_out % HEAD_BLOCK == 0 and T % (BLK * A_PAIRS[-1][1]) == 0

    xf = x.reshape(T, D)
    mem_b = mem.reshape(MEM_TOKENS, D).astype(BF16)
    target = loss_target.reshape(T, D)
    c_idx = lax.axis_index("x") * 2 + lax.axis_index("y")
    place = jnp.stack([c_idx, lax.axis_index("c")]).astype(jnp.int32)

    def gather_begin(i, group, after):
        jl, tag = i // 2, "a" if i % 2 == 0 else "b"
        if group == "mix":
            mats = [(w_in_a if i % 2 == 0 else w_in_b)[jl], w_mem_kv[i], w_out[i]]
        else:
            mats = [w_ff1[i], w_ff2[i]]
        bufs = [_place_shard(w, place[:1], BF16, name=f"place_{tag}_{group}_{n_}") for n_, w in enumerate(mats)]
        if group == "mix" and i % 2:
            bufs += [_place_shard(v_[jl].reshape(1, -1), place[:1], F32, name="place_vnorm") for v_ in (vnorm_g, vnorm_b)]
        return _gather_start(bufs, after, name=f"gather_start_{tag}_{group}")

    def gather_end(started, after, name):
        send, recv, bufs, _ = started
        bufs = _gather_wait(send, recv, bufs, after, name=f"gather_wait_{name}")
        return _gather_pass(bufs, name=f"gather_pass_{name}")

    def group_view(P, g, d):
        return (P, g) if d == 1 else (P[:, g * 3 * a_out:(g + 1) * 3 * a_out], 0)

    saved = []
    mix_started = gather_begin(0, "mix", ())
    ff_started = None
    for i in range(depth):
        jl = i // 2
        is_a = i % 2 == 0
        tag = "a" if is_a else "b"
        gathered = gather_end(mix_started, xf, f"{tag}_mix")
        win, wkv, wout = gathered[:3]
        wkv = wkv.reshape(1, D, 2 * MEM_WIDTH)
        token = None
        if i == 0:
            ff_started = gather_begin(0, "ff", (win,))
            token = ff_started[3]
        if i + 1 < depth:
            mix_started = gather_begin(i + 1, "mix", (win,) if token is None else (token,))
            ff_next = gather_begin(i + 1, "ff", (mix_started[3],))
            token = ff_next[3]
        xb = (xf if token is None else xf + token[0, 0]).astype(BF16)
        P = _mm_nn(xb, win, out_dtype=BF16 if is_a else F32, name=f"proj_in_{'a' if is_a else 'b'}")
        kv = _mm_nn(mem_b, wkv, name="proj_kv")
        if is_a:
            outs = [_attn_fwd(*group_view(P, g, d), d, a_out, name=f"attn_fwd_d{d}") for g, (_, d) in enumerate(A_PAIRS)]
            mix, lse = _attn_combine([o for o, _ in outs], [l for _, l in outs], name="attn_combine")
            qcol = a_cols // MEM_WIDTH
            extra = (lse,)
        else:
            vg_full = gathered[3].reshape(-1)
            vb_full = gathered[4].reshape(-1)
            bs_t = b_s[jl].T
            mix = _gmlp_fwd(P, w_s[jl], bs_t, vg_full, vb_full, b_width, name="gmlp_fwd")
            qcol = 2 * b_width // MEM_WIDTH
            extra = (vg_full, vb_full, bs_t)
        mem_o = _mem_fwd(P, qcol, kv, name=f"mem_fwd_{'a' if is_a else 'b'}")
        cat = jnp.concatenate([mix.astype(BF16), mem_o.astype(BF16)], axis=1)
        y = _mm_nn(cat, wout, out_dtype=F32, name="proj_out")
        x1, x1b = _ln_fwd(xf, y, ln1_g[i], ln1_b[i], alpha, name="ln_fwd")
        wff1, wff2 = gather_end(ff_started, x1b, f"{tag}_ff")
        wff2 = wff2.reshape(1, 4 * D, D)
        if i + 1 < depth:
            ff_started = ff_next
        a_pre, hid = _mm_nn(x1b, wff1, epi="relu2", name="ff1")
        f = _mm_nn(hid, wff2, out_dtype=F32, name="ff2")
        x2, _ = _ln_fwd(x1, f, ln2_g[i], ln2_b[i], alpha, name="ln_fwd")
        saved.append(dict(xf=xf, xb=xb, P=P, kv=kv, mix=mix, cat=cat, y=y, x1=x1, x1b=x1b, a_pre=a_pre, hid=hid, f=f,
                          extra=extra, w=(win, wkv, wout, wff1, wff2), qcol=qcol))
        xf = x2

    dx, sq = _loss_head(xf, target, name="loss_head")
    loss = lax.psum(sq[0, 0] * (0.5 / D), ("x", "y", "c"))

    g_big = dict(w_in_a=[None] * ((depth + 1) // 2), w_in_b=[None] * (depth // 2), w_mem_kv=[None] * depth,
                 w_out=[None] * depth, w_ff1=[None] * depth, w_ff2=[None] * depth)
    small = {k: [None] * depth for k in ("ln1_g", "ln1_b", "ln2_g", "ln2_b")}
    small_b = {k: [None] * (depth // 2) for k in ("w_s", "b_s", "vnorm_g", "vnorm_b")}
    def store(red, layer, group):
        if group == "ff":
            g_big["w_ff1"][layer], g_big["w_ff2"][layer] = red
        else:
            g_big["w_in_a" if layer % 2 == 0 else "w_in_b"][layer // 2], g_big["w_mem_kv"][layer], g_big["w_out"][layer] = red

    chips = []
    mix_pair = None
    tokens = ()
    for i in reversed(range(depth)):
        jl = i // 2
        is_a = i % 2 == 0
        s = saved[i]
        win, wkv, wout, wff1, wff2 = s["w"]
        tag = "a" if is_a else "b"
        d_f, adr2, dg2, db2 = _ln_bwd(dx, s["x1"], s["f"], ln2_g[i], alpha, tokens, name="ln_bwd")
        small["ln2_g"][i], small["ln2_b"][i] = dg2, db2
        gw_ff2 = _mm_tn(s["hid"], d_f, 1, name="grad_ff2").reshape(4, D, D)
        da = _mm_nt(d_f, wff2, epi="drelu2", extra=s["a_pre"], name="ff2_bwd")
        gw_ff1 = _mm_tn(s["x1b"], da, 4, name="grad_ff1")
        ff_pair = _scatter_pair_start([gw_ff1, gw_ff2], f"{tag}_ff")
        late = []
        if mix_pair is not None:
            above = "b" if is_a else "a"
            mix_chip = _scatter_chip_start(mix_pair, (d_f,), place, f"{above}_mix")
            chips.append((mix_chip, i + 1, "mix"))
            late.append(mix_chip[4])
        dx1 = _mm_nt(da, wff1, epi="resid", extra=adr2, after=(ff_pair[3],), name="ff1_bwd")
        ff_chip = _scatter_chip_start(ff_pair, (dx1,), place, f"{tag}_ff")
        chips.append((ff_chip, i, "ff"))
        late.append(ff_chip[4])
        d_y, adr1, dg1, db1 = _ln_bwd(dx1, s["xf"], s["y"], ln1_g[i], alpha, tuple(late), name="ln_bwd")
        small["ln1_g"][i], small["ln1_b"][i] = dg1, db1
        gw_out = _mm_tn(s["cat"], d_y, 4, name="grad_out")
        dcat = _mm_nt(d_y, wout, out_dtype=F32, name="proj_out_bwd")
        ocol = dcat.shape[1] // MEM_WIDTH - 1
        dmq, dkv = _mem_bwd(s["P"], s["qcol"], s["kv"], dcat, ocol, name=f"mem_bwd_{tag}")
        gw_kv = _mm_tn(mem_b, dkv.astype(BF16), 1, name="grad_kv").reshape(4, D // 4, 2 * MEM_WIDTH)
        if is_a:
            (lse,) = s["extra"]
            dmix = dcat[:, :a_out]
            parts = []
            for g, (_, d) in enumerate(A_PAIRS):
                Pg, g0 = group_view(s["P"], g, d)
                parts += _attn_bwd(Pg, dmix, s["mix"], lse, g0, d, a_out, name=f"attn_bwd_d{d}")
            dP = jnp.concatenate(parts + [dmq], axis=1)
        else:
            vg_full, vb_full, bs_t = s["extra"]
            dpu, dpv, dws, dbs, dvg, dvb = _gmlp_bwd(s["P"], w_s[jl], bs_t, vg_full, vb_full, dcat, b_width,
                                                     name="gmlp_bwd")
            small_b["w_s"][jl], small_b["b_s"][jl] = dws, dbs[:, :G].T
            small_b["vnorm_g"][jl], small_b["vnorm_b"][jl] = dvg, dvb
            dP = jnp.concatenate([dpu, dpv, dmq], axis=1)
        gw_in = _mm_tn(s["xb"], dP, 4, name=f"grad_in_{tag}")
        dx = _mm_nt(dP, win, epi="resid", extra=adr1, name=f"proj_in_bwd_{tag}")
        mix_pair = _scatter_pair_start([gw_in, gw_kv, gw_out], f"{tag}_mix")
        tokens = (mix_pair[3],)
    last_chip = _scatter_chip_start(mix_pair, (dx,), place, "a_mix")
    swaps, after = [], dx
    for chip_started, layer, group in chips + [(last_chip, 0, "mix")]:
        name = f"{'a' if layer % 2 == 0 else 'b'}_{group}"
        swaps.append((_scatter_swap_start(chip_started, after, place, name), layer, group, name))
        after = swaps[-1][0][3]
    for swap_started, layer, group, name in swaps:
        store(_scatter_finish(swap_started, name), layer, group)
    grad_x = dx.reshape(x.shape)

    nb_layers = depth // 2
    pieces = ([jnp.stack(small_b["w_s"]).reshape(-1, 128), jnp.stack(small_b["b_s"]).reshape(-1, 128),
               jnp.stack(small_b["vnorm_g"]).reshape(-1, 128), jnp.stack(small_b["vnorm_b"]).reshape(-1, 128)]
              + [jnp.stack(small[k]).reshape(-1, 128) for k in ("ln1_g", "ln1_b", "ln2_g", "ln2_b")])
    sizes = [p.shape[0] for p in pieces]
    pad = (-sum(sizes)) % 8
    packed = jnp.concatenate(pieces + ([jnp.zeros((pad, 128), F32)] if pad else []), axis=0)
    summed = _all_reduce_small(packed, name="all_reduce_small")
    offs = [0]
    for n_ in sizes:
        offs.append(offs[-1] + n_)
    sp = [summed[offs[k]:offs[k + 1]] for k in range(len(sizes))]
    vshard = vnorm_g.shape[1]
    g_small = dict(
        w_s=sp[0].reshape(w_s.shape), b_s=sp[1].reshape(b_s.shape),
        vnorm_g=lax.dynamic_slice_in_dim(sp[2].reshape(nb_layers, -1), c_idx * vshard, vshard, axis=1),
        vnorm_b=lax.dynamic_slice_in_dim(sp[3].reshape(nb_layers, -1), c_idx * vshard, vshard, axis=1),
        ln1_g=sp[4].reshape(ln1_g.shape), ln1_b=sp[5].reshape(ln1_b.shape),
        ln2_g=sp[6].reshape(ln2_g.shape), ln2_b=sp[7].reshape(ln2_b.shape))

    names = ["w_in_a", "w_in_b", "w_s", "b_s", "vnorm_g", "vnorm_b", "w_mem_kv", "w_out", "ln1_g", "ln1_b", "w_ff1",
             "w_ff2", "ln2_g", "ln2_b"]
    ws = dict(w_in_a=w_in_a, w_in_b=w_in_b, w_s=w_s, b_s=b_s, vnorm_g=vnorm_g, vnorm_b=vnorm_b, w_mem_kv=w_mem_kv,
              w_out=w_out, ln1_g=ln1_g, ln1_b=ln1_b, w_ff1=w_ff1, w_ff2=w_ff2, ln2_g=ln2_g, ln2_b=ln2_b)
    ms = dict(w_in_a=m_w_in_a, w_in_b=m_w_in_b, w_s=m_w_s, b_s=m_b_s, vnorm_g=m_vnorm_g, vnorm_b=m_vnorm_b,
              w_mem_kv=m_w_mem_kv, w_out=m_w_out, ln1_g=m_ln1_g, ln1_b=m_ln1_b, w_ff1=m_w_ff1, w_ff2=m_w_ff2,
              ln2_g=m_ln2_g, ln2_b=m_ln2_b)
    vs = dict(w_in_a=v_w_in_a, w_in_b=v_w_in_b, w_s=v_w_s, b_s=v_b_s, vnorm_g=v_vnorm_g, vnorm_b=v_vnorm_b,
              w_mem_kv=v_w_mem_kv, w_out=v_w_out, ln1_g=v_ln1_g, ln1_b=v_ln1_b, w_ff1=v_w_ff1, w_ff2=v_w_ff2,
              ln2_g=v_ln2_g, ln2_b=v_ln2_b)
    grads, deltas, new_m, new_v = {}, {}, {}, {}
    for k in g_big:
        g = jnp.stack(g_big[k]).reshape(ws[k].shape)
        cols = ws[k].shape[-1]
        d_, m_, v_ = _adamw(ws[k].reshape(-1, cols), g.reshape(-1, cols), ms[k].reshape(-1, cols),
                            vs[k].reshape(-1, cols), name=f"adamw_{k}")
        grads[k] = g
        deltas[k], new_m[k], new_v[k] = (t.reshape(ws[k].shape) for t in (d_, m_, v_))
    small_names = [k for k in names if k not in g_big]

    def pack(tree):
        flat = jnp.concatenate([tree[k].reshape(-1) for k in small_names])
        padn = (-flat.shape[0]) % 1024
        return jnp.pad(flat, (0, padn)).reshape(-1, 128)

    d_, m_, v_ = _adamw(pack(ws), pack(g_small), pack(ms), pack(vs), name="adamw_small")
    off = 0
    for k in small_names:
        n_ = ws[k].size
        grads[k] = g_small[k]
        deltas[k], new_m[k], new_v[k] = (t.reshape(-1)[off:off + n_].reshape(ws[k].shape) for t in (d_, m_, v_))
        off += n_

    return (loss, grad_x, *[grads[k] for k in names], *[deltas[k] for k in names], *[new_m[k] for k in names],
            *[new_v[k] for k in names])
```

```python
import functools
import math

import jax
import jax.numpy as jnp
from jax import lax
from jax.experimental import pallas as pl
from jax.experimental.pallas import tpu as pltpu

F32 = jnp.float32
BF16 = jnp.bfloat16

HEAD_DIM = 128
A_PAIRS = ((128, 1), (512, 4), (2048, 16))
BLK = 128
HEAD_BLOCK = 512
MEM_TOKENS = 256
MEM_HEADS = 4
MEM_WIDTH = MEM_HEADS * HEAD_DIM
LN_EPS = 1e-5
ADAM_LR, ADAM_B1, ADAM_B2, ADAM_EPS, ADAM_WD, ADAM_STEP = 0.001, 0.9, 0.999, 1e-08, 0.01, 10
NEG = -1e30
V7X_VMEM_LIMIT = 56 * 1024 * 1024
V7X_MATMUL_VMEM_BUDGET = 40 * 1024 * 1024
MESH = pl.DeviceIdType.MESH
HBM = pl.BlockSpec(memory_space=pltpu.HBM)


def _call(body, **kw):
    return pl.pallas_call(body, **kw)


def _params(sem):
    return pltpu.CompilerParams(dimension_semantics=sem, vmem_limit_bytes=V7X_VMEM_LIMIT)


def _tile(n, cap):
    best = 0
    for t in range(128, min(n, cap) + 1, 128):
        if n % t == 0:
            best = t
    if best == 0 or (best < 512 and n <= 2560):
        return n
    return best


def _depth(k, per_unit_bytes, fixed_bytes):
    t = _tile(k, 2048)
    while t > 512 and 2 * t * per_unit_bytes + fixed_bytes > V7X_MATMUL_VMEM_BUDGET:
        smaller = _tile(k, t // 2)
        if smaller >= t:
            break
        t = smaller
    return t


def _rows(n, cap):
    best = 8
    for t in range(8, min(n, cap) + 1, 8):
        if n % t == 0:
            best = t
    return best


def _epilogue(epi, acc, extra_ref, out_refs):
    if epi == "plain":
        out_refs[0][...] = acc.astype(out_refs[0].dtype)
    elif epi == "relu2":
        out_refs[0][...] = acc
        r = jnp.maximum(acc, 0.0)
        out_refs[1][...] = (r * r).astype(out_refs[1].dtype)
    elif epi == "drelu2":
        out_refs[0][...] = (acc * (2.0 * jnp.maximum(extra_ref[...], 0.0))).astype(out_refs[0].dtype)
    elif epi == "resid":
        out_refs[0][...] = acc + extra_ref[...]
    else:
        raise ValueError(epi)


def _mm_nn(a, w, *, epi="plain", out_dtype=BF16, extra=None, name):
    M, K = a.shape
    J, K2, Nj = w.shape
    assert K == K2
    tn = _tile(Nj, 1024)
    tm = _tile(M, 1024 if tn <= 1024 else 512)
    n_out = 2 if epi == "relu2" else 1
    has_extra = extra is not None
    out_bytes = 12 if epi == "relu2" else (8 if epi == "resid" or out_dtype == F32 else 4)
    tk = _depth(K, 2 * (tm + tn), tm * tn * (4 + out_bytes + 8 * has_extra))
    nn, nk = Nj // tn, K // tk

    def body(*refs):
        a_ref, w_ref = refs[0], refs[1]
        extra_ref = refs[2] if has_extra else None
        outs = refs[2 + has_extra: 2 + has_extra + n_out]
        acc_ref = refs[-1]
        k = pl.program_id(3)

        @pl.when(k == 0)
        def _():
            acc_ref[...] = jnp.zeros_like(acc_ref)

        acc_ref[...] += jnp.dot(a_ref[...], w_ref[...], preferred_element_type=F32)

        @pl.when(k == nk - 1)
        def _():
            _epilogue(epi, acc_ref[...], extra_ref, outs)

    omap = lambda i, j, n, k: (i, j * nn + n)
    in_specs = [pl.BlockSpec((tm, tk), lambda i, j, n, k: (i, k)),
                pl.BlockSpec((None, tk, tn), lambda i, j, n, k: (j, k, n))]
    args = [a, w]
    if has_extra:
        in_specs.append(pl.BlockSpec((tm, tn), omap))
        args.append(extra)
    if epi == "relu2":
        out_shape = (jax.ShapeDtypeStruct((M, J * Nj), F32), jax.ShapeDtypeStruct((M, J * Nj), BF16))
        out_specs = (pl.BlockSpec((tm, tn), omap), pl.BlockSpec((tm, tn), omap))
    else:
        out_shape = jax.ShapeDtypeStruct((M, J * Nj), F32 if epi == "resid" else out_dtype)
        out_specs = pl.BlockSpec((tm, tn), omap)
    return _call(body, name=name, grid=(M // tm, J, nn, nk), in_specs=in_specs, out_specs=out_specs,
                 out_shape=out_shape, scratch_shapes=[pltpu.VMEM((tm, tn), F32)],
                 compiler_params=_params(("parallel", "parallel", "parallel", "arbitrary")))(*args)


def _mm_nt(a, w, *, epi="plain", out_dtype=BF16, extra=None, after=(), name):
    M, N = a.shape
    J, K, Nj = w.shape
    assert N == J * Nj
    tko = _tile(K, 1024)
    tm = _tile(M, 1024 if _tile(Nj, 1024) <= 1024 else 512)
    has_extra = extra is not None
    out_bytes = 8 if epi == "resid" or out_dtype == F32 else 4
    tc = _depth(Nj, 2 * (tm + tko), tm * tko * (4 + out_bytes + 8 * has_extra))
    nc = Nj // tc

    def body(*refs):
        a_ref, w_ref = refs[0], refs[1]
        extra_ref = refs[2] if has_extra else None
        outs = refs[2 + has_extra + len(after): 3 + has_extra + len(after)]
        acc_ref = refs[-1]
        j, c = pl.program_id(2), pl.program_id(3)

        @pl.when(jnp.logical_and(j == 0, c == 0))
        def _():
            acc_ref[...] = jnp.zeros_like(acc_ref)

        acc_ref[...] += lax.dot_general(a_ref[...], w_ref[...], (((1,), (1,)), ((), ())),
                                        preferred_element_type=F32)

        @pl.when(jnp.logical_and(j == J - 1, c == nc - 1))
        def _():
            _epilogue(epi, acc_ref[...], extra_ref, outs)

    omap = lambda i, ko, j, c: (i, ko)
    in_specs = [pl.BlockSpec((tm, tc), lambda i, ko, j, c: (i, j * nc + c)),
                pl.BlockSpec((None, tko, tc), lambda i, ko, j, c: (j, ko, c))]
    args = [a, w]
    if has_extra:
        in_specs.append(pl.BlockSpec((tm, tko), omap))
        args.append(extra)
    in_specs += [pl.BlockSpec(memory_space=pl.ANY)] * len(after)
    args += list(after)
    out_shape = jax.ShapeDtypeStruct((M, K), F32 if epi == "resid" else out_dtype)
    return _call(body, name=name, grid=(M // tm, K // tko, J, nc), in_specs=in_specs,
                 out_specs=pl.BlockSpec((tm, tko), omap), out_shape=out_shape,
                 scratch_shapes=[pltpu.VMEM((tm, tko), F32)],
                 compiler_params=_params(("parallel", "parallel", "arbitrary", "arbitrary")))(*args)


def _mm_tn(a, b, J, *, name):
    T, K = a.shape
    T2, N = b.shape
    assert T == T2 and N % J == 0
    Nj = N // J
    tn = _tile(Nj, 1024)
    tkr = _tile(K, 1024 if tn <= 1024 else 512)
    tt = _depth(T, 2 * (tkr + tn), tkr * tn * (4 + 4))
    nn, nt = Nj // tn, T // tt

    def body(a_ref, b_ref, o_ref, acc_ref):
        t = pl.program_id(3)

        @pl.when(t == 0)
        def _():
            acc_ref[...] = jnp.zeros_like(acc_ref)

        acc_ref[...] += lax.dot_general(a_ref[...], b_ref[...], (((0,), (0,)), ((), ())),
                                        preferred_element_type=F32)

        @pl.when(t == nt - 1)
        def _():
            o_ref[...] = acc_ref[...].astype(o_ref.dtype)

    return _call(body, name=name, grid=(J, K // tkr, nn, nt),
                 in_specs=[pl.BlockSpec((tt, tkr), lambda j, kr, n, t: (t, kr)),
                           pl.BlockSpec((tt, tn), lambda j, kr, n, t: (t, j * nn + n))],
                 out_specs=pl.BlockSpec((None, tkr, tn), lambda j, kr, n, t: (j, kr, n)),
                 out_shape=jax.ShapeDtypeStruct((J, K, Nj), BF16),
                 scratch_shapes=[pltpu.VMEM((tkr, tn), F32)],
                 compiler_params=_params(("parallel", "parallel", "parallel", "arbitrary")))(a, b)


def _ln_fwd(x, y, g, b, alpha, *, name):
    T, D = x.shape
    tr = _rows(T, 256)

    def body(x_ref, y_ref, g_ref, b_ref, o_ref, ob_ref):
        r = alpha * x_ref[...] + y_ref[...]
        mu = jnp.mean(r, axis=-1, keepdims=True)
        xc = r - mu
        var = jnp.mean(xc * xc, axis=-1, keepdims=True)
        o = xc * lax.rsqrt(var + LN_EPS) * g_ref[...] + b_ref[...]
        o_ref[...] = o
        ob_ref[...] = o.astype(BF16)

    row = pl.BlockSpec((tr, D), lambda i: (i, 0))
    vec = pl.BlockSpec((1, D), lambda i: (0, 0))
    return _call(body, name=name, grid=(T // tr,), in_specs=[row, row, vec, vec], out_specs=(row, row),
                 out_shape=(jax.ShapeDtypeStruct((T, D), F32), jax.ShapeDtypeStruct((T, D), BF16)),
                 compiler_params=_params(("parallel",)))(x, y, g.reshape(1, D), b.reshape(1, D))


def _ln_bwd(dout, x, y, g, alpha, after=(), *, name):
    T, D = x.shape
    tr = _rows(T, 256)
    na = len(after)

    def body(*refs):
        do_ref, x_ref, y_ref, g_ref = refs[:4]
        drb_ref, adr_ref, dg_ref, db_ref = refs[4 + na:]
        i = pl.program_id(0)
        r = alpha * x_ref[...] + y_ref[...]
        mu = jnp.mean(r, axis=-1, keepdims=True)
        xc = r - mu
        var = jnp.mean(xc * xc, axis=-1, keepdims=True)
        rstd = lax.rsqrt(var + LN_EPS)
        xhat = xc * rstd
        do = do_ref[...]
        dxh = do * g_ref[...]
        dr = rstd * (dxh - jnp.mean(dxh, axis=-1, keepdims=True)
                     - xhat * jnp.mean(dxh * xhat, axis=-1, keepdims=True))
        drb_ref[...] = dr.astype(BF16)
        adr_ref[...] = alpha * dr

        @pl.when(i == 0)
        def _():
            dg_ref[...] = jnp.zeros_like(dg_ref)
            db_ref[...] = jnp.zeros_like(db_ref)

        dg_ref[...] += jnp.sum(do * xhat, axis=0, keepdims=True)
        db_ref[...] += jnp.sum(do, axis=0, keepdims=True)

    row = pl.BlockSpec((tr, D), lambda i: (i, 0))
    vec = pl.BlockSpec((1, D), lambda i: (0, 0))
    return _call(body, name=name, grid=(T // tr,),
                 in_specs=[row, row, row, vec] + [pl.BlockSpec(memory_space=pl.ANY)] * na,
                 out_specs=(row, row, vec, vec),
                 out_shape=(jax.ShapeDtypeStruct((T, D), BF16), jax.ShapeDtypeStruct((T, D), F32),
                            jax.ShapeDtypeStruct((1, D), F32), jax.ShapeDtypeStruct((1, D), F32)),
                 compiler_params=_params(("arbitrary",)))(dout, x, y, g.reshape(1, D), *after)


def _loss_head(xf, target, *, name):
    T, D = xf.shape
    tr = _rows(T, 256)

    def body(x_ref, t_ref, dy_ref, s_ref):
        i = pl.program_id(0)
        err = x_ref[...] - t_ref[...]
        dy_ref[...] = err * (1.0 / D)

        @pl.when(i == 0)
        def _():
            s_ref[...] = jnp.zeros_like(s_ref)

        s_ref[...] += jnp.sum(jnp.sum(err * err, axis=1, keepdims=True), axis=0, keepdims=True)

    row = pl.BlockSpec((tr, D), lambda i: (i, 0))
    return _call(body, name=name, grid=(T // tr,), in_specs=[row, row],
                 out_specs=(row, pl.BlockSpec((8, 128), lambda i: (0, 0))),
                 out_shape=(jax.ShapeDtypeStruct((T, D), F32), jax.ShapeDtypeStruct((8, 128), F32)),
                 compiler_params=_params(("arbitrary",)))(xf, target)


def _adamw(w, g, m, v, *, name):
    R, C = w.shape
    tr = _rows(R, max(8, (1 << 18) // C // 8 * 8))

    def body(w_ref, g_ref, m_ref, v_ref, d_ref, nm_ref, nv_ref):
        g_ = g_ref[...]
        m_ = ADAM_B1 * m_ref[...] + (1.0 - ADAM_B1) * g_
        v_ = ADAM_B2 * v_ref[...] + (1.0 - ADAM_B2) * (g_ * g_)
        m_hat = m_ / (1.0 - ADAM_B1 ** ADAM_STEP)
        v_hat = v_ / (1.0 - ADAM_B2 ** ADAM_STEP)
        d_ref[...] = -ADAM_LR * (m_hat / (jnp.sqrt(v_hat) + ADAM_EPS) + ADAM_WD * w_ref[...])
        nm_ref[...] = m_
        nv_ref[...] = v_

    blk = pl.BlockSpec((tr, C), lambda i: (i, 0))
    sds = jax.ShapeDtypeStruct((R, C), F32)
    return _call(body, name=name, grid=(R // tr,), in_specs=[blk] * 4, out_specs=(blk,) * 3,
                 out_shape=(sds,) * 3, compiler_params=_params(("parallel",)))(w, g, m, v)


def _dot_nt(a, b):
    return lax.dot_general(a, b, (((1,), (1,)), ((), ())), preferred_element_type=F32)


def _dot_tn(a, b):
    return lax.dot_general(a, b, (((0,), (0,)), ((), ())), preferred_element_type=F32)


def _band_masks():
    qi = lax.broadcasted_iota(jnp.int32, (BLK, BLK), 0)
    kj = lax.broadcasted_iota(jnp.int32, (BLK, BLK), 1)
    return qi >= kj, kj >= qi


def _attn_fwd(P, grp, d, a_out, *, name):
    T, C = P.shape
    L = T // d
    nb = L // BLK
    nhh = a_out // HEAD_BLOCK
    cb = C // HEAD_BLOCK
    q0 = grp * 3 * nhh
    scale = HEAD_DIM ** -0.5

    def body(q_ref, kc_ref, kp_ref, vc_ref, vp_ref, o_ref, l_ref):
        b = pl.program_id(1)
        mask_c, mask_p = _band_masks()
        mask_p = jnp.logical_and(mask_p, b > 0)
        for h in range(HEAD_BLOCK // HEAD_DIM):
            hs = slice(h * HEAD_DIM, (h + 1) * HEAD_DIM)
            q = q_ref[:, hs]
            s_c = jnp.where(mask_c, _dot_nt(q, kc_ref[:, hs]) * scale, NEG)
            s_p = jnp.where(mask_p, _dot_nt(q, kp_ref[:, hs]) * scale, NEG)
            m = jnp.maximum(jnp.max(s_c, axis=1, keepdims=True), jnp.max(s_p, axis=1, keepdims=True))
            p_c = jnp.exp(s_c - m)
            p_p = jnp.exp(s_p - m)
            l = jnp.sum(p_c, axis=1, keepdims=True) + jnp.sum(p_p, axis=1, keepdims=True)
            o = (jnp.dot(p_c.astype(BF16), vc_ref[:, hs], preferred_element_type=F32)
                 + jnp.dot(p_p.astype(BF16), vp_ref[:, hs], preferred_element_type=F32))
            o_ref[:, hs] = o / l
            l_ref[:, hs] = jnp.broadcast_to(m + jnp.log(l), (BLK, HEAD_DIM))

    def cur(part):
        return pl.BlockSpec((BLK, HEAD_BLOCK), lambda r, b, hh: (b, r * cb + q0 + part * nhh + hh))

    def prev(part):
        return pl.BlockSpec((BLK, HEAD_BLOCK), lambda r, b, hh: (jnp.maximum(b - 1, 0), r * cb + q0 + part * nhh + hh))

    out = pl.BlockSpec((BLK, HEAD_BLOCK), lambda r, b, hh: (b, r * nhh + hh))
    Pv = P.reshape(L, d * C)
    o, lse = _call(body, name=name, grid=(d, nb, nhh), in_specs=[cur(0), cur(1), prev(1), cur(2), prev(2)],
                   out_specs=(out, out),
                   out_shape=(jax.ShapeDtypeStruct((L, d * a_out), F32),) * 2,
                   compiler_params=_params(("parallel", "parallel", "parallel")))(Pv, Pv, Pv, Pv, Pv)
    return o.reshape(T, a_out), lse.reshape(T, a_out)


def _attn_combine(os_, lses, *, name):
    T, W = os_[0].shape
    tr = _rows(T, 256)
    n = len(os_)

    def body(*refs):
        o_refs, l_refs = refs[:n], refs[n:2 * n]
        mix_ref, lse_ref = refs[2 * n], refs[2 * n + 1]
        ls = [r[...] for r in l_refs]
        m = functools.reduce(jnp.maximum, ls)
        es = [jnp.exp(l - m) for l in ls]
        tot = functools.reduce(lambda a, b: a + b, es)
        mix = functools.reduce(lambda a, b: a + b, [(e / tot) * o[...] for e, o in zip(es, o_refs)])
        mix_ref[...] = mix
        lse_ref[...] = m + jnp.log(tot)

    blk = pl.BlockSpec((tr, W), lambda i: (i, 0))
    sds = jax.ShapeDtypeStruct((T, W), F32)
    return _call(body, name=name, grid=(T // tr,), in_specs=[blk] * (2 * n), out_specs=(blk, blk),
                 out_shape=(sds, sds), compiler_params=_params(("parallel",)))(*os_, *lses)


def _attn_bwd(P, dO, O, LSE, grp, d, a_out, *, name):
    T, C = P.shape
    L = T // d
    nb = L // BLK
    nhh = a_out // HEAD_BLOCK
    cb = C // HEAD_BLOCK
    q0 = grp * 3 * nhh
    scale = HEAD_DIM ** -0.5

    def body(q_ref, qn_ref, kc_ref, kp_ref, vc_ref, vp_ref, do_ref, don_ref, o_ref, on_ref, l_ref, ln_ref,
             dq_ref, dk_ref, dv_ref):
        b = pl.program_id(1)
        mask_c, mask_prev = _band_masks()
        mask_p = jnp.logical_and(mask_prev, b > 0)
        mask_n = jnp.logical_and(mask_prev, b < nb - 1)
        for h in range(HEAD_BLOCK // HEAD_DIM):
            hs = slice(h * HEAD_DIM, (h + 1) * HEAD_DIM)
            q, qn, kc, kp, vc, vp = (r[:, hs] for r in (q_ref, qn_ref, kc_ref, kp_ref, vc_ref, vp_ref))
            do, don = do_ref[:, hs], don_ref[:, hs]
            lse, lse_n = l_ref[:, hs], ln_ref[:, hs]
            delta = jnp.sum(do * o_ref[:, hs], axis=1, keepdims=True)
            delta_n = jnp.sum(don * on_ref[:, hs], axis=1, keepdims=True)
            dob, donb = do.astype(BF16), don.astype(BF16)
            p_c = jnp.exp(jnp.where(mask_c, _dot_nt(q, kc) * scale, NEG) - lse)
            p_p = jnp.exp(jnp.where(mask_p, _dot_nt(q, kp) * scale, NEG) - lse)
            p_n = jnp.exp(jnp.where(mask_n, _dot_nt(qn, kc) * scale, NEG) - lse_n)
            ds_c = (p_c * (_dot_nt(dob, vc) - delta) * scale).astype(BF16)
            ds_p = (p_p * (_dot_nt(dob, vp) - delta) * scale).astype(BF16)
            ds_n = (p_n * (_dot_nt(donb, vc) - delta_n) * scale).astype(BF16)
            dq = jnp.dot(ds_c, kc, preferred_element_type=F32) + jnp.dot(ds_p, kp, preferred_element_type=F32)
            dk = _dot_tn(ds_c, q) + _dot_tn(ds_n, qn)
            dv = _dot_tn(p_c.astype(BF16), dob) + _dot_tn(p_n.astype(BF16), donb)
            dq_ref[:, hs] = dq.astype(BF16)
            dk_ref[:, hs] = dk.astype(BF16)
            dv_ref[:, hs] = dv.astype(BF16)

    def pspec(part, shift):
        def imap(r, b, hh):
            return (jnp.clip(b + shift, 0, nb - 1), r * cb + q0 + part * nhh + hh)
        return pl.BlockSpec((BLK, HEAD_BLOCK), imap)

    def aspec(shift):
        return pl.BlockSpec((BLK, HEAD_BLOCK), lambda r, b, hh: (jnp.clip(b + shift, 0, nb - 1), r * nhh + hh))

    Pv = P.reshape(L, d * C)
    dOv, Ov, Lv = (t.reshape(L, d * a_out) for t in (dO, O, LSE))
    sds = jax.ShapeDtypeStruct((L, d * a_out), BF16)
    dq, dk, dv = _call(
        body, name=name, grid=(d, nb, nhh),
        in_specs=[pspec(0, 0), pspec(0, 1), pspec(1, 0), pspec(1, -1), pspec(2, 0), pspec(2, -1),
                  aspec(0), aspec(1), aspec(0), aspec(1), aspec(0), aspec(1)],
        out_specs=(aspec(0),) * 3, out_shape=(sds,) * 3,
        compiler_params=_params(("parallel", "parallel", "parallel")))(Pv, Pv, Pv, Pv, Pv, Pv, dOv, dOv, Ov, Ov, Lv, Lv)
    return [t.reshape(T, a_out) for t in (dq, dk, dv)]


def _mem_softmax(q, k, scale):
    s = _dot_nt(q, k) * scale
    e = jnp.exp(s - jnp.max(s, axis=1, keepdims=True))
    return e / jnp.sum(e, axis=1, keepdims=True)


def _mem_fwd(P, qcol, kv, *, name):
    T = P.shape[0]
    tq = _rows(T, 512)
    scale = HEAD_DIM ** -0.5

    def body(q_ref, kv_ref, o_ref):
        for h in range(MEM_HEADS):
            hs = slice(h * HEAD_DIM, (h + 1) * HEAD_DIM)
            vs = slice(MEM_WIDTH + h * HEAD_DIM, MEM_WIDTH + (h + 1) * HEAD_DIM)
            p = _mem_softmax(q_ref[:, hs].astype(BF16), kv_ref[:, hs], scale)
            o_ref[:, hs] = jnp.dot(p.astype(BF16), kv_ref[:, vs], preferred_element_type=F32)

    return _call(body, name=name, grid=(T // tq,),
                 in_specs=[pl.BlockSpec((tq, MEM_WIDTH), lambda i: (i, qcol)),
                           pl.BlockSpec((MEM_TOKENS, 2 * MEM_WIDTH), lambda i: (0, 0))],
                 out_specs=pl.BlockSpec((tq, MEM_WIDTH), lambda i: (i, 0)),
                 out_shape=jax.ShapeDtypeStruct((T, MEM_WIDTH), F32),
                 compiler_params=_params(("parallel",)))(P, kv)


def _mem_bwd(P, qcol, kv, dcat, ocol, *, name):
    T = P.shape[0]
    tq = _rows(T, 512)
    scale = HEAD_DIM ** -0.5

    def body(q_ref, kv_ref, do_ref, dq_ref, dkv_ref):
        i = pl.program_id(0)

        @pl.when(i == 0)
        def _():
            dkv_ref[...] = jnp.zeros_like(dkv_ref)

        for h in range(MEM_HEADS):
            hs = slice(h * HEAD_DIM, (h + 1) * HEAD_DIM)
            vs = slice(MEM_WIDTH + h * HEAD_DIM, MEM_WIDTH + (h + 1) * HEAD_DIM)
            q = q_ref[:, hs].astype(BF16)
            k, v = kv_ref[:, hs], kv_ref[:, vs]
            do = do_ref[:, hs].astype(BF16)
            p = _mem_softmax(q, k, scale)
            dp = _dot_nt(do, v)
            ds = (p * (dp - jnp.sum(p * dp, axis=1, keepdims=True)) * scale).astype(BF16)
            dq_ref[:, hs] = jnp.dot(ds, k, preferred_element_type=F32).astype(BF16)
            dkv_ref[:, hs] += _dot_tn(ds, q)
            dkv_ref[:, vs] += _dot_tn(p.astype(BF16), do)

    return _call(body, name=name, grid=(T // tq,),
                 in_specs=[pl.BlockSpec((tq, MEM_WIDTH), lambda i: (i, qcol)),
                           pl.BlockSpec((MEM_TOKENS, 2 * MEM_WIDTH), lambda i: (0, 0)),
                           pl.BlockSpec((tq, MEM_WIDTH), lambda i: (i, ocol))],
                 out_specs=(pl.BlockSpec((tq, MEM_WIDTH), lambda i: (i, 0)),
                            pl.BlockSpec((MEM_TOKENS, 2 * MEM_WIDTH), lambda i: (0, 0))),
                 out_shape=(jax.ShapeDtypeStruct((T, MEM_WIDTH), BF16),
                            jax.ShapeDtypeStruct((MEM_TOKENS, 2 * MEM_WIDTH), F32)),
                 compiler_params=_params(("arbitrary",)))(P, kv, dcat)


_SQRT_HALF = 0.7071067811865476
_INV_SQRT_2PI = 0.3989422804014327


def _gelu(x):
    return 0.5 * x * (1.0 + lax.erf(x * _SQRT_HALF))


def _gelu_grad(x):
    return 0.5 * (1.0 + lax.erf(x * _SQRT_HALF)) + x * (_INV_SQRT_2PI * jnp.exp(-0.5 * x * x))


def _tril():
    t = lax.broadcasted_iota(jnp.int32, (BLK, BLK), 0)
    s = lax.broadcasted_iota(jnp.int32, (BLK, BLK), 1)
    return t >= s


def _gmlp_fwd(P, w_s, bs_t, vg, vb, width, *, name):
    T = P.shape[0]
    G = width // HEAD_DIM
    tb = _rows(T, 512)

    def body(pu_ref, pv_ref, ws_ref, bs_ref, vg_ref, vb_ref, o_ref):
        u = _gelu(pu_ref[...])
        v = _gelu(pv_ref[...])
        mu = jnp.mean(v, axis=-1, keepdims=True)
        vc = v - mu
        var = jnp.mean(vc * vc, axis=-1, keepdims=True)
        vn = (vc * lax.rsqrt(var + LN_EPS) * vg_ref[...] + vb_ref[...]).astype(BF16)
        tril = _tril()
        for g in range(G):
            gs = slice(g * HEAD_DIM, (g + 1) * HEAD_DIM)
            ws = jnp.where(tril, ws_ref[g], 0.0).astype(BF16)
            bias = bs_ref[:, g:g + 1]
            for c in range(tb // BLK):
                cs = slice(c * BLK, (c + 1) * BLK)
                sg = jnp.dot(ws, vn[cs, gs], preferred_element_type=F32) + bias
                o_ref[cs, gs] = u[cs, gs] * sg

    blk = lambda col: pl.BlockSpec((tb, width), lambda i: (i, col))
    full = lambda shape: pl.BlockSpec(shape, lambda i: (0,) * len(shape))
    return _call(body, name=name, grid=(T // tb,),
                 in_specs=[blk(0), blk(1), full((G, BLK, BLK)), full((BLK, G)), full((1, width)), full((1, width))],
                 out_specs=blk(0), out_shape=jax.ShapeDtypeStruct((T, width), F32),
                 compiler_params=_params(("parallel",)))(P, P, w_s, bs_t, vg.reshape(1, width), vb.reshape(1, width))


def _gmlp_bwd(P, w_s, bs_t, vg, vb, dcat, width, *, name):
    T = P.shape[0]
    G = width // HEAD_DIM
    tb = _rows(T, 512)

    def body(pu_ref, pv_ref, ws_ref, bs_ref, vg_ref, vb_ref, dm_ref, dpu_ref, dpv_ref, dws_ref, dbs_ref, dvg_ref,
             dvb_ref, dvn_ref):
        i = pl.program_id(0)

        @pl.when(i == 0)
        def _():
            dws_ref[...] = jnp.zeros_like(dws_ref)
            dbs_ref[...] = jnp.zeros_like(dbs_ref)
            dvg_ref[...] = jnp.zeros_like(dvg_ref)
            dvb_ref[...] = jnp.zeros_like(dvb_ref)

        pu, pv = pu_ref[...], pv_ref[...]
        u = _gelu(pu)
        v = _gelu(pv)
        mu = jnp.mean(v, axis=-1, keepdims=True)
        vc = v - mu
        var = jnp.mean(vc * vc, axis=-1, keepdims=True)
        rstd = lax.rsqrt(var + LN_EPS)
        xhat = vc * rstd
        vn = (xhat * vg_ref[...] + vb_ref[...]).astype(BF16)
        dm = dm_ref[...]
        tril = _tril()
        lane = lax.broadcasted_iota(jnp.int32, (BLK, BLK), 1)
        dbs = jnp.zeros((BLK, BLK), F32)
        for g in range(G):
            gs = slice(g * HEAD_DIM, (g + 1) * HEAD_DIM)
            ws = jnp.where(tril, ws_ref[g], 0.0).astype(BF16)
            bias = bs_ref[:, g:g + 1]
            dws = jnp.zeros((BLK, BLK), F32)
            rs = jnp.zeros((BLK, 1), F32)
            for c in range(tb // BLK):
                cs = slice(c * BLK, (c + 1) * BLK)
                vn_cg = vn[cs, gs]
                sg = jnp.dot(ws, vn_cg, preferred_element_type=F32) + bias
                dm_cg = dm[cs, gs]
                dpu_ref[cs, gs] = (dm_cg * sg * _gelu_grad(pu[cs, gs])).astype(BF16)
                dsg = dm_cg * u[cs, gs]
                dsgb = dsg.astype(BF16)
                dvn_ref[cs, gs] = _dot_tn(ws, dsgb)
                dws = dws + _dot_nt(dsgb, vn_cg)
                rs = rs + jnp.sum(dsg, axis=1, keepdims=True)
            dws_ref[g] += jnp.where(tril, dws, 0.0)
            dbs = dbs + jnp.where(lane == g, rs, 0.0)
        dbs_ref[...] += dbs
        dvn = dvn_ref[...]
        dxh = dvn * vg_ref[...]
        dv = rstd * (dxh - jnp.mean(dxh, axis=-1, keepdims=True)
                     - xhat * jnp.mean(dxh * xhat, axis=-1, keepdims=True))
        dpv_ref[...] = (dv * _gelu_grad(pv)).astype(BF16)
        dvg_ref[...] += jnp.sum(dvn * xhat, axis=0, keepdims=True)
        dvb_ref[...] += jnp.sum(dvn, axis=0, keepdims=True)

    blk = lambda col: pl.BlockSpec((tb, width), lambda i: (i, col))
    full = lambda shape: pl.BlockSpec(shape, lambda i: (0,) * len(shape))
    return _call(body, name=name, grid=(T // tb,),
                 in_specs=[blk(0), blk(1), full((G, BLK, BLK)), full((BLK, G)), full((1, width)), full((1, width)),
                           blk(0)],
                 out_specs=(blk(0), blk(0), full((G, BLK, BLK)), full((BLK, BLK)), full((1, width)), full((1, width))),
                 out_shape=(jax.ShapeDtypeStruct((T, width), BF16), jax.ShapeDtypeStruct((T, width), BF16),
                            jax.ShapeDtypeStruct((G, BLK, BLK), F32), jax.ShapeDtypeStruct((BLK, BLK), F32),
                            jax.ShapeDtypeStruct((1, width), F32), jax.ShapeDtypeStruct((1, width), F32)),
                 scratch_shapes=[pltpu.VMEM((tb, width), F32)],
                 compiler_params=_params(("arbitrary",)))(P, P, w_s, bs_t, vg.reshape(1, width), vb.reshape(1, width), dcat)


def _place():
    x, y, c = lax.axis_index("x"), lax.axis_index("y"), lax.axis_index("c")
    chips = [(1 - x, y), (x, 1 - y), (1 - x, 1 - y)]
    return x, y, c, 2 * x + y, chips, [2 * px + py for px, py in chips]


def _half(ref, c, rows):
    return ref.at[pl.ds(c * (rows // 2), rows // 2)]


def _place_shard(w, chip, dtype, *, name):
    R, C = w.shape
    tr = _rows(R, max(8, (1 << 19) // C // 8 * 8)) if R % 8 == 0 else R

    def body(chip_ref, w_ref, o_ref):
        o_ref[...] = w_ref[...].astype(dtype)

    grid_spec = pltpu.PrefetchScalarGridSpec(
        num_scalar_prefetch=1, grid=(R // tr,), in_specs=[pl.BlockSpec((tr, C), lambda i, s: (i, 0))],
        out_specs=pl.BlockSpec((None, tr, C), lambda i, s: (s[0], i, 0)))
    return _call(body, name=name, grid_spec=grid_spec, out_shape=jax.ShapeDtypeStruct((4, R, C), dtype),
                 compiler_params=_params(("parallel",)))(chip, w)


SEM = pl.BlockSpec(memory_space=pltpu.SEMAPHORE)
EFFECT = pltpu.SideEffectType.DATAFLOW_SIDE_EFFECTING


def _hbm(a):
    return pltpu.with_memory_space_constraint(a, pltpu.HBM)


def _sibling_start(arrays, copies, ncopies, after, *, name):
    n, na = len(arrays), len(after)

    def body(*refs):
        a = refs[:n]
        send, recv = refs[n + na], refs[n + na + 1]
        token = refs[2 * n + na + 2]
        x, y, c, _, _, _ = _place()
        for q, (src, dst) in enumerate(copies(a, c)):
            pltpu.make_async_remote_copy(src_ref=src, dst_ref=dst, send_sem=send.at[q], recv_sem=recv.at[q],
                                         device_id=(x, y, 1 - c), device_id_type=MESH).start()
        token[...] = jnp.zeros_like(token)

    outs = _call(body, name=name, in_specs=[HBM] * n + [pl.BlockSpec(memory_space=pl.ANY)] * na,
                 out_specs=(SEM, SEM, *([HBM] * n), pl.BlockSpec(memory_space=pltpu.VMEM)),
                 out_shape=(pltpu.SemaphoreType.DMA((ncopies,)), pltpu.SemaphoreType.DMA((ncopies,)),
                            *[pltpu.HBM(a.shape, a.dtype) for a in arrays], jax.ShapeDtypeStruct((8, 128), F32)),
                 input_output_aliases={t: 2 + t for t in range(n)},
                 compiler_params=pltpu.CompilerParams(has_side_effects=EFFECT))(*[_hbm(a) for a in arrays], *after)
    return outs[0], outs[1], list(outs[2:2 + n]), outs[2 + n]


def _sibling_wait(send, recv, arrays, copies, after, *, name):
    n, na = len(arrays), len(after)

    def body(*refs):
        a = refs[:n]
        send, recv = refs[n], refs[n + 1]
        x, y, c, _, _, _ = _place()
        for q, ((src, dst), (_, landed)) in enumerate(zip(copies(a, c), copies(a, 1 - c))):
            pltpu.make_async_remote_copy(src_ref=src, dst_ref=dst, send_sem=send.at[q], recv_sem=recv.at[q],
                                         device_id=(x, y, 1 - c), device_id_type=MESH).wait_send()
            pltpu.make_async_remote_copy(src_ref=src, dst_ref=landed, send_sem=send.at[q], recv_sem=recv.at[q],
                                         device_id=(x, y, 1 - c), device_id_type=MESH).wait_recv()

    outs = _call(body, name=name, in_specs=[HBM] * n + [SEM, SEM] + [pl.BlockSpec(memory_space=pl.ANY)] * na,
                 out_specs=[HBM] * n, out_shape=[pltpu.HBM(a.shape, a.dtype) for a in arrays],
                 input_output_aliases={t: t for t in range(n)},
                 compiler_params=pltpu.CompilerParams(has_side_effects=EFFECT))(*arrays, send, recv, *after)
    return list(outs)


def _pair_copies(n):
    def copies(a, core):
        out = []
        for t in range(n):
            h = a[t].shape[1] // 2
            out.append((a[t].at[:, pl.ds((1 - core) * h, h)], a[n + t]))
        return out
    return copies


def _swap_copies(a, core):
    out = []
    for ref in a:
        h = ref.shape[0] // 2
        out.append((ref.at[pl.ds(core * h, h)],) * 2)
    return out


def _gather_windows(shapes):
    split = [s[1] % 16 == 0 for s in shapes]

    def window(ref, t, chip_idx, core):
        w = ref.at[chip_idx]
        return _half(w, core, shapes[t][1]) if split[t] else w

    return split, window


def _gather_start(bufs, after, *, name):
    n = len(bufs)
    split, window = _gather_windows([b.shape for b in bufs])

    na = len(after)

    def body(*refs):
        b = refs[:n]
        send, recv = refs[n + na], refs[n + na + 1]
        token = refs[2 * n + na + 2]
        x, y, c, j, chips, pj = _place()
        for t in range(n):
            for k in range(3):
                pltpu.make_async_remote_copy(src_ref=window(b[t], t, j, c), dst_ref=window(b[t], t, j, c),
                                             send_sem=send.at[3 * t + k], recv_sem=recv.at[3 * t + k],
                                             device_id=(*chips[k], c), device_id_type=MESH).start()
        token[...] = jnp.zeros_like(token)

    outs = _call(body, name=name, in_specs=[HBM] * n + [pl.BlockSpec(memory_space=pl.ANY)] * na,
                 out_specs=(SEM, SEM, *([HBM] * n), pl.BlockSpec(memory_space=pltpu.VMEM)),
                 out_shape=(pltpu.SemaphoreType.DMA((3 * n,)), pltpu.SemaphoreType.DMA((3 * n,)),
                            *[pltpu.HBM(b.shape, b.dtype) for b in bufs], jax.ShapeDtypeStruct((8, 128), F32)),
                 input_output_aliases={t: 2 + t for t in range(n)},
                 compiler_params=pltpu.CompilerParams(has_side_effects=EFFECT))(*[_hbm(b) for b in bufs], *after)
    return outs[0], outs[1], list(outs[2:2 + n]), outs[2 + n]


def _gather_wait(send, recv, bufs, after, *, name):
    n = len(bufs)
    split, window = _gather_windows([b.shape for b in bufs])

    def body(*refs):
        b = refs[:n]
        send, recv = refs[n], refs[n + 1]
        x, y, c, j, chips, pj = _place()
        for t in range(n):
            for k in range(3):
                out = pltpu.make_async_remote_copy(src_ref=window(b[t], t, j, c), dst_ref=window(b[t], t, j, c),
                                                   send_sem=send.at[3 * t + k], recv_sem=recv.at[3 * t + k],
                                                   device_id=(*chips[k], c), device_id_type=MESH)
                out.wait_send()
                back = pltpu.make_async_remote_copy(src_ref=window(b[t], t, pj[k], c), dst_ref=window(b[t], t, pj[k], c),
                                                    send_sem=send.at[3 * t + k], recv_sem=recv.at[3 * t + k],
                                                    device_id=(*chips[k], c), device_id_type=MESH)
                back.wait_recv()

    outs = _call(body, name=name, in_specs=[HBM] * n + [SEM, SEM] + [pl.BlockSpec(memory_space=pl.ANY)] * len(after),
                 out_specs=[HBM] * n, out_shape=[pltpu.HBM(b.shape, b.dtype) for b in bufs],
                 input_output_aliases={t: t for t in range(n)},
                 compiler_params=pltpu.CompilerParams(has_side_effects=EFFECT))(*bufs, send, recv, *after)
    return list(outs)


def _pass_copies(shapes):
    split, window = _gather_windows(shapes)

    def copies(a, core):
        _, _, _, _, _, pj = _place()
        return [(window(a[t], t, pj[k], core),) * 2 for t in range(len(shapes)) if split[t] for k in range(3)]

    return copies, 3 * sum(split)


def _gather_pass(bufs, *, name):
    n = len(bufs)
    split, window = _gather_windows([b.shape for b in bufs])
    idx = [t for t in range(n) if split[t]]

    def body(*refs):
        b = refs[n:2 * n]
        send, recv = refs[2 * n:]
        x, y, c, j, chips, pj = _place()

        def d2d(u, t, k, core):
            w = window(b[t], t, pj[k], core)
            return pltpu.make_async_remote_copy(src_ref=w, dst_ref=w, send_sem=send.at[3 * u + k],
                                                recv_sem=recv.at[3 * u + k], device_id=(x, y, 1 - c),
                                                device_id_type=MESH)

        sent = [d2d(u, t, k, c) for u, t in enumerate(idx) for k in range(3)]
        for cp in sent:
            cp.start()
        for u, t in enumerate(idx):
            for k in range(3):
                d2d(u, t, k, 1 - c).wait_recv()
        for cp in sent:
            cp.wait_send()

    return _call(body, name=name, in_specs=[HBM] * n, out_specs=[HBM] * n,
                 out_shape=[jax.ShapeDtypeStruct(b.shape, b.dtype) for b in bufs],
                 input_output_aliases={t: t for t in range(n)},
                 scratch_shapes=[pltpu.SemaphoreType.DMA((3 * len(idx),))] * 2)(*bufs)


def _chip_start(pairs, *, name):
    n = len(pairs)
    lands = [lax.empty((3,) + p.shape[1:], p.dtype) for p in pairs]

    def body(*refs):
        s, r = refs[:n], refs[n:2 * n]
        send, recv = refs[2 * n], refs[2 * n + 1]
        token = refs[4 * n + 2]
        x, y, c, j, chips, pj = _place()
        for t in range(n):
            for k in range(3):
                pltpu.make_async_remote_copy(src_ref=s[t].at[pj[k]], dst_ref=r[t].at[k], send_sem=send.at[3 * t + k],
                                             recv_sem=recv.at[3 * t + k], device_id=(*chips[k], c),
                                             device_id_type=MESH).start()
        token[...] = jnp.zeros_like(token)

    outs = _call(body, name=name, in_specs=[HBM] * (2 * n),
                 out_specs=(SEM, SEM, *([HBM] * (2 * n)), pl.BlockSpec(memory_space=pltpu.VMEM)),
                 out_shape=(pltpu.SemaphoreType.DMA((3 * n,)), pltpu.SemaphoreType.DMA((3 * n,)),
                            *[pltpu.HBM(a.shape, a.dtype) for a in list(pairs) + lands],
                            jax.ShapeDtypeStruct((8, 128), F32)),
                 input_output_aliases={t: 2 + t for t in range(2 * n)},
                 compiler_params=pltpu.CompilerParams(has_side_effects=EFFECT))(*[_hbm(a) for a in list(pairs) + lands])
    return outs[0], outs[1], list(outs[2:2 + n]), list(outs[2 + n:2 + 2 * n]), outs[2 + 2 * n]


def _chip_wait(send, recv, pairs, lands, after, *, name):
    n = len(pairs)

    def body(*refs):
        s, r = refs[:n], refs[n:2 * n]
        send, recv = refs[2 * n], refs[2 * n + 1]
        x, y, c, j, chips, pj = _place()
        for t in range(n):
            for k in range(3):
                cp = pltpu.make_async_remote_copy(src_ref=s[t].at[pj[k]], dst_ref=r[t].at[k], send_sem=send.at[3 * t + k],
                                                  recv_sem=recv.at[3 * t + k], device_id=(*chips[k], c),
                                                  device_id_type=MESH)
                cp.wait_send()
                cp.wait_recv()

    outs = _call(body, name=name,
                 in_specs=[HBM] * (2 * n) + [SEM, SEM] + [pl.BlockSpec(memory_space=pl.ANY)] * len(after),
                 out_specs=[HBM] * (2 * n), out_shape=[pltpu.HBM(a.shape, a.dtype) for a in list(pairs) + list(lands)],
                 input_output_aliases={t: t for t in range(2 * n)},
                 compiler_params=pltpu.CompilerParams(has_side_effects=EFFECT))(*pairs, *lands, send, recv, *after)
    return list(outs[:n]), list(outs[n:])


def _all_reduce_small(packed, *, name):
    R, C = packed.shape

    def body(p_ref, o_ref, slots, send, recv, lsem):
        x, y, c = lax.axis_index("x"), lax.axis_index("y"), lax.axis_index("c")
        me = 4 * x + 2 * y + c
        lc = pltpu.make_async_copy(p_ref, slots.at[me], lsem.at[0])
        lc.start()
        copies = []
        for rel in range(1, 8):
            fx, fy, fc = (rel >> 2) & 1, (rel >> 1) & 1, rel & 1
            to = (1 - x if fx else x, 1 - y if fy else y, 1 - c if fc else c)
            cp = pltpu.make_async_remote_copy(src_ref=p_ref, dst_ref=slots.at[me], send_sem=send.at[rel - 1],
                                              recv_sem=recv.at[rel - 1], device_id=to, device_id_type=MESH)
            cp.start()
            copies.append((cp, 4 * to[0] + 2 * to[1] + to[2]))
        for rel, (cp, frm) in enumerate(copies):
            cp.wait_send()
            pltpu.make_async_remote_copy(src_ref=p_ref, dst_ref=slots.at[frm], send_sem=send.at[rel],
                                         recv_sem=recv.at[rel], device_id=(x, y, c), device_id_type=MESH).wait_recv()
        lc.wait()
        acc = slots[0]
        for dev in range(1, 8):
            acc = acc + slots[dev]
        o_ref[...] = acc

    return _call(body, name=name, in_specs=[pl.BlockSpec(memory_space=pltpu.VMEM)],
                 out_specs=pl.BlockSpec(memory_space=pltpu.VMEM), out_shape=jax.ShapeDtypeStruct((R, C), F32),
                 scratch_shapes=[pltpu.VMEM((8, R, C), F32), pltpu.SemaphoreType.DMA((7,)),
                                 pltpu.SemaphoreType.DMA((7,)), pltpu.SemaphoreType.DMA((1,))],
                 compiler_params=pltpu.CompilerParams(vmem_limit_bytes=V7X_VMEM_LIMIT))(packed)


def _pair_sum(grad, theirs, core, *, name):
    J, H, C = theirs.shape
    tr = _rows(H, max(8, (1 << 19) // C // 8 * 8))

    def body(core_ref, a_ref, b_ref, o_ref):
        o_ref[...] = (a_ref[...].astype(F32) + b_ref[...].astype(F32)).astype(BF16)

    blk = pl.BlockSpec((None, tr, C), lambda j, i, s: (j, i, 0))
    mine = pl.BlockSpec((None, None, tr, C), lambda j, i, s: (j, s[0], i, 0))
    grid_spec = pltpu.PrefetchScalarGridSpec(num_scalar_prefetch=1, grid=(J, H // tr), in_specs=[mine, blk],
                                             out_specs=blk)
    return _call(body, name=name, grid_spec=grid_spec, out_shape=jax.ShapeDtypeStruct((J, H, C), BF16),
                 compiler_params=_params(("parallel", "parallel")))(core, grad.reshape(J, 2, H, C), theirs)


def _chip_sum(pairs, slots, place, *, name):
    _, H, C = slots.shape
    tr = _rows(H, max(8, (1 << 19) // C // 8 * 8))
    nr = H // tr

    def body(place_ref, s0, s1, s2, s3, o_ref):
        o_ref[...] = ((s0[...].astype(F32) + s1[...].astype(F32)) + s2[...].astype(F32)) + s3[...].astype(F32)

    def slot(k):
        return pl.BlockSpec((None, tr, C), lambda i, s: (k, i, 0))

    own = pl.BlockSpec((None, tr, C), lambda i, s: (s[0], i, 0))
    grid_spec = pltpu.PrefetchScalarGridSpec(
        num_scalar_prefetch=1, grid=(nr,), in_specs=[own, slot(0), slot(1), slot(2)],
        out_specs=pl.BlockSpec((tr, C), lambda i, s: (s[1] * nr + i, 0)))
    return _call(body, name=name, grid_spec=grid_spec, out_shape=jax.ShapeDtypeStruct((2 * H, C), F32),
                 compiler_params=_params(("parallel",)))(place, pairs, slots, slots, slots)


def _scatter_pair_start(grads, tag):
    lands = [lax.empty((4, g.shape[1] // 2, g.shape[2]), g.dtype) for g in grads]
    return _sibling_start(list(grads) + lands, _pair_copies(len(grads)), len(grads), (), name=f"rs_pair_start_{tag}")


def _scatter_chip_start(pair_started, after, place, tag):
    send, recv, arrays, _ = pair_started
    n = len(arrays) // 2
    arrays = _sibling_wait(send, recv, arrays, _pair_copies(n), after, name=f"rs_pair_wait_{tag}")
    pairs = [_pair_sum(g, t, place[1:], name=f"rs_pair_sum_{tag}_{i}")
             for i, (g, t) in enumerate(zip(arrays[:n], arrays[n:]))]
    return _chip_start(pairs, name=f"rs_chip_start_{tag}")


def _scatter_swap_start(chip_started, after, place, tag):
    send, recv, pairs, lands, _ = chip_started
    pairs, slots = _chip_wait(send, recv, pairs, lands, after, name=f"rs_chip_wait_{tag}")
    fulls = [_chip_sum(p, s, place, name=f"rs_chip_sum_{tag}_{i}") for i, (p, s) in enumerate(zip(pairs, slots))]
    return _sibling_start(fulls, _swap_copies, len(fulls), (), name=f"rs_swap_start_{tag}")


def _scatter_finish(swap_started, tag):
    send, recv, fulls, _ = swap_started
    return _sibling_wait(send, recv, fulls, _swap_copies, (), name=f"rs_swap_wait_{tag}")


def kernel(x, mem, w_in_a, w_in_b, w_s, b_s, vnorm_g, vnorm_b, w_mem_kv, w_out, ln1_g, ln1_b, w_ff1, w_ff2, ln2_g, ln2_b, loss_target, m_w_in_a, m_w_in_b, m_w_s, m_b_s, m_vnorm_g, m_vnorm_b, m_w_mem_kv, m_w_out, m_ln1_g, m_ln1_b, m_w_ff1, m_w_ff2, m_ln2_g, m_ln2_b, v_w_in_a, v_w_in_b, v_w_s, v_b_s, v_vnorm_g, v_vnorm_b, v_w_mem_kv, v_w_out, v_ln1_g, v_ln1_b, v_w_ff1, v_w_ff2, v_ln2_g, v_ln2_b):
    T, D = x.shape[1], x.shape[2]
    depth = w_ff1.shape[0]
    alpha = (2.0 * depth) ** 0.25
    a_out = (D // 256) * HEAD_DIM
    a_cols = len(A_PAIRS) * 3 * a_out
    b_width = (D // 256) * HEAD_DIM
    G = b_width // HEAD_DIM
    assert a_out % HEAD_BLOCK == 0 and T % (BLK * A_PAIRS[-1][1]) == 0

    xf = x.reshape(T, D)
    mem_b = mem.reshape(MEM_TOKENS, D).astype(BF16)
    target = loss_target.reshape(T, D)
    c_idx = lax.axis_index("x") * 2 + lax.axis_index("y")
    place = jnp.stack([c_idx, lax.axis_index("c")]).astype(jnp.int32)

    def gather_place(i, group):
        jl, tag = i // 2, "a" if i % 2 == 0 else "b"
        if group == "mix":
            mats = [(w_in_a if i % 2 == 0 else w_in_b)[jl], w_mem_kv[i], w_out[i]]
        else:
            mats = [w_ff1[i], w_ff2[i]]
        bufs = [_place_shard(w, place[:1], BF16, name=f"place_{tag}_{group}_{n_}") for n_, w in enumerate(mats)]
        if group == "mix" and i % 2:
            bufs += [_place_shard(v_[jl].reshape(1, -1), place[:1], F32, name="place_vnorm") for v_ in (vnorm_g, vnorm_b)]
        return bufs

    placed = {}

    def gather_begin(i, group, after):
        tag = "a" if i % 2 == 0 else "b"
        bufs = placed.pop((i, group), None) or gather_place(i, group)
        return _gather_start(bufs, after, name=f"gather_start_{tag}_{group}")

    def gather_end(started, after, name):
        send, recv, bufs, _ = started
        bufs = _gather_wait(send, recv, bufs, after, name=f"gather_wait_{name}")
        return _gather_pass(bufs, name=f"gather_pass_{name}")

    def group_view(P, g, d):
        return (P, g) if d == 1 else (P[:, g * 3 * a_out:(g + 1) * 3 * a_out], 0)

    saved = []
    mix_started = gather_begin(0, "mix", ())
    for key in ((0, "ff"), (1, "mix"), (1, "ff")):
        if key[0] < depth:
            placed[key] = gather_place(*key)
    early = [b for bufs in placed.values() for b in bufs]
    ff_started = None
    for i in range(depth):
        jl = i // 2
        is_a = i % 2 == 0
        tag = "a" if is_a else "b"
        gathered = gather_end(mix_started, (xf, *early) if i == 0 else (xf,), f"{tag}_mix")
        win, wkv, wout = gathered[:3]
        wkv = wkv.reshape(1, D, 2 * MEM_WIDTH)
        tokens = []
        if i > 0:
            send, recv, ff_bufs, _ = ff_started
            ff_bufs = _gather_wait(send, recv, ff_bufs, (xf,), name=f"gather_wait_{tag}_ff")
            pass_copies, ncopies = _pass_copies([b.shape for b in ff_bufs])
            ff_pass = _sibling_start(ff_bufs, pass_copies, ncopies, (), name=f"gather_pass_start_{tag}_ff")
            tokens.append(ff_pass[3])
        token = None
        if i == 0:
            ff_started = gather_begin(0, "ff", (win,))
            token = ff_started[3]
        if i + 1 < depth:
            mix_started = gather_begin(i + 1, "mix", (win,) if token is None else (token,))
            ff_next = gather_begin(i + 1, "ff", (mix_started[3],))
            token = ff_next[3]
        if token is not None:
            tokens.append(token)
        xb = functools.reduce(lambda acc, t: acc + t[0, 0], tokens, xf).astype(BF16)
        P = _mm_nn(xb, win, out_dtype=BF16 if is_a else F32, name=f"proj_in_{'a' if is_a else 'b'}")
        kv = _mm_nn(mem_b, wkv, name="proj_kv")
        if is_a:
            outs = [_attn_fwd(*group_view(P, g, d), d, a_out, name=f"attn_fwd_d{d}") for g, (_, d) in enumerate(A_PAIRS)]
            mix, lse = _attn_combine([o for o, _ in outs], [l for _, l in outs], name="attn_combine")
            qcol = a_cols // MEM_WIDTH
            extra = (lse,)
        else:
            vg_full = gathered[3].reshape(-1)
            vb_full = gathered[4].reshape(-1)
            bs_t = b_s[jl].T
            mix = _gmlp_fwd(P, w_s[jl], bs_t, vg_full, vb_full, b_width, name="gmlp_fwd")
            qcol = 2 * b_width // MEM_WIDTH
            extra = (vg_full, vb_full, bs_t)
        mem_o = _mem_fwd(P, qcol, kv, name=f"mem_fwd_{'a' if is_a else 'b'}")
        cat = jnp.concatenate([mix.astype(BF16), mem_o.astype(BF16)], axis=1)
        y = _mm_nn(cat, wout, out_dtype=F32, name="proj_out")
        x1, x1b = _ln_fwd(xf, y, ln1_g[i], ln1_b[i], alpha, name="ln_fwd")
        if i > 0:
            wff1, wff2 = _sibling_wait(ff_pass[0], ff_pass[1], ff_pass[2], pass_copies, (x1b,),
                                       name=f"gather_pass_wait_{tag}_ff")
        else:
            wff1, wff2 = gather_end(ff_started, (x1b,), f"{tag}_ff")
        wff2 = wff2.reshape(1, 4 * D, D)
        if i + 1 < depth:
            ff_started = ff_next
        a_pre, hid = _mm_nn(x1b, wff1, epi="relu2", name="ff1")
        f = _mm_nn(hid, wff2, out_dtype=F32, name="ff2")
        x2, _ = _ln_fwd(x1, f, ln2_g[i], ln2_b[i], alpha, name="ln_fwd")
        saved.append(dict(xf=xf, xb=xb, P=P, kv=kv, mix=mix, cat=cat, y=y, x1=x1, x1b=x1b, a_pre=a_pre, hid=hid, f=f,
                          extra=extra, w=(win, wkv, wout, wff1, wff2), qcol=qcol))
        xf = x2

    dx, sq = _loss_head(xf, target, name="loss_head")
    loss = lax.psum(sq[0, 0] * (0.5 / D), ("x", "y", "c"))

    g_big = dict(w_in_a=[None] * ((depth + 1) // 2), w_in_b=[None] * (depth // 2), w_mem_kv=[None] * depth,
                 w_out=[None] * depth, w_ff1=[None] * depth, w_ff2=[None] * depth)
    small = {k: [None] * depth for k in ("ln1_g", "ln1_b", "ln2_g", "ln2_b")}
    small_b = {k: [None] * (depth // 2) for k in ("w_s", "b_s", "vnorm_g", "vnorm_b")}
    def store(red, layer, group):
        if group == "ff":
            g_big["w_ff1"][layer], g_big["w_ff2"][layer] = red
        else:
            g_big["w_in_a" if layer % 2 == 0 else "w_in_b"][layer // 2], g_big["w_mem_kv"][layer], g_big["w_out"][layer] = red

    chips = []
    mix_pair = None
    tokens = ()
    for i in reversed(range(depth)):
        jl = i // 2
        is_a = i % 2 == 0
        s = saved[i]
        win, wkv, wout, wff1, wff2 = s["w"]
        tag = "a" if is_a else "b"
        d_f, adr2, dg2, db2 = _ln_bwd(dx, s["x1"], s["f"], ln2_g[i], alpha, tokens, name="ln_bwd")
        small["ln2_g"][i], small["ln2_b"][i] = dg2, db2
        gw_ff2 = _mm_tn(s["hid"], d_f, 1, name="grad_ff2").reshape(4, D, D)
        da = _mm_nt(d_f, wff2, epi="drelu2", extra=s["a_pre"], name="ff2_bwd")
        gw_ff1 = _mm_tn(s["x1b"], da, 4, name="grad_ff1")
        ff_pair = _scatter_pair_start([gw_ff1, gw_ff2], f"{tag}_ff")
        late = []
        if mix_pair is not None:
            above = "b" if is_a else "a"
            mix_chip = _scatter_chip_start(mix_pair, (d_f,), place, f"{above}_mix")
            chips.append((mix_chip, i + 1, "mix"))
            late.append(mix_chip[4])
        dx1 = _mm_nt(da, wff1, epi="resid", extra=adr2, after=(ff_pair[3],), name="ff1_bwd")
        ff_chip = _scatter_chip_start(ff_pair, (dx1,), place, f"{tag}_ff")
        chips.append((ff_chip, i, "ff"))
        late.append(ff_chip[4])
        d_y, adr1, dg1, db1 = _ln_bwd(dx1, s["xf"], s["y"], ln1_g[i], alpha, tuple(late), name="ln_bwd")
        small["ln1_g"][i], small["ln1_b"][i] = dg1, db1
        gw_out = _mm_tn(s["cat"], d_y, 4, name="grad_out")
        dcat = _mm_nt(d_y, wout, out_dtype=F32, name="proj_out_bwd")
        ocol = dcat.shape[1] // MEM_WIDTH - 1
        dmq, dkv = _mem_bwd(s["P"], s["qcol"], s["kv"], dcat, ocol, name=f"mem_bwd_{tag}")
        gw_kv = _mm_tn(mem_b, dkv.astype(BF16), 1, name="grad_kv").reshape(4, D // 4, 2 * MEM_WIDTH)
        if is_a:
            (lse,) = s["extra"]
            dmix = dcat[:, :a_out]
            parts = []
            for g, (_, d) in enumerate(A_PAIRS):
                Pg, g0 = group_view(s["P"], g, d)
                parts += _attn_bwd(Pg, dmix, s["mix"], lse, g0, d, a_out, name=f"attn_bwd_d{d}")
            dP = jnp.concatenate(parts + [dmq], axis=1)
        else:
            vg_full, vb_full, bs_t = s["extra"]
            dpu, dpv, dws, dbs, dvg, dvb = _gmlp_bwd(s["P"], w_s[jl], bs_t, vg_full, vb_full, dcat, b_width,
                                                     name="gmlp_bwd")
            small_b["w_s"][jl], small_b["b_s"][jl] = dws, dbs[:, :G].T
            small_b["vnorm_g"][jl], small_b["vnorm_b"][jl] = dvg, dvb
            dP = jnp.concatenate([dpu, dpv, dmq], axis=1)
        gw_in = _mm_tn(s["xb"], dP, 4, name=f"grad_in_{tag}")
        dx = _mm_nt(dP, win, epi="resid", extra=adr1, name=f"proj_in_bwd_{tag}")
        mix_pair = _scatter_pair_start([gw_in, gw_kv, gw_out], f"{tag}_mix")
        tokens = (mix_pair[3],)
    last_chip = _scatter_chip_start(mix_pair, (dx,), place, "a_mix")
    swaps, after = [], (dx,)
    for chip_started, layer, group in chips:
        name = f"{'a' if layer % 2 == 0 else 'b'}_{group}"
        swaps.append((_scatter_swap_start(chip_started, after, place, name), layer, group, name))
        after = (swaps[-1][0][3],)
    for swap_started, layer, group, name in swaps:
        store(_scatter_finish(swap_started, name), layer, group)

    def finish_last(after):
        store(_scatter_finish(_scatter_swap_start(last_chip, after, place, "a_mix"), "a_mix"), 0, "mix")

    grad_x = dx.reshape(x.shape)

    nb_layers = depth // 2
    pieces = ([jnp.stack(small_b["w_s"]).reshape(-1, 128), jnp.stack(small_b["b_s"]).reshape(-1, 128),
               jnp.stack(small_b["vnorm_g"]).reshape(-1, 128), jnp.stack(small_b["vnorm_b"]).reshape(-1, 128)]
              + [jnp.stack(small[k]).reshape(-1, 128) for k in ("ln1_g", "ln1_b", "ln2_g", "ln2_b")])
    sizes = [p.shape[0] for p in pieces]
    pad = (-sum(sizes)) % 8
    packed = jnp.concatenate(pieces + ([jnp.zeros((pad, 128), F32)] if pad else []), axis=0)
    summed = _all_reduce_small(packed, name="all_reduce_small")
    offs = [0]
    for n_ in sizes:
        offs.append(offs[-1] + n_)
    sp = [summed[offs[k]:offs[k + 1]] for k in range(len(sizes))]
    vshard = vnorm_g.shape[1]
    g_small = dict(
        w_s=sp[0].reshape(w_s.shape), b_s=sp[1].reshape(b_s.shape),
        vnorm_g=lax.dynamic_slice_in_dim(sp[2].reshape(nb_layers, -1), c_idx * vshard, vshard, axis=1),
        vnorm_b=lax.dynamic_slice_in_dim(sp[3].reshape(nb_layers, -1), c_idx * vshard, vshard, axis=1),
        ln1_g=sp[4].reshape(ln1_g.shape), ln1_b=sp[5].reshape(ln1_b.shape),
        ln2_g=sp[6].reshape(ln2_g.shape), ln2_b=sp[7].reshape(ln2_b.shape))

    names = ["w_in_a", "w_in_b", "w_s", "b_s", "vnorm_g", "vnorm_b", "w_mem_kv", "w_out", "ln1_g", "ln1_b", "w_ff1",
             "w_ff2", "ln2_g", "ln2_b"]
    ws = dict(w_in_a=w_in_a, w_in_b=w_in_b, w_s=w_s, b_s=b_s, vnorm_g=vnorm_g, vnorm_b=vnorm_b, w_mem_kv=w_mem_kv,
              w_out=w_out, ln1_g=ln1_g, ln1_b=ln1_b, w_ff1=w_ff1, w_ff2=w_ff2, ln2_g=ln2_g, ln2_b=ln2_b)
    ms = dict(w_in_a=m_w_in_a, w_in_b=m_w_in_b, w_s=m_w_s, b_s=m_b_s, vnorm_g=m_vnorm_g, vnorm_b=m_vnorm_b,
              w_mem_kv=m_w_mem_kv, w_out=m_w_out, ln1_g=m_ln1_g, ln1_b=m_ln1_b, w_ff1=m_w_ff1, w_ff2=m_w_ff2,
              ln2_g=m_ln2_g, ln2_b=m_ln2_b)
    vs = dict(w_in_a=v_w_in_a, w_in_b=v_w_in_b, w_s=v_w_s, b_s=v_b_s, vnorm_g=v_vnorm_g, vnorm_b=v_vnorm_b,
              w_mem_kv=v_w_mem_kv, w_out=v_w_out, ln1_g=v_ln1_g, ln1_b=v_ln1_b, w_ff1=v_w_ff1, w_ff2=v_w_ff2,
              ln2_g=v_ln2_g, ln2_b=v_ln2_b)
    grads, deltas, new_m, new_v = {}, {}, {}, {}
    for k in ("w_ff1", "w_ff2", "w_in_a", "w_in_b", "w_mem_kv", "w_out"):
        if k == "w_in_a":
            finish_last((deltas["w_ff2"], summed))
        g = jnp.stack(g_big[k]).reshape(ws[k].shape)
        cols = ws[k].shape[-1]
        d_, m_, v_ = _adamw(ws[k].reshape(-1, cols), g.reshape(-1, cols), ms[k].reshape(-1, cols),
                            vs[k].reshape(-1, cols), name=f"adamw_{k}")
        grads[k] = g
        deltas[k], new_m[k], new_v[k] = (t.reshape(ws[k].shape) for t in (d_, m_, v_))
    small_names = [k for k in names if k not in g_big]

    def pack(tree):
        flat = jnp.concatenate([tree[k].reshape(-1) for k in small_names])
        padn = (-flat.shape[0]) % 1024
        return jnp.pad(flat, (0, padn)).reshape(-1, 128)

    d_, m_, v_ = _adamw(pack(ws), pack(g_small), pack(ms), pack(vs), name="adamw_small")
    off = 0
    for k in small_names:
        n_ = ws[k].size
        grads[k] = g_small[k]
        deltas[k], new_m[k], new_v[k] = (t.reshape(-1)[off:off + n_].reshape(ws[k].shape) for t in (d_, m_, v_))
        off += n_

    return (loss, grad_x, *[grads[k] for k in names], *[deltas[k] for k in names], *[new_m[k] for k in names],
            *[new_v[k] for k in names])
```

```python
import functools
import math

import jax
import jax.numpy as jnp
from jax import lax
from jax.experimental import pallas as pl
from jax.experimental.pallas import tpu as pltpu

F32 = jnp.float32
BF16 = jnp.bfloat16

HEAD_DIM = 128
A_PAIRS = ((128, 1), (512, 4), (2048, 16))
BLK = 128
HEAD_BLOCK = 512
MEM_TOKENS = 256
MEM_HEADS = 4
MEM_WIDTH = MEM_HEADS * HEAD_DIM
LN_EPS = 1e-5
ADAM_LR, ADAM_B1, ADAM_B2, ADAM_EPS, ADAM_WD, ADAM_STEP = 0.001, 0.9, 0.999, 1e-08, 0.01, 10
NEG = -1e30
V7X_VMEM_LIMIT = 56 * 1024 * 1024
V7X_MATMUL_VMEM_BUDGET = 40 * 1024 * 1024
MESH = pl.DeviceIdType.MESH
HBM = pl.BlockSpec(memory_space=pltpu.HBM)


def _call(body, **kw):
    return pl.pallas_call(body, **kw)


def _params(sem):
    return pltpu.CompilerParams(dimension_semantics=sem, vmem_limit_bytes=V7X_VMEM_LIMIT)


def _tile(n, cap):
    best = 0
    for t in range(128, min(n, cap) + 1, 128):
        if n % t == 0:
            best = t
    if best == 0 or (best < 512 and n <= 2560):
        return n
    return best


def _depth(k, per_unit_bytes, fixed_bytes):
    t = _tile(k, 2048)
    while t > 512 and 2 * t * per_unit_bytes + fixed_bytes > V7X_MATMUL_VMEM_BUDGET:
        smaller = _tile(k, t // 2)
        if smaller >= t:
            break
        t = smaller
    return t


def _rows(n, cap):
    best = 8
    for t in range(8, min(n, cap) + 1, 8):
        if n % t == 0:
            best = t
    return best


def _epilogue(epi, acc, extra_ref, out_refs):
    if epi == "plain":
        out_refs[0][...] = acc.astype(out_refs[0].dtype)
    elif epi == "relu2":
        out_refs[0][...] = acc
        r = jnp.maximum(acc, 0.0)
        out_refs[1][...] = (r * r).astype(out_refs[1].dtype)
    elif epi == "drelu2":
        out_refs[0][...] = (acc * (2.0 * jnp.maximum(extra_ref[...], 0.0))).astype(out_refs[0].dtype)
    elif epi == "resid":
        out_refs[0][...] = acc + extra_ref[...]
    else:
        raise ValueError(epi)


def _mm_nn(a, w, *, epi="plain", out_dtype=BF16, extra=None, after=(), name):
    M, K = a.shape
    J, K2, Nj = w.shape
    assert K == K2
    tn = _tile(Nj, 1024)
    tm = _tile(M, 1024 if tn <= 1024 else 512)
    n_out = 2 if epi == "relu2" else 1
    has_extra = extra is not None
    out_bytes = 12 if epi == "relu2" else (8 if epi == "resid" or out_dtype == F32 else 4)
    tk = _depth(K, 2 * (tm + tn), tm * tn * (4 + out_bytes + 8 * has_extra))
    nn, nk = Nj // tn, K // tk

    def body(*refs):
        a_ref, w_ref = refs[0], refs[1]
        extra_ref = refs[2] if has_extra else None
        outs = refs[2 + has_extra + len(after): 2 + has_extra + len(after) + n_out]
        acc_ref = refs[-1]
        k = pl.program_id(3)

        @pl.when(k == 0)
        def _():
            acc_ref[...] = jnp.zeros_like(acc_ref)

        acc_ref[...] += jnp.dot(a_ref[...], w_ref[...], preferred_element_type=F32)

        @pl.when(k == nk - 1)
        def _():
            _epilogue(epi, acc_ref[...], extra_ref, outs)

    omap = lambda i, j, n, k: (i, j * nn + n)
    in_specs = [pl.BlockSpec((tm, tk), lambda i, j, n, k: (i, k)),
                pl.BlockSpec((None, tk, tn), lambda i, j, n, k: (j, k, n))]
    args = [a, w]
    if has_extra:
        in_specs.append(pl.BlockSpec((tm, tn), omap))
        args.append(extra)
    in_specs += [pl.BlockSpec(memory_space=pl.ANY)] * len(after)
    args += list(after)
    if epi == "relu2":
        out_shape = (jax.ShapeDtypeStruct((M, J * Nj), F32), jax.ShapeDtypeStruct((M, J * Nj), BF16))
        out_specs = (pl.BlockSpec((tm, tn), omap), pl.BlockSpec((tm, tn), omap))
    else:
        out_shape = jax.ShapeDtypeStruct((M, J * Nj), F32 if epi == "resid" else out_dtype)
        out_specs = pl.BlockSpec((tm, tn), omap)
    return _call(body, name=name, grid=(M // tm, J, nn, nk), in_specs=in_specs, out_specs=out_specs,
                 out_shape=out_shape, scratch_shapes=[pltpu.VMEM((tm, tn), F32)],
                 compiler_params=_params(("parallel", "parallel", "parallel", "arbitrary")))(*args)


def _mm_nt(a, w, *, epi="plain", out_dtype=BF16, extra=None, after=(), name):
    M, N = a.shape
    J, K, Nj = w.shape
    assert N == J * Nj
    tko = _tile(K, 1024)
    has_extra = extra is not None
    out_bytes = 8 if epi == "resid" or out_dtype == F32 else 4
    for tm in (_tile(M, 1024), _tile(M, 512)):
        fixed = tm * tko * (4 + out_bytes + 8 * has_extra)
        tc = _depth(Nj, 2 * (tm + tko), fixed)
        if 2 * tc * 2 * (tm + tko) + fixed <= V7X_MATMUL_VMEM_BUDGET:
            break
    nc = Nj // tc

    def body(*refs):
        a_ref, w_ref = refs[0], refs[1]
        extra_ref = refs[2] if has_extra else None
        outs = refs[2 + has_extra + len(after): 3 + has_extra + len(after)]
        acc_ref = refs[-1]
        j, c = pl.program_id(2), pl.program_id(3)

        @pl.when(jnp.logical_and(j == 0, c == 0))
        def _():
            acc_ref[...] = jnp.zeros_like(acc_ref)

        acc_ref[...] += lax.dot_general(a_ref[...], w_ref[...], (((1,), (1,)), ((), ())),
                                        preferred_element_type=F32)

        @pl.when(jnp.logical_and(j == J - 1, c == nc - 1))
        def _():
            _epilogue(epi, acc_ref[...], extra_ref, outs)

    omap = lambda i, ko, j, c: (i, ko)
    in_specs = [pl.BlockSpec((tm, tc), lambda i, ko, j, c: (i, j * nc + c)),
                pl.BlockSpec((None, tko, tc), lambda i, ko, j, c: (j, ko, c))]
    args = [a, w]
    if has_extra:
        in_specs.append(pl.BlockSpec((tm, tko), omap))
        args.append(extra)
    in_specs += [pl.BlockSpec(memory_space=pl.ANY)] * len(after)
    args += list(after)
    out_shape = jax.ShapeDtypeStruct((M, K), F32 if epi == "resid" else out_dtype)
    return _call(body, name=name, grid=(M // tm, K // tko, J, nc), in_specs=in_specs,
                 out_specs=pl.BlockSpec((tm, tko), omap), out_shape=out_shape,
                 scratch_shapes=[pltpu.VMEM((tm, tko), F32)],
                 compiler_params=_params(("parallel", "parallel", "arbitrary", "arbitrary")))(*args)


def _mm_tn(a, b, J, *, name):
    T, K = a.shape
    T2, N = b.shape
    assert T == T2 and N % J == 0
    Nj = N // J
    tn = _tile(Nj, 1024)
    tkr = _tile(K, 1024 if tn <= 1024 else 512)
    tt = _depth(T, 2 * (tkr + tn), tkr * tn * (4 + 4))
    nn, nt = Nj // tn, T // tt

    def body(a_ref, b_ref, o_ref, acc_ref):
        t = pl.program_id(3)

        @pl.when(t == 0)
        def _():
            acc_ref[...] = jnp.zeros_like(acc_ref)

        acc_ref[...] += lax.dot_general(a_ref[...], b_ref[...], (((0,), (0,)), ((), ())),
                                        preferred_element_type=F32)

        @pl.when(t == nt - 1)
        def _():
            o_ref[...] = acc_ref[...].astype(o_ref.dtype)

    return _call(body, name=name, grid=(J, K // tkr, nn, nt),
                 in_specs=[pl.BlockSpec((tt, tkr), lambda j, kr, n, t: (t, kr)),
                           pl.BlockSpec((tt, tn), lambda j, kr, n, t: (t, j * nn + n))],
                 out_specs=pl.BlockSpec((None, tkr, tn), lambda j, kr, n, t: (j, kr, n)),
                 out_shape=jax.ShapeDtypeStruct((J, K, Nj), BF16),
                 scratch_shapes=[pltpu.VMEM((tkr, tn), F32)],
                 compiler_params=_params(("parallel", "parallel", "parallel", "arbitrary")))(a, b)


def _ln_fwd(x, y, g, b, alpha, *, name):
    T, D = x.shape
    tr = _rows(T, 256)

    def body(x_ref, y_ref, g_ref, b_ref, o_ref, ob_ref):
        r = alpha * x_ref[...] + y_ref[...]
        mu = jnp.mean(r, axis=-1, keepdims=True)
        xc = r - mu
        var = jnp.mean(xc * xc, axis=-1, keepdims=True)
        o = xc * lax.rsqrt(var + LN_EPS) * g_ref[...] + b_ref[...]
        o_ref[...] = o
        ob_ref[...] = o.astype(BF16)

    row = pl.BlockSpec((tr, D), lambda i: (i, 0))
    vec = pl.BlockSpec((1, D), lambda i: (0, 0))
    return _call(body, name=name, grid=(T // tr,), in_specs=[row, row, vec, vec], out_specs=(row, row),
                 out_shape=(jax.ShapeDtypeStruct((T, D), F32), jax.ShapeDtypeStruct((T, D), BF16)),
                 compiler_params=_params(("parallel",)))(x, y, g.reshape(1, D), b.reshape(1, D))


def _ln_bwd(dout, x, y, g, alpha, after=(), *, name):
    T, D = x.shape
    tr = _rows(T, 256)
    na = len(after)

    def body(*refs):
        do_ref, x_ref, y_ref, g_ref = refs[:4]
        drb_ref, adr_ref, dg_ref, db_ref = refs[4 + na:]
        i = pl.program_id(0)
        r = alpha * x_ref[...] + y_ref[...]
        mu = jnp.mean(r, axis=-1, keepdims=True)
        xc = r - mu
        var = jnp.mean(xc * xc, axis=-1, keepdims=True)
        rstd = lax.rsqrt(var + LN_EPS)
        xhat = xc * rstd
        do = do_ref[...]
        dxh = do * g_ref[...]
        dr = rstd * (dxh - jnp.mean(dxh, axis=-1, keepdims=True)
                     - xhat * jnp.mean(dxh * xhat, axis=-1, keepdims=True))
        drb_ref[...] = dr.astype(BF16)
        adr_ref[...] = alpha * dr

        @pl.when(i == 0)
        def _():
            dg_ref[...] = jnp.zeros_like(dg_ref)
            db_ref[...] = jnp.zeros_like(db_ref)

        dg_ref[...] += jnp.sum(do * xhat, axis=0, keepdims=True)
        db_ref[...] += jnp.sum(do, axis=0, keepdims=True)

    row = pl.BlockSpec((tr, D), lambda i: (i, 0))
    vec = pl.BlockSpec((1, D), lambda i: (0, 0))
    return _call(body, name=name, grid=(T // tr,),
                 in_specs=[row, row, row, vec] + [pl.BlockSpec(memory_space=pl.ANY)] * na,
                 out_specs=(row, row, vec, vec),
                 out_shape=(jax.ShapeDtypeStruct((T, D), BF16), jax.ShapeDtypeStruct((T, D), F32),
                            jax.ShapeDtypeStruct((1, D), F32), jax.ShapeDtypeStruct((1, D), F32)),
                 compiler_params=_params(("arbitrary",)))(dout, x, y, g.reshape(1, D), *after)


def _loss_head(xf, target, *, name):
    T, D = xf.shape
    tr = _rows(T, 256)

    def body(x_ref, t_ref, dy_ref, s_ref):
        i = pl.program_id(0)
        err = x_ref[...] - t_ref[...]
        dy_ref[...] = err * (1.0 / D)

        @pl.when(i == 0)
        def _():
            s_ref[...] = jnp.zeros_like(s_ref)

        s_ref[...] += jnp.sum(jnp.sum(err * err, axis=1, keepdims=True), axis=0, keepdims=True)

    row = pl.BlockSpec((tr, D), lambda i: (i, 0))
    return _call(body, name=name, grid=(T // tr,), in_specs=[row, row],
                 out_specs=(row, pl.BlockSpec((8, 128), lambda i: (0, 0))),
                 out_shape=(jax.ShapeDtypeStruct((T, D), F32), jax.ShapeDtypeStruct((8, 128), F32)),
                 compiler_params=_params(("arbitrary",)))(xf, target)


def _adamw(w, g, m, v, *, name):
    R, C = w.shape
    tr = _rows(R, max(8, (1 << 18) // C // 8 * 8))

    def body(w_ref, g_ref, m_ref, v_ref, d_ref, nm_ref, nv_ref):
        g_ = g_ref[...]
        m_ = ADAM_B1 * m_ref[...] + (1.0 - ADAM_B1) * g_
        v_ = ADAM_B2 * v_ref[...] + (1.0 - ADAM_B2) * (g_ * g_)
        m_hat = m_ / (1.0 - ADAM_B1 ** ADAM_STEP)
        v_hat = v_ / (1.0 - ADAM_B2 ** ADAM_STEP)
        d_ref[...] = -ADAM_LR * (m_hat / (jnp.sqrt(v_hat) + ADAM_EPS) + ADAM_WD * w_ref[...])
        nm_ref[...] = m_
        nv_ref[...] = v_

    blk = pl.BlockSpec((tr, C), lambda i: (i, 0))
    sds = jax.ShapeDtypeStruct((R, C), F32)
    return _call(body, name=name, grid=(R // tr,), in_specs=[blk] * 4, out_specs=(blk,) * 3,
                 out_shape=(sds,) * 3, compiler_params=_params(("parallel",)))(w, g, m, v)


def _dot_nt(a, b):
    return lax.dot_general(a, b, (((1,), (1,)), ((), ())), preferred_element_type=F32)


def _dot_tn(a, b):
    return lax.dot_general(a, b, (((0,), (0,)), ((), ())), preferred_element_type=F32)


def _band_masks():
    qi = lax.broadcasted_iota(jnp.int32, (BLK, BLK), 0)
    kj = lax.broadcasted_iota(jnp.int32, (BLK, BLK), 1)
    return qi >= kj, kj >= qi


def _attn_fwd(P, grp, d, a_out, *, name):
    T, C = P.shape
    L = T // d
    nb = L // BLK
    nhh = a_out // HEAD_BLOCK
    cb = C // HEAD_BLOCK
    q0 = grp * 3 * nhh
    scale = HEAD_DIM ** -0.5

    def body(q_ref, kc_ref, kp_ref, vc_ref, vp_ref, o_ref, l_ref):
        b = pl.program_id(1)
        mask_c, mask_p = _band_masks()
        mask_p = jnp.logical_and(mask_p, b > 0)
        for h in range(HEAD_BLOCK // HEAD_DIM):
            hs = slice(h * HEAD_DIM, (h + 1) * HEAD_DIM)
            q = q_ref[:, hs]
            s_c = jnp.where(mask_c, _dot_nt(q, kc_ref[:, hs]) * scale, NEG)
            s_p = jnp.where(mask_p, _dot_nt(q, kp_ref[:, hs]) * scale, NEG)
            m = jnp.maximum(jnp.max(s_c, axis=1, keepdims=True), jnp.max(s_p, axis=1, keepdims=True))
            p_c = jnp.exp(s_c - m)
            p_p = jnp.exp(s_p - m)
            l = jnp.sum(p_c, axis=1, keepdims=True) + jnp.sum(p_p, axis=1, keepdims=True)
            o = (jnp.dot(p_c.astype(BF16), vc_ref[:, hs], preferred_element_type=F32)
                 + jnp.dot(p_p.astype(BF16), vp_ref[:, hs], preferred_element_type=F32))
            o_ref[:, hs] = o / l
            l_ref[:, hs] = jnp.broadcast_to(m + jnp.log(l), (BLK, HEAD_DIM))

    def cur(part):
        return pl.BlockSpec((BLK, HEAD_BLOCK), lambda r, b, hh: (b, r * cb + q0 + part * nhh + hh))

    def prev(part):
        return pl.BlockSpec((BLK, HEAD_BLOCK), lambda r, b, hh: (jnp.maximum(b - 1, 0), r * cb + q0 + part * nhh + hh))

    out = pl.BlockSpec((BLK, HEAD_BLOCK), lambda r, b, hh: (b, r * nhh + hh))
    Pv = P.reshape(L, d * C)
    o, lse = _call(body, name=name, grid=(d, nb, nhh), in_specs=[cur(0), cur(1), prev(1), cur(2), prev(2)],
                   out_specs=(out, out),
                   out_shape=(jax.ShapeDtypeStruct((L, d * a_out), F32),) * 2,
                   compiler_params=_params(("parallel", "parallel", "parallel")))(Pv, Pv, Pv, Pv, Pv)
    return o.reshape(T, a_out), lse.reshape(T, a_out)


def _attn_combine(os_, lses, *, name):
    T, W = os_[0].shape
    tr = _rows(T, 256)
    n = len(os_)

    def body(*refs):
        o_refs, l_refs = refs[:n], refs[n:2 * n]
        mix_ref, lse_ref = refs[2 * n], refs[2 * n + 1]
        ls = [r[...] for r in l_refs]
        m = functools.reduce(jnp.maximum, ls)
        es = [jnp.exp(l - m) for l in ls]
        tot = functools.reduce(lambda a, b: a + b, es)
        mix = functools.reduce(lambda a, b: a + b, [(e / tot) * o[...] for e, o in zip(es, o_refs)])
        mix_ref[...] = mix
        lse_ref[...] = m + jnp.log(tot)

    blk = pl.BlockSpec((tr, W), lambda i: (i, 0))
    sds = jax.ShapeDtypeStruct((T, W), F32)
    return _call(body, name=name, grid=(T // tr,), in_specs=[blk] * (2 * n), out_specs=(blk, blk),
                 out_shape=(sds, sds), compiler_params=_params(("parallel",)))(*os_, *lses)


def _attn_bwd(P, dO, O, LSE, grp, d, a_out, *, name):
    T, C = P.shape
    L = T // d
    nb = L // BLK
    nhh = a_out // HEAD_BLOCK
    cb = C // HEAD_BLOCK
    q0 = grp * 3 * nhh
    scale = HEAD_DIM ** -0.5

    def body(q_ref, qn_ref, kc_ref, kp_ref, vc_ref, vp_ref, do_ref, don_ref, o_ref, on_ref, l_ref, ln_ref,
             dq_ref, dk_ref, dv_ref):
        b = pl.program_id(1)
        mask_c, mask_prev = _band_masks()
        mask_p = jnp.logical_and(mask_prev, b > 0)
        mask_n = jnp.logical_and(mask_prev, b < nb - 1)
        for h in range(HEAD_BLOCK // HEAD_DIM):
            hs = slice(h * HEAD_DIM, (h + 1) * HEAD_DIM)
            q, qn, kc, kp, vc, vp = (r[:, hs] for r in (q_ref, qn_ref, kc_ref, kp_ref, vc_ref, vp_ref))
            do, don = do_ref[:, hs], don_ref[:, hs]
            lse, lse_n = l_ref[:, hs], ln_ref[:, hs]
            delta = jnp.sum(do * o_ref[:, hs], axis=1, keepdims=True)
            delta_n = jnp.sum(don * on_ref[:, hs], axis=1, keepdims=True)
            dob, donb = do.astype(BF16), don.astype(BF16)
            p_c = jnp.exp(jnp.where(mask_c, _dot_nt(q, kc) * scale, NEG) - lse)
            p_p = jnp.exp(jnp.where(mask_p, _dot_nt(q, kp) * scale, NEG) - lse)
            p_n = jnp.exp(jnp.where(mask_n, _dot_nt(qn, kc) * scale, NEG) - lse_n)
            ds_c = (p_c * (_dot_nt(dob, vc) - delta) * scale).astype(BF16)
            ds_p = (p_p * (_dot_nt(dob, vp) - delta) * scale).astype(BF16)
            ds_n = (p_n * (_dot_nt(donb, vc) - delta_n) * scale).astype(BF16)
            dq = jnp.dot(ds_c, kc, preferred_element_type=F32) + jnp.dot(ds_p, kp, preferred_element_type=F32)
            dk = _dot_tn(ds_c, q) + _dot_tn(ds_n, qn)
            dv = _dot_tn(p_c.astype(BF16), dob) + _dot_tn(p_n.astype(BF16), donb)
            dq_ref[:, hs] = dq.astype(BF16)
            dk_ref[:, hs] = dk.astype(BF16)
            dv_ref[:, hs] = dv.astype(BF16)

    def pspec(part, shift):
        def imap(r, b, hh):
            return (jnp.clip(b + shift, 0, nb - 1), r * cb + q0 + part * nhh + hh)
        return pl.BlockSpec((BLK, HEAD_BLOCK), imap)

    def aspec(shift, width=a_out):
        per = width // HEAD_BLOCK
        return pl.BlockSpec((BLK, HEAD_BLOCK), lambda r, b, hh: (jnp.clip(b + shift, 0, nb - 1), r * per + hh))

    Pv = P.reshape(L, d * C)
    w_do = dO.shape[1]
    dOv = dO.reshape(L, d * w_do)
    Ov, Lv = (t.reshape(L, d * a_out) for t in (O, LSE))
    sds = jax.ShapeDtypeStruct((L, d * a_out), BF16)
    dq, dk, dv = _call(
        body, name=name, grid=(d, nb, nhh),
        in_specs=[pspec(0, 0), pspec(0, 1), pspec(1, 0), pspec(1, -1), pspec(2, 0), pspec(2, -1),
                  aspec(0, w_do), aspec(1, w_do), aspec(0), aspec(1), aspec(0), aspec(1)],
        out_specs=(aspec(0),) * 3, out_shape=(sds,) * 3,
        compiler_params=_params(("parallel", "parallel", "parallel")))(Pv, Pv, Pv, Pv, Pv, Pv, dOv, dOv, Ov, Ov, Lv, Lv)
    return [t.reshape(T, a_out) for t in (dq, dk, dv)]


def _mem_softmax(q, k, scale):
    s = _dot_nt(q, k) * scale
    e = jnp.exp(s - jnp.max(s, axis=1, keepdims=True))
    return e / jnp.sum(e, axis=1, keepdims=True)


def _mem_fwd(P, qcol, kv, mix, *, name):
    T, W = mix.shape
    tq = _rows(T, 512)
    scale = HEAD_DIM ** -0.5

    def body(q_ref, kv_ref, mix_ref, o_ref):
        o_ref[:, :W] = mix_ref[...].astype(BF16)
        for h in range(MEM_HEADS):
            hs = slice(h * HEAD_DIM, (h + 1) * HEAD_DIM)
            vs = slice(MEM_WIDTH + h * HEAD_DIM, MEM_WIDTH + (h + 1) * HEAD_DIM)
            p = _mem_softmax(q_ref[:, hs].astype(BF16), kv_ref[:, hs], scale)
            o = jnp.dot(p.astype(BF16), kv_ref[:, vs], preferred_element_type=F32)
            o_ref[:, W + h * HEAD_DIM:W + (h + 1) * HEAD_DIM] = o.astype(BF16)

    return _call(body, name=name, grid=(T // tq,),
                 in_specs=[pl.BlockSpec((tq, MEM_WIDTH), lambda i: (i, qcol)),
                           pl.BlockSpec((MEM_TOKENS, 2 * MEM_WIDTH), lambda i: (0, 0)),
                           pl.BlockSpec((tq, W), lambda i: (i, 0))],
                 out_specs=pl.BlockSpec((tq, W + MEM_WIDTH), lambda i: (i, 0)),
                 out_shape=jax.ShapeDtypeStruct((T, W + MEM_WIDTH), BF16),
                 compiler_params=_params(("parallel",)))(P, kv, mix)


def _mem_bwd(P, qcol, kv, dcat, ocol, *, name):
    T = P.shape[0]
    tq = _rows(T, 512)
    scale = HEAD_DIM ** -0.5

    def body(q_ref, kv_ref, do_ref, dq_ref, dkv_ref):
        i = pl.program_id(0)

        @pl.when(i == 0)
        def _():
            dkv_ref[...] = jnp.zeros_like(dkv_ref)

        for h in range(MEM_HEADS):
            hs = slice(h * HEAD_DIM, (h + 1) * HEAD_DIM)
            vs = slice(MEM_WIDTH + h * HEAD_DIM, MEM_WIDTH + (h + 1) * HEAD_DIM)
            q = q_ref[:, hs].astype(BF16)
            k, v = kv_ref[:, hs], kv_ref[:, vs]
            do = do_ref[:, hs].astype(BF16)
            p = _mem_softmax(q, k, scale)
            dp = _dot_nt(do, v)
            ds = (p * (dp - jnp.sum(p * dp, axis=1, keepdims=True)) * scale).astype(BF16)
            dq_ref[:, hs] = jnp.dot(ds, k, preferred_element_type=F32).astype(BF16)
            dkv_ref[:, hs] += _dot_tn(ds, q)
            dkv_ref[:, vs] += _dot_tn(p.astype(BF16), do)

    return _call(body, name=name, grid=(T // tq,),
                 in_specs=[pl.BlockSpec((tq, MEM_WIDTH), lambda i: (i, qcol)),
                           pl.BlockSpec((MEM_TOKENS, 2 * MEM_WIDTH), lambda i: (0, 0)),
                           pl.BlockSpec((tq, MEM_WIDTH), lambda i: (i, ocol))],
                 out_specs=(pl.BlockSpec((tq, MEM_WIDTH), lambda i: (i, 0)),
                            pl.BlockSpec((MEM_TOKENS, 2 * MEM_WIDTH), lambda i: (0, 0))),
                 out_shape=(jax.ShapeDtypeStruct((T, MEM_WIDTH), BF16),
                            jax.ShapeDtypeStruct((MEM_TOKENS, 2 * MEM_WIDTH), F32)),
                 compiler_params=_params(("arbitrary",)))(P, kv, dcat)


_SQRT_HALF = 0.7071067811865476
_INV_SQRT_2PI = 0.3989422804014327


def _gelu(x):
    return 0.5 * x * (1.0 + lax.erf(x * _SQRT_HALF))


def _gelu_grad(x):
    return 0.5 * (1.0 + lax.erf(x * _SQRT_HALF)) + x * (_INV_SQRT_2PI * jnp.exp(-0.5 * x * x))


def _tril():
    t = lax.broadcasted_iota(jnp.int32, (BLK, BLK), 0)
    s = lax.broadcasted_iota(jnp.int32, (BLK, BLK), 1)
    return t >= s


def _gmlp_fwd(P, w_s, bs_t, vg, vb, width, *, name):
    T = P.shape[0]
    G = width // HEAD_DIM
    tb = _rows(T, 512)

    def body(pu_ref, pv_ref, ws_ref, bs_ref, vg_ref, vb_ref, o_ref):
        u = _gelu(pu_ref[...])
        v = _gelu(pv_ref[...])
        mu = jnp.mean(v, axis=-1, keepdims=True)
        vc = v - mu
        var = jnp.mean(vc * vc, axis=-1, keepdims=True)
        vn = (vc * lax.rsqrt(var + LN_EPS) * vg_ref[...] + vb_ref[...]).astype(BF16)
        tril = _tril()
        for g in range(G):
            gs = slice(g * HEAD_DIM, (g + 1) * HEAD_DIM)
            ws = jnp.where(tril, ws_ref[g], 0.0).astype(BF16)
            bias = bs_ref[:, g:g + 1]
            for c in range(tb // BLK):
                cs = slice(c * BLK, (c + 1) * BLK)
                sg = jnp.dot(ws, vn[cs, gs], preferred_element_type=F32) + bias
                o_ref[cs, gs] = u[cs, gs] * sg

    blk = lambda col: pl.BlockSpec((tb, width), lambda i: (i, col))
    full = lambda shape: pl.BlockSpec(shape, lambda i: (0,) * len(shape))
    return _call(body, name=name, grid=(T // tb,),
                 in_specs=[blk(0), blk(1), full((G, BLK, BLK)), full((BLK, G)), full((1, width)), full((1, width))],
                 out_specs=blk(0), out_shape=jax.ShapeDtypeStruct((T, width), F32),
                 compiler_params=_params(("parallel",)))(P, P, w_s, bs_t, vg.reshape(1, width), vb.reshape(1, width))


def _gmlp_bwd(P, w_s, bs_t, vg, vb, dcat, width, *, name):
    T = P.shape[0]
    G = width // HEAD_DIM
    tb = _rows(T, 512)

    def body(pu_ref, pv_ref, ws_ref, bs_ref, vg_ref, vb_ref, dm_ref, dpu_ref, dpv_ref, dws_ref, dbs_ref, dvg_ref,
             dvb_ref, dvn_ref):
        i = pl.program_id(0)

        @pl.when(i == 0)
        def _():
            dws_ref[...] = jnp.zeros_like(dws_ref)
            dbs_ref[...] = jnp.zeros_like(dbs_ref)
            dvg_ref[...] = jnp.zeros_like(dvg_ref)
            dvb_ref[...] = jnp.zeros_like(dvb_ref)

        pu, pv = pu_ref[...], pv_ref[...]
        u = _gelu(pu)
        v = _gelu(pv)
        mu = jnp.mean(v, axis=-1, keepdims=True)
        vc = v - mu
        var = jnp.mean(vc * vc, axis=-1, keepdims=True)
        rstd = lax.rsqrt(var + LN_EPS)
        xhat = vc * rstd
        vn = (xhat * vg_ref[...] + vb_ref[...]).astype(BF16)
        dm = dm_ref[...]
        tril = _tril()
        lane = lax.broadcasted_iota(jnp.int32, (BLK, BLK), 1)
        dbs = jnp.zeros((BLK, BLK), F32)
        for g in range(G):
            gs = slice(g * HEAD_DIM, (g + 1) * HEAD_DIM)
            ws = jnp.where(tril, ws_ref[g], 0.0).astype(BF16)
            bias = bs_ref[:, g:g + 1]
            dws = jnp.zeros((BLK, BLK), F32)
            rs = jnp.zeros((BLK, 1), F32)
            for c in range(tb // BLK):
                cs = slice(c * BLK, (c + 1) * BLK)
                vn_cg = vn[cs, gs]
                sg = jnp.dot(ws, vn_cg, preferred_element_type=F32) + bias
                dm_cg = dm[cs, gs]
                dpu_ref[cs, gs] = (dm_cg * sg * _gelu_grad(pu[cs, gs])).astype(BF16)
                dsg = dm_cg * u[cs, gs]
                dsgb = dsg.astype(BF16)
                dvn_ref[cs, gs] = _dot_tn(ws, dsgb)
                dws = dws + _dot_nt(dsgb, vn_cg)
                rs = rs + jnp.sum(dsg, axis=1, keepdims=True)
            dws_ref[g] += jnp.where(tril, dws, 0.0)
            dbs = dbs + jnp.where(lane == g, rs, 0.0)
        dbs_ref[...] += dbs
        dvn = dvn_ref[...]
        dxh = dvn * vg_ref[...]
        dv = rstd * (dxh - jnp.mean(dxh, axis=-1, keepdims=True)
                     - xhat * jnp.mean(dxh * xhat, axis=-1, keepdims=True))
        dpv_ref[...] = (dv * _gelu_grad(pv)).astype(BF16)
        dvg_ref[...] += jnp.sum(dvn * xhat, axis=0, keepdims=True)
        dvb_ref[...] += jnp.sum(dvn, axis=0, keepdims=True)

    blk = lambda col: pl.BlockSpec((tb, width), lambda i: (i, col))
    full = lambda shape: pl.BlockSpec(shape, lambda i: (0,) * len(shape))
    return _call(body, name=name, grid=(T // tb,),
                 in_specs=[blk(0), blk(1), full((G, BLK, BLK)), full((BLK, G)), full((1, width)), full((1, width)),
                           blk(0)],
                 out_specs=(blk(0), blk(0), full((G, BLK, BLK)), full((BLK, BLK)), full((1, width)), full((1, width))),
                 out_shape=(jax.ShapeDtypeStruct((T, width), BF16), jax.ShapeDtypeStruct((T, width), BF16),
                            jax.ShapeDtypeStruct((G, BLK, BLK), F32), jax.ShapeDtypeStruct((BLK, BLK), F32),
                            jax.ShapeDtypeStruct((1, width), F32), jax.ShapeDtypeStruct((1, width), F32)),
                 scratch_shapes=[pltpu.VMEM((tb, width), F32)],
                 compiler_params=_params(("arbitrary",)))(P, P, w_s, bs_t, vg.reshape(1, width), vb.reshape(1, width), dcat)


def _place():
    x, y, c = lax.axis_index("x"), lax.axis_index("y"), lax.axis_index("c")
    chips = [(1 - x, y), (x, 1 - y), (1 - x, 1 - y)]
    return x, y, c, 2 * x + y, chips, [2 * px + py for px, py in chips]


def _half(ref, c, rows):
    return ref.at[pl.ds(c * (rows // 2), rows // 2)]


def _place_shard(w, chip, dtype, *, name):
    R, C = w.shape
    tr = _rows(R, max(8, (1 << 19) // C // 8 * 8)) if R % 8 == 0 else R

    def body(chip_ref, w_ref, o_ref):
        o_ref[...] = w_ref[...].astype(dtype)

    grid_spec = pltpu.PrefetchScalarGridSpec(
        num_scalar_prefetch=1, grid=(R // tr,), in_specs=[pl.BlockSpec((tr, C), lambda i, s: (i, 0))],
        out_specs=pl.BlockSpec((None, tr, C), lambda i, s: (s[0], i, 0)))
    return _call(body, name=name, grid_spec=grid_spec, out_shape=jax.ShapeDtypeStruct((4, R, C), dtype),
                 compiler_params=_params(("parallel",)))(chip, w)


SEM = pl.BlockSpec(memory_space=pltpu.SEMAPHORE)
EFFECT = pltpu.SideEffectType.DATAFLOW_SIDE_EFFECTING


def _hbm(a):
    return pltpu.with_memory_space_constraint(a, pltpu.HBM)


def _sibling_start(arrays, copies, ncopies, after, *, name):
    n, na = len(arrays), len(after)

    def body(*refs):
        a = refs[:n]
        send, recv = refs[n + na], refs[n + na + 1]
        token = refs[2 * n + na + 2]
        x, y, c, _, _, _ = _place()
        for q, (src, dst) in enumerate(copies(a, c)):
            pltpu.make_async_remote_copy(src_ref=src, dst_ref=dst, send_sem=send.at[q], recv_sem=recv.at[q],
                                         device_id=(x, y, 1 - c), device_id_type=MESH).start()
        token[...] = jnp.zeros_like(token)

    outs = _call(body, name=name, in_specs=[HBM] * n + [pl.BlockSpec(memory_space=pl.ANY)] * na,
                 out_specs=(SEM, SEM, *([HBM] * n), pl.BlockSpec(memory_space=pltpu.VMEM)),
                 out_shape=(pltpu.SemaphoreType.DMA((ncopies,)), pltpu.SemaphoreType.DMA((ncopies,)),
                            *[pltpu.HBM(a.shape, a.dtype) for a in arrays], jax.ShapeDtypeStruct((8, 128), F32)),
                 input_output_aliases={t: 2 + t for t in range(n)},
                 compiler_params=pltpu.CompilerParams(has_side_effects=EFFECT))(*[_hbm(a) for a in arrays], *after)
    return outs[0], outs[1], list(outs[2:2 + n]), outs[2 + n]


def _sibling_wait(send, recv, arrays, copies, after, *, name):
    n, na = len(arrays), len(after)

    def body(*refs):
        a = refs[:n]
        send, recv = refs[n], refs[n + 1]
        x, y, c, _, _, _ = _place()
        for q, ((src, dst), (_, landed)) in enumerate(zip(copies(a, c), copies(a, 1 - c))):
            pltpu.make_async_remote_copy(src_ref=src, dst_ref=dst, send_sem=send.at[q], recv_sem=recv.at[q],
                                         device_id=(x, y, 1 - c), device_id_type=MESH).wait_send()
            pltpu.make_async_remote_copy(src_ref=src, dst_ref=landed, send_sem=send.at[q], recv_sem=recv.at[q],
                                         device_id=(x, y, 1 - c), device_id_type=MESH).wait_recv()

    outs = _call(body, name=name, in_specs=[HBM] * n + [SEM, SEM] + [pl.BlockSpec(memory_space=pl.ANY)] * na,
                 out_specs=[HBM] * n, out_shape=[pltpu.HBM(a.shape, a.dtype) for a in arrays],
                 input_output_aliases={t: t for t in range(n)},
                 compiler_params=pltpu.CompilerParams(has_side_effects=EFFECT))(*arrays, send, recv, *after)
    return list(outs)


def _pair_copies(n):
    def copies(a, core):
        out = []
        for t in range(n):
            h = a[t].shape[1] // 2
            out.append((a[t].at[:, pl.ds((1 - core) * h, h)], a[n + t]))
        return out
    return copies


def _swap_copies(a, core):
    out = []
    for ref in a:
        h = ref.shape[0] // 2
        out.append((ref.at[pl.ds(core * h, h)],) * 2)
    return out


def _gather_windows(shapes):
    split = [s[1] % 16 == 0 for s in shapes]

    def window(ref, t, chip_idx, core):
        w = ref.at[chip_idx]
        return _half(w, core, shapes[t][1]) if split[t] else w

    return split, window


def _gather_start(bufs, after, *, name):
    n = len(bufs)
    split, window = _gather_windows([b.shape for b in bufs])

    na = len(after)

    def body(*refs):
        b = refs[:n]
        send, recv = refs[n + na], refs[n + na + 1]
        token = refs[2 * n + na + 2]
        x, y, c, j, chips, pj = _place()
        for t in range(n):
            for k in range(3):
                pltpu.make_async_remote_copy(src_ref=window(b[t], t, j, c), dst_ref=window(b[t], t, j, c),
                                             send_sem=send.at[3 * t + k], recv_sem=recv.at[3 * t + k],
                                             device_id=(*chips[k], c), device_id_type=MESH).start()
        token[...] = jnp.zeros_like(token)

    outs = _call(body, name=name, in_specs=[HBM] * n + [pl.BlockSpec(memory_space=pl.ANY)] * na,
                 out_specs=(SEM, SEM, *([HBM] * n), pl.BlockSpec(memory_space=pltpu.VMEM)),
                 out_shape=(pltpu.SemaphoreType.DMA((3 * n,)), pltpu.SemaphoreType.DMA((3 * n,)),
                            *[pltpu.HBM(b.shape, b.dtype) for b in bufs], jax.ShapeDtypeStruct((8, 128), F32)),
                 input_output_aliases={t: 2 + t for t in range(n)},
                 compiler_params=pltpu.CompilerParams(has_side_effects=EFFECT))(*[_hbm(b) for b in bufs], *after)
    return outs[0], outs[1], list(outs[2:2 + n]), outs[2 + n]


def _gather_wait(send, recv, bufs, after, *, name):
    n = len(bufs)
    split, window = _gather_windows([b.shape for b in bufs])

    def body(*refs):
        b = refs[:n]
        send, recv = refs[n], refs[n + 1]
        x, y, c, j, chips, pj = _place()
        for t in range(n):
            for k in range(3):
                out = pltpu.make_async_remote_copy(src_ref=window(b[t], t, j, c), dst_ref=window(b[t], t, j, c),
                                                   send_sem=send.at[3 * t + k], recv_sem=recv.at[3 * t + k],
                                                   device_id=(*chips[k], c), device_id_type=MESH)
                out.wait_send()
                back = pltpu.make_async_remote_copy(src_ref=window(b[t], t, pj[k], c), dst_ref=window(b[t], t, pj[k], c),
                                                    send_sem=send.at[3 * t + k], recv_sem=recv.at[3 * t + k],
                                                    device_id=(*chips[k], c), device_id_type=MESH)
                back.wait_recv()

    outs = _call(body, name=name, in_specs=[HBM] * n + [SEM, SEM] + [pl.BlockSpec(memory_space=pl.ANY)] * len(after),
                 out_specs=[HBM] * n, out_shape=[pltpu.HBM(b.shape, b.dtype) for b in bufs],
                 input_output_aliases={t: t for t in range(n)},
                 compiler_params=pltpu.CompilerParams(has_side_effects=EFFECT))(*bufs, send, recv, *after)
    return list(outs)


def _pass_copies(shapes):
    split, window = _gather_windows(shapes)

    def copies(a, core):
        _, _, _, _, _, pj = _place()
        return [(window(a[t], t, pj[k], core),) * 2 for t in range(len(shapes)) if split[t] for k in range(3)]

    return copies, 3 * sum(split)


def _gather_pass(bufs, *, name):
    n = len(bufs)
    split, window = _gather_windows([b.shape for b in bufs])
    idx = [t for t in range(n) if split[t]]

    def body(*refs):
        b = refs[n:2 * n]
        send, recv = refs[2 * n:]
        x, y, c, j, chips, pj = _place()

        def d2d(u, t, k, core):
            w = window(b[t], t, pj[k], core)
            return pltpu.make_async_remote_copy(src_ref=w, dst_ref=w, send_sem=send.at[3 * u + k],
                                                recv_sem=recv.at[3 * u + k], device_id=(x, y, 1 - c),
                                                device_id_type=MESH)

        sent = [d2d(u, t, k, c) for u, t in enumerate(idx) for k in range(3)]
        for cp in sent:
            cp.start()
        for u, t in enumerate(idx):
            for k in range(3):
                d2d(u, t, k, 1 - c).wait_recv()
        for cp in sent:
            cp.wait_send()

    return _call(body, name=name, in_specs=[HBM] * n, out_specs=[HBM] * n,
                 out_shape=[jax.ShapeDtypeStruct(b.shape, b.dtype) for b in bufs],
                 input_output_aliases={t: t for t in range(n)},
                 scratch_shapes=[pltpu.SemaphoreType.DMA((3 * len(idx),))] * 2)(*bufs)


def _chip_start(pairs, *, name):
    n = len(pairs)
    lands = [lax.empty((3,) + p.shape[1:], p.dtype) for p in pairs]

    def body(*refs):
        s, r = refs[:n], refs[n:2 * n]
        send, recv = refs[2 * n], refs[2 * n + 1]
        token = refs[4 * n + 2]
        x, y, c, j, chips, pj = _place()
        for t in range(n):
            for k in range(3):
                pltpu.make_async_remote_copy(src_ref=s[t].at[pj[k]], dst_ref=r[t].at[k], send_sem=send.at[3 * t + k],
                                             recv_sem=recv.at[3 * t + k], device_id=(*chips[k], c),
                                             device_id_type=MESH).start()
        token[...] = jnp.zeros_like(token)

    outs = _call(body, name=name, in_specs=[HBM] * (2 * n),
                 out_specs=(SEM, SEM, *([HBM] * (2 * n)), pl.BlockSpec(memory_space=pltpu.VMEM)),
                 out_shape=(pltpu.SemaphoreType.DMA((3 * n,)), pltpu.SemaphoreType.DMA((3 * n,)),
                            *[pltpu.HBM(a.shape, a.dtype) for a in list(pairs) + lands],
                            jax.ShapeDtypeStruct((8, 128), F32)),
                 input_output_aliases={t: 2 + t for t in range(2 * n)},
                 compiler_params=pltpu.CompilerParams(has_side_effects=EFFECT))(*[_hbm(a) for a in list(pairs) + lands])
    return outs[0], outs[1], list(outs[2:2 + n]), list(outs[2 + n:2 + 2 * n]), outs[2 + 2 * n]


def _chip_wait(send, recv, pairs, lands, after, *, name):
    n = len(pairs)

    def body(*refs):
        s, r = refs[:n], refs[n:2 * n]
        send, recv = refs[2 * n], refs[2 * n + 1]
        x, y, c, j, chips, pj = _place()
        for t in range(n):
            for k in range(3):
                cp = pltpu.make_async_remote_copy(src_ref=s[t].at[pj[k]], dst_ref=r[t].at[k], send_sem=send.at[3 * t + k],
                                                  recv_sem=recv.at[3 * t + k], device_id=(*chips[k], c),
                                                  device_id_type=MESH)
                cp.wait_send()
                cp.wait_recv()

    outs = _call(body, name=name,
                 in_specs=[HBM] * (2 * n) + [SEM, SEM] + [pl.BlockSpec(memory_space=pl.ANY)] * len(after),
                 out_specs=[HBM] * (2 * n), out_shape=[pltpu.HBM(a.shape, a.dtype) for a in list(pairs) + list(lands)],
                 input_output_aliases={t: t for t in range(2 * n)},
                 compiler_params=pltpu.CompilerParams(has_side_effects=EFFECT))(*pairs, *lands, send, recv, *after)
    return list(outs[:n]), list(outs[n:])


def _all_reduce_small(packed, *, name):
    R, C = packed.shape

    def body(p_ref, o_ref, slots, send, recv, lsem):
        x, y, c = lax.axis_index("x"), lax.axis_index("y"), lax.axis_index("c")
        me = 4 * x + 2 * y + c
        lc = pltpu.make_async_copy(p_ref, slots.at[me], lsem.at[0])
        lc.start()
        copies = []
        for rel in range(1, 8):
            fx, fy, fc = (rel >> 2) & 1, (rel >> 1) & 1, rel & 1
            to = (1 - x if fx else x, 1 - y if fy else y, 1 - c if fc else c)
            cp = pltpu.make_async_remote_copy(src_ref=p_ref, dst_ref=slots.at[me], send_sem=send.at[rel - 1],
                                              recv_sem=recv.at[rel - 1], device_id=to, device_id_type=MESH)
            cp.start()
            copies.append((cp, 4 * to[0] + 2 * to[1] + to[2]))
        for rel, (cp, frm) in enumerate(copies):
            cp.wait_send()
            pltpu.make_async_remote_copy(src_ref=p_ref, dst_ref=slots.at[frm], send_sem=send.at[rel],
                                         recv_sem=recv.at[rel], device_id=(x, y, c), device_id_type=MESH).wait_recv()
        lc.wait()
        acc = slots[0]
        for dev in range(1, 8):
            acc = acc + slots[dev]
        o_ref[...] = acc

    return _call(body, name=name, in_specs=[pl.BlockSpec(memory_space=pltpu.VMEM)],
                 out_specs=pl.BlockSpec(memory_space=pltpu.VMEM), out_shape=jax.ShapeDtypeStruct((R, C), F32),
                 scratch_shapes=[pltpu.VMEM((8, R, C), F32), pltpu.SemaphoreType.DMA((7,)),
                                 pltpu.SemaphoreType.DMA((7,)), pltpu.SemaphoreType.DMA((1,))],
                 compiler_params=pltpu.CompilerParams(vmem_limit_bytes=V7X_VMEM_LIMIT))(packed)


def _pair_sum(grad, theirs, core, *, name):
    J, H, C = theirs.shape
    tr = _rows(H, max(8, (1 << 19) // C // 8 * 8))

    def body(core_ref, a_ref, b_ref, o_ref):
        o_ref[...] = (a_ref[...].astype(F32) + b_ref[...].astype(F32)).astype(BF16)

    blk = pl.BlockSpec((None, tr, C), lambda j, i, s: (j, i, 0))
    mine = pl.BlockSpec((None, None, tr, C), lambda j, i, s: (j, s[0], i, 0))
    grid_spec = pltpu.PrefetchScalarGridSpec(num_scalar_prefetch=1, grid=(J, H // tr), in_specs=[mine, blk],
                                             out_specs=blk)
    return _call(body, name=name, grid_spec=grid_spec, out_shape=jax.ShapeDtypeStruct((J, H, C), BF16),
                 compiler_params=_params(("parallel", "parallel")))(core, grad.reshape(J, 2, H, C), theirs)


def _chip_sum(pairs, slots, place, *, name):
    _, H, C = slots.shape
    tr = _rows(H, max(8, (1 << 19) // C // 8 * 8))
    nr = H // tr

    def body(place_ref, s0, s1, s2, s3, o_ref):
        o_ref[...] = ((s0[...].astype(F32) + s1[...].astype(F32)) + s2[...].astype(F32)) + s3[...].astype(F32)

    def slot(k):
        return pl.BlockSpec((None, tr, C), lambda i, s: (k, i, 0))

    own = pl.BlockSpec((None, tr, C), lambda i, s: (s[0], i, 0))
    grid_spec = pltpu.PrefetchScalarGridSpec(
        num_scalar_prefetch=1, grid=(nr,), in_specs=[own, slot(0), slot(1), slot(2)],
        out_specs=pl.BlockSpec((tr, C), lambda i, s: (s[1] * nr + i, 0)))
    return _call(body, name=name, grid_spec=grid_spec, out_shape=jax.ShapeDtypeStruct((2 * H, C), F32),
                 compiler_params=_params(("parallel",)))(place, pairs, slots, slots, slots)


def _scatter_pair_start(grads, tag):
    lands = [lax.empty((4, g.shape[1] // 2, g.shape[2]), g.dtype) for g in grads]
    return _sibling_start(list(grads) + lands, _pair_copies(len(grads)), len(grads), (), name=f"rs_pair_start_{tag}")


def _scatter_chip_start(pair_started, after, place, tag):
    send, recv, arrays, _ = pair_started
    n = len(arrays) // 2
    arrays = _sibling_wait(send, recv, arrays, _pair_copies(n), after, name=f"rs_pair_wait_{tag}")
    pairs = [_pair_sum(g, t, place[1:], name=f"rs_pair_sum_{tag}_{i}")
             for i, (g, t) in enumerate(zip(arrays[:n], arrays[n:]))]
    return _chip_start(pairs, name=f"rs_chip_start_{tag}")


def _scatter_swap_start(chip_started, after, place, tag):
    send, recv, pairs, lands, _ = chip_started
    pairs, slots = _chip_wait(send, recv, pairs, lands, after, name=f"rs_chip_wait_{tag}")
    fulls = [_chip_sum(p, s, place, name=f"rs_chip_sum_{tag}_{i}") for i, (p, s) in enumerate(zip(pairs, slots))]
    return _sibling_start(fulls, _swap_copies, len(fulls), (), name=f"rs_swap_start_{tag}")


def _scatter_finish(swap_started, tag):
    send, recv, fulls, _ = swap_started
    return _sibling_wait(send, recv, fulls, _swap_copies, (), name=f"rs_swap_wait_{tag}")


def kernel(x, mem, w_in_a, w_in_b, w_s, b_s, vnorm_g, vnorm_b, w_mem_kv, w_out, ln1_g, ln1_b, w_ff1, w_ff2, ln2_g, ln2_b, loss_target, m_w_in_a, m_w_in_b, m_w_s, m_b_s, m_vnorm_g, m_vnorm_b, m_w_mem_kv, m_w_out, m_ln1_g, m_ln1_b, m_w_ff1, m_w_ff2, m_ln2_g, m_ln2_b, v_w_in_a, v_w_in_b, v_w_s, v_b_s, v_vnorm_g, v_vnorm_b, v_w_mem_kv, v_w_out, v_ln1_g, v_ln1_b, v_w_ff1, v_w_ff2, v_ln2_g, v_ln2_b):
    T, D = x.shape[1], x.shape[2]
    depth = w_ff1.shape[0]
    alpha = (2.0 * depth) ** 0.25
    a_out = (D // 256) * HEAD_DIM
    a_cols = len(A_PAIRS) * 3 * a_out
    b_width = (D // 256) * HEAD_DIM
    G = b_width // HEAD_DIM
    assert a_out % HEAD_BLOCK == 0 and T % (BLK * A_PAIRS[-1][1]) == 0

    xf = x.reshape(T, D)
    mem_b = mem.reshape(MEM_TOKENS, D).astype(BF16)
    target = loss_target.reshape(T, D)
    c_idx = lax.axis_index("x") * 2 + lax.axis_index("y")
    place = jnp.stack([c_idx, lax.axis_index("c")]).astype(jnp.int32)

    def gather_place(i, group):
        jl, tag = i // 2, "a" if i % 2 == 0 else "b"
        if group == "mix":
            mats = [(w_in_a if i % 2 == 0 else w_in_b)[jl], w_mem_kv[i], w_out[i]]
        else:
            mats = [w_ff1[i], w_ff2[i]]
        bufs = [_place_shard(w, place[:1], BF16, name=f"place_{tag}_{group}_{n_}") for n_, w in enumerate(mats)]
        if group == "mix" and i % 2:
            bufs += [_place_shard(v_[jl].reshape(1, -1), place[:1], F32, name="place_vnorm") for v_ in (vnorm_g, vnorm_b)]
        return bufs

    placed = {}

    def gather_begin(i, group, after):
        tag = "a" if i % 2 == 0 else "b"
        bufs = placed.pop((i, group), None) or gather_place(i, group)
        return _gather_start(bufs, after, name=f"gather_start_{tag}_{group}")

    def gather_end(started, after, name):
        send, recv, bufs, _ = started
        bufs = _gather_wait(send, recv, bufs, after, name=f"gather_wait_{name}")
        return _gather_pass(bufs, name=f"gather_pass_{name}")

    def group_view(P, g, d):
        return (P, g) if d == 1 else (P[:, g * 3 * a_out:(g + 1) * 3 * a_out], 0)

    saved = []
    xb = xf.astype(BF16)
    mix_started = gather_begin(0, "mix", ())
    for key in ((0, "ff"), (1, "mix"), (1, "ff")):
        if key[0] < depth:
            placed[key] = gather_place(*key)
    early = [b for bufs in placed.values() for b in bufs]
    ff_started = None
    for i in range(depth):
        jl = i // 2
        is_a = i % 2 == 0
        tag = "a" if is_a else "b"
        gathered = gather_end(mix_started, (xf, *early) if i == 0 else (xf,), f"{tag}_mix")
        win, wkv, wout = gathered[:3]
        wkv = wkv.reshape(1, D, 2 * MEM_WIDTH)
        tokens = []
        if i > 0:
            send, recv, ff_bufs, _ = ff_started
            ff_bufs = _gather_wait(send, recv, ff_bufs, (xf,), name=f"gather_wait_{tag}_ff")
            pass_copies, ncopies = _pass_copies([b.shape for b in ff_bufs])
            ff_pass = _sibling_start(ff_bufs, pass_copies, ncopies, (), name=f"gather_pass_start_{tag}_ff")
            tokens.append(ff_pass[3])
        token = None
        if i == 0:
            ff_started = gather_begin(0, "ff", (win,))
            token = ff_started[3]
        if i + 1 < depth:
            mix_started = gather_begin(i + 1, "mix", (win,) if token is None else (token,))
            ff_next = gather_begin(i + 1, "ff", (mix_started[3],))
            token = ff_next[3]
        if token is not None:
            tokens.append(token)
        P = _mm_nn(xb, win, out_dtype=BF16 if is_a else F32, after=tuple(tokens), name=f"proj_in_{'a' if is_a else 'b'}")
        kv = _mm_nn(mem_b, wkv, name="proj_kv")
        if is_a:
            outs = [_attn_fwd(*group_view(P, g, d), d, a_out, name=f"attn_fwd_d{d}") for g, (_, d) in enumerate(A_PAIRS)]
            mix, lse = _attn_combine([o for o, _ in outs], [l for _, l in outs], name="attn_combine")
            qcol = a_cols // MEM_WIDTH
            extra = (lse,)
        else:
            vg_full = gathered[3].reshape(-1)
            vb_full = gathered[4].reshape(-1)
            bs_t = b_s[jl].T
            mix = _gmlp_fwd(P, w_s[jl], bs_t, vg_full, vb_full, b_width, name="gmlp_fwd")
            qcol = 2 * b_width // MEM_WIDTH
            extra = (vg_full, vb_full, bs_t)
        cat = _mem_fwd(P, qcol, kv, mix, name=f"mem_fwd_{'a' if is_a else 'b'}")
        y = _mm_nn(cat, wout, out_dtype=F32, name="proj_out")
        x1, x1b = _ln_fwd(xf, y, ln1_g[i], ln1_b[i], alpha, name="ln_fwd")
        if i > 0:
            wff1, wff2 = _sibling_wait(ff_pass[0], ff_pass[1], ff_pass[2], pass_copies, (x1b,),
                                       name=f"gather_pass_wait_{tag}_ff")
        else:
            wff1, wff2 = gather_end(ff_started, (x1b,), f"{tag}_ff")
        wff2 = wff2.reshape(1, 4 * D, D)
        if i + 1 < depth:
            ff_started = ff_next
        a_pre, hid = _mm_nn(x1b, wff1, epi="relu2", name="ff1")
        f = _mm_nn(hid, wff2, out_dtype=F32, name="ff2")
        x2, x2b = _ln_fwd(x1, f, ln2_g[i], ln2_b[i], alpha, name="ln_fwd")
        saved.append(dict(xf=xf, xb=xb, P=P, kv=kv, mix=mix, cat=cat, y=y, x1=x1, x1b=x1b, a_pre=a_pre, hid=hid, f=f,
                          extra=extra, w=(win, wkv, wout, wff1, wff2), qcol=qcol))
        xf, xb = x2, x2b

    dx, sq = _loss_head(xf, target, name="loss_head")
    loss = lax.psum(sq[0, 0] * (0.5 / D), ("x", "y", "c"))

    g_big = dict(w_in_a=[None] * ((depth + 1) // 2), w_in_b=[None] * (depth // 2), w_mem_kv=[None] * depth,
                 w_out=[None] * depth, w_ff1=[None] * depth, w_ff2=[None] * depth)
    small = {k: [None] * depth for k in ("ln1_g", "ln1_b", "ln2_g", "ln2_b")}
    small_b = {k: [None] * (depth // 2) for k in ("w_s", "b_s", "vnorm_g", "vnorm_b")}
    def store(red, layer, group):
        if group == "ff":
            g_big["w_ff1"][layer], g_big["w_ff2"][layer] = red
        else:
            g_big["w_in_a" if layer % 2 == 0 else "w_in_b"][layer // 2], g_big["w_mem_kv"][layer], g_big["w_out"][layer] = red

    chips = []
    mix_pair = None
    tokens = ()
    for i in reversed(range(depth)):
        jl = i // 2
        is_a = i % 2 == 0
        s = saved[i]
        win, wkv, wout, wff1, wff2 = s["w"]
        tag = "a" if is_a else "b"
        d_f, adr2, dg2, db2 = _ln_bwd(dx, s["x1"], s["f"], ln2_g[i], alpha, tokens, name="ln_bwd")
        small["ln2_g"][i], small["ln2_b"][i] = dg2, db2
        gw_ff2 = _mm_tn(s["hid"], d_f, 1, name="grad_ff2").reshape(4, D, D)
        da = _mm_nt(d_f, wff2, epi="drelu2", extra=s["a_pre"], name="ff2_bwd")
        gw_ff1 = _mm_tn(s["x1b"], da, 4, name="grad_ff1")
        ff_pair = _scatter_pair_start([gw_ff1, gw_ff2], f"{tag}_ff")
        late = []
        if mix_pair is not None:
            above = "b" if is_a else "a"
            mix_chip = _scatter_chip_start(mix_pair, (d_f,), place, f"{above}_mix")
            chips.append((mix_chip, i + 1, "mix"))
            late.append(mix_chip[4])
        dx1 = _mm_nt(da, wff1, epi="resid", extra=adr2, after=(ff_pair[3],), name="ff1_bwd")
        ff_chip = _scatter_chip_start(ff_pair, (dx1,), place, f"{tag}_ff")
        chips.append((ff_chip, i, "ff"))
        late.append(ff_chip[4])
        d_y, adr1, dg1, db1 = _ln_bwd(dx1, s["xf"], s["y"], ln1_g[i], alpha, tuple(late), name="ln_bwd")
        small["ln1_g"][i], small["ln1_b"][i] = dg1, db1
        gw_out = _mm_tn(s["cat"], d_y, 4, name="grad_out")
        dcat = _mm_nt(d_y, wout, out_dtype=F32, name="proj_out_bwd")
        ocol = dcat.shape[1] // MEM_WIDTH - 1
        dmq, dkv = _mem_bwd(s["P"], s["qcol"], s["kv"], dcat, ocol, name=f"mem_bwd_{tag}")
        gw_kv = _mm_tn(mem_b, dkv.astype(BF16), 1, name="grad_kv").reshape(4, D // 4, 2 * MEM_WIDTH)
        if is_a:
            (lse,) = s["extra"]
            dmix = dcat[:, :a_out]
            parts = []
            for g, (_, d) in enumerate(A_PAIRS):
                Pg, g0 = group_view(s["P"], g, d)
                parts += _attn_bwd(Pg, dcat if d == 1 else dmix, s["mix"], lse, g0, d, a_out, name=f"attn_bwd_d{d}")
            dP = jnp.concatenate(parts + [dmq], axis=1)
        else:
            vg_full, vb_full, bs_t = s["extra"]
            dpu, dpv, dws, dbs, dvg, dvb = _gmlp_bwd(s["P"], w_s[jl], bs_t, vg_full, vb_full, dcat, b_width,
                                                     name="gmlp_bwd")
            small_b["w_s"][jl], small_b["b_s"][jl] = dws, dbs[:, :G].T
            small_b["vnorm_g"][jl], small_b["vnorm_b"][jl] = dvg, dvb
            dP = jnp.concatenate([dpu, dpv, dmq], axis=1)
        gw_in = _mm_tn(s["xb"], dP, 4, name=f"grad_in_{tag}")
        dx = _mm_nt(dP, win, epi="resid", extra=adr1, name=f"proj_in_bwd_{tag}")
        mix_pair = _scatter_pair_start([gw_in, gw_kv, gw_out], f"{tag}_mix")
        tokens = (mix_pair[3],)
    last_chip = _scatter_chip_start(mix_pair, (dx,), place, "a_mix")
    swaps, after = [], (dx,)
    for chip_started, layer, group in chips:
        name = f"{'a' if layer % 2 == 0 else 'b'}_{group}"
        swaps.append((_scatter_swap_start(chip_started, after, place, name), layer, group, name))
        after = (swaps[-1][0][3],)
    for swap_started, layer, group, name in swaps:
        store(_scatter_finish(swap_started, name), layer, group)

    def finish_last(after):
        store(_scatter_finish(_scatter_swap_start(last_chip, after, place, "a_mix"), "a_mix"), 0, "mix")

    grad_x = dx.reshape(x.shape)

    nb_layers = depth // 2
    pieces = ([jnp.stack(small_b["w_s"]).reshape(-1, 128), jnp.stack(small_b["b_s"]).reshape(-1, 128),
               jnp.stack(small_b["vnorm_g"]).reshape(-1, 128), jnp.stack(small_b["vnorm_b"]).reshape(-1, 128)]
              + [jnp.stack(small[k]).reshape(-1, 128) for k in ("ln1_g", "ln1_b", "ln2_g", "ln2_b")])
    sizes = [p.shape[0] for p in pieces]
    pad = (-sum(sizes)) % 8
    packed = jnp.concatenate(pieces + ([jnp.zeros((pad, 128), F32)] if pad else []), axis=0)
    summed = _all_reduce_small(packed, name="all_reduce_small")
    offs = [0]
    for n_ in sizes:
        offs.append(offs[-1] + n_)
    sp = [summed[offs[k]:offs[k + 1]] for k in range(len(sizes))]
    vshard = vnorm_g.shape[1]
    g_small = dict(
        w_s=sp[0].reshape(w_s.shape), b_s=sp[1].reshape(b_s.shape),
        vnorm_g=lax.dynamic_slice_in_dim(sp[2].reshape(nb_layers, -1), c_idx * vshard, vshard, axis=1),
        vnorm_b=lax.dynamic_slice_in_dim(sp[3].reshape(nb_layers, -1), c_idx * vshard, vshard, axis=1),
        ln1_g=sp[4].reshape(ln1_g.shape), ln1_b=sp[5].reshape(ln1_b.shape),
        ln2_g=sp[6].reshape(ln2_g.shape), ln2_b=sp[7].reshape(ln2_b.shape))

    names = ["w_in_a", "w_in_b", "w_s", "b_s", "vnorm_g", "vnorm_b", "w_mem_kv", "w_out", "ln1_g", "ln1_b", "w_ff1",
             "w_ff2", "ln2_g", "ln2_b"]
    ws = dict(w_in_a=w_in_a, w_in_b=w_in_b, w_s=w_s, b_s=b_s, vnorm_g=vnorm_g, vnorm_b=vnorm_b, w_mem_kv=w_mem_kv,
              w_out=w_out, ln1_g=ln1_g, ln1_b=ln1_b, w_ff1=w_ff1, w_ff2=w_ff2, ln2_g=ln2_g, ln2_b=ln2_b)
    ms = dict(w_in_a=m_w_in_a, w_in_b=m_w_in_b, w_s=m_w_s, b_s=m_b_s, vnorm_g=m_vnorm_g, vnorm_b=m_vnorm_b,
              w_mem_kv=m_w_mem_kv, w_out=m_w_out, ln1_g=m_ln1_g, ln1_b=m_ln1_b, w_ff1=m_w_ff1, w_ff2=m_w_ff2,
              ln2_g=m_ln2_g, ln2_b=m_ln2_b)
    vs = dict(w_in_a=v_w_in_a, w_in_b=v_w_in_b, w_s=v_w_s, b_s=v_b_s, vnorm_g=v_vnorm_g, vnorm_b=v_vnorm_b,
              w_mem_kv=v_w_mem_kv, w_out=v_w_out, ln1_g=v_ln1_g, ln1_b=v_ln1_b, w_ff1=v_w_ff1, w_ff2=v_w_ff2,
              ln2_g=v_ln2_g, ln2_b=v_ln2_b)
    grads, deltas, new_m, new_v = {}, {}, {}, {}
    for k in ("w_ff1", "w_ff2", "w_in_a", "w_in_b", "w_mem_kv", "w_out"):
        if k == "w_in_a":
            finish_last((deltas["w_ff2"], summed))
        g = jnp.stack(g_big[k]).reshape(ws[k].shape)
        cols = ws[k].shape[-1]
        d_, m_, v_ = _adamw(ws[k].reshape(-1, cols), g.reshape(-1, cols), ms[k].reshape(-1, cols),
                            vs[k].reshape(-1, cols), name=f"adamw_{k}")
        grads[k] = g
        deltas[k], new_m[k], new_v[k] = (t.reshape(ws[k].shape) for t in (d_, m_, v_))
    small_names = [k for k in names if k not in g_big]

    def pack(tree):
        flat = jnp.concatenate([tree[k].reshape(-1) for k in small_names])
        padn = (-flat.shape[0]) % 1024
        return jnp.pad(flat, (0, padn)).reshape(-1, 128)

    d_, m_, v_ = _adamw(pack(ws), pack(g_small), pack(ms), pack(vs), name="adamw_small")
    off = 0
    for k in small_names:
        n_ = ws[k].size
        grads[k] = g_small[k]
        deltas[k], new_m[k], new_v[k] = (t.reshape(-1)[off:off + n_].reshape(ws[k].shape) for t in (d_, m_, v_))
        off += n_

    return (loss, grad_x, *[grads[k] for k in names], *[deltas[k] for k in names], *[new_m[k] for k in names],
            *[new_v[k] for k in names])
```

```python
import functools
import math

import jax
import jax.numpy as jnp
from jax import lax
from jax.experimental import pallas as pl
from jax.experimental.pallas import tpu as pltpu

F32 = jnp.float32
BF16 = jnp.bfloat16

HEAD_DIM = 128
A_PAIRS = ((128, 1), (512, 4), (2048, 16))
BLK = 128
HEAD_BLOCK = 512
MEM_TOKENS = 256
MEM_HEADS = 4
MEM_WIDTH = MEM_HEADS * HEAD_DIM
LN_EPS = 1e-5
ADAM_LR, ADAM_B1, ADAM_B2, ADAM_EPS, ADAM_WD, ADAM_STEP = 0.001, 0.9, 0.999, 1e-08, 0.01, 10
NEG = -1e30
V7X_VMEM_LIMIT = 56 * 1024 * 1024
V7X_MATMUL_VMEM_BUDGET = 40 * 1024 * 1024
MESH = pl.DeviceIdType.MESH
HBM = pl.BlockSpec(memory_space=pltpu.HBM)


def _call(body, **kw):
    return pl.pallas_call(body, **kw)


def _params(sem):
    return pltpu.CompilerParams(dimension_semantics=sem, vmem_limit_bytes=V7X_VMEM_LIMIT)


def _tile(n, cap):
    best = 0
    for t in range(128, min(n, cap) + 1, 128):
        if n % t == 0:
            best = t
    if best == 0 or (best < 512 and n <= 2560):
        return n
    return best


def _depth(k, per_unit_bytes, fixed_bytes):
    t = _tile(k, 2048)
    while t > 512 and 2 * t * per_unit_bytes + fixed_bytes > V7X_MATMUL_VMEM_BUDGET:
        smaller = _tile(k, t // 2)
        if smaller >= t:
            break
        t = smaller
    return t


def _rows(n, cap):
    best = 8
    for t in range(8, min(n, cap) + 1, 8):
        if n % t == 0:
            best = t
    return best


def _epilogue(epi, acc, extra_ref, out_refs):
    if epi == "plain":
        out_refs[0][...] = acc.astype(out_refs[0].dtype)
    elif epi == "relu2":
        out_refs[0][...] = acc
        r = jnp.maximum(acc, 0.0)
        out_refs[1][...] = (r * r).astype(out_refs[1].dtype)
    elif epi == "drelu2":
        out_refs[0][...] = (acc * (2.0 * jnp.maximum(extra_ref[...], 0.0))).astype(out_refs[0].dtype)
    elif epi == "resid":
        out_refs[0][...] = acc + extra_ref[...]
    else:
        raise ValueError(epi)


def _mm_nn(a, w, *, epi="plain", out_dtype=BF16, extra=None, after=(), name):
    M, K = a.shape
    J, K2, Nj = w.shape
    assert K == K2
    tn = _tile(Nj, 1024)
    tm = _tile(M, 1024 if tn <= 1024 else 512)
    n_out = 2 if epi == "relu2" else 1
    has_extra = extra is not None
    out_bytes = 12 if epi == "relu2" else (8 if epi == "resid" or out_dtype == F32 else 4)
    tk = _depth(K, 2 * (tm + tn), tm * tn * (4 + out_bytes + 8 * has_extra))
    nn, nk = Nj // tn, K // tk

    def body(*refs):
        a_ref, w_ref = refs[0], refs[1]
        extra_ref = refs[2] if has_extra else None
        outs = refs[2 + has_extra + len(after): 2 + has_extra + len(after) + n_out]
        acc_ref = refs[-1]
        k = pl.program_id(3)

        @pl.when(k == 0)
        def _():
            acc_ref[...] = jnp.zeros_like(acc_ref)

        acc_ref[...] += jnp.dot(a_ref[...], w_ref[...], preferred_element_type=F32)

        @pl.when(k == nk - 1)
        def _():
            _epilogue(epi, acc_ref[...], extra_ref, outs)

    omap = lambda i, j, n, k: (i, j * nn + n)
    in_specs = [pl.BlockSpec((tm, tk), lambda i, j, n, k: (i, k)),
                pl.BlockSpec((None, tk, tn), lambda i, j, n, k: (j, k, n))]
    args = [a, w]
    if has_extra:
        in_specs.append(pl.BlockSpec((tm, tn), omap))
        args.append(extra)
    in_specs += [pl.BlockSpec(memory_space=pl.ANY)] * len(after)
    args += list(after)
    if epi == "relu2":
        out_shape = (jax.ShapeDtypeStruct((M, J * Nj), F32), jax.ShapeDtypeStruct((M, J * Nj), BF16))
        out_specs = (pl.BlockSpec((tm, tn), omap), pl.BlockSpec((tm, tn), omap))
    else:
        out_shape = jax.ShapeDtypeStruct((M, J * Nj), F32 if epi == "resid" else out_dtype)
        out_specs = pl.BlockSpec((tm, tn), omap)
    return _call(body, name=name, grid=(M // tm, J, nn, nk), in_specs=in_specs, out_specs=out_specs,
                 out_shape=out_shape, scratch_shapes=[pltpu.VMEM((tm, tn), F32)],
                 compiler_params=_params(("parallel", "parallel", "parallel", "arbitrary")))(*args)


def _mm_nt(a, w, *, epi="plain", out_dtype=BF16, extra=None, after=(), name):
    M, N = a.shape
    J, K, Nj = w.shape
    assert N == J * Nj
    tko = _tile(K, 1024)
    has_extra = extra is not None
    out_bytes = 8 if epi == "resid" or out_dtype == F32 else 4
    for tm in (_tile(M, 1024), _tile(M, 512)):
        fixed = tm * tko * (4 + out_bytes + 8 * has_extra)
        tc = _depth(Nj, 2 * (tm + tko), fixed)
        if 2 * tc * 2 * (tm + tko) + fixed <= V7X_MATMUL_VMEM_BUDGET:
            break
    nc = Nj // tc

    def body(*refs):
        a_ref, w_ref = refs[0], refs[1]
        extra_ref = refs[2] if has_extra else None
        outs = refs[2 + has_extra + len(after): 3 + has_extra + len(after)]
        acc_ref = refs[-1]
        j, c = pl.program_id(2), pl.program_id(3)

        @pl.when(jnp.logical_and(j == 0, c == 0))
        def _():
            acc_ref[...] = jnp.zeros_like(acc_ref)

        acc_ref[...] += lax.dot_general(a_ref[...], w_ref[...], (((1,), (1,)), ((), ())),
                                        preferred_element_type=F32)

        @pl.when(jnp.logical_and(j == J - 1, c == nc - 1))
        def _():
            _epilogue(epi, acc_ref[...], extra_ref, outs)

    omap = lambda i, ko, j, c: (i, ko)
    in_specs = [pl.BlockSpec((tm, tc), lambda i, ko, j, c: (i, j * nc + c)),
                pl.BlockSpec((None, tko, tc), lambda i, ko, j, c: (j, ko, c))]
    args = [a, w]
    if has_extra:
        in_specs.append(pl.BlockSpec((tm, tko), omap))
        args.append(extra)
    in_specs += [pl.BlockSpec(memory_space=pl.ANY)] * len(after)
    args += list(after)
    out_shape = jax.ShapeDtypeStruct((M, K), F32 if epi == "resid" else out_dtype)
    return _call(body, name=name, grid=(M // tm, K // tko, J, nc), in_specs=in_specs,
                 out_specs=pl.BlockSpec((tm, tko), omap), out_shape=out_shape,
                 scratch_shapes=[pltpu.VMEM((tm, tko), F32)],
                 compiler_params=_params(("parallel", "parallel", "arbitrary", "arbitrary")))(*args)


def _mm_tn(a, b, J, *, name):
    T, K = a.shape
    T2, N = b.shape
    assert T == T2 and N % J == 0
    Nj = N // J
    tn = _tile(Nj, 1024)
    tkr = _tile(K, 1024 if tn <= 1024 else 512)
    tt = _depth(T, 2 * (tkr + tn), tkr * tn * (4 + 4))
    nn, nt = Nj // tn, T // tt

    def body(a_ref, b_ref, o_ref, acc_ref):
        t = pl.program_id(3)

        @pl.when(t == 0)
        def _():
            acc_ref[...] = jnp.zeros_like(acc_ref)

        acc_ref[...] += lax.dot_general(a_ref[...], b_ref[...], (((0,), (0,)), ((), ())),
                                        preferred_element_type=F32)

        @pl.when(t == nt - 1)
        def _():
            o_ref[...] = acc_ref[...].astype(o_ref.dtype)

    return _call(body, name=name, grid=(J, K // tkr, nn, nt),
                 in_specs=[pl.BlockSpec((tt, tkr), lambda j, kr, n, t: (t, kr)),
                           pl.BlockSpec((tt, tn), lambda j, kr, n, t: (t, j * nn + n))],
                 out_specs=pl.BlockSpec((None, tkr, tn), lambda j, kr, n, t: (j, kr, n)),
                 out_shape=jax.ShapeDtypeStruct((J, K, Nj), BF16),
                 scratch_shapes=[pltpu.VMEM((tkr, tn), F32)],
                 compiler_params=_params(("parallel", "parallel", "parallel", "arbitrary")))(a, b)


def _ln_fwd(x, y, g, b, alpha, *, name):
    T, D = x.shape
    tr = _rows(T, 512)

    def body(x_ref, y_ref, g_ref, b_ref, o_ref, ob_ref):
        r = alpha * x_ref[...] + y_ref[...]
        mu = jnp.mean(r, axis=-1, keepdims=True)
        xc = r - mu
        var = jnp.mean(xc * xc, axis=-1, keepdims=True)
        o = xc * lax.rsqrt(var + LN_EPS) * g_ref[...] + b_ref[...]
        o_ref[...] = o
        ob_ref[...] = o.astype(BF16)

    row = pl.BlockSpec((tr, D), lambda i: (i, 0))
    vec = pl.BlockSpec((1, D), lambda i: (0, 0))
    return _call(body, name=name, grid=(T // tr,), in_specs=[row, row, vec, vec], out_specs=(row, row),
                 out_shape=(jax.ShapeDtypeStruct((T, D), F32), jax.ShapeDtypeStruct((T, D), BF16)),
                 compiler_params=_params(("parallel",)))(x, y, g.reshape(1, D), b.reshape(1, D))


def _ln_bwd(dout, x, y, g, alpha, after=(), *, name):
    T, D = x.shape
    tr = _rows(T, 512)
    na = len(after)

    def body(*refs):
        do_ref, x_ref, y_ref, g_ref = refs[:4]
        drb_ref, adr_ref, dg_ref, db_ref = refs[4 + na:]
        i = pl.program_id(0)
        r = alpha * x_ref[...] + y_ref[...]
        mu = jnp.mean(r, axis=-1, keepdims=True)
        xc = r - mu
        var = jnp.mean(xc * xc, axis=-1, keepdims=True)
        rstd = lax.rsqrt(var + LN_EPS)
        xhat = xc * rstd
        do = do_ref[...]
        dxh = do * g_ref[...]
        dr = rstd * (dxh - jnp.mean(dxh, axis=-1, keepdims=True)
                     - xhat * jnp.mean(dxh * xhat, axis=-1, keepdims=True))
        drb_ref[...] = dr.astype(BF16)
        adr_ref[...] = alpha * dr

        @pl.when(i == 0)
        def _():
            dg_ref[...] = jnp.zeros_like(dg_ref)
            db_ref[...] = jnp.zeros_like(db_ref)

        dg_ref[...] += jnp.sum(do * xhat, axis=0, keepdims=True)
        db_ref[...] += jnp.sum(do, axis=0, keepdims=True)

    row = pl.BlockSpec((tr, D), lambda i: (i, 0))
    vec = pl.BlockSpec((1, D), lambda i: (0, 0))
    return _call(body, name=name, grid=(T // tr,),
                 in_specs=[row, row, row, vec] + [pl.BlockSpec(memory_space=pl.ANY)] * na,
                 out_specs=(row, row, vec, vec),
                 out_shape=(jax.ShapeDtypeStruct((T, D), BF16), jax.ShapeDtypeStruct((T, D), F32),
                            jax.ShapeDtypeStruct((1, D), F32), jax.ShapeDtypeStruct((1, D), F32)),
                 compiler_params=_params(("arbitrary",)))(dout, x, y, g.reshape(1, D), *after)


def _loss_head(xf, target, *, name):
    T, D = xf.shape
    tr = _rows(T, 256)

    def body(x_ref, t_ref, dy_ref, s_ref):
        i = pl.program_id(0)
        err = x_ref[...] - t_ref[...]
        dy_ref[...] = err * (1.0 / D)

        @pl.when(i == 0)
        def _():
            s_ref[...] = jnp.zeros_like(s_ref)

        s_ref[...] += jnp.sum(jnp.sum(err * err, axis=1, keepdims=True), axis=0, keepdims=True)

    row = pl.BlockSpec((tr, D), lambda i: (i, 0))
    return _call(body, name=name, grid=(T // tr,), in_specs=[row, row],
                 out_specs=(row, pl.BlockSpec((8, 128), lambda i: (0, 0))),
                 out_shape=(jax.ShapeDtypeStruct((T, D), F32), jax.ShapeDtypeStruct((8, 128), F32)),
                 compiler_params=_params(("arbitrary",)))(xf, target)


def _adamw(w, g, m, v, *, name):
    R, C = w.shape
    tr = _rows(R, max(8, (1 << 18) // C // 8 * 8))

    def body(w_ref, g_ref, m_ref, v_ref, d_ref, nm_ref, nv_ref):
        g_ = g_ref[...]
        m_ = ADAM_B1 * m_ref[...] + (1.0 - ADAM_B1) * g_
        v_ = ADAM_B2 * v_ref[...] + (1.0 - ADAM_B2) * (g_ * g_)
        m_hat = m_ / (1.0 - ADAM_B1 ** ADAM_STEP)
        v_hat = v_ / (1.0 - ADAM_B2 ** ADAM_STEP)
        d_ref[...] = -ADAM_LR * (m_hat / (jnp.sqrt(v_hat) + ADAM_EPS) + ADAM_WD * w_ref[...])
        nm_ref[...] = m_
        nv_ref[...] = v_

    blk = pl.BlockSpec((tr, C), lambda i: (i, 0))
    sds = jax.ShapeDtypeStruct((R, C), F32)
    return _call(body, name=name, grid=(R // tr,), in_specs=[blk] * 4, out_specs=(blk,) * 3,
                 out_shape=(sds,) * 3, compiler_params=_params(("parallel",)))(w, g, m, v)


def _dot_nt(a, b):
    return lax.dot_general(a, b, (((1,), (1,)), ((), ())), preferred_element_type=F32)


def _dot_tn(a, b):
    return lax.dot_general(a, b, (((0,), (0,)), ((), ())), preferred_element_type=F32)


def _band_masks():
    qi = lax.broadcasted_iota(jnp.int32, (BLK, BLK), 0)
    kj = lax.broadcasted_iota(jnp.int32, (BLK, BLK), 1)
    return qi >= kj, kj >= qi


def _blocks_per_step(nb):
    return 2 if nb % 2 == 0 else 1


def _attn_fwd(P, grp, d, a_out, *, name):
    T, C = P.shape
    L = T // d
    nb = L // BLK
    nhh = a_out // HEAD_BLOCK
    cb = C // HEAD_BLOCK
    q0 = grp * 3 * nhh
    scale = HEAD_DIM ** -0.5
    tb = _blocks_per_step(nb)

    def body(q_ref, kc_ref, kp_ref, vc_ref, vp_ref, o_ref, l_ref):
        b = pl.program_id(1)
        mask_c, mask_prev = _band_masks()
        for sub in range(tb):
            rs = slice(sub * BLK, (sub + 1) * BLK)
            ps = slice((sub - 1) * BLK, sub * BLK)
            mask_p = jnp.logical_and(mask_prev, b > 0) if sub == 0 else mask_prev
            for h in range(HEAD_BLOCK // HEAD_DIM):
                hs = slice(h * HEAD_DIM, (h + 1) * HEAD_DIM)
                q = q_ref[rs, hs]
                kp, vp = (kp_ref[:, hs], vp_ref[:, hs]) if sub == 0 else (kc_ref[ps, hs], vc_ref[ps, hs])
                s_c = jnp.where(mask_c, _dot_nt(q, kc_ref[rs, hs]) * scale, NEG)
                s_p = jnp.where(mask_p, _dot_nt(q, kp) * scale, NEG)
                m = jnp.maximum(jnp.max(s_c, axis=1, keepdims=True), jnp.max(s_p, axis=1, keepdims=True))
                p_c = jnp.exp(s_c - m)
                p_p = jnp.exp(s_p - m)
                l = jnp.sum(p_c, axis=1, keepdims=True) + jnp.sum(p_p, axis=1, keepdims=True)
                o = (jnp.dot(p_c.astype(BF16), vc_ref[rs, hs], preferred_element_type=F32)
                     + jnp.dot(p_p.astype(BF16), vp, preferred_element_type=F32))
                o_ref[rs, hs] = o / l
                l_ref[rs, hs] = jnp.broadcast_to(m + jnp.log(l), (BLK, HEAD_DIM))

    def cur(part):
        return pl.BlockSpec((tb * BLK, HEAD_BLOCK), lambda r, b, hh: (b, r * cb + q0 + part * nhh + hh))

    def prev(part):
        return pl.BlockSpec((BLK, HEAD_BLOCK),
                            lambda r, b, hh: (jnp.maximum(b * tb - 1, 0), r * cb + q0 + part * nhh + hh))

    out = pl.BlockSpec((tb * BLK, HEAD_BLOCK), lambda r, b, hh: (b, r * nhh + hh))
    Pv = P.reshape(L, d * C)
    o, lse = _call(body, name=name, grid=(d, nb // tb, nhh), in_specs=[cur(0), cur(1), prev(1), cur(2), prev(2)],
                   out_specs=(out, out),
                   out_shape=(jax.ShapeDtypeStruct((L, d * a_out), F32),) * 2,
                   compiler_params=_params(("parallel", "parallel", "parallel")))(Pv, Pv, Pv, Pv, Pv)
    return o.reshape(T, a_out), lse.reshape(T, a_out)


def _attn_combine(os_, lses, *, name):
    T, W = os_[0].shape
    tr = _rows(T, 256)
    n = len(os_)

    def body(*refs):
        o_refs, l_refs = refs[:n], refs[n:2 * n]
        mix_ref, lse_ref = refs[2 * n], refs[2 * n + 1]
        ls = [r[...] for r in l_refs]
        m = functools.reduce(jnp.maximum, ls)
        es = [jnp.exp(l - m) for l in ls]
        tot = functools.reduce(lambda a, b: a + b, es)
        mix = functools.reduce(lambda a, b: a + b, [(e / tot) * o[...] for e, o in zip(es, o_refs)])
        mix_ref[...] = mix
        lse_ref[...] = m + jnp.log(tot)

    blk = pl.BlockSpec((tr, W), lambda i: (i, 0))
    sds = jax.ShapeDtypeStruct((T, W), F32)
    return _call(body, name=name, grid=(T // tr,), in_specs=[blk] * (2 * n), out_specs=(blk, blk),
                 out_shape=(sds, sds), compiler_params=_params(("parallel",)))(*os_, *lses)


def _attn_bwd(P, dO, O, LSE, grp, d, a_out, *, name):
    T, C = P.shape
    L = T // d
    nb = L // BLK
    nhh = a_out // HEAD_BLOCK
    cb = C // HEAD_BLOCK
    q0 = grp * 3 * nhh
    scale = HEAD_DIM ** -0.5
    tb = _blocks_per_step(nb)
    nsteps = nb // tb

    def body(q_ref, qn_ref, kc_ref, kp_ref, vc_ref, vp_ref, do_ref, don_ref, o_ref, on_ref, l_ref, ln_ref,
             dq_ref, dk_ref, dv_ref):
        b = pl.program_id(1)
        mask_c, mask_prev = _band_masks()
        for sub in range(tb):
            rs = slice(sub * BLK, (sub + 1) * BLK)
            ps = slice((sub - 1) * BLK, sub * BLK)
            ns = slice((sub + 1) * BLK, (sub + 2) * BLK)
            first, last = sub == 0, sub == tb - 1
            mask_p = jnp.logical_and(mask_prev, b > 0) if first else mask_prev
            mask_n = jnp.logical_and(mask_prev, b < nsteps - 1) if last else mask_prev
            for h in range(HEAD_BLOCK // HEAD_DIM):
                hs = slice(h * HEAD_DIM, (h + 1) * HEAD_DIM)
                q, kc, vc = q_ref[rs, hs], kc_ref[rs, hs], vc_ref[rs, hs]
                kp, vp = (kp_ref[:, hs], vp_ref[:, hs]) if first else (kc_ref[ps, hs], vc_ref[ps, hs])
                qn = qn_ref[:, hs] if last else q_ref[ns, hs]
                do = do_ref[rs, hs]
                don = don_ref[:, hs] if last else do_ref[ns, hs]
                on = on_ref[:, hs] if last else o_ref[ns, hs]
                lse = l_ref[rs, hs]
                lse_n = ln_ref[:, hs] if last else l_ref[ns, hs]
                delta = jnp.sum(do * o_ref[rs, hs], axis=1, keepdims=True)
                delta_n = jnp.sum(don * on, axis=1, keepdims=True)
                dob, donb = do.astype(BF16), don.astype(BF16)
                p_c = jnp.exp(jnp.where(mask_c, _dot_nt(q, kc) * scale, NEG) - lse)
                p_p = jnp.exp(jnp.where(mask_p, _dot_nt(q, kp) * scale, NEG) - lse)
                p_n = jnp.exp(jnp.where(mask_n, _dot_nt(qn, kc) * scale, NEG) - lse_n)
                ds_c = (p_c * (_dot_nt(dob, vc) - delta) * scale).astype(BF16)
                ds_p = (p_p * (_dot_nt(dob, vp) - delta) * scale).astype(BF16)
                ds_n = (p_n * (_dot_nt(donb, vc) - delta_n) * scale).astype(BF16)
                dq = jnp.dot(ds_c, kc, preferred_element_type=F32) + jnp.dot(ds_p, kp, preferred_element_type=F32)
                dk = _dot_tn(ds_c, q) + _dot_tn(ds_n, qn)
                dv = _dot_tn(p_c.astype(BF16), dob) + _dot_tn(p_n.astype(BF16), donb)
                dq_ref[rs, hs] = dq.astype(BF16)
                dk_ref[rs, hs] = dk.astype(BF16)
                dv_ref[rs, hs] = dv.astype(BF16)

    def rows(shift):
        if shift == 0:
            return tb * BLK, lambda b: b
        return BLK, (lambda b: jnp.maximum(b * tb - 1, 0)) if shift < 0 else (lambda b: jnp.minimum((b + 1) * tb, nb - 1))

    def pspec(part, shift):
        size, row = rows(shift)
        return pl.BlockSpec((size, HEAD_BLOCK), lambda r, b, hh: (row(b), r * cb + q0 + part * nhh + hh))

    def aspec(shift, width=a_out):
        per = width // HEAD_BLOCK
        size, row = rows(shift)
        return pl.BlockSpec((size, HEAD_BLOCK), lambda r, b, hh: (row(b), r * per + hh))

    Pv = P.reshape(L, d * C)
    w_do = dO.shape[1]
    dOv = dO.reshape(L, d * w_do)
    Ov, Lv = (t.reshape(L, d * a_out) for t in (O, LSE))
    sds = jax.ShapeDtypeStruct((L, d * a_out), BF16)
    dq, dk, dv = _call(
        body, name=name, grid=(d, nsteps, nhh),
        in_specs=[pspec(0, 0), pspec(0, 1), pspec(1, 0), pspec(1, -1), pspec(2, 0), pspec(2, -1),
                  aspec(0, w_do), aspec(1, w_do), aspec(0), aspec(1), aspec(0), aspec(1)],
        out_specs=(aspec(0),) * 3, out_shape=(sds,) * 3,
        compiler_params=_params(("parallel", "parallel", "parallel")))(Pv, Pv, Pv, Pv, Pv, Pv, dOv, dOv, Ov, Ov, Lv, Lv)
    return [t.reshape(T, a_out) for t in (dq, dk, dv)]


def _mem_softmax(q, k, scale):
    s = _dot_nt(q, k) * scale
    e = jnp.exp(s - jnp.max(s, axis=1, keepdims=True))
    return e / jnp.sum(e, axis=1, keepdims=True)


def _mem_fwd(P, qcol, kv, mix, *, name):
    T, W = mix.shape
    tq = _rows(T, 512)
    scale = HEAD_DIM ** -0.5

    def body(q_ref, kv_ref, mix_ref, o_ref):
        o_ref[:, :W] = mix_ref[...].astype(BF16)
        for h in range(MEM_HEADS):
            hs = slice(h * HEAD_DIM, (h + 1) * HEAD_DIM)
            vs = slice(MEM_WIDTH + h * HEAD_DIM, MEM_WIDTH + (h + 1) * HEAD_DIM)
            p = _mem_softmax(q_ref[:, hs].astype(BF16), kv_ref[:, hs], scale)
            o = jnp.dot(p.astype(BF16), kv_ref[:, vs], preferred_element_type=F32)
            o_ref[:, W + h * HEAD_DIM:W + (h + 1) * HEAD_DIM] = o.astype(BF16)

    return _call(body, name=name, grid=(T // tq,),
                 in_specs=[pl.BlockSpec((tq, MEM_WIDTH), lambda i: (i, qcol)),
                           pl.BlockSpec((MEM_TOKENS, 2 * MEM_WIDTH), lambda i: (0, 0)),
                           pl.BlockSpec((tq, W), lambda i: (i, 0))],
                 out_specs=pl.BlockSpec((tq, W + MEM_WIDTH), lambda i: (i, 0)),
                 out_shape=jax.ShapeDtypeStruct((T, W + MEM_WIDTH), BF16),
                 compiler_params=_params(("parallel",)))(P, kv, mix)


def _mem_bwd(P, qcol, kv, dcat, ocol, *, name):
    T = P.shape[0]
    tq = _rows(T, 512)
    scale = HEAD_DIM ** -0.5

    def body(q_ref, kv_ref, do_ref, dq_ref, dkv_ref):
        i = pl.program_id(0)

        @pl.when(i == 0)
        def _():
            dkv_ref[...] = jnp.zeros_like(dkv_ref)

        for h in range(MEM_HEADS):
            hs = slice(h * HEAD_DIM, (h + 1) * HEAD_DIM)
            vs = slice(MEM_WIDTH + h * HEAD_DIM, MEM_WIDTH + (h + 1) * HEAD_DIM)
            q = q_ref[:, hs].astype(BF16)
            k, v = kv_ref[:, hs], kv_ref[:, vs]
            do = do_ref[:, hs].astype(BF16)
            p = _mem_softmax(q, k, scale)
            dp = _dot_nt(do, v)
            ds = (p * (dp - jnp.sum(p * dp, axis=1, keepdims=True)) * scale).astype(BF16)
            dq_ref[:, hs] = jnp.dot(ds, k, preferred_element_type=F32).astype(BF16)
            dkv_ref[:, hs] += _dot_tn(ds, q)
            dkv_ref[:, vs] += _dot_tn(p.astype(BF16), do)

    return _call(body, name=name, grid=(T // tq,),
                 in_specs=[pl.BlockSpec((tq, MEM_WIDTH), lambda i: (i, qcol)),
                           pl.BlockSpec((MEM_TOKENS, 2 * MEM_WIDTH), lambda i: (0, 0)),
                           pl.BlockSpec((tq, MEM_WIDTH), lambda i: (i, ocol))],
                 out_specs=(pl.BlockSpec((tq, MEM_WIDTH), lambda i: (i, 0)),
                            pl.BlockSpec((MEM_TOKENS, 2 * MEM_WIDTH), lambda i: (0, 0))),
                 out_shape=(jax.ShapeDtypeStruct((T, MEM_WIDTH), BF16),
                            jax.ShapeDtypeStruct((MEM_TOKENS, 2 * MEM_WIDTH), F32)),
                 compiler_params=_params(("arbitrary",)))(P, kv, dcat)


_SQRT_HALF = 0.7071067811865476
_INV_SQRT_2PI = 0.3989422804014327


def _gelu(x):
    return 0.5 * x * (1.0 + lax.erf(x * _SQRT_HALF))


def _gelu_grad(x):
    return 0.5 * (1.0 + lax.erf(x * _SQRT_HALF)) + x * (_INV_SQRT_2PI * jnp.exp(-0.5 * x * x))


def _tril():
    t = lax.broadcasted_iota(jnp.int32, (BLK, BLK), 0)
    s = lax.broadcasted_iota(jnp.int32, (BLK, BLK), 1)
    return t >= s


def _gmlp_fwd(P, w_s, bs_t, vg, vb, width, *, name):
    T = P.shape[0]
    G = width // HEAD_DIM
    tb = _rows(T, 512)

    def body(pu_ref, pv_ref, ws_ref, bs_ref, vg_ref, vb_ref, o_ref):
        u = _gelu(pu_ref[...])
        v = _gelu(pv_ref[...])
        mu = jnp.mean(v, axis=-1, keepdims=True)
        vc = v - mu
        var = jnp.mean(vc * vc, axis=-1, keepdims=True)
        vn = (vc * lax.rsqrt(var + LN_EPS) * vg_ref[...] + vb_ref[...]).astype(BF16)
        tril = _tril()
        for g in range(G):
            gs = slice(g * HEAD_DIM, (g + 1) * HEAD_DIM)
            ws = jnp.where(tril, ws_ref[g], 0.0).astype(BF16)
            bias = bs_ref[:, g:g + 1]
            for c in range(tb // BLK):
                cs = slice(c * BLK, (c + 1) * BLK)
                sg = jnp.dot(ws, vn[cs, gs], preferred_element_type=F32) + bias
                o_ref[cs, gs] = u[cs, gs] * sg

    blk = lambda col: pl.BlockSpec((tb, width), lambda i: (i, col))
    full = lambda shape: pl.BlockSpec(shape, lambda i: (0,) * len(shape))
    return _call(body, name=name, grid=(T // tb,),
                 in_specs=[blk(0), blk(1), full((G, BLK, BLK)), full((BLK, G)), full((1, width)), full((1, width))],
                 out_specs=blk(0), out_shape=jax.ShapeDtypeStruct((T, width), F32),
                 compiler_params=_params(("parallel",)))(P, P, w_s, bs_t, vg.reshape(1, width), vb.reshape(1, width))


def _gmlp_bwd(P, w_s, bs_t, vg, vb, dcat, width, *, name):
    T = P.shape[0]
    G = width // HEAD_DIM
    tb = _rows(T, 512)

    def body(pu_ref, pv_ref, ws_ref, bs_ref, vg_ref, vb_ref, dm_ref, dpu_ref, dpv_ref, dws_ref, dbs_ref, dvg_ref,
             dvb_ref, dvn_ref):
        i = pl.program_id(0)

        @pl.when(i == 0)
        def _():
            dws_ref[...] = jnp.zeros_like(dws_ref)
            dbs_ref[...] = jnp.zeros_like(dbs_ref)
            dvg_ref[...] = jnp.zeros_like(dvg_ref)
            dvb_ref[...] = jnp.zeros_like(dvb_ref)

        pu, pv = pu_ref[...], pv_ref[...]
        u = _gelu(pu)
        v = _gelu(pv)
        mu = jnp.mean(v, axis=-1, keepdims=True)
        vc = v - mu
        var = jnp.mean(vc * vc, axis=-1, keepdims=True)
        rstd = lax.rsqrt(var + LN_EPS)
        xhat = vc * rstd
        vn = (xhat * vg_ref[...] + vb_ref[...]).astype(BF16)
        dm = dm_ref[...]
        tril = _tril()
        lane = lax.broadcasted_iota(jnp.int32, (BLK, BLK), 1)
        dbs = jnp.zeros((BLK, BLK), F32)
        for g in range(G):
            gs = slice(g * HEAD_DIM, (g + 1) * HEAD_DIM)
            ws = jnp.where(tril, ws_ref[g], 0.0).astype(BF16)
            bias = bs_ref[:, g:g + 1]
            dws = jnp.zeros((BLK, BLK), F32)
            rs = jnp.zeros((BLK, 1), F32)
            for c in range(tb // BLK):
                cs = slice(c * BLK, (c + 1) * BLK)
                vn_cg = vn[cs, gs]
                sg = jnp.dot(ws, vn_cg, preferred_element_type=F32) + bias
                dm_cg = dm[cs, gs]
                dpu_ref[cs, gs] = (dm_cg * sg * _gelu_grad(pu[cs, gs])).astype(BF16)
                dsg = dm_cg * u[cs, gs]
                dsgb = dsg.astype(BF16)
                dvn_ref[cs, gs] = _dot_tn(ws, dsgb)
                dws = dws + _dot_nt(dsgb, vn_cg)
                rs = rs + jnp.sum(dsg, axis=1, keepdims=True)
            dws_ref[g] += jnp.where(tril, dws, 0.0)
            dbs = dbs + jnp.where(lane == g, rs, 0.0)
        dbs_ref[...] += dbs
        dvn = dvn_ref[...]
        dxh = dvn * vg_ref[...]
        dv = rstd * (dxh - jnp.mean(dxh, axis=-1, keepdims=True)
                     - xhat * jnp.mean(dxh * xhat, axis=-1, keepdims=True))
        dpv_ref[...] = (dv * _gelu_grad(pv)).astype(BF16)
        dvg_ref[...] += jnp.sum(dvn * xhat, axis=0, keepdims=True)
        dvb_ref[...] += jnp.sum(dvn, axis=0, keepdims=True)

    blk = lambda col: pl.BlockSpec((tb, width), lambda i: (i, col))
    full = lambda shape: pl.BlockSpec(shape, lambda i: (0,) * len(shape))
    return _call(body, name=name, grid=(T // tb,),
                 in_specs=[blk(0), blk(1), full((G, BLK, BLK)), full((BLK, G)), full((1, width)), full((1, width)),
                           blk(0)],
                 out_specs=(blk(0), blk(0), full((G, BLK, BLK)), full((BLK, BLK)), full((1, width)), full((1, width))),
                 out_shape=(jax.ShapeDtypeStruct((T, width), BF16), jax.ShapeDtypeStruct((T, width), BF16),
                            jax.ShapeDtypeStruct((G, BLK, BLK), F32), jax.ShapeDtypeStruct((BLK, BLK), F32),
                            jax.ShapeDtypeStruct((1, width), F32), jax.ShapeDtypeStruct((1, width), F32)),
                 scratch_shapes=[pltpu.VMEM((tb, width), F32)],
                 compiler_params=_params(("arbitrary",)))(P, P, w_s, bs_t, vg.reshape(1, width), vb.reshape(1, width), dcat)


def _place():
    x, y, c = lax.axis_index("x"), lax.axis_index("y"), lax.axis_index("c")
    chips = [(1 - x, y), (x, 1 - y), (1 - x, 1 - y)]
    return x, y, c, 2 * x + y, chips, [2 * px + py for px, py in chips]


def _half(ref, c, rows):
    return ref.at[pl.ds(c * (rows // 2), rows // 2)]


def _place_shard(w, chip, dtype, *, name):
    R, C = w.shape
    tr = _rows(R, max(8, (1 << 19) // C // 8 * 8)) if R % 8 == 0 else R

    def body(chip_ref, w_ref, o_ref):
        o_ref[...] = w_ref[...].astype(dtype)

    grid_spec = pltpu.PrefetchScalarGridSpec(
        num_scalar_prefetch=1, grid=(R // tr,), in_specs=[pl.BlockSpec((tr, C), lambda i, s: (i, 0))],
        out_specs=pl.BlockSpec((None, tr, C), lambda i, s: (s[0], i, 0)))
    return _call(body, name=name, grid_spec=grid_spec, out_shape=jax.ShapeDtypeStruct((4, R, C), dtype),
                 compiler_params=_params(("parallel",)))(chip, w)


SEM = pl.BlockSpec(memory_space=pltpu.SEMAPHORE)
EFFECT = pltpu.SideEffectType.DATAFLOW_SIDE_EFFECTING


def _hbm(a):
    return pltpu.with_memory_space_constraint(a, pltpu.HBM)


def _sibling_start(arrays, copies, ncopies, after, *, name):
    n, na = len(arrays), len(after)

    def body(*refs):
        a = refs[:n]
        send, recv = refs[n + na], refs[n + na + 1]
        token = refs[2 * n + na + 2]
        x, y, c, _, _, _ = _place()
        for q, (src, dst) in enumerate(copies(a, c)):
            pltpu.make_async_remote_copy(src_ref=src, dst_ref=dst, send_sem=send.at[q], recv_sem=recv.at[q],
                                         device_id=(x, y, 1 - c), device_id_type=MESH).start()
        token[...] = jnp.zeros_like(token)

    outs = _call(body, name=name, in_specs=[HBM] * n + [pl.BlockSpec(memory_space=pl.ANY)] * na,
                 out_specs=(SEM, SEM, *([HBM] * n), pl.BlockSpec(memory_space=pltpu.VMEM)),
                 out_shape=(pltpu.SemaphoreType.DMA((ncopies,)), pltpu.SemaphoreType.DMA((ncopies,)),
                            *[pltpu.HBM(a.shape, a.dtype) for a in arrays], jax.ShapeDtypeStruct((8, 128), F32)),
                 input_output_aliases={t: 2 + t for t in range(n)},
                 compiler_params=pltpu.CompilerParams(has_side_effects=EFFECT))(*[_hbm(a) for a in arrays], *after)
    return outs[0], outs[1], list(outs[2:2 + n]), outs[2 + n]


def _sibling_wait(send, recv, arrays, copies, after, *, name):
    n, na = len(arrays), len(after)

    def body(*refs):
        a = refs[:n]
        send, recv = refs[n], refs[n + 1]
        x, y, c, _, _, _ = _place()
        for q, ((src, dst), (_, landed)) in enumerate(zip(copies(a, c), copies(a, 1 - c))):
            pltpu.make_async_remote_copy(src_ref=src, dst_ref=dst, send_sem=send.at[q], recv_sem=recv.at[q],
                                         device_id=(x, y, 1 - c), device_id_type=MESH).wait_send()
            pltpu.make_async_remote_copy(src_ref=src, dst_ref=landed, send_sem=send.at[q], recv_sem=recv.at[q],
                                         device_id=(x, y, 1 - c), device_id_type=MESH).wait_recv()

    outs = _call(body, name=name, in_specs=[HBM] * n + [SEM, SEM] + [pl.BlockSpec(memory_space=pl.ANY)] * na,
                 out_specs=[HBM] * n, out_shape=[pltpu.HBM(a.shape, a.dtype) for a in arrays],
                 input_output_aliases={t: t for t in range(n)},
                 compiler_params=pltpu.CompilerParams(has_side_effects=EFFECT))(*arrays, send, recv, *after)
    return list(outs)


def _pair_copies(n):
    def copies(a, core):
        out = []
        for t in range(n):
            h = a[t].shape[1] // 2
            out.append((a[t].at[:, pl.ds((1 - core) * h, h)], a[n + t]))
        return out
    return copies


def _swap_copies(a, core):
    out = []
    for ref in a:
        h = ref.shape[0] // 2
        out.append((ref.at[pl.ds(core * h, h)],) * 2)
    return out


def _gather_windows(shapes):
    split = [s[1] % 16 == 0 for s in shapes]

    def window(ref, t, chip_idx, core):
        w = ref.at[chip_idx]
        return _half(w, core, shapes[t][1]) if split[t] else w

    return split, window


def _gather_start(bufs, after, *, name):
    n = len(bufs)
    split, window = _gather_windows([b.shape for b in bufs])

    na = len(after)

    def body(*refs):
        b = refs[:n]
        send, recv = refs[n + na], refs[n + na + 1]
        token = refs[2 * n + na + 2]
        x, y, c, j, chips, pj = _place()
        for t in range(n):
            for k in range(3):
                pltpu.make_async_remote_copy(src_ref=window(b[t], t, j, c), dst_ref=window(b[t], t, j, c),
                                             send_sem=send.at[3 * t + k], recv_sem=recv.at[3 * t + k],
                                             device_id=(*chips[k], c), device_id_type=MESH).start()
        token[...] = jnp.zeros_like(token)

    outs = _call(body, name=name, in_specs=[HBM] * n + [pl.BlockSpec(memory_space=pl.ANY)] * na,
                 out_specs=(SEM, SEM, *([HBM] * n), pl.BlockSpec(memory_space=pltpu.VMEM)),
                 out_shape=(pltpu.SemaphoreType.DMA((3 * n,)), pltpu.SemaphoreType.DMA((3 * n,)),
                            *[pltpu.HBM(b.shape, b.dtype) for b in bufs], jax.ShapeDtypeStruct((8, 128), F32)),
                 input_output_aliases={t: 2 + t for t in range(n)},
                 compiler_params=pltpu.CompilerParams(has_side_effects=EFFECT))(*[_hbm(b) for b in bufs], *after)
    return outs[0], outs[1], list(outs[2:2 + n]), outs[2 + n]


def _gather_wait(send, recv, bufs, after, *, name):
    n = len(bufs)
    split, window = _gather_windows([b.shape for b in bufs])

    def body(*refs):
        b = refs[:n]
        send, recv = refs[n], refs[n + 1]
        x, y, c, j, chips, pj = _place()
        for t in range(n):
            for k in range(3):
                out = pltpu.make_async_remote_copy(src_ref=window(b[t], t, j, c), dst_ref=window(b[t], t, j, c),
                                                   send_sem=send.at[3 * t + k], recv_sem=recv.at[3 * t + k],
                                                   device_id=(*chips[k], c), device_id_type=MESH)
                out.wait_send()
                back = pltpu.make_async_remote_copy(src_ref=window(b[t], t, pj[k], c), dst_ref=window(b[t], t, pj[k], c),
                                                    send_sem=send.at[3 * t + k], recv_sem=recv.at[3 * t + k],
                                                    device_id=(*chips[k], c), device_id_type=MESH)
                back.wait_recv()

    outs = _call(body, name=name, in_specs=[HBM] * n + [SEM, SEM] + [pl.BlockSpec(memory_space=pl.ANY)] * len(after),
                 out_specs=[HBM] * n, out_shape=[pltpu.HBM(b.shape, b.dtype) for b in bufs],
                 input_output_aliases={t: t for t in range(n)},
                 compiler_params=pltpu.CompilerParams(has_side_effects=EFFECT))(*bufs, send, recv, *after)
    return list(outs)


def _pass_copies(shapes):
    split, window = _gather_windows(shapes)

    def copies(a, core):
        _, _, _, _, _, pj = _place()
        return [(window(a[t], t, pj[k], core),) * 2 for t in range(len(shapes)) if split[t] for k in range(3)]

    return copies, 3 * sum(split)


def _gather_pass(bufs, *, name):
    n = len(bufs)
    split, window = _gather_windows([b.shape for b in bufs])
    idx = [t for t in range(n) if split[t]]

    def body(*refs):
        b = refs[n:2 * n]
        send, recv = refs[2 * n:]
        x, y, c, j, chips, pj = _place()

        def d2d(u, t, k, core):
            w = window(b[t], t, pj[k], core)
            return pltpu.make_async_remote_copy(src_ref=w, dst_ref=w, send_sem=send.at[3 * u + k],
                                                recv_sem=recv.at[3 * u + k], device_id=(x, y, 1 - c),
                                                device_id_type=MESH)

        sent = [d2d(u, t, k, c) for u, t in enumerate(idx) for k in range(3)]
        for cp in sent:
            cp.start()
        for u, t in enumerate(idx):
            for k in range(3):
                d2d(u, t, k, 1 - c).wait_recv()
        for cp in sent:
            cp.wait_send()

    return _call(body, name=name, in_specs=[HBM] * n, out_specs=[HBM] * n,
                 out_shape=[jax.ShapeDtypeStruct(b.shape, b.dtype) for b in bufs],
                 input_output_aliases={t: t for t in range(n)},
                 scratch_shapes=[pltpu.SemaphoreType.DMA((3 * len(idx),))] * 2)(*bufs)


def _chip_start(pairs, *, name):
    n = len(pairs)
    lands = [lax.empty((3,) + p.shape[1:], p.dtype) for p in pairs]

    def body(*refs):
        s, r = refs[:n], refs[n:2 * n]
        send, recv = refs[2 * n], refs[2 * n + 1]
        token = refs[4 * n + 2]
        x, y, c, j, chips, pj = _place()
        for t in range(n):
            for k in range(3):
                pltpu.make_async_remote_copy(src_ref=s[t].at[pj[k]], dst_ref=r[t].at[k], send_sem=send.at[3 * t + k],
                                             recv_sem=recv.at[3 * t + k], device_id=(*chips[k], c),
                                             device_id_type=MESH).start()
        token[...] = jnp.zeros_like(token)

    outs = _call(body, name=name, in_specs=[HBM] * (2 * n),
                 out_specs=(SEM, SEM, *([HBM] * (2 * n)), pl.BlockSpec(memory_space=pltpu.VMEM)),
                 out_shape=(pltpu.SemaphoreType.DMA((3 * n,)), pltpu.SemaphoreType.DMA((3 * n,)),
                            *[pltpu.HBM(a.shape, a.dtype) for a in list(pairs) + lands],
                            jax.ShapeDtypeStruct((8, 128), F32)),
                 input_output_aliases={t: 2 + t for t in range(2 * n)},
                 compiler_params=pltpu.CompilerParams(has_side_effects=EFFECT))(*[_hbm(a) for a in list(pairs) + lands])
    return outs[0], outs[1], list(outs[2:2 + n]), list(outs[2 + n:2 + 2 * n]), outs[2 + 2 * n]


def _chip_wait(send, recv, pairs, lands, after, *, name):
    n = len(pairs)

    def body(*refs):
        s, r = refs[:n], refs[n:2 * n]
        send, recv = refs[2 * n], refs[2 * n + 1]
        x, y, c, j, chips, pj = _place()
        for t in range(n):
            for k in range(3):
                cp = pltpu.make_async_remote_copy(src_ref=s[t].at[pj[k]], dst_ref=r[t].at[k], send_sem=send.at[3 * t + k],
                                                  recv_sem=recv.at[3 * t + k], device_id=(*chips[k], c),
                                                  device_id_type=MESH)
                cp.wait_send()
                cp.wait_recv()

    outs = _call(body, name=name,
                 in_specs=[HBM] * (2 * n) + [SEM, SEM] + [pl.BlockSpec(memory_space=pl.ANY)] * len(after),
                 out_specs=[HBM] * (2 * n), out_shape=[pltpu.HBM(a.shape, a.dtype) for a in list(pairs) + list(lands)],
                 input_output_aliases={t: t for t in range(2 * n)},
                 compiler_params=pltpu.CompilerParams(has_side_effects=EFFECT))(*pairs, *lands, send, recv, *after)
    return list(outs[:n]), list(outs[n:])


def _all_reduce_small(packed, *, name):
    R, C = packed.shape

    def body(p_ref, o_ref, slots, send, recv, lsem):
        x, y, c = lax.axis_index("x"), lax.axis_index("y"), lax.axis_index("c")
        me = 4 * x + 2 * y + c
        lc = pltpu.make_async_copy(p_ref, slots.at[me], lsem.at[0])
        lc.start()
        copies = []
        for rel in range(1, 8):
            fx, fy, fc = (rel >> 2) & 1, (rel >> 1) & 1, rel & 1
            to = (1 - x if fx else x, 1 - y if fy else y, 1 - c if fc else c)
            cp = pltpu.make_async_remote_copy(src_ref=p_ref, dst_ref=slots.at[me], send_sem=send.at[rel - 1],
                                              recv_sem=recv.at[rel - 1], device_id=to, device_id_type=MESH)
            cp.start()
            copies.append((cp, 4 * to[0] + 2 * to[1] + to[2]))
        for rel, (cp, frm) in enumerate(copies):
            cp.wait_send()
            pltpu.make_async_remote_copy(src_ref=p_ref, dst_ref=slots.at[frm], send_sem=send.at[rel],
                                         recv_sem=recv.at[rel], device_id=(x, y, c), device_id_type=MESH).wait_recv()
        lc.wait()
        acc = slots[0]
        for dev in range(1, 8):
            acc = acc + slots[dev]
        o_ref[...] = acc

    return _call(body, name=name, in_specs=[pl.BlockSpec(memory_space=pltpu.VMEM)],
                 out_specs=pl.BlockSpec(memory_space=pltpu.VMEM), out_shape=jax.ShapeDtypeStruct((R, C), F32),
                 scratch_shapes=[pltpu.VMEM((8, R, C), F32), pltpu.SemaphoreType.DMA((7,)),
                                 pltpu.SemaphoreType.DMA((7,)), pltpu.SemaphoreType.DMA((1,))],
                 compiler_params=pltpu.CompilerParams(vmem_limit_bytes=V7X_VMEM_LIMIT))(packed)


def _pair_sum(grad, theirs, core, *, name):
    J, H, C = theirs.shape
    tr = _rows(H, max(8, (1 << 19) // C // 8 * 8))

    def body(core_ref, a_ref, b_ref, o_ref):
        o_ref[...] = (a_ref[...].astype(F32) + b_ref[...].astype(F32)).astype(BF16)

    blk = pl.BlockSpec((None, tr, C), lambda j, i, s: (j, i, 0))
    mine = pl.BlockSpec((None, None, tr, C), lambda j, i, s: (j, s[0], i, 0))
    grid_spec = pltpu.PrefetchScalarGridSpec(num_scalar_prefetch=1, grid=(J, H // tr), in_specs=[mine, blk],
                                             out_specs=blk)
    return _call(body, name=name, grid_spec=grid_spec, out_shape=jax.ShapeDtypeStruct((J, H, C), BF16),
                 compiler_params=_params(("parallel", "parallel")))(core, grad.reshape(J, 2, H, C), theirs)


def _chip_sum(pairs, slots, place, *, name):
    _, H, C = slots.shape
    tr = _rows(H, max(8, (1 << 19) // C // 8 * 8))
    nr = H // tr

    def body(place_ref, s0, s1, s2, s3, o_ref):
        o_ref[...] = ((s0[...].astype(F32) + s1[...].astype(F32)) + s2[...].astype(F32)) + s3[...].astype(F32)

    def slot(k):
        return pl.BlockSpec((None, tr, C), lambda i, s: (k, i, 0))

    own = pl.BlockSpec((None, tr, C), lambda i, s: (s[0], i, 0))
    grid_spec = pltpu.PrefetchScalarGridSpec(
        num_scalar_prefetch=1, grid=(nr,), in_specs=[own, slot(0), slot(1), slot(2)],
        out_specs=pl.BlockSpec((tr, C), lambda i, s: (s[1] * nr + i, 0)))
    return _call(body, name=name, grid_spec=grid_spec, out_shape=jax.ShapeDtypeStruct((2 * H, C), F32),
                 compiler_params=_params(("parallel",)))(place, pairs, slots, slots, slots)


def _scatter_pair_start(grads, tag):
    lands = [lax.empty((4, g.shape[1] // 2, g.shape[2]), g.dtype) for g in grads]
    return _sibling_start(list(grads) + lands, _pair_copies(len(grads)), len(grads), (), name=f"rs_pair_start_{tag}")


def _scatter_chip_start(pair_started, after, place, tag):
    send, recv, arrays, _ = pair_started
    n = len(arrays) // 2
    arrays = _sibling_wait(send, recv, arrays, _pair_copies(n), after, name=f"rs_pair_wait_{tag}")
    pairs = [_pair_sum(g, t, place[1:], name=f"rs_pair_sum_{tag}_{i}")
             for i, (g, t) in enumerate(zip(arrays[:n], arrays[n:]))]
    return _chip_start(pairs, name=f"rs_chip_start_{tag}")


def _scatter_swap_start(chip_started, after, place, tag):
    send, recv, pairs, lands, _ = chip_started
    pairs, slots = _chip_wait(send, recv, pairs, lands, after, name=f"rs_chip_wait_{tag}")
    fulls = [_chip_sum(p, s, place, name=f"rs_chip_sum_{tag}_{i}") for i, (p, s) in enumerate(zip(pairs, slots))]
    return _sibling_start(fulls, _swap_copies, len(fulls), (), name=f"rs_swap_start_{tag}")


def _scatter_finish(swap_started, tag):
    send, recv, fulls, _ = swap_started
    return _sibling_wait(send, recv, fulls, _swap_copies, (), name=f"rs_swap_wait_{tag}")


def kernel(x, mem, w_in_a, w_in_b, w_s, b_s, vnorm_g, vnorm_b, w_mem_kv, w_out, ln1_g, ln1_b, w_ff1, w_ff2, ln2_g, ln2_b, loss_target, m_w_in_a, m_w_in_b, m_w_s, m_b_s, m_vnorm_g, m_vnorm_b, m_w_mem_kv, m_w_out, m_ln1_g, m_ln1_b, m_w_ff1, m_w_ff2, m_ln2_g, m_ln2_b, v_w_in_a, v_w_in_b, v_w_s, v_b_s, v_vnorm_g, v_vnorm_b, v_w_mem_kv, v_w_out, v_ln1_g, v_ln1_b, v_w_ff1, v_w_ff2, v_ln2_g, v_ln2_b):
    T, D = x.shape[1], x.shape[2]
    depth = w_ff1.shape[0]
    alpha = (2.0 * depth) ** 0.25
    a_out = (D // 256) * HEAD_DIM
    a_cols = len(A_PAIRS) * 3 * a_out
    b_width = (D // 256) * HEAD_DIM
    G = b_width // HEAD_DIM
    assert a_out % HEAD_BLOCK == 0 and T % (BLK * A_PAIRS[-1][1]) == 0

    xf = x.reshape(T, D)
    mem_b = mem.reshape(MEM_TOKENS, D).astype(BF16)
    target = loss_target.reshape(T, D)
    c_idx = lax.axis_index("x") * 2 + lax.axis_index("y")
    place = jnp.stack([c_idx, lax.axis_index("c")]).astype(jnp.int32)

    def gather_place(i, group):
        jl, tag = i // 2, "a" if i % 2 == 0 else "b"
        if group == "mix":
            mats = [(w_in_a if i % 2 == 0 else w_in_b)[jl], w_mem_kv[i], w_out[i]]
        else:
            mats = [w_ff1[i], w_ff2[i]]
        bufs = [_place_shard(w, place[:1], BF16, name=f"place_{tag}_{group}_{n_}") for n_, w in enumerate(mats)]
        if group == "mix" and i % 2:
            bufs += [_place_shard(v_[jl].reshape(1, -1), place[:1], F32, name="place_vnorm") for v_ in (vnorm_g, vnorm_b)]
        return bufs

    placed = {}

    def gather_begin(i, group, after):
        tag = "a" if i % 2 == 0 else "b"
        bufs = placed.pop((i, group), None) or gather_place(i, group)
        return _gather_start(bufs, after, name=f"gather_start_{tag}_{group}")

    def gather_end(started, after, name):
        send, recv, bufs, _ = started
        bufs = _gather_wait(send, recv, bufs, after, name=f"gather_wait_{name}")
        return _gather_pass(bufs, name=f"gather_pass_{name}")

    def group_view(P, g, d):
        return (P, g) if d == 1 else (P[:, g * 3 * a_out:(g + 1) * 3 * a_out], 0)

    saved = []
    xb = xf.astype(BF16)
    mix_started = gather_begin(0, "mix", ())
    for key in ((0, "ff"), (1, "mix"), (1, "ff")):
        if key[0] < depth:
            placed[key] = gather_place(*key)
    early = [b for bufs in placed.values() for b in bufs]
    ff_started = None
    for i in range(depth):
        jl = i // 2
        is_a = i % 2 == 0
        tag = "a" if is_a else "b"
        gathered = gather_end(mix_started, (xf, *early) if i == 0 else (xf,), f"{tag}_mix")
        win, wkv, wout = gathered[:3]
        wkv = wkv.reshape(1, D, 2 * MEM_WIDTH)
        tokens = []
        if i > 0:
            send, recv, ff_bufs, _ = ff_started
            ff_bufs = _gather_wait(send, recv, ff_bufs, (xf,), name=f"gather_wait_{tag}_ff")
            pass_copies, ncopies = _pass_copies([b.shape for b in ff_bufs])
            ff_pass = _sibling_start(ff_bufs, pass_copies, ncopies, (), name=f"gather_pass_start_{tag}_ff")
            tokens.append(ff_pass[3])
        token = None
        if i == 0:
            ff_started = gather_begin(0, "ff", (win,))
            token = ff_started[3]
        if i + 1 < depth:
            mix_started = gather_begin(i + 1, "mix", (win,) if token is None else (token,))
            ff_next = gather_begin(i + 1, "ff", (mix_started[3],))
            token = ff_next[3]
        if token is not None:
            tokens.append(token)
        P = _mm_nn(xb, win, out_dtype=BF16 if is_a else F32, after=tuple(tokens), name=f"proj_in_{'a' if is_a else 'b'}")
        kv = _mm_nn(mem_b, wkv, name="proj_kv")
        if is_a:
            outs = [_attn_fwd(*group_view(P, g, d), d, a_out, name=f"attn_fwd_d{d}") for g, (_, d) in enumerate(A_PAIRS)]
            mix, lse = _attn_combine([o for o, _ in outs], [l for _, l in outs], name="attn_combine")
            qcol = a_cols // MEM_WIDTH
            extra = (lse,)
        else:
            vg_full = gathered[3].reshape(-1)
            vb_full = gathered[4].reshape(-1)
            bs_t = b_s[jl].T
            mix = _gmlp_fwd(P, w_s[jl], bs_t, vg_full, vb_full, b_width, name="gmlp_fwd")
            qcol = 2 * b_width // MEM_WIDTH
            extra = (vg_full, vb_full, bs_t)
        cat = _mem_fwd(P, qcol, kv, mix, name=f"mem_fwd_{'a' if is_a else 'b'}")
        y = _mm_nn(cat, wout, out_dtype=F32, name="proj_out")
        x1, x1b = _ln_fwd(xf, y, ln1_g[i], ln1_b[i], alpha, name="ln_fwd")
        if i > 0:
            wff1, wff2 = _sibling_wait(ff_pass[0], ff_pass[1], ff_pass[2], pass_copies, (x1b,),
                                       name=f"gather_pass_wait_{tag}_ff")
        else:
            wff1, wff2 = gather_end(ff_started, (x1b,), f"{tag}_ff")
        wff2 = wff2.reshape(1, 4 * D, D)
        if i + 1 < depth:
            ff_started = ff_next
        a_pre, hid = _mm_nn(x1b, wff1, epi="relu2", name="ff1")
        f = _mm_nn(hid, wff2, out_dtype=F32, name="ff2")
        x2, x2b = _ln_fwd(x1, f, ln2_g[i], ln2_b[i], alpha, name="ln_fwd")
        saved.append(dict(xf=xf, xb=xb, P=P, kv=kv, mix=mix, cat=cat, y=y, x1=x1, x1b=x1b, a_pre=a_pre, hid=hid, f=f,
                          extra=extra, w=(win, wkv, wout, wff1, wff2), qcol=qcol))
        xf, xb = x2, x2b

    dx, sq = _loss_head(xf, target, name="loss_head")
    loss = lax.psum(sq[0, 0] * (0.5 / D), ("x", "y", "c"))

    g_big = dict(w_in_a=[None] * ((depth + 1) // 2), w_in_b=[None] * (depth // 2), w_mem_kv=[None] * depth,
                 w_out=[None] * depth, w_ff1=[None] * depth, w_ff2=[None] * depth)
    small = {k: [None] * depth for k in ("ln1_g", "ln1_b", "ln2_g", "ln2_b")}
    small_b = {k: [None] * (depth // 2) for k in ("w_s", "b_s", "vnorm_g", "vnorm_b")}
    def store(red, layer, group):
        if group == "ff":
            g_big["w_ff1"][layer], g_big["w_ff2"][layer] = red
        else:
            g_big["w_in_a" if layer % 2 == 0 else "w_in_b"][layer // 2], g_big["w_mem_kv"][layer], g_big["w_out"][layer] = red

    chips = []
    mix_pair = None
    tokens = ()
    for i in reversed(range(depth)):
        jl = i // 2
        is_a = i % 2 == 0
        s = saved[i]
        win, wkv, wout, wff1, wff2 = s["w"]
        tag = "a" if is_a else "b"
        d_f, adr2, dg2, db2 = _ln_bwd(dx, s["x1"], s["f"], ln2_g[i], alpha, tokens, name="ln_bwd")
        small["ln2_g"][i], small["ln2_b"][i] = dg2, db2
        gw_ff2 = _mm_tn(s["hid"], d_f, 1, name="grad_ff2").reshape(4, D, D)
        da = _mm_nt(d_f, wff2, epi="drelu2", extra=s["a_pre"], name="ff2_bwd")
        gw_ff1 = _mm_tn(s["x1b"], da, 4, name="grad_ff1")
        ff_pair = _scatter_pair_start([gw_ff1, gw_ff2], f"{tag}_ff")
        late = []
        if mix_pair is not None:
            above = "b" if is_a else "a"
            mix_chip = _scatter_chip_start(mix_pair, (d_f,), place, f"{above}_mix")
            chips.append((mix_chip, i + 1, "mix"))
            late.append(mix_chip[4])
        dx1 = _mm_nt(da, wff1, epi="resid", extra=adr2, after=(ff_pair[3],), name="ff1_bwd")
        ff_chip = _scatter_chip_start(ff_pair, (dx1,), place, f"{tag}_ff")
        chips.append((ff_chip, i, "ff"))
        late.append(ff_chip[4])
        d_y, adr1, dg1, db1 = _ln_bwd(dx1, s["xf"], s["y"], ln1_g[i], alpha, tuple(late), name="ln_bwd")
        small["ln1_g"][i], small["ln1_b"][i] = dg1, db1
        gw_out = _mm_tn(s["cat"], d_y, 4, name="grad_out")
        dcat = _mm_nt(d_y, wout, out_dtype=F32, name="proj_out_bwd")
        ocol = dcat.shape[1] // MEM_WIDTH - 1
        dmq, dkv = _mem_bwd(s["P"], s["qcol"], s["kv"], dcat, ocol, name=f"mem_bwd_{tag}")
        gw_kv = _mm_tn(mem_b, dkv.astype(BF16), 1, name="grad_kv").reshape(4, D // 4, 2 * MEM_WIDTH)
        if is_a:
            (lse,) = s["extra"]
            dmix = dcat[:, :a_out]
            parts = []
            for g, (_, d) in enumerate(A_PAIRS):
                Pg, g0 = group_view(s["P"], g, d)
                parts += _attn_bwd(Pg, dcat if d == 1 else dmix, s["mix"], lse, g0, d, a_out, name=f"attn_bwd_d{d}")
            dP = jnp.concatenate(parts + [dmq], axis=1)
        else:
            vg_full, vb_full, bs_t = s["extra"]
            dpu, dpv, dws, dbs, dvg, dvb = _gmlp_bwd(s["P"], w_s[jl], bs_t, vg_full, vb_full, dcat, b_width,
                                                     name="gmlp_bwd")
            small_b["w_s"][jl], small_b["b_s"][jl] = dws, dbs[:, :G].T
            small_b["vnorm_g"][jl], small_b["vnorm_b"][jl] = dvg, dvb
            dP = jnp.concatenate([dpu, dpv, dmq], axis=1)
        gw_in = _mm_tn(s["xb"], dP, 4, name=f"grad_in_{tag}")
        dx = _mm_nt(dP, win, epi="resid", extra=adr1, name=f"proj_in_bwd_{tag}")
        mix_pair = _scatter_pair_start([gw_in, gw_kv, gw_out], f"{tag}_mix")
        tokens = (mix_pair[3],)
    last_chip = _scatter_chip_start(mix_pair, (dx,), place, "a_mix")
    swaps, after = [], (dx,)
    for chip_started, layer, group in chips:
        name = f"{'a' if layer % 2 == 0 else 'b'}_{group}"
        swaps.append((_scatter_swap_start(chip_started, after, place, name), layer, group, name))
        after = (swaps[-1][0][3],)
    for swap_started, layer, group, name in swaps:
        store(_scatter_finish(swap_started, name), layer, group)

    def finish_last(after):
        store(_scatter_finish(_scatter_swap_start(last_chip, after, place, "a_mix"), "a_mix"), 0, "mix")

    grad_x = dx.reshape(x.shape)

    nb_layers = depth // 2
    pieces = ([jnp.stack(small_b["w_s"]).reshape(-1, 128), jnp.stack(small_b["b_s"]).reshape(-1, 128),
               jnp.stack(small_b["vnorm_g"]).reshape(-1, 128), jnp.stack(small_b["vnorm_b"]).reshape(-1, 128)]
              + [jnp.stack(small[k]).reshape(-1, 128) for k in ("ln1_g", "ln1_b", "ln2_g", "ln2_b")])
    sizes = [p.shape[0] for p in pieces]
    pad = (-sum(sizes)) % 8
    packed = jnp.concatenate(pieces + ([jnp.zeros((pad, 128), F32)] if pad else []), axis=0)
    summed = _all_reduce_small(packed, name="all_reduce_small")
    offs = [0]
    for n_ in sizes:
        offs.append(offs[-1] + n_)
    sp = [summed[offs[k]:offs[k + 1]] for k in range(len(sizes))]
    vshard = vnorm_g.shape[1]
    g_small = dict(
        w_s=sp[0].reshape(w_s.shape), b_s=sp[1].reshape(b_s.shape),
        vnorm_g=lax.dynamic_slice_in_dim(sp[2].reshape(nb_layers, -1), c_idx * vshard, vshard, axis=1),
        vnorm_b=lax.dynamic_slice_in_dim(sp[3].reshape(nb_layers, -1), c_idx * vshard, vshard, axis=1),
        ln1_g=sp[4].reshape(ln1_g.shape), ln1_b=sp[5].reshape(ln1_b.shape),
        ln2_g=sp[6].reshape(ln2_g.shape), ln2_b=sp[7].reshape(ln2_b.shape))

    names = ["w_in_a", "w_in_b", "w_s", "b_s", "vnorm_g", "vnorm_b", "w_mem_kv", "w_out", "ln1_g", "ln1_b", "w_ff1",
             "w_ff2", "ln2_g", "ln2_b"]
    ws = dict(w_in_a=w_in_a, w_in_b=w_in_b, w_s=w_s, b_s=b_s, vnorm_g=vnorm_g, vnorm_b=vnorm_b, w_mem_kv=w_mem_kv,
              w_out=w_out, ln1_g=ln1_g, ln1_b=ln1_b, w_ff1=w_ff1, w_ff2=w_ff2, ln2_g=ln2_g, ln2_b=ln2_b)
    ms = dict(w_in_a=m_w_in_a, w_in_b=m_w_in_b, w_s=m_w_s, b_s=m_b_s, vnorm_g=m_vnorm_g, vnorm_b=m_vnorm_b,
              w_mem_kv=m_w_mem_kv, w_out=m_w_out, ln1_g=m_ln1_g, ln1_b=m_ln1_b, w_ff1=m_w_ff1, w_ff2=m_w_ff2,
              ln2_g=m_ln2_g, ln2_b=m_ln2_b)
    vs = dict(w_in_a=v_w_in_a, w_in_b=v_w_in_b, w_s=v_w_s, b_s=v_b_s, vnorm_g=v_vnorm_g, vnorm_b=v_vnorm_b,
              w_mem_kv=v_w_mem_kv, w_out=v_w_out, ln1_g=v_ln1_g, ln1_b=v_ln1_b, w_ff1=v_w_ff1, w_ff2=v_w_ff2,
              ln2_g=v_ln2_g, ln2_b=v_ln2_b)
    grads, deltas, new_m, new_v = {}, {}, {}, {}
    for k in ("w_ff1", "w_ff2", "w_in_a", "w_in_b", "w_mem_kv", "w_out"):
        if k == "w_in_a":
            finish_last((deltas["w_ff2"], summed))
        g = jnp.stack(g_big[k]).reshape(ws[k].shape)
        cols = ws[k].shape[-1]
        d_, m_, v_ = _adamw(ws[k].reshape(-1, cols), g.reshape(-1, cols), ms[k].reshape(-1, cols),
                            vs[k].reshape(-1, cols), name=f"adamw_{k}")
        grads[k] = g
        deltas[k], new_m[k], new_v[k] = (t.reshape(ws[k].shape) for t in (d_, m_, v_))
    small_names = [k for k in names if k not in g_big]

    def pack(tree):
        flat = jnp.concatenate([tree[k].reshape(-1) for k in small_names])
        padn = (-flat.shape[0]) % 1024
        return jnp.pad(flat, (0, padn)).reshape(-1, 128)

    d_, m_, v_ = _adamw(pack(ws), pack(g_small), pack(ms), pack(vs), name="adamw_small")
    off = 0
    for k in small_names:
        n_ = ws[k].size
        grads[k] = g_small[k]
        deltas[k], new_m[k], new_v[k] = (t.reshape(-1)[off:off + n_].reshape(ws[k].shape) for t in (d_, m_, v_))
        off += n_

    return (loss, grad_x, *[grads[k] for k in names], *[deltas[k] for k in names], *[new_m[k] for k in names],
            *[new_v[k] for k in names])
```

```python
import functools
import math

import jax
import jax.numpy as jnp
from jax import lax
from jax.experimental import pallas as pl
from jax.experimental.pallas import tpu as pltpu

F32 = jnp.float32
BF16 = jnp.bfloat16

HEAD_DIM = 128
A_PAIRS = ((128, 1), (512, 4), (2048, 16))
BLK = 128
HEAD_BLOCK = 512
MEM_TOKENS = 256
MEM_HEADS = 4
MEM_WIDTH = MEM_HEADS * HEAD_DIM
LN_EPS = 1e-5
ADAM_LR, ADAM_B1, ADAM_B2, ADAM_EPS, ADAM_WD, ADAM_STEP = 0.001, 0.9, 0.999, 1e-08, 0.01, 10
NEG = -1e30
V7X_VMEM_LIMIT = 56 * 1024 * 1024
V7X_MATMUL_VMEM_BUDGET = 40 * 1024 * 1024
MESH = pl.DeviceIdType.MESH
HBM = pl.BlockSpec(memory_space=pltpu.HBM)


def _call(body, **kw):
    return pl.pallas_call(body, **kw)


def _params(sem):
    return pltpu.CompilerParams(dimension_semantics=sem, vmem_limit_bytes=V7X_VMEM_LIMIT)


def _tile(n, cap):
    best = 0
    for t in range(128, min(n, cap) + 1, 128):
        if n % t == 0:
            best = t
    if best == 0 or (best < 512 and n <= 2560):
        return n
    return best


def _depth(k, per_unit_bytes, fixed_bytes):
    t = _tile(k, 2048)
    while t > 512 and 2 * t * per_unit_bytes + fixed_bytes > V7X_MATMUL_VMEM_BUDGET:
        smaller = _tile(k, t // 2)
        if smaller >= t:
            break
        t = smaller
    return t


def _rows(n, cap):
    best = 8
    for t in range(8, min(n, cap) + 1, 8):
        if n % t == 0:
            best = t
    return best


def _epilogue(epi, acc, extra_ref, out_refs):
    if epi == "plain":
        out_refs[0][...] = acc.astype(out_refs[0].dtype)
    elif epi == "relu2":
        out_refs[0][...] = acc
        r = jnp.maximum(acc, 0.0)
        out_refs[1][...] = (r * r).astype(out_refs[1].dtype)
    elif epi == "drelu2":
        out_refs[0][...] = (acc * (2.0 * jnp.maximum(extra_ref[...], 0.0))).astype(out_refs[0].dtype)
    elif epi == "resid":
        out_refs[0][...] = acc + extra_ref[...]
    else:
        raise ValueError(epi)


def _mm_nn(a, w, *, epi="plain", out_dtype=BF16, extra=None, after=(), name):
    M, K = a.shape
    J, K2, Nj = w.shape
    assert K == K2
    tn = _tile(Nj, 1024)
    tm = _tile(M, 1024 if tn <= 1024 else 512)
    n_out = 2 if epi == "relu2" else 1
    has_extra = extra is not None
    out_bytes = 12 if epi == "relu2" else (8 if epi == "resid" or out_dtype == F32 else 4)
    tk = _depth(K, 2 * (tm + tn), tm * tn * (4 + out_bytes + 8 * has_extra))
    nn, nk = Nj // tn, K // tk

    def body(*refs):
        a_ref, w_ref = refs[0], refs[1]
        extra_ref = refs[2] if has_extra else None
        outs = refs[2 + has_extra + len(after): 2 + has_extra + len(after) + n_out]
        acc_ref = refs[-1]
        k = pl.program_id(3)

        @pl.when(k == 0)
        def _():
            acc_ref[...] = jnp.zeros_like(acc_ref)

        acc_ref[...] += jnp.dot(a_ref[...], w_ref[...], preferred_element_type=F32)

        @pl.when(k == nk - 1)
        def _():
            _epilogue(epi, acc_ref[...], extra_ref, outs)

    omap = lambda i, j, n, k: (i, j * nn + n)
    in_specs = [pl.BlockSpec((tm, tk), lambda i, j, n, k: (i, k)),
                pl.BlockSpec((None, tk, tn), lambda i, j, n, k: (j, k, n))]
    args = [a, w]
    if has_extra:
        in_specs.append(pl.BlockSpec((tm, tn), omap))
        args.append(extra)
    in_specs += [pl.BlockSpec(memory_space=pl.ANY)] * len(after)
    args += list(after)
    if epi == "relu2":
        out_shape = (jax.ShapeDtypeStruct((M, J * Nj), F32), jax.ShapeDtypeStruct((M, J * Nj), BF16))
        out_specs = (pl.BlockSpec((tm, tn), omap), pl.BlockSpec((tm, tn), omap))
    else:
        out_shape = jax.ShapeDtypeStruct((M, J * Nj), F32 if epi == "resid" else out_dtype)
        out_specs = pl.BlockSpec((tm, tn), omap)
    return _call(body, name=name, grid=(M // tm, J, nn, nk), in_specs=in_specs, out_specs=out_specs,
                 out_shape=out_shape, scratch_shapes=[pltpu.VMEM((tm, tn), F32)],
                 compiler_params=_params(("parallel", "parallel", "parallel", "arbitrary")))(*args)


def _mm_nt(a, w, *, epi="plain", out_dtype=BF16, extra=None, after=(), name):
    M, N = a.shape
    J, K, Nj = w.shape
    assert N == J * Nj
    tko = _tile(K, 1024)
    has_extra = extra is not None
    out_bytes = 8 if epi == "resid" or out_dtype == F32 else 4
    for tm in (_tile(M, 1024), _tile(M, 512)):
        fixed = tm * tko * (4 + out_bytes + 8 * has_extra)
        tc = _depth(Nj, 2 * (tm + tko), fixed)
        if 2 * tc * 2 * (tm + tko) + fixed <= V7X_MATMUL_VMEM_BUDGET:
            break
    nc = Nj // tc

    def body(*refs):
        a_ref, w_ref = refs[0], refs[1]
        extra_ref = refs[2] if has_extra else None
        outs = refs[2 + has_extra + len(after): 3 + has_extra + len(after)]
        acc_ref = refs[-1]
        j, c = pl.program_id(2), pl.program_id(3)

        @pl.when(jnp.logical_and(j == 0, c == 0))
        def _():
            acc_ref[...] = jnp.zeros_like(acc_ref)

        acc_ref[...] += lax.dot_general(a_ref[...], w_ref[...], (((1,), (1,)), ((), ())),
                                        preferred_element_type=F32)

        @pl.when(jnp.logical_and(j == J - 1, c == nc - 1))
        def _():
            _epilogue(epi, acc_ref[...], extra_ref, outs)

    omap = lambda i, ko, j, c: (i, ko)
    in_specs = [pl.BlockSpec((tm, tc), lambda i, ko, j, c: (i, j * nc + c)),
                pl.BlockSpec((None, tko, tc), lambda i, ko, j, c: (j, ko, c))]
    args = [a, w]
    if has_extra:
        in_specs.append(pl.BlockSpec((tm, tko), omap))
        args.append(extra)
    in_specs += [pl.BlockSpec(memory_space=pl.ANY)] * len(after)
    args += list(after)
    out_shape = jax.ShapeDtypeStruct((M, K), F32 if epi == "resid" else out_dtype)
    return _call(body, name=name, grid=(M // tm, K // tko, J, nc), in_specs=in_specs,
                 out_specs=pl.BlockSpec((tm, tko), omap), out_shape=out_shape,
                 scratch_shapes=[pltpu.VMEM((tm, tko), F32)],
                 compiler_params=_params(("parallel", "parallel", "arbitrary", "arbitrary")))(*args)


def _mm_tn(a, b, J, *, name):
    T, K = a.shape
    T2, N = b.shape
    assert T == T2 and N % J == 0
    Nj = N // J
    tn = _tile(Nj, 1024)
    tkr = _tile(K, 1024 if tn <= 1024 else 512)
    tt = _depth(T, 2 * (tkr + tn), tkr * tn * (4 + 4))
    nn, nt = Nj // tn, T // tt

    def body(a_ref, b_ref, o_ref, acc_ref):
        t = pl.program_id(3)

        @pl.when(t == 0)
        def _():
            acc_ref[...] = jnp.zeros_like(acc_ref)

        acc_ref[...] += lax.dot_general(a_ref[...], b_ref[...], (((0,), (0,)), ((), ())),
                                        preferred_element_type=F32)

        @pl.when(t == nt - 1)
        def _():
            o_ref[...] = acc_ref[...].astype(o_ref.dtype)

    return _call(body, name=name, grid=(J, K // tkr, nn, nt),
                 in_specs=[pl.BlockSpec((tt, tkr), lambda j, kr, n, t: (t, kr)),
                           pl.BlockSpec((tt, tn), lambda j, kr, n, t: (t, j * nn + n))],
                 out_specs=pl.BlockSpec((None, tkr, tn), lambda j, kr, n, t: (j, kr, n)),
                 out_shape=jax.ShapeDtypeStruct((J, K, Nj), BF16),
                 scratch_shapes=[pltpu.VMEM((tkr, tn), F32)],
                 compiler_params=_params(("parallel", "parallel", "parallel", "arbitrary")))(a, b)


def _ln_fwd(x, y, g, b, alpha, *, name):
    T, D = x.shape
    tr = _rows(T, 512)

    def body(x_ref, y_ref, g_ref, b_ref, o_ref, ob_ref):
        r = alpha * x_ref[...] + y_ref[...]
        mu = jnp.mean(r, axis=-1, keepdims=True)
        xc = r - mu
        var = jnp.mean(xc * xc, axis=-1, keepdims=True)
        o = xc * lax.rsqrt(var + LN_EPS) * g_ref[...] + b_ref[...]
        o_ref[...] = o
        ob_ref[...] = o.astype(BF16)

    row = pl.BlockSpec((tr, D), lambda i: (i, 0))
    vec = pl.BlockSpec((1, D), lambda i: (0, 0))
    return _call(body, name=name, grid=(T // tr,), in_specs=[row, row, vec, vec], out_specs=(row, row),
                 out_shape=(jax.ShapeDtypeStruct((T, D), F32), jax.ShapeDtypeStruct((T, D), BF16)),
                 compiler_params=_params(("parallel",)))(x, y, g.reshape(1, D), b.reshape(1, D))


def _ln_bwd(dout, x, y, g, alpha, after=(), *, name):
    T, D = x.shape
    tr = _rows(T, 512)
    na = len(after)

    def body(*refs):
        do_ref, x_ref, y_ref, g_ref = refs[:4]
        drb_ref, adr_ref, dg_ref, db_ref = refs[4 + na:]
        i = pl.program_id(0)
        r = alpha * x_ref[...] + y_ref[...]
        mu = jnp.mean(r, axis=-1, keepdims=True)
        xc = r - mu
        var = jnp.mean(xc * xc, axis=-1, keepdims=True)
        rstd = lax.rsqrt(var + LN_EPS)
        xhat = xc * rstd
        do = do_ref[...]
        dxh = do * g_ref[...]
        dr = rstd * (dxh - jnp.mean(dxh, axis=-1, keepdims=True)
                     - xhat * jnp.mean(dxh * xhat, axis=-1, keepdims=True))
        drb_ref[...] = dr.astype(BF16)
        adr_ref[...] = alpha * dr

        @pl.when(i == 0)
        def _():
            dg_ref[...] = jnp.zeros_like(dg_ref)
            db_ref[...] = jnp.zeros_like(db_ref)

        dg_ref[...] += jnp.sum(do * xhat, axis=0, keepdims=True)
        db_ref[...] += jnp.sum(do, axis=0, keepdims=True)

    row = pl.BlockSpec((tr, D), lambda i: (i, 0))
    vec = pl.BlockSpec((1, D), lambda i: (0, 0))
    return _call(body, name=name, grid=(T // tr,),
                 in_specs=[row, row, row, vec] + [pl.BlockSpec(memory_space=pl.ANY)] * na,
                 out_specs=(row, row, vec, vec),
                 out_shape=(jax.ShapeDtypeStruct((T, D), BF16), jax.ShapeDtypeStruct((T, D), F32),
                            jax.ShapeDtypeStruct((1, D), F32), jax.ShapeDtypeStruct((1, D), F32)),
                 compiler_params=_params(("arbitrary",)))(dout, x, y, g.reshape(1, D), *after)


def _loss_head(xf, target, *, name):
    T, D = xf.shape
    tr = _rows(T, 256)

    def body(x_ref, t_ref, dy_ref, s_ref):
        i = pl.program_id(0)
        err = x_ref[...] - t_ref[...]
        dy_ref[...] = err * (1.0 / D)

        @pl.when(i == 0)
        def _():
            s_ref[...] = jnp.zeros_like(s_ref)

        s_ref[...] += jnp.sum(jnp.sum(err * err, axis=1, keepdims=True), axis=0, keepdims=True)

    row = pl.BlockSpec((tr, D), lambda i: (i, 0))
    return _call(body, name=name, grid=(T // tr,), in_specs=[row, row],
                 out_specs=(row, pl.BlockSpec((8, 128), lambda i: (0, 0))),
                 out_shape=(jax.ShapeDtypeStruct((T, D), F32), jax.ShapeDtypeStruct((8, 128), F32)),
                 compiler_params=_params(("arbitrary",)))(xf, target)


def _adamw(w, g, m, v, *, name):
    R, C = w.shape
    tr = _rows(R, max(8, (1 << 18) // C // 8 * 8))

    def body(w_ref, g_ref, m_ref, v_ref, d_ref, nm_ref, nv_ref):
        g_ = g_ref[...]
        m_ = ADAM_B1 * m_ref[...] + (1.0 - ADAM_B1) * g_
        v_ = ADAM_B2 * v_ref[...] + (1.0 - ADAM_B2) * (g_ * g_)
        m_hat = m_ / (1.0 - ADAM_B1 ** ADAM_STEP)
        v_hat = v_ / (1.0 - ADAM_B2 ** ADAM_STEP)
        d_ref[...] = -ADAM_LR * (m_hat / (jnp.sqrt(v_hat) + ADAM_EPS) + ADAM_WD * w_ref[...])
        nm_ref[...] = m_
        nv_ref[...] = v_

    blk = pl.BlockSpec((tr, C), lambda i: (i, 0))
    sds = jax.ShapeDtypeStruct((R, C), F32)
    return _call(body, name=name, grid=(R // tr,), in_specs=[blk] * 4, out_specs=(blk,) * 3,
                 out_shape=(sds,) * 3, compiler_params=_params(("parallel",)))(w, g, m, v)


def _dot_nt(a, b):
    return lax.dot_general(a, b, (((1,), (1,)), ((), ())), preferred_element_type=F32)


def _dot_tn(a, b):
    return lax.dot_general(a, b, (((0,), (0,)), ((), ())), preferred_element_type=F32)


def _band_masks():
    qi = lax.broadcasted_iota(jnp.int32, (BLK, BLK), 0)
    kj = lax.broadcasted_iota(jnp.int32, (BLK, BLK), 1)
    return qi >= kj, kj >= qi


def _blocks_per_step(nb):
    return next(t for t in (4, 2, 1) if nb % t == 0)


def _attn_fwd(P, grp, d, a_out, *, name):
    T, C = P.shape
    L = T // d
    nb = L // BLK
    nhh = a_out // HEAD_BLOCK
    cb = C // HEAD_BLOCK
    q0 = grp * 3 * nhh
    scale = HEAD_DIM ** -0.5
    tb = _blocks_per_step(nb)

    def body(q_ref, kc_ref, kp_ref, vc_ref, vp_ref, o_ref, l_ref):
        b = pl.program_id(1)
        mask_c, mask_prev = _band_masks()
        for sub in range(tb):
            rs = slice(sub * BLK, (sub + 1) * BLK)
            ps = slice((sub - 1) * BLK, sub * BLK)
            mask_p = jnp.logical_and(mask_prev, b > 0) if sub == 0 else mask_prev
            for h in range(HEAD_BLOCK // HEAD_DIM):
                hs = slice(h * HEAD_DIM, (h + 1) * HEAD_DIM)
                q = q_ref[rs, hs]
                kp, vp = (kp_ref[:, hs], vp_ref[:, hs]) if sub == 0 else (kc_ref[ps, hs], vc_ref[ps, hs])
                s_c = jnp.where(mask_c, _dot_nt(q, kc_ref[rs, hs]) * scale, NEG)
                s_p = jnp.where(mask_p, _dot_nt(q, kp) * scale, NEG)
                m = jnp.maximum(jnp.max(s_c, axis=1, keepdims=True), jnp.max(s_p, axis=1, keepdims=True))
                p_c = jnp.exp(s_c - m)
                p_p = jnp.exp(s_p - m)
                l = jnp.sum(p_c, axis=1, keepdims=True) + jnp.sum(p_p, axis=1, keepdims=True)
                o = (jnp.dot(p_c.astype(BF16), vc_ref[rs, hs], preferred_element_type=F32)
                     + jnp.dot(p_p.astype(BF16), vp, preferred_element_type=F32))
                o_ref[rs, hs] = o / l
                l_ref[rs, hs] = jnp.broadcast_to(m + jnp.log(l), (BLK, HEAD_DIM))

    def cur(part):
        return pl.BlockSpec((tb * BLK, HEAD_BLOCK), lambda r, b, hh: (b, r * cb + q0 + part * nhh + hh))

    def prev(part):
        return pl.BlockSpec((BLK, HEAD_BLOCK),
                            lambda r, b, hh: (jnp.maximum(b * tb - 1, 0), r * cb + q0 + part * nhh + hh))

    out = pl.BlockSpec((tb * BLK, HEAD_BLOCK), lambda r, b, hh: (b, r * nhh + hh))
    Pv = P.reshape(L, d * C)
    o, lse = _call(body, name=name, grid=(d, nb // tb, nhh), in_specs=[cur(0), cur(1), prev(1), cur(2), prev(2)],
                   out_specs=(out, out),
                   out_shape=(jax.ShapeDtypeStruct((L, d * a_out), F32),) * 2,
                   compiler_params=_params(("parallel", "parallel", "parallel")))(Pv, Pv, Pv, Pv, Pv)
    return o.reshape(T, a_out), lse.reshape(T, a_out)


def _attn_combine(os_, lses, *, name):
    T, W = os_[0].shape
    tr = _rows(T, 256)
    n = len(os_)

    def body(*refs):
        o_refs, l_refs = refs[:n], refs[n:2 * n]
        mix_ref, lse_ref = refs[2 * n], refs[2 * n + 1]
        ls = [r[...] for r in l_refs]
        m = functools.reduce(jnp.maximum, ls)
        es = [jnp.exp(l - m) for l in ls]
        tot = functools.reduce(lambda a, b: a + b, es)
        mix = functools.reduce(lambda a, b: a + b, [(e / tot) * o[...] for e, o in zip(es, o_refs)])
        mix_ref[...] = mix
        lse_ref[...] = m + jnp.log(tot)

    blk = pl.BlockSpec((tr, W), lambda i: (i, 0))
    sds = jax.ShapeDtypeStruct((T, W), F32)
    return _call(body, name=name, grid=(T // tr,), in_specs=[blk] * (2 * n), out_specs=(blk, blk),
                 out_shape=(sds, sds), compiler_params=_params(("parallel",)))(*os_, *lses)


def _attn_bwd(P, dO, O, LSE, grp, d, a_out, *, name):
    T, C = P.shape
    L = T // d
    nb = L // BLK
    nhh = a_out // HEAD_BLOCK
    cb = C // HEAD_BLOCK
    q0 = grp * 3 * nhh
    scale = HEAD_DIM ** -0.5
    tb = _blocks_per_step(nb)
    nsteps = nb // tb

    def body(q_ref, qn_ref, kc_ref, kp_ref, vc_ref, vp_ref, do_ref, don_ref, o_ref, on_ref, l_ref, ln_ref,
             dq_ref, dk_ref, dv_ref):
        b = pl.program_id(1)
        mask_c, mask_prev = _band_masks()
        for sub in range(tb):
            rs = slice(sub * BLK, (sub + 1) * BLK)
            ps = slice((sub - 1) * BLK, sub * BLK)
            ns = slice((sub + 1) * BLK, (sub + 2) * BLK)
            first, last = sub == 0, sub == tb - 1
            mask_p = jnp.logical_and(mask_prev, b > 0) if first else mask_prev
            mask_n = jnp.logical_and(mask_prev, b < nsteps - 1) if last else mask_prev
            for h in range(HEAD_BLOCK // HEAD_DIM):
                hs = slice(h * HEAD_DIM, (h + 1) * HEAD_DIM)
                q, kc, vc = q_ref[rs, hs], kc_ref[rs, hs], vc_ref[rs, hs]
                kp, vp = (kp_ref[:, hs], vp_ref[:, hs]) if first else (kc_ref[ps, hs], vc_ref[ps, hs])
                qn = qn_ref[:, hs] if last else q_ref[ns, hs]
                do = do_ref[rs, hs]
                don = don_ref[:, hs] if last else do_ref[ns, hs]
                on = on_ref[:, hs] if last else o_ref[ns, hs]
                lse = l_ref[rs, hs]
                lse_n = ln_ref[:, hs] if last else l_ref[ns, hs]
                delta = jnp.sum(do * o_ref[rs, hs], axis=1, keepdims=True)
                delta_n = jnp.sum(don * on, axis=1, keepdims=True)
                dob, donb = do.astype(BF16), don.astype(BF16)
                p_c = jnp.exp(jnp.where(mask_c, _dot_nt(q, kc) * scale, NEG) - lse)
                p_p = jnp.exp(jnp.where(mask_p, _dot_nt(q, kp) * scale, NEG) - lse)
                p_n = jnp.exp(jnp.where(mask_n, _dot_nt(qn, kc) * scale, NEG) - lse_n)
                ds_c = (p_c * (_dot_nt(dob, vc) - delta) * scale).astype(BF16)
                ds_p = (p_p * (_dot_nt(dob, vp) - delta) * scale).astype(BF16)
                ds_n = (p_n * (_dot_nt(donb, vc) - delta_n) * scale).astype(BF16)
                dq = jnp.dot(ds_c, kc, preferred_element_type=F32) + jnp.dot(ds_p, kp, preferred_element_type=F32)
                dk = _dot_tn(ds_c, q) + _dot_tn(ds_n, qn)
                dv = _dot_tn(p_c.astype(BF16), dob) + _dot_tn(p_n.astype(BF16), donb)
                dq_ref[rs, hs] = dq.astype(BF16)
                dk_ref[rs, hs] = dk.astype(BF16)
                dv_ref[rs, hs] = dv.astype(BF16)

    def rows(shift):
        if shift == 0:
            return tb * BLK, lambda b: b
        return BLK, (lambda b: jnp.maximum(b * tb - 1, 0)) if shift < 0 else (lambda b: jnp.minimum((b + 1) * tb, nb - 1))

    def pspec(part, shift):
        size, row = rows(shift)
        return pl.BlockSpec((size, HEAD_BLOCK), lambda r, b, hh: (row(b), r * cb + q0 + part * nhh + hh))

    def aspec(shift, width=a_out):
        per = width // HEAD_BLOCK
        size, row = rows(shift)
        return pl.BlockSpec((size, HEAD_BLOCK), lambda r, b, hh: (row(b), r * per + hh))

    Pv = P.reshape(L, d * C)
    w_do = dO.shape[1]
    dOv = dO.reshape(L, d * w_do)
    Ov, Lv = (t.reshape(L, d * a_out) for t in (O, LSE))
    sds = jax.ShapeDtypeStruct((L, d * a_out), BF16)
    dq, dk, dv = _call(
        body, name=name, grid=(d, nsteps, nhh),
        in_specs=[pspec(0, 0), pspec(0, 1), pspec(1, 0), pspec(1, -1), pspec(2, 0), pspec(2, -1),
                  aspec(0, w_do), aspec(1, w_do), aspec(0), aspec(1), aspec(0), aspec(1)],
        out_specs=(aspec(0),) * 3, out_shape=(sds,) * 3,
        compiler_params=_params(("parallel", "parallel", "parallel")))(Pv, Pv, Pv, Pv, Pv, Pv, dOv, dOv, Ov, Ov, Lv, Lv)
    return [t.reshape(T, a_out) for t in (dq, dk, dv)]


def _mem_softmax(q, k, scale):
    s = _dot_nt(q, k) * scale
    e = jnp.exp(s - jnp.max(s, axis=1, keepdims=True))
    return e / jnp.sum(e, axis=1, keepdims=True)


def _mem_fwd(P, qcol, kv, mix, *, name):
    T, W = mix.shape
    tq = _rows(T, 512)
    scale = HEAD_DIM ** -0.5

    def body(q_ref, kv_ref, mix_ref, o_ref):
        o_ref[:, :W] = mix_ref[...].astype(BF16)
        for h in range(MEM_HEADS):
            hs = slice(h * HEAD_DIM, (h + 1) * HEAD_DIM)
            vs = slice(MEM_WIDTH + h * HEAD_DIM, MEM_WIDTH + (h + 1) * HEAD_DIM)
            p = _mem_softmax(q_ref[:, hs].astype(BF16), kv_ref[:, hs], scale)
            o = jnp.dot(p.astype(BF16), kv_ref[:, vs], preferred_element_type=F32)
            o_ref[:, W + h * HEAD_DIM:W + (h + 1) * HEAD_DIM] = o.astype(BF16)

    return _call(body, name=name, grid=(T // tq,),
                 in_specs=[pl.BlockSpec((tq, MEM_WIDTH), lambda i: (i, qcol)),
                           pl.BlockSpec((MEM_TOKENS, 2 * MEM_WIDTH), lambda i: (0, 0)),
                           pl.BlockSpec((tq, W), lambda i: (i, 0))],
                 out_specs=pl.BlockSpec((tq, W + MEM_WIDTH), lambda i: (i, 0)),
                 out_shape=jax.ShapeDtypeStruct((T, W + MEM_WIDTH), BF16),
                 compiler_params=_params(("parallel",)))(P, kv, mix)


def _mem_bwd(P, qcol, kv, dcat, ocol, *, name):
    T = P.shape[0]
    tq = _rows(T, 512)
    scale = HEAD_DIM ** -0.5

    def body(q_ref, kv_ref, do_ref, dq_ref, dkv_ref):
        i = pl.program_id(0)

        @pl.when(i == 0)
        def _():
            dkv_ref[...] = jnp.zeros_like(dkv_ref)

        for h in range(MEM_HEADS):
            hs = slice(h * HEAD_DIM, (h + 1) * HEAD_DIM)
            vs = slice(MEM_WIDTH + h * HEAD_DIM, MEM_WIDTH + (h + 1) * HEAD_DIM)
            q = q_ref[:, hs].astype(BF16)
            k, v = kv_ref[:, hs], kv_ref[:, vs]
            do = do_ref[:, hs].astype(BF16)
            p = _mem_softmax(q, k, scale)
            dp = _dot_nt(do, v)
            ds = (p * (dp - jnp.sum(p * dp, axis=1, keepdims=True)) * scale).astype(BF16)
            dq_ref[:, hs] = jnp.dot(ds, k, preferred_element_type=F32).astype(BF16)
            dkv_ref[:, hs] += _dot_tn(ds, q)
            dkv_ref[:, vs] += _dot_tn(p.astype(BF16), do)

    return _call(body, name=name, grid=(T // tq,),
                 in_specs=[pl.BlockSpec((tq, MEM_WIDTH), lambda i: (i, qcol)),
                           pl.BlockSpec((MEM_TOKENS, 2 * MEM_WIDTH), lambda i: (0, 0)),
                           pl.BlockSpec((tq, MEM_WIDTH), lambda i: (i, ocol))],
                 out_specs=(pl.BlockSpec((tq, MEM_WIDTH), lambda i: (i, 0)),
                            pl.BlockSpec((MEM_TOKENS, 2 * MEM_WIDTH), lambda i: (0, 0))),
                 out_shape=(jax.ShapeDtypeStruct((T, MEM_WIDTH), BF16),
                            jax.ShapeDtypeStruct((MEM_TOKENS, 2 * MEM_WIDTH), F32)),
                 compiler_params=_params(("arbitrary",)))(P, kv, dcat)


_SQRT_HALF = 0.7071067811865476
_INV_SQRT_2PI = 0.3989422804014327


def _gelu(x):
    return 0.5 * x * (1.0 + lax.erf(x * _SQRT_HALF))


def _gelu_grad(x):
    return 0.5 * (1.0 + lax.erf(x * _SQRT_HALF)) + x * (_INV_SQRT_2PI * jnp.exp(-0.5 * x * x))


def _tril():
    t = lax.broadcasted_iota(jnp.int32, (BLK, BLK), 0)
    s = lax.broadcasted_iota(jnp.int32, (BLK, BLK), 1)
    return t >= s


def _gmlp_fwd(P, w_s, bs_t, vg, vb, width, *, name):
    T = P.shape[0]
    G = width // HEAD_DIM
    tb = _rows(T, 512)

    def body(pu_ref, pv_ref, ws_ref, bs_ref, vg_ref, vb_ref, o_ref):
        u = _gelu(pu_ref[...])
        v = _gelu(pv_ref[...])
        mu = jnp.mean(v, axis=-1, keepdims=True)
        vc = v - mu
        var = jnp.mean(vc * vc, axis=-1, keepdims=True)
        vn = (vc * lax.rsqrt(var + LN_EPS) * vg_ref[...] + vb_ref[...]).astype(BF16)
        tril = _tril()
        for g in range(G):
            gs = slice(g * HEAD_DIM, (g + 1) * HEAD_DIM)
            ws = jnp.where(tril, ws_ref[g], 0.0).astype(BF16)
            bias = bs_ref[:, g:g + 1]
            for c in range(tb // BLK):
                cs = slice(c * BLK, (c + 1) * BLK)
                sg = jnp.dot(ws, vn[cs, gs], preferred_element_type=F32) + bias
                o_ref[cs, gs] = u[cs, gs] * sg

    blk = lambda col: pl.BlockSpec((tb, width), lambda i: (i, col))
    full = lambda shape: pl.BlockSpec(shape, lambda i: (0,) * len(shape))
    return _call(body, name=name, grid=(T // tb,),
                 in_specs=[blk(0), blk(1), full((G, BLK, BLK)), full((BLK, G)), full((1, width)), full((1, width))],
                 out_specs=blk(0), out_shape=jax.ShapeDtypeStruct((T, width), F32),
                 compiler_params=_params(("parallel",)))(P, P, w_s, bs_t, vg.reshape(1, width), vb.reshape(1, width))


def _gmlp_bwd(P, w_s, bs_t, vg, vb, dcat, width, *, name):
    T = P.shape[0]
    G = width // HEAD_DIM
    tb = _rows(T, 512)

    def body(pu_ref, pv_ref, ws_ref, bs_ref, vg_ref, vb_ref, dm_ref, dpu_ref, dpv_ref, dws_ref, dbs_ref, dvg_ref,
             dvb_ref, dvn_ref):
        i = pl.program_id(0)

        @pl.when(i == 0)
        def _():
            dws_ref[...] = jnp.zeros_like(dws_ref)
            dbs_ref[...] = jnp.zeros_like(dbs_ref)
            dvg_ref[...] = jnp.zeros_like(dvg_ref)
            dvb_ref[...] = jnp.zeros_like(dvb_ref)

        pu, pv = pu_ref[...], pv_ref[...]
        u = _gelu(pu)
        v = _gelu(pv)
        mu = jnp.mean(v, axis=-1, keepdims=True)
        vc = v - mu
        var = jnp.mean(vc * vc, axis=-1, keepdims=True)
        rstd = lax.rsqrt(var + LN_EPS)
        xhat = vc * rstd
        vn = (xhat * vg_ref[...] + vb_ref[...]).astype(BF16)
        dm = dm_ref[...]
        tril = _tril()
        lane = lax.broadcasted_iota(jnp.int32, (BLK, BLK), 1)
        dbs = jnp.zeros((BLK, BLK), F32)
        for g in range(G):
            gs = slice(g * HEAD_DIM, (g + 1) * HEAD_DIM)
            ws = jnp.where(tril, ws_ref[g], 0.0).astype(BF16)
            bias = bs_ref[:, g:g + 1]
            dws = jnp.zeros((BLK, BLK), F32)
            rs = jnp.zeros((BLK, 1), F32)
            for c in range(tb // BLK):
                cs = slice(c * BLK, (c + 1) * BLK)
                vn_cg = vn[cs, gs]
                sg = jnp.dot(ws, vn_cg, preferred_element_type=F32) + bias
                dm_cg = dm[cs, gs]
                dpu_ref[cs, gs] = (dm_cg * sg * _gelu_grad(pu[cs, gs])).astype(BF16)
                dsg = dm_cg * u[cs, gs]
                dsgb = dsg.astype(BF16)
                dvn_ref[cs, gs] = _dot_tn(ws, dsgb)
                dws = dws + _dot_nt(dsgb, vn_cg)
                rs = rs + jnp.sum(dsg, axis=1, keepdims=True)
            dws_ref[g] += jnp.where(tril, dws, 0.0)
            dbs = dbs + jnp.where(lane == g, rs, 0.0)
        dbs_ref[...] += dbs
        dvn = dvn_ref[...]
        dxh = dvn * vg_ref[...]
        dv = rstd * (dxh - jnp.mean(dxh, axis=-1, keepdims=True)
                     - xhat * jnp.mean(dxh * xhat, axis=-1, keepdims=True))
        dpv_ref[...] = (dv * _gelu_grad(pv)).astype(BF16)
        dvg_ref[...] += jnp.sum(dvn * xhat, axis=0, keepdims=True)
        dvb_ref[...] += jnp.sum(dvn, axis=0, keepdims=True)

    blk = lambda col: pl.BlockSpec((tb, width), lambda i: (i, col))
    full = lambda shape: pl.BlockSpec(shape, lambda i: (0,) * len(shape))
    return _call(body, name=name, grid=(T // tb,),
                 in_specs=[blk(0), blk(1), full((G, BLK, BLK)), full((BLK, G)), full((1, width)), full((1, width)),
                           blk(0)],
                 out_specs=(blk(0), blk(0), full((G, BLK, BLK)), full((BLK, BLK)), full((1, width)), full((1, width))),
                 out_shape=(jax.ShapeDtypeStruct((T, width), BF16), jax.ShapeDtypeStruct((T, width), BF16),
                            jax.ShapeDtypeStruct((G, BLK, BLK), F32), jax.ShapeDtypeStruct((BLK, BLK), F32),
                            jax.ShapeDtypeStruct((1, width), F32), jax.ShapeDtypeStruct((1, width), F32)),
                 scratch_shapes=[pltpu.VMEM((tb, width), F32)],
                 compiler_params=_params(("arbitrary",)))(P, P, w_s, bs_t, vg.reshape(1, width), vb.reshape(1, width), dcat)


def _place():
    x, y, c = lax.axis_index("x"), lax.axis_index("y"), lax.axis_index("c")
    chips = [(1 - x, y), (x, 1 - y), (1 - x, 1 - y)]
    return x, y, c, 2 * x + y, chips, [2 * px + py for px, py in chips]


def _half(ref, c, rows):
    return ref.at[pl.ds(c * (rows // 2), rows // 2)]


def _place_shard(w, chip, dtype, *, name):
    R, C = w.shape
    tr = _rows(R, max(8, (1 << 19) // C // 8 * 8)) if R % 8 == 0 else R

    def body(chip_ref, w_ref, o_ref):
        o_ref[...] = w_ref[...].astype(dtype)

    grid_spec = pltpu.PrefetchScalarGridSpec(
        num_scalar_prefetch=1, grid=(R // tr,), in_specs=[pl.BlockSpec((tr, C), lambda i, s: (i, 0))],
        out_specs=pl.BlockSpec((None, tr, C), lambda i, s: (s[0], i, 0)))
    return _call(body, name=name, grid_spec=grid_spec, out_shape=jax.ShapeDtypeStruct((4, R, C), dtype),
                 compiler_params=_params(("parallel",)))(chip, w)


SEM = pl.BlockSpec(memory_space=pltpu.SEMAPHORE)
EFFECT = pltpu.SideEffectType.DATAFLOW_SIDE_EFFECTING


def _hbm(a):
    return pltpu.with_memory_space_constraint(a, pltpu.HBM)


def _sibling_start(arrays, copies, ncopies, after, *, name):
    n, na = len(arrays), len(after)

    def body(*refs):
        a = refs[:n]
        send, recv = refs[n + na], refs[n + na + 1]
        token = refs[2 * n + na + 2]
        x, y, c, _, _, _ = _place()
        for q, (src, dst) in enumerate(copies(a, c)):
            pltpu.make_async_remote_copy(src_ref=src, dst_ref=dst, send_sem=send.at[q], recv_sem=recv.at[q],
                                         device_id=(x, y, 1 - c), device_id_type=MESH).start()
        token[...] = jnp.zeros_like(token)

    outs = _call(body, name=name, in_specs=[HBM] * n + [pl.BlockSpec(memory_space=pl.ANY)] * na,
                 out_specs=(SEM, SEM, *([HBM] * n), pl.BlockSpec(memory_space=pltpu.VMEM)),
                 out_shape=(pltpu.SemaphoreType.DMA((ncopies,)), pltpu.SemaphoreType.DMA((ncopies,)),
                            *[pltpu.HBM(a.shape, a.dtype) for a in arrays], jax.ShapeDtypeStruct((8, 128), F32)),
                 input_output_aliases={t: 2 + t for t in range(n)},
                 compiler_params=pltpu.CompilerParams(has_side_effects=EFFECT))(*[_hbm(a) for a in arrays], *after)
    return outs[0], outs[1], list(outs[2:2 + n]), outs[2 + n]


def _sibling_wait(send, recv, arrays, copies, after, *, name):
    n, na = len(arrays), len(after)

    def body(*refs):
        a = refs[:n]
        send, recv = refs[n], refs[n + 1]
        x, y, c, _, _, _ = _place()
        for q, ((src, dst), (_, landed)) in enumerate(zip(copies(a, c), copies(a, 1 - c))):
            pltpu.make_async_remote_copy(src_ref=src, dst_ref=dst, send_sem=send.at[q], recv_sem=recv.at[q],
                                         device_id=(x, y, 1 - c), device_id_type=MESH).wait_send()
            pltpu.make_async_remote_copy(src_ref=src, dst_ref=landed, send_sem=send.at[q], recv_sem=recv.at[q],
                                         device_id=(x, y, 1 - c), device_id_type=MESH).wait_recv()

    outs = _call(body, name=name, in_specs=[HBM] * n + [SEM, SEM] + [pl.BlockSpec(memory_space=pl.ANY)] * na,
                 out_specs=[HBM] * n, out_shape=[pltpu.HBM(a.shape, a.dtype) for a in arrays],
                 input_output_aliases={t: t for t in range(n)},
                 compiler_params=pltpu.CompilerParams(has_side_effects=EFFECT))(*arrays, send, recv, *after)
    return list(outs)


def _pair_copies(n):
    def copies(a, core):
        out = []
        for t in range(n):
            h = a[t].shape[1] // 2
            out.append((a[t].at[:, pl.ds((1 - core) * h, h)], a[n + t]))
        return out
    return copies


def _swap_copies(a, core):
    out = []
    for ref in a:
        h = ref.shape[0] // 2
        out.append((ref.at[pl.ds(core * h, h)],) * 2)
    return out


def _gather_windows(shapes):
    split = [s[1] % 16 == 0 for s in shapes]

    def window(ref, t, chip_idx, core):
        w = ref.at[chip_idx]
        return _half(w, core, shapes[t][1]) if split[t] else w

    return split, window


def _gather_start(bufs, after, *, name):
    n = len(bufs)
    split, window = _gather_windows([b.shape for b in bufs])

    na = len(after)

    def body(*refs):
        b = refs[:n]
        send, recv = refs[n + na], refs[n + na + 1]
        token = refs[2 * n + na + 2]
        x, y, c, j, chips, pj = _place()
        for t in range(n):
            for k in range(3):
                pltpu.make_async_remote_copy(src_ref=window(b[t], t, j, c), dst_ref=window(b[t], t, j, c),
                                             send_sem=send.at[3 * t + k], recv_sem=recv.at[3 * t + k],
                                             device_id=(*chips[k], c), device_id_type=MESH).start()
        token[...] = jnp.zeros_like(token)

    outs = _call(body, name=name, in_specs=[HBM] * n + [pl.BlockSpec(memory_space=pl.ANY)] * na,
                 out_specs=(SEM, SEM, *([HBM] * n), pl.BlockSpec(memory_space=pltpu.VMEM)),
                 out_shape=(pltpu.SemaphoreType.DMA((3 * n,)), pltpu.SemaphoreType.DMA((3 * n,)),
                            *[pltpu.HBM(b.shape, b.dtype) for b in bufs], jax.ShapeDtypeStruct((8, 128), F32)),
                 input_output_aliases={t: 2 + t for t in range(n)},
                 compiler_params=pltpu.CompilerParams(has_side_effects=EFFECT))(*[_hbm(b) for b in bufs], *after)
    return outs[0], outs[1], list(outs[2:2 + n]), outs[2 + n]


def _gather_wait(send, recv, bufs, after, *, name):
    n = len(bufs)
    split, window = _gather_windows([b.shape for b in bufs])

    def body(*refs):
        b = refs[:n]
        send, recv = refs[n], refs[n + 1]
        x, y, c, j, chips, pj = _place()
        for t in range(n):
            for k in range(3):
                out = pltpu.make_async_remote_copy(src_ref=window(b[t], t, j, c), dst_ref=window(b[t], t, j, c),
                                                   send_sem=send.at[3 * t + k], recv_sem=recv.at[3 * t + k],
                                                   device_id=(*chips[k], c), device_id_type=MESH)
                out.wait_send()
                back = pltpu.make_async_remote_copy(src_ref=window(b[t], t, pj[k], c), dst_ref=window(b[t], t, pj[k], c),
                                                    send_sem=send.at[3 * t + k], recv_sem=recv.at[3 * t + k],
                                                    device_id=(*chips[k], c), device_id_type=MESH)
                back.wait_recv()

    outs = _call(body, name=name, in_specs=[HBM] * n + [SEM, SEM] + [pl.BlockSpec(memory_space=pl.ANY)] * len(after),
                 out_specs=[HBM] * n, out_shape=[pltpu.HBM(b.shape, b.dtype) for b in bufs],
                 input_output_aliases={t: t for t in range(n)},
                 compiler_params=pltpu.CompilerParams(has_side_effects=EFFECT))(*bufs, send, recv, *after)
    return list(outs)


def _pass_copies(shapes):
    split, window = _gather_windows(shapes)

    def copies(a, core):
        _, _, _, _, _, pj = _place()
        return [(window(a[t], t, pj[k], core),) * 2 for t in range(len(shapes)) if split[t] for k in range(3)]

    return copies, 3 * sum(split)


def _gather_pass(bufs, *, name):
    n = len(bufs)
    split, window = _gather_windows([b.shape for b in bufs])
    idx = [t for t in range(n) if split[t]]

    def body(*refs):
        b = refs[n:2 * n]
        send, recv = refs[2 * n:]
        x, y, c, j, chips, pj = _place()

        def d2d(u, t, k, core):
            w = window(b[t], t, pj[k], core)
            return pltpu.make_async_remote_copy(src_ref=w, dst_ref=w, send_sem=send.at[3 * u + k],
                                                recv_sem=recv.at[3 * u + k], device_id=(x, y, 1 - c),
                                                device_id_type=MESH)

        sent = [d2d(u, t, k, c) for u, t in enumerate(idx) for k in range(3)]
        for cp in sent:
            cp.start()
        for u, t in enumerate(idx):
            for k in range(3):
                d2d(u, t, k, 1 - c).wait_recv()
        for cp in sent:
            cp.wait_send()

    return _call(body, name=name, in_specs=[HBM] * n, out_specs=[HBM] * n,
                 out_shape=[jax.ShapeDtypeStruct(b.shape, b.dtype) for b in bufs],
                 input_output_aliases={t: t for t in range(n)},
                 scratch_shapes=[pltpu.SemaphoreType.DMA((3 * len(idx),))] * 2)(*bufs)


def _chip_start(pairs, *, name):
    n = len(pairs)
    lands = [lax.empty((3,) + p.shape[1:], p.dtype) for p in pairs]

    def body(*refs):
        s, r = refs[:n], refs[n:2 * n]
        send, recv = refs[2 * n], refs[2 * n + 1]
        token = refs[4 * n + 2]
        x, y, c, j, chips, pj = _place()
        for t in range(n):
            for k in range(3):
                pltpu.make_async_remote_copy(src_ref=s[t].at[pj[k]], dst_ref=r[t].at[k], send_sem=send.at[3 * t + k],
                                             recv_sem=recv.at[3 * t + k], device_id=(*chips[k], c),
                                             device_id_type=MESH).start()
        token[...] = jnp.zeros_like(token)

    outs = _call(body, name=name, in_specs=[HBM] * (2 * n),
                 out_specs=(SEM, SEM, *([HBM] * (2 * n)), pl.BlockSpec(memory_space=pltpu.VMEM)),
                 out_shape=(pltpu.SemaphoreType.DMA((3 * n,)), pltpu.SemaphoreType.DMA((3 * n,)),
                            *[pltpu.HBM(a.shape, a.dtype) for a in list(pairs) + lands],
                            jax.ShapeDtypeStruct((8, 128), F32)),
                 input_output_aliases={t: 2 + t for t in range(2 * n)},
                 compiler_params=pltpu.CompilerParams(has_side_effects=EFFECT))(*[_hbm(a) for a in list(pairs) + lands])
    return outs[0], outs[1], list(outs[2:2 + n]), list(outs[2 + n:2 + 2 * n]), outs[2 + 2 * n]


def _chip_wait(send, recv, pairs, lands, after, *, name):
    n = len(pairs)

    def body(*refs):
        s, r = refs[:n], refs[n:2 * n]
        send, recv = refs[2 * n], refs[2 * n + 1]
        x, y, c, j, chips, pj = _place()
        for t in range(n):
            for k in range(3):
                cp = pltpu.make_async_remote_copy(src_ref=s[t].at[pj[k]], dst_ref=r[t].at[k], send_sem=send.at[3 * t + k],
                                                  recv_sem=recv.at[3 * t + k], device_id=(*chips[k], c),
                                                  device_id_type=MESH)
                cp.wait_send()
                cp.wait_recv()

    outs = _call(body, name=name,
                 in_specs=[HBM] * (2 * n) + [SEM, SEM] + [pl.BlockSpec(memory_space=pl.ANY)] * len(after),
                 out_specs=[HBM] * (2 * n), out_shape=[pltpu.HBM(a.shape, a.dtype) for a in list(pairs) + list(lands)],
                 input_output_aliases={t: t for t in range(2 * n)},
                 compiler_params=pltpu.CompilerParams(has_side_effects=EFFECT))(*pairs, *lands, send, recv, *after)
    return list(outs[:n]), list(outs[n:])


def _all_reduce_small(packed, *, name):
    R, C = packed.shape

    def body(p_ref, o_ref, slots, send, recv, lsem):
        x, y, c = lax.axis_index("x"), lax.axis_index("y"), lax.axis_index("c")
        me = 4 * x + 2 * y + c
        lc = pltpu.make_async_copy(p_ref, slots.at[me], lsem.at[0])
        lc.start()
        copies = []
        for rel in range(1, 8):
            fx, fy, fc = (rel >> 2) & 1, (rel >> 1) & 1, rel & 1
            to = (1 - x if fx else x, 1 - y if fy else y, 1 - c if fc else c)
            cp = pltpu.make_async_remote_copy(src_ref=p_ref, dst_ref=slots.at[me], send_sem=send.at[rel - 1],
                                              recv_sem=recv.at[rel - 1], device_id=to, device_id_type=MESH)
            cp.start()
            copies.append((cp, 4 * to[0] + 2 * to[1] + to[2]))
        for rel, (cp, frm) in enumerate(copies):
            cp.wait_send()
            pltpu.make_async_remote_copy(src_ref=p_ref, dst_ref=slots.at[frm], send_sem=send.at[rel],
                                         recv_sem=recv.at[rel], device_id=(x, y, c), device_id_type=MESH).wait_recv()
        lc.wait()
        acc = slots[0]
        for dev in range(1, 8):
            acc = acc + slots[dev]
        o_ref[...] = acc

    return _call(body, name=name, in_specs=[pl.BlockSpec(memory_space=pltpu.VMEM)],
                 out_specs=pl.BlockSpec(memory_space=pltpu.VMEM), out_shape=jax.ShapeDtypeStruct((R, C), F32),
                 scratch_shapes=[pltpu.VMEM((8, R, C), F32), pltpu.SemaphoreType.DMA((7,)),
                                 pltpu.SemaphoreType.DMA((7,)), pltpu.SemaphoreType.DMA((1,))],
                 compiler_params=pltpu.CompilerParams(vmem_limit_bytes=V7X_VMEM_LIMIT))(packed)


def _pair_sum(grad, theirs, core, *, name):
    J, H, C = theirs.shape
    tr = _rows(H, max(8, (1 << 19) // C // 8 * 8))

    def body(core_ref, a_ref, b_ref, o_ref):
        o_ref[...] = (a_ref[...].astype(F32) + b_ref[...].astype(F32)).astype(BF16)

    blk = pl.BlockSpec((None, tr, C), lambda j, i, s: (j, i, 0))
    mine = pl.BlockSpec((None, None, tr, C), lambda j, i, s: (j, s[0], i, 0))
    grid_spec = pltpu.PrefetchScalarGridSpec(num_scalar_prefetch=1, grid=(J, H // tr), in_specs=[mine, blk],
                                             out_specs=blk)
    return _call(body, name=name, grid_spec=grid_spec, out_shape=jax.ShapeDtypeStruct((J, H, C), BF16),
                 compiler_params=_params(("parallel", "parallel")))(core, grad.reshape(J, 2, H, C), theirs)


def _chip_sum(pairs, slots, place, *, name):
    _, H, C = slots.shape
    tr = _rows(H, max(8, (1 << 19) // C // 8 * 8))
    nr = H // tr

    def body(place_ref, s0, s1, s2, s3, o_ref):
        o_ref[...] = ((s0[...].astype(F32) + s1[...].astype(F32)) + s2[...].astype(F32)) + s3[...].astype(F32)

    def slot(k):
        return pl.BlockSpec((None, tr, C), lambda i, s: (k, i, 0))

    own = pl.BlockSpec((None, tr, C), lambda i, s: (s[0], i, 0))
    grid_spec = pltpu.PrefetchScalarGridSpec(
        num_scalar_prefetch=1, grid=(nr,), in_specs=[own, slot(0), slot(1), slot(2)],
        out_specs=pl.BlockSpec((tr, C), lambda i, s: (s[1] * nr + i, 0)))
    return _call(body, name=name, grid_spec=grid_spec, out_shape=jax.ShapeDtypeStruct((2 * H, C), F32),
                 compiler_params=_params(("parallel",)))(place, pairs, slots, slots, slots)


def _scatter_pair_start(grads, tag):
    lands = [lax.empty((4, g.shape[1] // 2, g.shape[2]), g.dtype) for g in grads]
    return _sibling_start(list(grads) + lands, _pair_copies(len(grads)), len(grads), (), name=f"rs_pair_start_{tag}")


def _scatter_chip_start(pair_started, after, place, tag):
    send, recv, arrays, _ = pair_started
    n = len(arrays) // 2
    arrays = _sibling_wait(send, recv, arrays, _pair_copies(n), after, name=f"rs_pair_wait_{tag}")
    pairs = [_pair_sum(g, t, place[1:], name=f"rs_pair_sum_{tag}_{i}")
             for i, (g, t) in enumerate(zip(arrays[:n], arrays[n:]))]
    return _chip_start(pairs, name=f"rs_chip_start_{tag}")


def _scatter_swap_start(chip_started, after, place, tag):
    send, recv, pairs, lands, _ = chip_started
    pairs, slots = _chip_wait(send, recv, pairs, lands, after, name=f"rs_chip_wait_{tag}")
    fulls = [_chip_sum(p, s, place, name=f"rs_chip_sum_{tag}_{i}") for i, (p, s) in enumerate(zip(pairs, slots))]
    return _sibling_start(fulls, _swap_copies, len(fulls), (), name=f"rs_swap_start_{tag}")


def _scatter_finish(swap_started, tag):
    send, recv, fulls, _ = swap_started
    return _sibling_wait(send, recv, fulls, _swap_copies, (), name=f"rs_swap_wait_{tag}")


def kernel(x, mem, w_in_a, w_in_b, w_s, b_s, vnorm_g, vnorm_b, w_mem_kv, w_out, ln1_g, ln1_b, w_ff1, w_ff2, ln2_g, ln2_b, loss_target, m_w_in_a, m_w_in_b, m_w_s, m_b_s, m_vnorm_g, m_vnorm_b, m_w_mem_kv, m_w_out, m_ln1_g, m_ln1_b, m_w_ff1, m_w_ff2, m_ln2_g, m_ln2_b, v_w_in_a, v_w_in_b, v_w_s, v_b_s, v_vnorm_g, v_vnorm_b, v_w_mem_kv, v_w_out, v_ln1_g, v_ln1_b, v_w_ff1, v_w_ff2, v_ln2_g, v_ln2_b):
    T, D = x.shape[1], x.shape[2]
    depth = w_ff1.shape[0]
    alpha = (2.0 * depth) ** 0.25
    a_out = (D // 256) * HEAD_DIM
    a_cols = len(A_PAIRS) * 3 * a_out
    b_width = (D // 256) * HEAD_DIM
    G = b_width // HEAD_DIM
    assert a_out % HEAD_BLOCK == 0 and T % (BLK * A_PAIRS[-1][1]) == 0

    xf = x.reshape(T, D)
    mem_b = mem.reshape(MEM_TOKENS, D).astype(BF16)
    target = loss_target.reshape(T, D)
    c_idx = lax.axis_index("x") * 2 + lax.axis_index("y")
    place = jnp.stack([c_idx, lax.axis_index("c")]).astype(jnp.int32)

    def gather_place(i, group):
        jl, tag = i // 2, "a" if i % 2 == 0 else "b"
        if group == "mix":
            mats = [(w_in_a if i % 2 == 0 else w_in_b)[jl], w_mem_kv[i], w_out[i]]
        else:
            mats = [w_ff1[i], w_ff2[i]]
        bufs = [_place_shard(w, place[:1], BF16, name=f"place_{tag}_{group}_{n_}") for n_, w in enumerate(mats)]
        if group == "mix" and i % 2:
            bufs += [_place_shard(v_[jl].reshape(1, -1), place[:1], F32, name="place_vnorm") for v_ in (vnorm_g, vnorm_b)]
        return bufs

    placed = {}

    def gather_begin(i, group, after):
        tag = "a" if i % 2 == 0 else "b"
        bufs = placed.pop((i, group), None) or gather_place(i, group)
        return _gather_start(bufs, after, name=f"gather_start_{tag}_{group}")

    def gather_end(started, after, name):
        send, recv, bufs, _ = started
        bufs = _gather_wait(send, recv, bufs, after, name=f"gather_wait_{name}")
        return _gather_pass(bufs, name=f"gather_pass_{name}")

    def group_view(P, g, d):
        return (P, g) if d == 1 else (P[:, g * 3 * a_out:(g + 1) * 3 * a_out], 0)

    saved = []
    xb = xf.astype(BF16)
    mix_started = gather_begin(0, "mix", ())
    for key in ((0, "ff"), (1, "mix"), (1, "ff")):
        if key[0] < depth:
            placed[key] = gather_place(*key)
    early = [b for bufs in placed.values() for b in bufs]
    ff_started = None
    for i in range(depth):
        jl = i // 2
        is_a = i % 2 == 0
        tag = "a" if is_a else "b"
        gathered = gather_end(mix_started, (xf, *early) if i == 0 else (xf,), f"{tag}_mix")
        win, wkv, wout = gathered[:3]
        wkv = wkv.reshape(1, D, 2 * MEM_WIDTH)
        tokens = []
        if i > 0:
            send, recv, ff_bufs, _ = ff_started
            ff_bufs = _gather_wait(send, recv, ff_bufs, (xf,), name=f"gather_wait_{tag}_ff")
            pass_copies, ncopies = _pass_copies([b.shape for b in ff_bufs])
            ff_pass = _sibling_start(ff_bufs, pass_copies, ncopies, (), name=f"gather_pass_start_{tag}_ff")
            tokens.append(ff_pass[3])
        token = None
        if i == 0:
            ff_started = gather_begin(0, "ff", (win,))
            token = ff_started[3]
        if i + 1 < depth:
            mix_started = gather_begin(i + 1, "mix", (win,) if token is None else (token,))
            ff_next = gather_begin(i + 1, "ff", (mix_started[3],))
            token = ff_next[3]
        if token is not None:
            tokens.append(token)
        P = _mm_nn(xb, win, out_dtype=BF16 if is_a else F32, after=tuple(tokens), name=f"proj_in_{'a' if is_a else 'b'}")
        kv = _mm_nn(mem_b, wkv, name="proj_kv")
        if is_a:
            outs = [_attn_fwd(*group_view(P, g, d), d, a_out, name=f"attn_fwd_d{d}") for g, (_, d) in enumerate(A_PAIRS)]
            mix, lse = _attn_combine([o for o, _ in outs], [l for _, l in outs], name="attn_combine")
            qcol = a_cols // MEM_WIDTH
            extra = (lse,)
        else:
            vg_full = gathered[3].reshape(-1)
            vb_full = gathered[4].reshape(-1)
            bs_t = b_s[jl].T
            mix = _gmlp_fwd(P, w_s[jl], bs_t, vg_full, vb_full, b_width, name="gmlp_fwd")
            qcol = 2 * b_width // MEM_WIDTH
            extra = (vg_full, vb_full, bs_t)
        cat = _mem_fwd(P, qcol, kv, mix, name=f"mem_fwd_{'a' if is_a else 'b'}")
        y = _mm_nn(cat, wout, out_dtype=F32, name="proj_out")
        x1, x1b = _ln_fwd(xf, y, ln1_g[i], ln1_b[i], alpha, name="ln_fwd")
        if i > 0:
            wff1, wff2 = _sibling_wait(ff_pass[0], ff_pass[1], ff_pass[2], pass_copies, (x1b,),
                                       name=f"gather_pass_wait_{tag}_ff")
        else:
            wff1, wff2 = gather_end(ff_started, (x1b,), f"{tag}_ff")
        wff2 = wff2.reshape(1, 4 * D, D)
        if i + 1 < depth:
            ff_started = ff_next
        a_pre, hid = _mm_nn(x1b, wff1, epi="relu2", name="ff1")
        f = _mm_nn(hid, wff2, out_dtype=F32, name="ff2")
        x2, x2b = _ln_fwd(x1, f, ln2_g[i], ln2_b[i], alpha, name="ln_fwd")
        saved.append(dict(xf=xf, xb=xb, P=P, kv=kv, mix=mix, cat=cat, y=y, x1=x1, x1b=x1b, a_pre=a_pre, hid=hid, f=f,
                          extra=extra, w=(win, wkv, wout, wff1, wff2), qcol=qcol))
        xf, xb = x2, x2b

    dx, sq = _loss_head(xf, target, name="loss_head")
    loss = lax.psum(sq[0, 0] * (0.5 / D), ("x", "y", "c"))

    g_big = dict(w_in_a=[None] * ((depth + 1) // 2), w_in_b=[None] * (depth // 2), w_mem_kv=[None] * depth,
                 w_out=[None] * depth, w_ff1=[None] * depth, w_ff2=[None] * depth)
    small = {k: [None] * depth for k in ("ln1_g", "ln1_b", "ln2_g", "ln2_b")}
    small_b = {k: [None] * (depth // 2) for k in ("w_s", "b_s", "vnorm_g", "vnorm_b")}
    def store(red, layer, group):
        if group == "ff":
            g_big["w_ff1"][layer], g_big["w_ff2"][layer] = red
        else:
            g_big["w_in_a" if layer % 2 == 0 else "w_in_b"][layer // 2], g_big["w_mem_kv"][layer], g_big["w_out"][layer] = red

    chips = []
    mix_pair = None
    tokens = ()
    for i in reversed(range(depth)):
        jl = i // 2
        is_a = i % 2 == 0
        s = saved[i]
        win, wkv, wout, wff1, wff2 = s["w"]
        tag = "a" if is_a else "b"
        d_f, adr2, dg2, db2 = _ln_bwd(dx, s["x1"], s["f"], ln2_g[i], alpha, tokens, name="ln_bwd")
        small["ln2_g"][i], small["ln2_b"][i] = dg2, db2
        gw_ff2 = _mm_tn(s["hid"], d_f, 1, name="grad_ff2").reshape(4, D, D)
        da = _mm_nt(d_f, wff2, epi="drelu2", extra=s["a_pre"], name="ff2_bwd")
        gw_ff1 = _mm_tn(s["x1b"], da, 4, name="grad_ff1")
        ff_pair = _scatter_pair_start([gw_ff1, gw_ff2], f"{tag}_ff")
        late = []
        if mix_pair is not None:
            above = "b" if is_a else "a"
            mix_chip = _scatter_chip_start(mix_pair, (d_f,), place, f"{above}_mix")
            chips.append((mix_chip, i + 1, "mix"))
            late.append(mix_chip[4])
        dx1 = _mm_nt(da, wff1, epi="resid", extra=adr2, after=(ff_pair[3],), name="ff1_bwd")
        ff_chip = _scatter_chip_start(ff_pair, (dx1,), place, f"{tag}_ff")
        chips.append((ff_chip, i, "ff"))
        late.append(ff_chip[4])
        d_y, adr1, dg1, db1 = _ln_bwd(dx1, s["xf"], s["y"], ln1_g[i], alpha, tuple(late), name="ln_bwd")
        small["ln1_g"][i], small["ln1_b"][i] = dg1, db1
        gw_out = _mm_tn(s["cat"], d_y, 4, name="grad_out")
        dcat = _mm_nt(d_y, wout, out_dtype=F32, name="proj_out_bwd")
        ocol = dcat.shape[1] // MEM_WIDTH - 1
        dmq, dkv = _mem_bwd(s["P"], s["qcol"], s["kv"], dcat, ocol, name=f"mem_bwd_{tag}")
        gw_kv = _mm_tn(mem_b, dkv.astype(BF16), 1, name="grad_kv").reshape(4, D // 4, 2 * MEM_WIDTH)
        if is_a:
            (lse,) = s["extra"]
            dmix = dcat[:, :a_out]
            parts = []
            for g, (_, d) in enumerate(A_PAIRS):
                Pg, g0 = group_view(s["P"], g, d)
                parts += _attn_bwd(Pg, dcat if d == 1 else dmix, s["mix"], lse, g0, d, a_out, name=f"attn_bwd_d{d}")
            dP = jnp.concatenate(parts + [dmq], axis=1)
        else:
            vg_full, vb_full, bs_t = s["extra"]
            dpu, dpv, dws, dbs, dvg, dvb = _gmlp_bwd(s["P"], w_s[jl], bs_t, vg_full, vb_full, dcat, b_width,
                                                     name="gmlp_bwd")
            small_b["w_s"][jl], small_b["b_s"][jl] = dws, dbs[:, :G].T
            small_b["vnorm_g"][jl], small_b["vnorm_b"][jl] = dvg, dvb
            dP = jnp.concatenate([dpu, dpv, dmq], axis=1)
        gw_in = _mm_tn(s["xb"], dP, 4, name=f"grad_in_{tag}")
        dx = _mm_nt(dP, win, epi="resid", extra=adr1, name=f"proj_in_bwd_{tag}")
        mix_pair = _scatter_pair_start([gw_in, gw_kv, gw_out], f"{tag}_mix")
        tokens = (mix_pair[3],)
    last_chip = _scatter_chip_start(mix_pair, (dx,), place, "a_mix")
    swaps, after = [], (dx,)
    for chip_started, layer, group in chips:
        name = f"{'a' if layer % 2 == 0 else 'b'}_{group}"
        swaps.append((_scatter_swap_start(chip_started, after, place, name), layer, group, name))
        after = (swaps[-1][0][3],)
    for swap_started, layer, group, name in swaps:
        store(_scatter_finish(swap_started, name), layer, group)

    def finish_last(after):
        store(_scatter_finish(_scatter_swap_start(last_chip, after, place, "a_mix"), "a_mix"), 0, "mix")

    grad_x = dx.reshape(x.shape)

    nb_layers = depth // 2
    pieces = ([jnp.stack(small_b["w_s"]).reshape(-1, 128), jnp.stack(small_b["b_s"]).reshape(-1, 128),
               jnp.stack(small_b["vnorm_g"]).reshape(-1, 128), jnp.stack(small_b["vnorm_b"]).reshape(-1, 128)]
              + [jnp.stack(small[k]).reshape(-1, 128) for k in ("ln1_g", "ln1_b", "ln2_g", "ln2_b")])
    sizes = [p.shape[0] for p in pieces]
    pad = (-sum(sizes)) % 8
    packed = jnp.concatenate(pieces + ([jnp.zeros((pad, 128), F32)] if pad else []), axis=0)
    summed = _all_reduce_small(packed, name="all_reduce_small")
    offs = [0]
    for n_ in sizes:
        offs.append(offs[-1] + n_)
    sp = [summed[offs[k]:offs[k + 1]] for k in range(len(sizes))]
    vshard = vnorm_g.shape[1]
    g_small = dict(
        w_s=sp[0].reshape(w_s.shape), b_s=sp[1].reshape(b_s.shape),
        vnorm_g=lax.dynamic_slice_in_dim(sp[2].reshape(nb_layers, -1), c_idx * vshard, vshard, axis=1),
        vnorm_b=lax.dynamic_slice_in_dim(sp[3].reshape(nb_layers, -1), c_idx * vshard, vshard, axis=1),
        ln1_g=sp[4].reshape(ln1_g.shape), ln1_b=sp[5].reshape(ln1_b.shape),
        ln2_g=sp[6].reshape(ln2_g.shape), ln2_b=sp[7].reshape(ln2_b.shape))

    names = ["w_in_a", "w_in_b", "w_s", "b_s", "vnorm_g", "vnorm_b", "w_mem_kv", "w_out", "ln1_g", "ln1_b", "w_ff1",
             "w_ff2", "ln2_g", "ln2_b"]
    ws = dict(w_in_a=w_in_a, w_in_b=w_in_b, w_s=w_s, b_s=b_s, vnorm_g=vnorm_g, vnorm_b=vnorm_b, w_mem_kv=w_mem_kv,
              w_out=w_out, ln1_g=ln1_g, ln1_b=ln1_b, w_ff1=w_ff1, w_ff2=w_ff2, ln2_g=ln2_g, ln2_b=ln2_b)
    ms = dict(w_in_a=m_w_in_a, w_in_b=m_w_in_b, w_s=m_w_s, b_s=m_b_s, vnorm_g=m_vnorm_g, vnorm_b=m_vnorm_b,
              w_mem_kv=m_w_mem_kv, w_out=m_w_out, ln1_g=m_ln1_g, ln1_b=m_ln1_b, w_ff1=m_w_ff1, w_ff2=m_w_ff2,
              ln2_g=m_ln2_g, ln2_b=m_ln2_b)
    vs = dict(w_in_a=v_w_in_a, w_in_b=v_w_in_b, w_s=v_w_s, b_s=v_b_s, vnorm_g=v_vnorm_g, vnorm_b=v_vnorm_b,
              w_mem_kv=v_w_mem_kv, w_out=v_w_out, ln1_g=v_ln1_g, ln1_b=v_ln1_b, w_ff1=v_w_ff1, w_ff2=v_w_ff2,
              ln2_g=v_ln2_g, ln2_b=v_ln2_b)
    grads, deltas, new_m, new_v = {}, {}, {}, {}
    for k in ("w_ff1", "w_ff2", "w_in_a", "w_in_b", "w_mem_kv", "w_out"):
        if k == "w_in_a":
            finish_last((deltas["w_ff2"], summed))
        g = jnp.stack(g_big[k]).reshape(ws[k].shape)
        cols = ws[k].shape[-1]
        d_, m_, v_ = _adamw(ws[k].reshape(-1, cols), g.reshape(-1, cols), ms[k].reshape(-1, cols),
                            vs[k].reshape(-1, cols), name=f"adamw_{k}")
        grads[k] = g
        deltas[k], new_m[k], new_v[k] = (t.reshape(ws[k].shape) for t in (d_, m_, v_))
    small_names = [k for k in names if k not in g_big]

    def pack(tree):
        flat = jnp.concatenate([tree[k].reshape(-1) for k in small_names])
        padn = (-flat.shape[0]) % 1024
        return jnp.pad(flat, (0, padn)).reshape(-1, 128)

    d_, m_, v_ = _adamw(pack(ws), pack(g_small), pack(ms), pack(vs), name="adamw_small")
    off = 0
    for k in small_names:
        n_ = ws[k].size
        grads[k] = g_small[k]
        deltas[k], new_m[k], new_v[k] = (t.reshape(-1)[off:off + n_].reshape(ws[k].shape) for t in (d_, m_, v_))
        off += n_

    return (loss, grad_x, *[grads[k] for k in names], *[deltas[k] for k in names], *[new_m[k] for k in names],
            *[new_v[k] for k in names])
```

```python
import functools
import math

import jax
import jax.numpy as jnp
from jax import lax
from jax.experimental import pallas as pl
from jax.experimental.pallas import tpu as pltpu

F32 = jnp.float32
BF16 = jnp.bfloat16

HEAD_DIM = 128
A_PAIRS = ((128, 1), (512, 4), (2048, 16))
BLK = 128
HEAD_BLOCK = 512
MEM_TOKENS = 256
MEM_HEADS = 4
MEM_WIDTH = MEM_HEADS * HEAD_DIM
LN_EPS = 1e-5
ADAM_LR, ADAM_B1, ADAM_B2, ADAM_EPS, ADAM_WD, ADAM_STEP = 0.001, 0.9, 0.999, 1e-08, 0.01, 10
NEG = -1e30
V7X_VMEM_LIMIT = 56 * 1024 * 1024
V7X_MATMUL_VMEM_BUDGET = 40 * 1024 * 1024
MESH = pl.DeviceIdType.MESH
HBM = pl.BlockSpec(memory_space=pltpu.HBM)


def _call(body, **kw):
    return pl.pallas_call(body, **kw)


def _params(sem):
    return pltpu.CompilerParams(dimension_semantics=sem, vmem_limit_bytes=V7X_VMEM_LIMIT)


def _tile(n, cap):
    best = 0
    for t in range(128, min(n, cap) + 1, 128):
        if n % t == 0:
            best = t
    if best == 0 or (best < 512 and n <= 2560):
        return n
    return best


def _depth(k, per_unit_bytes, fixed_bytes):
    t = _tile(k, 2048)
    while t > 512 and 2 * t * per_unit_bytes + fixed_bytes > V7X_MATMUL_VMEM_BUDGET:
        smaller = _tile(k, t // 2)
        if smaller >= t:
            break
        t = smaller
    return t


def _rows(n, cap):
    best = 8
    for t in range(8, min(n, cap) + 1, 8):
        if n % t == 0:
            best = t
    return best


def _epilogue(epi, acc, extra_ref, out_refs):
    if epi == "plain":
        out_refs[0][...] = acc.astype(out_refs[0].dtype)
    elif epi == "relu2":
        out_refs[0][...] = acc
        r = jnp.maximum(acc, 0.0)
        out_refs[1][...] = (r * r).astype(out_refs[1].dtype)
    elif epi == "drelu2":
        out_refs[0][...] = (acc * (2.0 * jnp.maximum(extra_ref[...], 0.0))).astype(out_refs[0].dtype)
    elif epi == "resid":
        out_refs[0][...] = acc + extra_ref[...]
    else:
        raise ValueError(epi)


def _mm_nn(a, w, *, epi="plain", out_dtype=BF16, extra=None, after=(), name):
    M, K = a.shape
    J, K2, Nj = w.shape
    assert K == K2
    tn = _tile(Nj, 1024)
    tm = _tile(M, 1024 if tn <= 1024 else 512)
    n_out = 2 if epi == "relu2" else 1
    has_extra = extra is not None
    out_bytes = 12 if epi == "relu2" else (8 if epi == "resid" or out_dtype == F32 else 4)
    tk = _depth(K, 2 * (tm + tn), tm * tn * (4 + out_bytes + 8 * has_extra))
    nn, nk = Nj // tn, K // tk

    def body(*refs):
        a_ref, w_ref = refs[0], refs[1]
        extra_ref = refs[2] if has_extra else None
        outs = refs[2 + has_extra + len(after): 2 + has_extra + len(after) + n_out]
        acc_ref = refs[-1]
        k = pl.program_id(3)

        @pl.when(k == 0)
        def _():
            acc_ref[...] = jnp.zeros_like(acc_ref)

        acc_ref[...] += jnp.dot(a_ref[...], w_ref[...], preferred_element_type=F32)

        @pl.when(k == nk - 1)
        def _():
            _epilogue(epi, acc_ref[...], extra_ref, outs)

    omap = lambda i, j, n, k: (i, j * nn + n)
    in_specs = [pl.BlockSpec((tm, tk), lambda i, j, n, k: (i, k)),
                pl.BlockSpec((None, tk, tn), lambda i, j, n, k: (j, k, n))]
    args = [a, w]
    if has_extra:
        in_specs.append(pl.BlockSpec((tm, tn), omap))
        args.append(extra)
    in_specs += [pl.BlockSpec(memory_space=pl.ANY)] * len(after)
    args += list(after)
    if epi == "relu2":
        out_shape = (jax.ShapeDtypeStruct((M, J * Nj), F32), jax.ShapeDtypeStruct((M, J * Nj), BF16))
        out_specs = (pl.BlockSpec((tm, tn), omap), pl.BlockSpec((tm, tn), omap))
    else:
        out_shape = jax.ShapeDtypeStruct((M, J * Nj), F32 if epi == "resid" else out_dtype)
        out_specs = pl.BlockSpec((tm, tn), omap)
    return _call(body, name=name, grid=(M // tm, J, nn, nk), in_specs=in_specs, out_specs=out_specs,
                 out_shape=out_shape, scratch_shapes=[pltpu.VMEM((tm, tn), F32)],
                 compiler_params=_params(("parallel", "parallel", "parallel", "arbitrary")))(*args)


def _mm_nt(a, w, *, epi="plain", out_dtype=BF16, extra=None, after=(), name):
    M, N = a.shape
    J, K, Nj = w.shape
    assert N == J * Nj
    tko = _tile(K, 1536)
    has_extra = extra is not None
    out_bytes = 8 if epi == "resid" or out_dtype == F32 else 4
    for tm in (_tile(M, 1024), _tile(M, 512)):
        fixed = tm * tko * (4 + out_bytes + 8 * has_extra)
        tc = _depth(Nj, 2 * (tm + tko), fixed)
        if 2 * tc * 2 * (tm + tko) + fixed <= V7X_MATMUL_VMEM_BUDGET:
            break
    nc = Nj // tc

    def body(*refs):
        a_ref, w_ref = refs[0], refs[1]
        extra_ref = refs[2] if has_extra else None
        outs = refs[2 + has_extra + len(after): 3 + has_extra + len(after)]
        acc_ref = refs[-1]
        j, c = pl.program_id(2), pl.program_id(3)

        @pl.when(jnp.logical_and(j == 0, c == 0))
        def _():
            acc_ref[...] = jnp.zeros_like(acc_ref)

        acc_ref[...] += lax.dot_general(a_ref[...], w_ref[...], (((1,), (1,)), ((), ())),
                                        preferred_element_type=F32)

        @pl.when(jnp.logical_and(j == J - 1, c == nc - 1))
        def _():
            _epilogue(epi, acc_ref[...], extra_ref, outs)

    omap = lambda i, ko, j, c: (i, ko)
    in_specs = [pl.BlockSpec((tm, tc), lambda i, ko, j, c: (i, j * nc + c)),
                pl.BlockSpec((None, tko, tc), lambda i, ko, j, c: (j, ko, c))]
    args = [a, w]
    if has_extra:
        in_specs.append(pl.BlockSpec((tm, tko), omap))
        args.append(extra)
    in_specs += [pl.BlockSpec(memory_space=pl.ANY)] * len(after)
    args += list(after)
    out_shape = jax.ShapeDtypeStruct((M, K), F32 if epi == "resid" else out_dtype)
    return _call(body, name=name, grid=(M // tm, K // tko, J, nc), in_specs=in_specs,
                 out_specs=pl.BlockSpec((tm, tko), omap), out_shape=out_shape,
                 scratch_shapes=[pltpu.VMEM((tm, tko), F32)],
                 compiler_params=_params(("parallel", "parallel", "arbitrary", "arbitrary")))(*args)


def _mm_tn(a, b, J, *, name):
    T, K = a.shape
    T2, N = b.shape
    assert T == T2 and N % J == 0
    Nj = N // J
    tn = _tile(Nj, 1024)
    tkr = _tile(K, (1536 if tn <= 512 else 1024) if tn <= 1024 else 512)
    tt = _depth(T, 2 * (tkr + tn), tkr * tn * (4 + 4))
    nn, nt = Nj // tn, T // tt

    def body(a_ref, b_ref, o_ref, acc_ref):
        t = pl.program_id(3)

        @pl.when(t == 0)
        def _():
            acc_ref[...] = jnp.zeros_like(acc_ref)

        acc_ref[...] += lax.dot_general(a_ref[...], b_ref[...], (((0,), (0,)), ((), ())),
                                        preferred_element_type=F32)

        @pl.when(t == nt - 1)
        def _():
            o_ref[...] = acc_ref[...].astype(o_ref.dtype)

    return _call(body, name=name, grid=(J, K // tkr, nn, nt),
                 in_specs=[pl.BlockSpec((tt, tkr), lambda j, kr, n, t: (t, kr)),
                           pl.BlockSpec((tt, tn), lambda j, kr, n, t: (t, j * nn + n))],
                 out_specs=pl.BlockSpec((None, tkr, tn), lambda j, kr, n, t: (j, kr, n)),
                 out_shape=jax.ShapeDtypeStruct((J, K, Nj), BF16),
                 scratch_shapes=[pltpu.VMEM((tkr, tn), F32)],
                 compiler_params=_params(("parallel", "parallel", "parallel", "arbitrary")))(a, b)


def _ln_fwd(x, y, g, b, alpha, *, name):
    T, D = x.shape
    tr = _rows(T, 512)

    def body(x_ref, y_ref, g_ref, b_ref, o_ref, ob_ref):
        r = alpha * x_ref[...] + y_ref[...]
        mu = jnp.mean(r, axis=-1, keepdims=True)
        xc = r - mu
        var = jnp.mean(xc * xc, axis=-1, keepdims=True)
        o = xc * lax.rsqrt(var + LN_EPS) * g_ref[...] + b_ref[...]
        o_ref[...] = o
        ob_ref[...] = o.astype(BF16)

    row = pl.BlockSpec((tr, D), lambda i: (i, 0))
    vec = pl.BlockSpec((1, D), lambda i: (0, 0))
    return _call(body, name=name, grid=(T // tr,), in_specs=[row, row, vec, vec], out_specs=(row, row),
                 out_shape=(jax.ShapeDtypeStruct((T, D), F32), jax.ShapeDtypeStruct((T, D), BF16)),
                 compiler_params=_params(("parallel",)))(x, y, g.reshape(1, D), b.reshape(1, D))


def _ln_bwd(dout, x, y, g, alpha, after=(), *, name):
    T, D = x.shape
    tr = _rows(T, 512)
    na = len(after)

    def body(*refs):
        do_ref, x_ref, y_ref, g_ref = refs[:4]
        drb_ref, adr_ref, dg_ref, db_ref = refs[4 + na:]
        i = pl.program_id(0)
        r = alpha * x_ref[...] + y_ref[...]
        mu = jnp.mean(r, axis=-1, keepdims=True)
        xc = r - mu
        var = jnp.mean(xc * xc, axis=-1, keepdims=True)
        rstd = lax.rsqrt(var + LN_EPS)
        xhat = xc * rstd
        do = do_ref[...]
        dxh = do * g_ref[...]
        dr = rstd * (dxh - jnp.mean(dxh, axis=-1, keepdims=True)
                     - xhat * jnp.mean(dxh * xhat, axis=-1, keepdims=True))
        drb_ref[...] = dr.astype(BF16)
        adr_ref[...] = alpha * dr

        @pl.when(i == 0)
        def _():
            dg_ref[...] = jnp.zeros_like(dg_ref)
            db_ref[...] = jnp.zeros_like(db_ref)

        dg_ref[...] += jnp.sum(do * xhat, axis=0, keepdims=True)
        db_ref[...] += jnp.sum(do, axis=0, keepdims=True)

    row = pl.BlockSpec((tr, D), lambda i: (i, 0))
    vec = pl.BlockSpec((1, D), lambda i: (0, 0))
    return _call(body, name=name, grid=(T // tr,),
                 in_specs=[row, row, row, vec] + [pl.BlockSpec(memory_space=pl.ANY)] * na,
                 out_specs=(row, row, vec, vec),
                 out_shape=(jax.ShapeDtypeStruct((T, D), BF16), jax.ShapeDtypeStruct((T, D), F32),
                            jax.ShapeDtypeStruct((1, D), F32), jax.ShapeDtypeStruct((1, D), F32)),
                 compiler_params=_params(("arbitrary",)))(dout, x, y, g.reshape(1, D), *after)


def _loss_head(xf, target, *, name):
    T, D = xf.shape
    tr = _rows(T, 256)

    def body(x_ref, t_ref, dy_ref, s_ref):
        i = pl.program_id(0)
        err = x_ref[...] - t_ref[...]
        dy_ref[...] = err * (1.0 / D)

        @pl.when(i == 0)
        def _():
            s_ref[...] = jnp.zeros_like(s_ref)

        s_ref[...] += jnp.sum(jnp.sum(err * err, axis=1, keepdims=True), axis=0, keepdims=True)

    row = pl.BlockSpec((tr, D), lambda i: (i, 0))
    return _call(body, name=name, grid=(T // tr,), in_specs=[row, row],
                 out_specs=(row, pl.BlockSpec((8, 128), lambda i: (0, 0))),
                 out_shape=(jax.ShapeDtypeStruct((T, D), F32), jax.ShapeDtypeStruct((8, 128), F32)),
                 compiler_params=_params(("arbitrary",)))(xf, target)


def _adamw(w, g, m, v, *, name):
    R, C = w.shape
    tr = _rows(R, max(8, (1 << 18) // C // 8 * 8))

    def body(w_ref, g_ref, m_ref, v_ref, d_ref, nm_ref, nv_ref):
        g_ = g_ref[...]
        m_ = ADAM_B1 * m_ref[...] + (1.0 - ADAM_B1) * g_
        v_ = ADAM_B2 * v_ref[...] + (1.0 - ADAM_B2) * (g_ * g_)
        m_hat = m_ / (1.0 - ADAM_B1 ** ADAM_STEP)
        v_hat = v_ / (1.0 - ADAM_B2 ** ADAM_STEP)
        d_ref[...] = -ADAM_LR * (m_hat / (jnp.sqrt(v_hat) + ADAM_EPS) + ADAM_WD * w_ref[...])
        nm_ref[...] = m_
        nv_ref[...] = v_

    blk = pl.BlockSpec((tr, C), lambda i: (i, 0))
    sds = jax.ShapeDtypeStruct((R, C), F32)
    return _call(body, name=name, grid=(R // tr,), in_specs=[blk] * 4, out_specs=(blk,) * 3,
                 out_shape=(sds,) * 3, compiler_params=_params(("parallel",)))(w, g, m, v)


def _dot_nt(a, b):
    return lax.dot_general(a, b, (((1,), (1,)), ((), ())), preferred_element_type=F32)


def _dot_tn(a, b):
    return lax.dot_general(a, b, (((0,), (0,)), ((), ())), preferred_element_type=F32)


def _band_masks():
    qi = lax.broadcasted_iota(jnp.int32, (BLK, BLK), 0)
    kj = lax.broadcasted_iota(jnp.int32, (BLK, BLK), 1)
    return qi >= kj, kj >= qi


def _blocks_per_step(nb):
    return next(t for t in (4, 2, 1) if nb % t == 0)


def _attn_fwd(P, grp, d, a_out, *, name):
    T, C = P.shape
    L = T // d
    nb = L // BLK
    nhh = a_out // HEAD_BLOCK
    cb = C // HEAD_BLOCK
    q0 = grp * 3 * nhh
    scale = HEAD_DIM ** -0.5
    tb = _blocks_per_step(nb)

    def body(q_ref, kc_ref, kp_ref, vc_ref, vp_ref, o_ref, l_ref):
        b = pl.program_id(1)
        mask_c, mask_prev = _band_masks()
        for sub in range(tb):
            rs = slice(sub * BLK, (sub + 1) * BLK)
            ps = slice((sub - 1) * BLK, sub * BLK)
            mask_p = jnp.logical_and(mask_prev, b > 0) if sub == 0 else mask_prev
            for h in range(HEAD_BLOCK // HEAD_DIM):
                hs = slice(h * HEAD_DIM, (h + 1) * HEAD_DIM)
                q = q_ref[rs, hs]
                kp, vp = (kp_ref[:, hs], vp_ref[:, hs]) if sub == 0 else (kc_ref[ps, hs], vc_ref[ps, hs])
                s_c = jnp.where(mask_c, _dot_nt(q, kc_ref[rs, hs]) * scale, NEG)
                s_p = jnp.where(mask_p, _dot_nt(q, kp) * scale, NEG)
                m = jnp.maximum(jnp.max(s_c, axis=1, keepdims=True), jnp.max(s_p, axis=1, keepdims=True))
                p_c = jnp.exp(s_c - m)
                p_p = jnp.exp(s_p - m)
                l = jnp.sum(p_c, axis=1, keepdims=True) + jnp.sum(p_p, axis=1, keepdims=True)
                o = (jnp.dot(p_c.astype(BF16), vc_ref[rs, hs], preferred_element_type=F32)
                     + jnp.dot(p_p.astype(BF16), vp, preferred_element_type=F32))
                o_ref[rs, hs] = o / l
                l_ref[rs, hs] = jnp.broadcast_to(m + jnp.log(l), (BLK, HEAD_DIM))

    def cur(part):
        return pl.BlockSpec((tb * BLK, HEAD_BLOCK), lambda r, b, hh: (b, r * cb + q0 + part * nhh + hh))

    def prev(part):
        return pl.BlockSpec((BLK, HEAD_BLOCK),
                            lambda r, b, hh: (jnp.maximum(b * tb - 1, 0), r * cb + q0 + part * nhh + hh))

    out = pl.BlockSpec((tb * BLK, HEAD_BLOCK), lambda r, b, hh: (b, r * nhh + hh))
    Pv = P.reshape(L, d * C)
    o, lse = _call(body, name=name, grid=(d, nb // tb, nhh), in_specs=[cur(0), cur(1), prev(1), cur(2), prev(2)],
                   out_specs=(out, out),
                   out_shape=(jax.ShapeDtypeStruct((L, d * a_out), F32),) * 2,
                   compiler_params=_params(("parallel", "parallel", "parallel")))(Pv, Pv, Pv, Pv, Pv)
    return o.reshape(T, a_out), lse.reshape(T, a_out)


def _attn_combine(os_, lses, *, name):
    T, W = os_[0].shape
    tr = _rows(T, 256)
    n = len(os_)

    def body(*refs):
        o_refs, l_refs = refs[:n], refs[n:2 * n]
        mix_ref, lse_ref = refs[2 * n], refs[2 * n + 1]
        ls = [r[...] for r in l_refs]
        m = functools.reduce(jnp.maximum, ls)
        es = [jnp.exp(l - m) for l in ls]
        tot = functools.reduce(lambda a, b: a + b, es)
        mix = functools.reduce(lambda a, b: a + b, [(e / tot) * o[...] for e, o in zip(es, o_refs)])
        mix_ref[...] = mix
        lse_ref[...] = m + jnp.log(tot)

    blk = pl.BlockSpec((tr, W), lambda i: (i, 0))
    sds = jax.ShapeDtypeStruct((T, W), F32)
    return _call(body, name=name, grid=(T // tr,), in_specs=[blk] * (2 * n), out_specs=(blk, blk),
                 out_shape=(sds, sds), compiler_params=_params(("parallel",)))(*os_, *lses)


def _attn_bwd(P, dO, O, LSE, grp, d, a_out, *, name):
    T, C = P.shape
    L = T // d
    nb = L // BLK
    nhh = a_out // HEAD_BLOCK
    cb = C // HEAD_BLOCK
    q0 = grp * 3 * nhh
    scale = HEAD_DIM ** -0.5
    tb = _blocks_per_step(nb)
    nsteps = nb // tb

    def body(q_ref, qn_ref, kc_ref, kp_ref, vc_ref, vp_ref, do_ref, don_ref, o_ref, on_ref, l_ref, ln_ref,
             dq_ref, dk_ref, dv_ref):
        b = pl.program_id(1)
        mask_c, mask_prev = _band_masks()
        for sub in range(tb):
            rs = slice(sub * BLK, (sub + 1) * BLK)
            ps = slice((sub - 1) * BLK, sub * BLK)
            ns = slice((sub + 1) * BLK, (sub + 2) * BLK)
            first, last = sub == 0, sub == tb - 1
            mask_p = jnp.logical_and(mask_prev, b > 0) if first else mask_prev
            mask_n = jnp.logical_and(mask_prev, b < nsteps - 1) if last else mask_prev
            for h in range(HEAD_BLOCK // HEAD_DIM):
                hs = slice(h * HEAD_DIM, (h + 1) * HEAD_DIM)
                q, kc, vc = q_ref[rs, hs], kc_ref[rs, hs], vc_ref[rs, hs]
                kp, vp = (kp_ref[:, hs], vp_ref[:, hs]) if first else (kc_ref[ps, hs], vc_ref[ps, hs])
                qn = qn_ref[:, hs] if last else q_ref[ns, hs]
                do = do_ref[rs, hs]
                don = don_ref[:, hs] if last else do_ref[ns, hs]
                on = on_ref[:, hs] if last else o_ref[ns, hs]
                lse = l_ref[rs, hs]
                lse_n = ln_ref[:, hs] if last else l_ref[ns, hs]
                delta = jnp.sum(do * o_ref[rs, hs], axis=1, keepdims=True)
                delta_n = jnp.sum(don * on, axis=1, keepdims=True)
                dob, donb = do.astype(BF16), don.astype(BF16)
                p_c = jnp.exp(jnp.where(mask_c, _dot_nt(q, kc) * scale, NEG) - lse)
                p_p = jnp.exp(jnp.where(mask_p, _dot_nt(q, kp) * scale, NEG) - lse)
                p_n = jnp.exp(jnp.where(mask_n, _dot_nt(qn, kc) * scale, NEG) - lse_n)
                ds_c = (p_c * (_dot_nt(dob, vc) - delta) * scale).astype(BF16)
                ds_p = (p_p * (_dot_nt(dob, vp) - delta) * scale).astype(BF16)
                ds_n = (p_n * (_dot_nt(donb, vc) - delta_n) * scale).astype(BF16)
                dq = jnp.dot(ds_c, kc, preferred_element_type=F32) + jnp.dot(ds_p, kp, preferred_element_type=F32)
                dk = _dot_tn(ds_c, q) + _dot_tn(ds_n, qn)
                dv = _dot_tn(p_c.astype(BF16), dob) + _dot_tn(p_n.astype(BF16), donb)
                dq_ref[rs, hs] = dq.astype(BF16)
                dk_ref[rs, hs] = dk.astype(BF16)
                dv_ref[rs, hs] = dv.astype(BF16)

    def rows(shift):
        if shift == 0:
            return tb * BLK, lambda b: b
        return BLK, (lambda b: jnp.maximum(b * tb - 1, 0)) if shift < 0 else (lambda b: jnp.minimum((b + 1) * tb, nb - 1))

    def pspec(part, shift):
        size, row = rows(shift)
        return pl.BlockSpec((size, HEAD_BLOCK), lambda r, b, hh: (row(b), r * cb + q0 + part * nhh + hh))

    def aspec(shift, width=a_out):
        per = width // HEAD_BLOCK
        size, row = rows(shift)
        return pl.BlockSpec((size, HEAD_BLOCK), lambda r, b, hh: (row(b), r * per + hh))

    Pv = P.reshape(L, d * C)
    w_do = dO.shape[1]
    dOv = dO.reshape(L, d * w_do)
    Ov, Lv = (t.reshape(L, d * a_out) for t in (O, LSE))
    sds = jax.ShapeDtypeStruct((L, d * a_out), BF16)
    dq, dk, dv = _call(
        body, name=name, grid=(d, nsteps, nhh),
        in_specs=[pspec(0, 0), pspec(0, 1), pspec(1, 0), pspec(1, -1), pspec(2, 0), pspec(2, -1),
                  aspec(0, w_do), aspec(1, w_do), aspec(0), aspec(1), aspec(0), aspec(1)],
        out_specs=(aspec(0),) * 3, out_shape=(sds,) * 3,
        compiler_params=_params(("parallel", "parallel", "parallel")))(Pv, Pv, Pv, Pv, Pv, Pv, dOv, dOv, Ov, Ov, Lv, Lv)
    return [t.reshape(T, a_out) for t in (dq, dk, dv)]


def _mem_softmax(q, k, scale):
    s = _dot_nt(q, k) * scale
    e = jnp.exp(s - jnp.max(s, axis=1, keepdims=True))
    return e / jnp.sum(e, axis=1, keepdims=True)


def _mem_fwd(P, qcol, kv, mix, *, name):
    T, W = mix.shape
    tq = _rows(T, 512)
    scale = HEAD_DIM ** -0.5

    def body(q_ref, kv_ref, mix_ref, o_ref):
        o_ref[:, :W] = mix_ref[...].astype(BF16)
        for h in range(MEM_HEADS):
            hs = slice(h * HEAD_DIM, (h + 1) * HEAD_DIM)
            vs = slice(MEM_WIDTH + h * HEAD_DIM, MEM_WIDTH + (h + 1) * HEAD_DIM)
            p = _mem_softmax(q_ref[:, hs].astype(BF16), kv_ref[:, hs], scale)
            o = jnp.dot(p.astype(BF16), kv_ref[:, vs], preferred_element_type=F32)
            o_ref[:, W + h * HEAD_DIM:W + (h + 1) * HEAD_DIM] = o.astype(BF16)

    return _call(body, name=name, grid=(T // tq,),
                 in_specs=[pl.BlockSpec((tq, MEM_WIDTH), lambda i: (i, qcol)),
                           pl.BlockSpec((MEM_TOKENS, 2 * MEM_WIDTH), lambda i: (0, 0)),
                           pl.BlockSpec((tq, W), lambda i: (i, 0))],
                 out_specs=pl.BlockSpec((tq, W + MEM_WIDTH), lambda i: (i, 0)),
                 out_shape=jax.ShapeDtypeStruct((T, W + MEM_WIDTH), BF16),
                 compiler_params=_params(("parallel",)))(P, kv, mix)


def _mem_bwd(P, qcol, kv, dcat, ocol, *, name):
    T = P.shape[0]
    tq = _rows(T, 512)
    scale = HEAD_DIM ** -0.5

    def body(q_ref, kv_ref, do_ref, dq_ref, dkv_ref):
        i = pl.program_id(0)

        @pl.when(i == 0)
        def _():
            dkv_ref[...] = jnp.zeros_like(dkv_ref)

        for h in range(MEM_HEADS):
            hs = slice(h * HEAD_DIM, (h + 1) * HEAD_DIM)
            vs = slice(MEM_WIDTH + h * HEAD_DIM, MEM_WIDTH + (h + 1) * HEAD_DIM)
            q = q_ref[:, hs].astype(BF16)
            k, v = kv_ref[:, hs], kv_ref[:, vs]
            do = do_ref[:, hs].astype(BF16)
            p = _mem_softmax(q, k, scale)
            dp = _dot_nt(do, v)
            ds = (p * (dp - jnp.sum(p * dp, axis=1, keepdims=True)) * scale).astype(BF16)
            dq_ref[:, hs] = jnp.dot(ds, k, preferred_element_type=F32).astype(BF16)
            dkv_ref[:, hs] += _dot_tn(ds, q)
            dkv_ref[:, vs] += _dot_tn(p.astype(BF16), do)

    return _call(body, name=name, grid=(T // tq,),
                 in_specs=[pl.BlockSpec((tq, MEM_WIDTH), lambda i: (i, qcol)),
                           pl.BlockSpec((MEM_TOKENS, 2 * MEM_WIDTH), lambda i: (0, 0)),
                           pl.BlockSpec((tq, MEM_WIDTH), lambda i: (i, ocol))],
                 out_specs=(pl.BlockSpec((tq, MEM_WIDTH), lambda i: (i, 0)),
                            pl.BlockSpec((MEM_TOKENS, 2 * MEM_WIDTH), lambda i: (0, 0))),
                 out_shape=(jax.ShapeDtypeStruct((T, MEM_WIDTH), BF16),
                            jax.ShapeDtypeStruct((MEM_TOKENS, 2 * MEM_WIDTH), F32)),
                 compiler_params=_params(("arbitrary",)))(P, kv, dcat)


_SQRT_HALF = 0.7071067811865476
_INV_SQRT_2PI = 0.3989422804014327


def _gelu(x):
    return 0.5 * x * (1.0 + lax.erf(x * _SQRT_HALF))


def _gelu_grad(x):
    return 0.5 * (1.0 + lax.erf(x * _SQRT_HALF)) + x * (_INV_SQRT_2PI * jnp.exp(-0.5 * x * x))


def _tril():
    t = lax.broadcasted_iota(jnp.int32, (BLK, BLK), 0)
    s = lax.broadcasted_iota(jnp.int32, (BLK, BLK), 1)
    return t >= s


def _gmlp_fwd(P, w_s, bs_t, vg, vb, width, *, name):
    T = P.shape[0]
    G = width // HEAD_DIM
    tb = _rows(T, 512)

    def body(pu_ref, pv_ref, ws_ref, bs_ref, vg_ref, vb_ref, o_ref):
        u = _gelu(pu_ref[...])
        v = _gelu(pv_ref[...])
        mu = jnp.mean(v, axis=-1, keepdims=True)
        vc = v - mu
        var = jnp.mean(vc * vc, axis=-1, keepdims=True)
        vn = (vc * lax.rsqrt(var + LN_EPS) * vg_ref[...] + vb_ref[...]).astype(BF16)
        tril = _tril()
        for g in range(G):
            gs = slice(g * HEAD_DIM, (g + 1) * HEAD_DIM)
            ws = jnp.where(tril, ws_ref[g], 0.0).astype(BF16)
            bias = bs_ref[:, g:g + 1]
            for c in range(tb // BLK):
                cs = slice(c * BLK, (c + 1) * BLK)
                sg = jnp.dot(ws, vn[cs, gs], preferred_element_type=F32) + bias
                o_ref[cs, gs] = u[cs, gs] * sg

    blk = lambda col: pl.BlockSpec((tb, width), lambda i: (i, col))
    full = lambda shape: pl.BlockSpec(shape, lambda i: (0,) * len(shape))
    return _call(body, name=name, grid=(T // tb,),
                 in_specs=[blk(0), blk(1), full((G, BLK, BLK)), full((BLK, G)), full((1, width)), full((1, width))],
                 out_specs=blk(0), out_shape=jax.ShapeDtypeStruct((T, width), F32),
                 compiler_params=_params(("parallel",)))(P, P, w_s, bs_t, vg.reshape(1, width), vb.reshape(1, width))


def _gmlp_bwd(P, w_s, bs_t, vg, vb, dcat, width, *, name):
    T = P.shape[0]
    G = width // HEAD_DIM
    tb = _rows(T, 512)

    def body(pu_ref, pv_ref, ws_ref, bs_ref, vg_ref, vb_ref, dm_ref, dpu_ref, dpv_ref, dws_ref, dbs_ref, dvg_ref,
             dvb_ref, dvn_ref):
        i = pl.program_id(0)

        @pl.when(i == 0)
        def _():
            dws_ref[...] = jnp.zeros_like(dws_ref)
            dbs_ref[...] = jnp.zeros_like(dbs_ref)
            dvg_ref[...] = jnp.zeros_like(dvg_ref)
            dvb_ref[...] = jnp.zeros_like(dvb_ref)

        pu, pv = pu_ref[...], pv_ref[...]
        u = _gelu(pu)
        v = _gelu(pv)
        mu = jnp.mean(v, axis=-1, keepdims=True)
        vc = v - mu
        var = jnp.mean(vc * vc, axis=-1, keepdims=True)
        rstd = lax.rsqrt(var + LN_EPS)
        xhat = vc * rstd
        vn = (xhat * vg_ref[...] + vb_ref[...]).astype(BF16)
        dm = dm_ref[...]
        tril = _tril()
        lane = lax.broadcasted_iota(jnp.int32, (BLK, BLK), 1)
        dbs = jnp.zeros((BLK, BLK), F32)
        for g in range(G):
            gs = slice(g * HEAD_DIM, (g + 1) * HEAD_DIM)
            ws = jnp.where(tril, ws_ref[g], 0.0).astype(BF16)
            bias = bs_ref[:, g:g + 1]
            dws = jnp.zeros((BLK, BLK), F32)
            rs = jnp.zeros((BLK, 1), F32)
            for c in range(tb // BLK):
                cs = slice(c * BLK, (c + 1) * BLK)
                vn_cg = vn[cs, gs]
                sg = jnp.dot(ws, vn_cg, preferred_element_type=F32) + bias
                dm_cg = dm[cs, gs]
                dpu_ref[cs, gs] = (dm_cg * sg * _gelu_grad(pu[cs, gs])).astype(BF16)
                dsg = dm_cg * u[cs, gs]
                dsgb = dsg.astype(BF16)
                dvn_ref[cs, gs] = _dot_tn(ws, dsgb)
                dws = dws + _dot_nt(dsgb, vn_cg)
                rs = rs + jnp.sum(dsg, axis=1, keepdims=True)
            dws_ref[g] += jnp.where(tril, dws, 0.0)
            dbs = dbs + jnp.where(lane == g, rs, 0.0)
        dbs_ref[...] += dbs
        dvn = dvn_ref[...]
        dxh = dvn * vg_ref[...]
        dv = rstd * (dxh - jnp.mean(dxh, axis=-1, keepdims=True)
                     - xhat * jnp.mean(dxh * xhat, axis=-1, keepdims=True))
        dpv_ref[...] = (dv * _gelu_grad(pv)).astype(BF16)
        dvg_ref[...] += jnp.sum(dvn * xhat, axis=0, keepdims=True)
        dvb_ref[...] += jnp.sum(dvn, axis=0, keepdims=True)

    blk = lambda col: pl.BlockSpec((tb, width), lambda i: (i, col))
    full = lambda shape: pl.BlockSpec(shape, lambda i: (0,) * len(shape))
    return _call(body, name=name, grid=(T // tb,),
                 in_specs=[blk(0), blk(1), full((G, BLK, BLK)), full((BLK, G)), full((1, width)), full((1, width)),
                           blk(0)],
                 out_specs=(blk(0), blk(0), full((G, BLK, BLK)), full((BLK, BLK)), full((1, width)), full((1, width))),
                 out_shape=(jax.ShapeDtypeStruct((T, width), BF16), jax.ShapeDtypeStruct((T, width), BF16),
                            jax.ShapeDtypeStruct((G, BLK, BLK), F32), jax.ShapeDtypeStruct((BLK, BLK), F32),
                            jax.ShapeDtypeStruct((1, width), F32), jax.ShapeDtypeStruct((1, width), F32)),
                 scratch_shapes=[pltpu.VMEM((tb, width), F32)],
                 compiler_params=_params(("arbitrary",)))(P, P, w_s, bs_t, vg.reshape(1, width), vb.reshape(1, width), dcat)


def _place():
    x, y, c = lax.axis_index("x"), lax.axis_index("y"), lax.axis_index("c")
    chips = [(1 - x, y), (x, 1 - y), (1 - x, 1 - y)]
    return x, y, c, 2 * x + y, chips, [2 * px + py for px, py in chips]


def _half(ref, c, rows):
    return ref.at[pl.ds(c * (rows // 2), rows // 2)]


def _place_shard(w, chip, dtype, *, name):
    R, C = w.shape
    tr = _rows(R, max(8, (1 << 19) // C // 8 * 8)) if R % 8 == 0 else R

    def body(chip_ref, w_ref, o_ref):
        o_ref[...] = w_ref[...].astype(dtype)

    grid_spec = pltpu.PrefetchScalarGridSpec(
        num_scalar_prefetch=1, grid=(R // tr,), in_specs=[pl.BlockSpec((tr, C), lambda i, s: (i, 0))],
        out_specs=pl.BlockSpec((None, tr, C), lambda i, s: (s[0], i, 0)))
    return _call(body, name=name, grid_spec=grid_spec, out_shape=jax.ShapeDtypeStruct((4, R, C), dtype),
                 compiler_params=_params(("parallel",)))(chip, w)


SEM = pl.BlockSpec(memory_space=pltpu.SEMAPHORE)
EFFECT = pltpu.SideEffectType.DATAFLOW_SIDE_EFFECTING


def _hbm(a):
    return pltpu.with_memory_space_constraint(a, pltpu.HBM)


def _sibling_start(arrays, copies, ncopies, after, *, name):
    n, na = len(arrays), len(after)

    def body(*refs):
        a = refs[:n]
        send, recv = refs[n + na], refs[n + na + 1]
        token = refs[2 * n + na + 2]
        x, y, c, _, _, _ = _place()
        for q, (src, dst) in enumerate(copies(a, c)):
            pltpu.make_async_remote_copy(src_ref=src, dst_ref=dst, send_sem=send.at[q], recv_sem=recv.at[q],
                                         device_id=(x, y, 1 - c), device_id_type=MESH).start()
        token[...] = jnp.zeros_like(token)

    outs = _call(body, name=name, in_specs=[HBM] * n + [pl.BlockSpec(memory_space=pl.ANY)] * na,
                 out_specs=(SEM, SEM, *([HBM] * n), pl.BlockSpec(memory_space=pltpu.VMEM)),
                 out_shape=(pltpu.SemaphoreType.DMA((ncopies,)), pltpu.SemaphoreType.DMA((ncopies,)),
                            *[pltpu.HBM(a.shape, a.dtype) for a in arrays], jax.ShapeDtypeStruct((8, 128), F32)),
                 input_output_aliases={t: 2 + t for t in range(n)},
                 compiler_params=pltpu.CompilerParams(has_side_effects=EFFECT))(*[_hbm(a) for a in arrays], *after)
    return outs[0], outs[1], list(outs[2:2 + n]), outs[2 + n]


def _sibling_wait(send, recv, arrays, copies, after, *, name):
    n, na = len(arrays), len(after)

    def body(*refs):
        a = refs[:n]
        send, recv = refs[n], refs[n + 1]
        x, y, c, _, _, _ = _place()
        for q, ((src, dst), (_, landed)) in enumerate(zip(copies(a, c), copies(a, 1 - c))):
            pltpu.make_async_remote_copy(src_ref=src, dst_ref=dst, send_sem=send.at[q], recv_sem=recv.at[q],
                                         device_id=(x, y, 1 - c), device_id_type=MESH).wait_send()
            pltpu.make_async_remote_copy(src_ref=src, dst_ref=landed, send_sem=send.at[q], recv_sem=recv.at[q],
                                         device_id=(x, y, 1 - c), device_id_type=MESH).wait_recv()

    outs = _call(body, name=name, in_specs=[HBM] * n + [SEM, SEM] + [pl.BlockSpec(memory_space=pl.ANY)] * na,
                 out_specs=[HBM] * n, out_shape=[pltpu.HBM(a.shape, a.dtype) for a in arrays],
                 input_output_aliases={t: t for t in range(n)},
                 compiler_params=pltpu.CompilerParams(has_side_effects=EFFECT))(*arrays, send, recv, *after)
    return list(outs)


def _pair_copies(n):
    def copies(a, core):
        out = []
        for t in range(n):
            h = a[t].shape[1] // 2
            out.append((a[t].at[:, pl.ds((1 - core) * h, h)], a[n + t]))
        return out
    return copies


def _swap_copies(a, core):
    out = []
    for ref in a:
        h = ref.shape[0] // 2
        out.append((ref.at[pl.ds(core * h, h)],) * 2)
    return out


def _gather_windows(shapes):
    split = [s[1] % 16 == 0 for s in shapes]

    def window(ref, t, chip_idx, core):
        w = ref.at[chip_idx]
        return _half(w, core, shapes[t][1]) if split[t] else w

    return split, window


def _gather_start(bufs, after, *, name):
    n = len(bufs)
    split, window = _gather_windows([b.shape for b in bufs])

    na = len(after)

    def body(*refs):
        b = refs[:n]
        send, recv = refs[n + na], refs[n + na + 1]
        token = refs[2 * n + na + 2]
        x, y, c, j, chips, pj = _place()
        for t in range(n):
            for k in range(3):
                pltpu.make_async_remote_copy(src_ref=window(b[t], t, j, c), dst_ref=window(b[t], t, j, c),
                                             send_sem=send.at[3 * t + k], recv_sem=recv.at[3 * t + k],
                                             device_id=(*chips[k], c), device_id_type=MESH).start()
        token[...] = jnp.zeros_like(token)

    outs = _call(body, name=name, in_specs=[HBM] * n + [pl.BlockSpec(memory_space=pl.ANY)] * na,
                 out_specs=(SEM, SEM, *([HBM] * n), pl.BlockSpec(memory_space=pltpu.VMEM)),
                 out_shape=(pltpu.SemaphoreType.DMA((3 * n,)), pltpu.SemaphoreType.DMA((3 * n,)),
                            *[pltpu.HBM(b.shape, b.dtype) for b in bufs], jax.ShapeDtypeStruct((8, 128), F32)),
                 input_output_aliases={t: 2 + t for t in range(n)},
                 compiler_params=pltpu.CompilerParams(has_side_effects=EFFECT))(*[_hbm(b) for b in bufs], *after)
    return outs[0], outs[1], list(outs[2:2 + n]), outs[2 + n]


def _gather_wait(send, recv, bufs, after, *, name):
    n = len(bufs)
    split, window = _gather_windows([b.shape for b in bufs])

    def body(*refs):
        b = refs[:n]
        send, recv = refs[n], refs[n + 1]
        x, y, c, j, chips, pj = _place()
        for t in range(n):
            for k in range(3):
                out = pltpu.make_async_remote_copy(src_ref=window(b[t], t, j, c), dst_ref=window(b[t], t, j, c),
                                                   send_sem=send.at[3 * t + k], recv_sem=recv.at[3 * t + k],
                                                   device_id=(*chips[k], c), device_id_type=MESH)
                out.wait_send()
                back = pltpu.make_async_remote_copy(src_ref=window(b[t], t, pj[k], c), dst_ref=window(b[t], t, pj[k], c),
                                                    send_sem=send.at[3 * t + k], recv_sem=recv.at[3 * t + k],
                                                    device_id=(*chips[k], c), device_id_type=MESH)
                back.wait_recv()

    outs = _call(body, name=name, in_specs=[HBM] * n + [SEM, SEM] + [pl.BlockSpec(memory_space=pl.ANY)] * len(after),
                 out_specs=[HBM] * n, out_shape=[pltpu.HBM(b.shape, b.dtype) for b in bufs],
                 input_output_aliases={t: t for t in range(n)},
                 compiler_params=pltpu.CompilerParams(has_side_effects=EFFECT))(*bufs, send, recv, *after)
    return list(outs)


def _pass_copies(shapes):
    split, window = _gather_windows(shapes)

    def copies(a, core):
        _, _, _, _, _, pj = _place()
        return [(window(a[t], t, pj[k], core),) * 2 for t in range(len(shapes)) if split[t] for k in range(3)]

    return copies, 3 * sum(split)


def _gather_pass(bufs, *, name):
    n = len(bufs)
    split, window = _gather_windows([b.shape for b in bufs])
    idx = [t for t in range(n) if split[t]]

    def body(*refs):
        b = refs[n:2 * n]
        send, recv = refs[2 * n:]
        x, y, c, j, chips, pj = _place()

        def d2d(u, t, k, core):
            w = window(b[t], t, pj[k], core)
            return pltpu.make_async_remote_copy(src_ref=w, dst_ref=w, send_sem=send.at[3 * u + k],
                                                recv_sem=recv.at[3 * u + k], device_id=(x, y, 1 - c),
                                                device_id_type=MESH)

        sent = [d2d(u, t, k, c) for u, t in enumerate(idx) for k in range(3)]
        for cp in sent:
            cp.start()
        for u, t in enumerate(idx):
            for k in range(3):
                d2d(u, t, k, 1 - c).wait_recv()
        for cp in sent:
            cp.wait_send()

    return _call(body, name=name, in_specs=[HBM] * n, out_specs=[HBM] * n,
                 out_shape=[jax.ShapeDtypeStruct(b.shape, b.dtype) for b in bufs],
                 input_output_aliases={t: t for t in range(n)},
                 scratch_shapes=[pltpu.SemaphoreType.DMA((3 * len(idx),))] * 2)(*bufs)


def _chip_start(pairs, *, name):
    n = len(pairs)
    lands = [lax.empty((3,) + p.shape[1:], p.dtype) for p in pairs]

    def body(*refs):
        s, r = refs[:n], refs[n:2 * n]
        send, recv = refs[2 * n], refs[2 * n + 1]
        token = refs[4 * n + 2]
        x, y, c, j, chips, pj = _place()
        for t in range(n):
            for k in range(3):
                pltpu.make_async_remote_copy(src_ref=s[t].at[pj[k]], dst_ref=r[t].at[k], send_sem=send.at[3 * t + k],
                                             recv_sem=recv.at[3 * t + k], device_id=(*chips[k], c),
                                             device_id_type=MESH).start()
        token[...] = jnp.zeros_like(token)

    outs = _call(body, name=name, in_specs=[HBM] * (2 * n),
                 out_specs=(SEM, SEM, *([HBM] * (2 * n)), pl.BlockSpec(memory_space=pltpu.VMEM)),
                 out_shape=(pltpu.SemaphoreType.DMA((3 * n,)), pltpu.SemaphoreType.DMA((3 * n,)),
                            *[pltpu.HBM(a.shape, a.dtype) for a in list(pairs) + lands],
                            jax.ShapeDtypeStruct((8, 128), F32)),
                 input_output_aliases={t: 2 + t for t in range(2 * n)},
                 compiler_params=pltpu.CompilerParams(has_side_effects=EFFECT))(*[_hbm(a) for a in list(pairs) + lands])
    return outs[0], outs[1], list(outs[2:2 + n]), list(outs[2 + n:2 + 2 * n]), outs[2 + 2 * n]


def _chip_wait(send, recv, pairs, lands, after, *, name):
    n = len(pairs)

    def body(*refs):
        s, r = refs[:n], refs[n:2 * n]
        send, recv = refs[2 * n], refs[2 * n + 1]
        x, y, c, j, chips, pj = _place()
        for t in range(n):
            for k in range(3):
                cp = pltpu.make_async_remote_copy(src_ref=s[t].at[pj[k]], dst_ref=r[t].at[k], send_sem=send.at[3 * t + k],
                                                  recv_sem=recv.at[3 * t + k], device_id=(*chips[k], c),
                                                  device_id_type=MESH)
                cp.wait_send()
                cp.wait_recv()

    outs = _call(body, name=name,
                 in_specs=[HBM] * (2 * n) + [SEM, SEM] + [pl.BlockSpec(memory_space=pl.ANY)] * len(after),
                 out_specs=[HBM] * (2 * n), out_shape=[pltpu.HBM(a.shape, a.dtype) for a in list(pairs) + list(lands)],
                 input_output_aliases={t: t for t in range(2 * n)},
                 compiler_params=pltpu.CompilerParams(has_side_effects=EFFECT))(*pairs, *lands, send, recv, *after)
    return list(outs[:n]), list(outs[n:])


def _all_reduce_small(packed, *, name):
    R, C = packed.shape

    def body(p_ref, o_ref, slots, send, recv, lsem):
        x, y, c = lax.axis_index("x"), lax.axis_index("y"), lax.axis_index("c")
        me = 4 * x + 2 * y + c
        lc = pltpu.make_async_copy(p_ref, slots.at[me], lsem.at[0])
        lc.start()
        copies = []
        for rel in range(1, 8):
            fx, fy, fc = (rel >> 2) & 1, (rel >> 1) & 1, rel & 1
            to = (1 - x if fx else x, 1 - y if fy else y, 1 - c if fc else c)
            cp = pltpu.make_async_remote_copy(src_ref=p_ref, dst_ref=slots.at[me], send_sem=send.at[rel - 1],
                                              recv_sem=recv.at[rel - 1], device_id=to, device_id_type=MESH)
            cp.start()
            copies.append((cp, 4 * to[0] + 2 * to[1] + to[2]))
        for rel, (cp, frm) in enumerate(copies):
            cp.wait_send()
            pltpu.make_async_remote_copy(src_ref=p_ref, dst_ref=slots.at[frm], send_sem=send.at[rel],
                                         recv_sem=recv.at[rel], device_id=(x, y, c), device_id_type=MESH).wait_recv()
        lc.wait()
        acc = slots[0]
        for dev in range(1, 8):
            acc = acc + slots[dev]
        o_ref[...] = acc

    return _call(body, name=name, in_specs=[pl.BlockSpec(memory_space=pltpu.VMEM)],
                 out_specs=pl.BlockSpec(memory_space=pltpu.VMEM), out_shape=jax.ShapeDtypeStruct((R, C), F32),
                 scratch_shapes=[pltpu.VMEM((8, R, C), F32), pltpu.SemaphoreType.DMA((7,)),
                                 pltpu.SemaphoreType.DMA((7,)), pltpu.SemaphoreType.DMA((1,))],
                 compiler_params=pltpu.CompilerParams(vmem_limit_bytes=V7X_VMEM_LIMIT))(packed)


def _pair_sum(grad, theirs, core, *, name):
    J, H, C = theirs.shape
    tr = _rows(H, max(8, (1 << 19) // C // 8 * 8))

    def body(core_ref, a_ref, b_ref, o_ref):
        o_ref[...] = (a_ref[...].astype(F32) + b_ref[...].astype(F32)).astype(BF16)

    blk = pl.BlockSpec((None, tr, C), lambda j, i, s: (j, i, 0))
    mine = pl.BlockSpec((None, None, tr, C), lambda j, i, s: (j, s[0], i, 0))
    grid_spec = pltpu.PrefetchScalarGridSpec(num_scalar_prefetch=1, grid=(J, H // tr), in_specs=[mine, blk],
                                             out_specs=blk)
    return _call(body, name=name, grid_spec=grid_spec, out_shape=jax.ShapeDtypeStruct((J, H, C), BF16),
                 compiler_params=_params(("parallel", "parallel")))(core, grad.reshape(J, 2, H, C), theirs)


def _chip_sum(pairs, slots, place, *, name):
    _, H, C = slots.shape
    tr = _rows(H, max(8, (1 << 19) // C // 8 * 8))
    nr = H // tr

    def body(place_ref, s0, s1, s2, s3, o_ref):
        o_ref[...] = ((s0[...].astype(F32) + s1[...].astype(F32)) + s2[...].astype(F32)) + s3[...].astype(F32)

    def slot(k):
        return pl.BlockSpec((None, tr, C), lambda i, s: (k, i, 0))

    own = pl.BlockSpec((None, tr, C), lambda i, s: (s[0], i, 0))
    grid_spec = pltpu.PrefetchScalarGridSpec(
        num_scalar_prefetch=1, grid=(nr,), in_specs=[own, slot(0), slot(1), slot(2)],
        out_specs=pl.BlockSpec((tr, C), lambda i, s: (s[1] * nr + i, 0)))
    return _call(body, name=name, grid_spec=grid_spec, out_shape=jax.ShapeDtypeStruct((2 * H, C), F32),
                 compiler_params=_params(("parallel",)))(place, pairs, slots, slots, slots)


def _scatter_pair_start(grads, tag):
    lands = [lax.empty((4, g.shape[1] // 2, g.shape[2]), g.dtype) for g in grads]
    return _sibling_start(list(grads) + lands, _pair_copies(len(grads)), len(grads), (), name=f"rs_pair_start_{tag}")


def _scatter_chip_start(pair_started, after, place, tag):
    send, recv, arrays, _ = pair_started
    n = len(arrays) // 2
    arrays = _sibling_wait(send, recv, arrays, _pair_copies(n), after, name=f"rs_pair_wait_{tag}")
    pairs = [_pair_sum(g, t, place[1:], name=f"rs_pair_sum_{tag}_{i}")
             for i, (g, t) in enumerate(zip(arrays[:n], arrays[n:]))]
    return _chip_start(pairs, name=f"rs_chip_start_{tag}")


def _scatter_swap_start(chip_started, after, place, tag):
    send, recv, pairs, lands, _ = chip_started
    pairs, slots = _chip_wait(send, recv, pairs, lands, after, name=f"rs_chip_wait_{tag}")
    fulls = [_chip_sum(p, s, place, name=f"rs_chip_sum_{tag}_{i}") for i, (p, s) in enumerate(zip(pairs, slots))]
    return _sibling_start(fulls, _swap_copies, len(fulls), (), name=f"rs_swap_start_{tag}")


def _scatter_finish(swap_started, tag):
    send, recv, fulls, _ = swap_started
    return _sibling_wait(send, recv, fulls, _swap_copies, (), name=f"rs_swap_wait_{tag}")


def kernel(x, mem, w_in_a, w_in_b, w_s, b_s, vnorm_g, vnorm_b, w_mem_kv, w_out, ln1_g, ln1_b, w_ff1, w_ff2, ln2_g, ln2_b, loss_target, m_w_in_a, m_w_in_b, m_w_s, m_b_s, m_vnorm_g, m_vnorm_b, m_w_mem_kv, m_w_out, m_ln1_g, m_ln1_b, m_w_ff1, m_w_ff2, m_ln2_g, m_ln2_b, v_w_in_a, v_w_in_b, v_w_s, v_b_s, v_vnorm_g, v_vnorm_b, v_w_mem_kv, v_w_out, v_ln1_g, v_ln1_b, v_w_ff1, v_w_ff2, v_ln2_g, v_ln2_b):
    T, D = x.shape[1], x.shape[2]
    depth = w_ff1.shape[0]
    alpha = (2.0 * depth) ** 0.25
    a_out = (D // 256) * HEAD_DIM
    a_cols = len(A_PAIRS) * 3 * a_out
    b_width = (D // 256) * HEAD_DIM
    G = b_width // HEAD_DIM
    assert a_out % HEAD_BLOCK == 0 and T % (BLK * A_PAIRS[-1][1]) == 0

    xf = x.reshape(T, D)
    mem_b = mem.reshape(MEM_TOKENS, D).astype(BF16)
    target = loss_target.reshape(T, D)
    c_idx = lax.axis_index("x") * 2 + lax.axis_index("y")
    place = jnp.stack([c_idx, lax.axis_index("c")]).astype(jnp.int32)

    def gather_place(i, group):
        jl, tag = i // 2, "a" if i % 2 == 0 else "b"
        if group == "mix":
            mats = [(w_in_a if i % 2 == 0 else w_in_b)[jl], w_mem_kv[i], w_out[i]]
        else:
            mats = [w_ff1[i], w_ff2[i]]
        bufs = [_place_shard(w, place[:1], BF16, name=f"place_{tag}_{group}_{n_}") for n_, w in enumerate(mats)]
        if group == "mix" and i % 2:
            bufs += [_place_shard(v_[jl].reshape(1, -1), place[:1], F32, name="place_vnorm") for v_ in (vnorm_g, vnorm_b)]
        return bufs

    placed = {}

    def gather_begin(i, group, after):
        tag = "a" if i % 2 == 0 else "b"
        bufs = placed.pop((i, group), None) or gather_place(i, group)
        return _gather_start(bufs, after, name=f"gather_start_{tag}_{group}")

    def gather_end(started, after, name):
        send, recv, bufs, _ = started
        bufs = _gather_wait(send, recv, bufs, after, name=f"gather_wait_{name}")
        return _gather_pass(bufs, name=f"gather_pass_{name}")

    def group_view(P, g, d):
        return (P, g) if d == 1 else (P[:, g * 3 * a_out:(g + 1) * 3 * a_out], 0)

    saved = []
    xb = xf.astype(BF16)
    mix_started = gather_begin(0, "mix", ())
    for key in ((0, "ff"), (1, "mix"), (1, "ff")):
        if key[0] < depth:
            placed[key] = gather_place(*key)
    early = [b for bufs in placed.values() for b in bufs]
    ff_started = None
    for i in range(depth):
        jl = i // 2
        is_a = i % 2 == 0
        tag = "a" if is_a else "b"
        gathered = gather_end(mix_started, (xf, *early) if i == 0 else (xf,), f"{tag}_mix")
        win, wkv, wout = gathered[:3]
        wkv = wkv.reshape(1, D, 2 * MEM_WIDTH)
        tokens = []
        if i > 0:
            send, recv, ff_bufs, _ = ff_started
            ff_bufs = _gather_wait(send, recv, ff_bufs, (xf,), name=f"gather_wait_{tag}_ff")
            pass_copies, ncopies = _pass_copies([b.shape for b in ff_bufs])
            ff_pass = _sibling_start(ff_bufs, pass_copies, ncopies, (), name=f"gather_pass_start_{tag}_ff")
            tokens.append(ff_pass[3])
        token = None
        if i == 0:
            ff_started = gather_begin(0, "ff", (win,))
            token = ff_started[3]
        if i + 1 < depth:
            mix_started = gather_begin(i + 1, "mix", (win,) if token is None else (token,))
            ff_next = gather_begin(i + 1, "ff", (mix_started[3],))
            token = ff_next[3]
        if token is not None:
            tokens.append(token)
        P = _mm_nn(xb, win, out_dtype=BF16 if is_a else F32, after=tuple(tokens), name=f"proj_in_{'a' if is_a else 'b'}")
        kv = _mm_nn(mem_b, wkv, name="proj_kv")
        if is_a:
            outs = [_attn_fwd(*group_view(P, g, d), d, a_out, name=f"attn_fwd_d{d}") for g, (_, d) in enumerate(A_PAIRS)]
            mix, lse = _attn_combine([o for o, _ in outs], [l for _, l in outs], name="attn_combine")
            qcol = a_cols // MEM_WIDTH
            extra = (lse,)
        else:
            vg_full = gathered[3].reshape(-1)
            vb_full = gathered[4].reshape(-1)
            bs_t = b_s[jl].T
            mix = _gmlp_fwd(P, w_s[jl], bs_t, vg_full, vb_full, b_width, name="gmlp_fwd")
            qcol = 2 * b_width // MEM_WIDTH
            extra = (vg_full, vb_full, bs_t)
        cat = _mem_fwd(P, qcol, kv, mix, name=f"mem_fwd_{'a' if is_a else 'b'}")
        y = _mm_nn(cat, wout, out_dtype=F32, name="proj_out")
        x1, x1b = _ln_fwd(xf, y, ln1_g[i], ln1_b[i], alpha, name="ln_fwd")
        if i > 0:
            wff1, wff2 = _sibling_wait(ff_pass[0], ff_pass[1], ff_pass[2], pass_copies, (x1b,),
                                       name=f"gather_pass_wait_{tag}_ff")
        else:
            wff1, wff2 = gather_end(ff_started, (x1b,), f"{tag}_ff")
        wff2 = wff2.reshape(1, 4 * D, D)
        if i + 1 < depth:
            ff_started = ff_next
        a_pre, hid = _mm_nn(x1b, wff1, epi="relu2", name="ff1")
        f = _mm_nn(hid, wff2, out_dtype=F32, name="ff2")
        x2, x2b = _ln_fwd(x1, f, ln2_g[i], ln2_b[i], alpha, name="ln_fwd")
        saved.append(dict(xf=xf, xb=xb, P=P, kv=kv, mix=mix, cat=cat, y=y, x1=x1, x1b=x1b, a_pre=a_pre, hid=hid, f=f,
                          extra=extra, w=(win, wkv, wout, wff1, wff2), qcol=qcol))
        xf, xb = x2, x2b

    dx, sq = _loss_head(xf, target, name="loss_head")
    loss = lax.psum(sq[0, 0] * (0.5 / D), ("x", "y", "c"))

    g_big = dict(w_in_a=[None] * ((depth + 1) // 2), w_in_b=[None] * (depth // 2), w_mem_kv=[None] * depth,
                 w_out=[None] * depth, w_ff1=[None] * depth, w_ff2=[None] * depth)
    small = {k: [None] * depth for k in ("ln1_g", "ln1_b", "ln2_g", "ln2_b")}
    small_b = {k: [None] * (depth // 2) for k in ("w_s", "b_s", "vnorm_g", "vnorm_b")}
    def store(red, layer, group):
        if group == "ff":
            g_big["w_ff1"][layer], g_big["w_ff2"][layer] = red
        else:
            g_big["w_in_a" if layer % 2 == 0 else "w_in_b"][layer // 2], g_big["w_mem_kv"][layer], g_big["w_out"][layer] = red

    chips = []
    mix_pair = None
    tokens = ()
    for i in reversed(range(depth)):
        jl = i // 2
        is_a = i % 2 == 0
        s = saved[i]
        win, wkv, wout, wff1, wff2 = s["w"]
        tag = "a" if is_a else "b"
        d_f, adr2, dg2, db2 = _ln_bwd(dx, s["x1"], s["f"], ln2_g[i], alpha, tokens, name="ln_bwd")
        small["ln2_g"][i], small["ln2_b"][i] = dg2, db2
        gw_ff2 = _mm_tn(s["hid"], d_f, 1, name="grad_ff2").reshape(4, D, D)
        da = _mm_nt(d_f, wff2, epi="drelu2", extra=s["a_pre"], name="ff2_bwd")
        gw_ff1 = _mm_tn(s["x1b"], da, 4, name="grad_ff1")
        ff_pair = _scatter_pair_start([gw_ff1, gw_ff2], f"{tag}_ff")
        late = []
        if mix_pair is not None:
            above = "b" if is_a else "a"
            mix_chip = _scatter_chip_start(mix_pair, (d_f,), place, f"{above}_mix")
            chips.append((mix_chip, i + 1, "mix"))
            late.append(mix_chip[4])
        dx1 = _mm_nt(da, wff1, epi="resid", extra=adr2, after=(ff_pair[3],), name="ff1_bwd")
        ff_chip = _scatter_chip_start(ff_pair, (dx1,), place, f"{tag}_ff")
        chips.append((ff_chip, i, "ff"))
        late.append(ff_chip[4])
        d_y, adr1, dg1, db1 = _ln_bwd(dx1, s["xf"], s["y"], ln1_g[i], alpha, tuple(late), name="ln_bwd")
        small["ln1_g"][i], small["ln1_b"][i] = dg1, db1
        gw_out = _mm_tn(s["cat"], d_y, 4, name="grad_out")
        dcat = _mm_nt(d_y, wout, out_dtype=F32, name="proj_out_bwd")
        ocol = dcat.shape[1] // MEM_WIDTH - 1
        dmq, dkv = _mem_bwd(s["P"], s["qcol"], s["kv"], dcat, ocol, name=f"mem_bwd_{tag}")
        gw_kv = _mm_tn(mem_b, dkv.astype(BF16), 1, name="grad_kv").reshape(4, D // 4, 2 * MEM_WIDTH)
        if is_a:
            (lse,) = s["extra"]
            dmix = dcat[:, :a_out]
            parts = []
            for g, (_, d) in enumerate(A_PAIRS):
                Pg, g0 = group_view(s["P"], g, d)
                parts += _attn_bwd(Pg, dcat if d == 1 else dmix, s["mix"], lse, g0, d, a_out, name=f"attn_bwd_d{d}")
            dP = jnp.concatenate(parts + [dmq], axis=1)
        else:
            vg_full, vb_full, bs_t = s["extra"]
            dpu, dpv, dws, dbs, dvg, dvb = _gmlp_bwd(s["P"], w_s[jl], bs_t, vg_full, vb_full, dcat, b_width,
                                                     name="gmlp_bwd")
            small_b["w_s"][jl], small_b["b_s"][jl] = dws, dbs[:, :G].T
            small_b["vnorm_g"][jl], small_b["vnorm_b"][jl] = dvg, dvb
            dP = jnp.concatenate([dpu, dpv, dmq], axis=1)
        gw_in = _mm_tn(s["xb"], dP, 4, name=f"grad_in_{tag}")
        dx = _mm_nt(dP, win, epi="resid", extra=adr1, name=f"proj_in_bwd_{tag}")
        mix_pair = _scatter_pair_start([gw_in, gw_kv, gw_out], f"{tag}_mix")
        tokens = (mix_pair[3],)
    last_chip = _scatter_chip_start(mix_pair, (dx,), place, "a_mix")
    swaps, after = [], (dx,)
    for chip_started, layer, group in chips:
        name = f"{'a' if layer % 2 == 0 else 'b'}_{group}"
        swaps.append((_scatter_swap_start(chip_started, after, place, name), layer, group, name))
        after = (swaps[-1][0][3],)
    for swap_started, layer, group, name in swaps:
        store(_scatter_finish(swap_started, name), layer, group)

    def finish_last(after):
        store(_scatter_finish(_scatter_swap_start(last_chip, after, place, "a_mix"), "a_mix"), 0, "mix")

    grad_x = dx.reshape(x.shape)

    nb_layers = depth // 2
    pieces = ([jnp.stack(small_b["w_s"]).reshape(-1, 128), jnp.stack(small_b["b_s"]).reshape(-1, 128),
               jnp.stack(small_b["vnorm_g"]).reshape(-1, 128), jnp.stack(small_b["vnorm_b"]).reshape(-1, 128)]
              + [jnp.stack(small[k]).reshape(-1, 128) for k in ("ln1_g", "ln1_b", "ln2_g", "ln2_b")])
    sizes = [p.shape[0] for p in pieces]
    pad = (-sum(sizes)) % 8
    packed = jnp.concatenate(pieces + ([jnp.zeros((pad, 128), F32)] if pad else []), axis=0)
    summed = _all_reduce_small(packed, name="all_reduce_small")
    offs = [0]
    for n_ in sizes:
        offs.append(offs[-1] + n_)
    sp = [summed[offs[k]:offs[k + 1]] for k in range(len(sizes))]
    vshard = vnorm_g.shape[1]
    g_small = dict(
        w_s=sp[0].reshape(w_s.shape), b_s=sp[1].reshape(b_s.shape),
        vnorm_g=lax.dynamic_slice_in_dim(sp[2].reshape(nb_layers, -1), c_idx * vshard, vshard, axis=1),
        vnorm_b=lax.dynamic_slice_in_dim(sp[3].reshape(nb_layers, -1), c_idx * vshard, vshard, axis=1),
        ln1_g=sp[4].reshape(ln1_g.shape), ln1_b=sp[5].reshape(ln1_b.shape),
        ln2_g=sp[6].reshape(ln2_g.shape), ln2_b=sp[7].reshape(ln2_b.shape))

    names = ["w_in_a", "w_in_b", "w_s", "b_s", "vnorm_g", "vnorm_b", "w_mem_kv", "w_out", "ln1_g", "ln1_b", "w_ff1",
             "w_ff2", "ln2_g", "ln2_b"]
    ws = dict(w_in_a=w_in_a, w_in_b=w_in_b, w_s=w_s, b_s=b_s, vnorm_g=vnorm_g, vnorm_b=vnorm_b, w_mem_kv=w_mem_kv,
              w_out=w_out, ln1_g=ln1_g, ln1_b=ln1_b, w_ff1=w_ff1, w_ff2=w_ff2, ln2_g=ln2_g, ln2_b=ln2_b)
    ms = dict(w_in_a=m_w_in_a, w_in_b=m_w_in_b, w_s=m_w_s, b_s=m_b_s, vnorm_g=m_vnorm_g, vnorm_b=m_vnorm_b,
              w_mem_kv=m_w_mem_kv, w_out=m_w_out, ln1_g=m_ln1_g, ln1_b=m_ln1_b, w_ff1=m_w_ff1, w_ff2=m_w_ff2,
              ln2_g=m_ln2_g, ln2_b=m_ln2_b)
    vs = dict(w_in_a=v_w_in_a, w_in_b=v_w_in_b, w_s=v_w_s, b_s=v_b_s, vnorm_g=v_vnorm_g, vnorm_b=v_vnorm_b,
              w_mem_kv=v_w_mem_kv, w_out=v_w_out, ln1_g=v_ln1_g, ln1_b=v_ln1_b, w_ff1=v_w_ff1, w_ff2=v_w_ff2,
              ln2_g=v_ln2_g, ln2_b=v_ln2_b)
    grads, deltas, new_m, new_v = {}, {}, {}, {}
    for k in ("w_ff1", "w_ff2", "w_in_a", "w_in_b", "w_mem_kv", "w_out"):
        if k == "w_in_a":
            finish_last((deltas["w_ff2"], summed))
        g = jnp.stack(g_big[k]).reshape(ws[k].shape)
        cols = ws[k].shape[-1]
        d_, m_, v_ = _adamw(ws[k].reshape(-1, cols), g.reshape(-1, cols), ms[k].reshape(-1, cols),
                            vs[k].reshape(-1, cols), name=f"adamw_{k}")
        grads[k] = g
        deltas[k], new_m[k], new_v[k] = (t.reshape(ws[k].shape) for t in (d_, m_, v_))
    small_names = [k for k in names if k not in g_big]

    def pack(tree):
        flat = jnp.concatenate([tree[k].reshape(-1) for k in small_names])
        padn = (-flat.shape[0]) % 1024
        return jnp.pad(flat, (0, padn)).reshape(-1, 128)

    d_, m_, v_ = _adamw(pack(ws), pack(g_small), pack(ms), pack(vs), name="adamw_small")
    off = 0
    for k in small_names:
        n_ = ws[k].size
        grads[k] = g_small[k]
        deltas[k], new_m[k], new_v[k] = (t.reshape(-1)[off:off + n_].reshape(ws[k].shape) for t in (d_, m_, v_))
        off += n_

    return (loss, grad_x, *[grads[k] for k in names], *[deltas[k] for k in names], *[new_m[k] for k in names],
            *[new_v[k] for k in names])
```
